```python
import math
import jax
import jax.numpy as jnp
from jax import lax
import numpy as np

D_MODEL = 1024
BATCH = 8
SEQ = 2048
DEPTH = 2
DEC_BATCH = 128
DEC_SEQ = 4
PAST_LEN = 16384
PAGE_SIZE = 128

GDN_HEADS = 6
GDN_DK = 64
GDN_DV = 64
GDN_QK = GDN_HEADS * GDN_DK
GDN_VW = GDN_HEADS * GDN_DV
GDN_CONV_CH = 2 * GDN_QK + GDN_VW
CONV_W = 4
S5_GROUPS = 16
S5_GROUP_CH = 16
S5_CH = S5_GROUPS * S5_GROUP_CH
S5_STATE = 64
ML_HEADS = 6
ML_DH = 64
ML_W = ML_HEADS * ML_DH
MIX_W = GDN_VW + S5_CH + ML_W
CHUNK = 64
N_GROUPS = 4
EXPERTS_PER_GROUP = 4
N_EXPERTS = N_GROUPS * EXPERTS_PER_GROUP
EXPERT_TOP_K = 2
D_EXPERT = 256
PLE_DIM = 256
EPS = 1e-6
IN_SPLITS = (GDN_CONV_CH, GDN_VW, GDN_HEADS, GDN_HEADS, S5_CH, 3 * ML_W, ML_HEADS, ML_HEADS, ML_W)
IN_W = GDN_CONV_CH + GDN_VW + 2 * GDN_HEADS + S5_CH + 3 * ML_W + 2 * ML_HEADS + ML_W

kernel_name = 'hymba_gdn_s5_mlstm_hmoe_decode_step'


def _split(x, sizes):
    idx = []
    acc = 0
    for s in sizes[:-1]:
        acc += s
        idx.append(acc)
    return jnp.split(x, idx, axis=-1)


def _rmsnorm(x, gain):
    xf = x.astype(jnp.float32)
    return xf * lax.rsqrt(jnp.mean(xf * xf, axis=-1, keepdims=True) + EPS) * gain.astype(jnp.float32)


def _l2norm(x):
    return x * lax.rsqrt(jnp.sum(x * x, axis=-1, keepdims=True) + EPS)


def _to_chunks(a, L):
    B, T, H = a.shape[:3]
    rest = a.shape[3:]
    a = a.reshape((B, T // L, L, H) + rest)
    return a.transpose((1, 0, 3, 2) + tuple(range(4, a.ndim)))


def _from_chunks(a):
    N, B, H, L, d = a.shape
    return a.transpose(1, 0, 3, 2, 4).reshape(B, N * L, H, d)


def _causal_conv(x, buf, w):
    T = x.shape[1]
    xc = jnp.concatenate([buf, x], axis=1)
    out = xc[:, 0:T] * w[0]
    for j in range(1, CONV_W):
        out = out + xc[:, j:j + T] * w[j]
    return jax.nn.silu(out), xc[:, T:]


def _gated_delta_rule(q, k, v, beta, g, S0):
    B, T, H, dk = q.shape
    dv = v.shape[-1]
    L = math.gcd(T, CHUNK)
    qc = _to_chunks(q * dk ** -0.5, L)
    kc = _to_chunks(k, L)
    vc = _to_chunks(v, L)
    bc = _to_chunks(beta, L)
    gc = jnp.cumsum(_to_chunks(g, L), axis=-1)
    incl = jnp.tril(jnp.ones((L, L), dtype=bool))
    strict = jnp.tril(jnp.ones((L, L), dtype=bool), -1)
    eye = jnp.eye(L, dtype=jnp.float32)

    def chunk_step(S, inp):
        q_, k_, v_, b_, gcum = inp
        gamma = jnp.exp(jnp.where(incl, gcum[..., :, None] - gcum[..., None, :], -jnp.inf))
        kbk = jnp.einsum('bhlk,bhmk->bhlm', k_ * b_[..., None], k_)
        a_mat = eye + jnp.where(strict, kbk * gamma, 0.0)
        rhs = jnp.concatenate([v_ * b_[..., None], k_ * (b_ * jnp.exp(gcum))[..., None]], axis=-1)
        sol = lax.linalg.triangular_solve(a_mat, rhs, left_side=True, lower=True)
        u_, w_ = sol[..., :dv], sol[..., dv:]
        v_new = u_ - jnp.einsum('bhlk,bhkv->bhlv', w_, S)
        qk = jnp.einsum('bhlk,bhmk->bhlm', q_, k_) * gamma
        o = (jnp.einsum('bhlk,bhkv->bhlv', q_ * jnp.exp(gcum)[..., None], S)
             + jnp.einsum('bhlm,bhmv->bhlv', qk, v_new))
        g_last = gcum[..., -1]
        S = (S * jnp.exp(g_last)[..., None, None]
             + jnp.einsum('bhlk,bhlv->bhkv', k_ * jnp.exp(g_last[..., None] - gcum)[..., None], v_new))
        return S, o

    S, o = lax.scan(chunk_step, S0, (qc, kc, vc, bc, gc))
    return _from_chunks(o), S


def _gdn_mixer(qkv, z, b, a, conv_buf, S0, conv_w, A_log, dt_bias, norm_g):
    f32 = jnp.float32
    Bsz, T, _ = qkv.shape
    qkv_c, conv_new = _causal_conv(qkv.astype(f32), conv_buf.astype(f32), conv_w.astype(f32))
    q, k, v = _split(qkv_c, (GDN_QK, GDN_QK, GDN_VW))
    q = _l2norm(q.reshape(Bsz, T, GDN_HEADS, GDN_DK))
    k = _l2norm(k.reshape(Bsz, T, GDN_HEADS, GDN_DK))
    v = v.reshape(Bsz, T, GDN_HEADS, GDN_DV)
    beta = jax.nn.sigmoid(b.astype(f32))
    g = -jnp.exp(A_log.astype(f32)) * jax.nn.softplus(a.astype(f32) + dt_bias.astype(f32))
    o, S = _gated_delta_rule(q, k, v, beta, g, S0.astype(f32))
    o = _rmsnorm(o, norm_g) * jax.nn.silu(z.astype(f32).reshape(Bsz, T, GDN_HEADS, GDN_DV))
    return o.reshape(Bsz, T, GDN_VW), conv_new, S


def _s5_mixer(u, h0_re, h0_im, lam_re, lam_im, log_dt, B_re, B_im, C_re, C_im, D_skip, w_glu):
    f32 = jnp.float32
    Bsz, T, _ = u.shape
    uf = u.astype(f32)
    lr = lam_re.astype(f32)
    li = lam_im.astype(f32)
    dt = jnp.exp(log_dt.astype(f32))[:, None]
    mag = jnp.exp(lr * dt)
    lb_re = mag * jnp.cos(li * dt)
    lb_im = mag * jnp.sin(li * dt)
    den = lr * lr + li * li
    c_re = ((lb_re - 1.0) * lr + lb_im * li) / den
    c_im = (lb_im * lr - (lb_re - 1.0) * li) / den
    Br = B_re.astype(f32)
    Bi = B_im.astype(f32)
    Bb_re = c_re[..., None] * Br - c_im[..., None] * Bi
    Bb_im = c_re[..., None] * Bi + c_im[..., None] * Br
    ug = uf.reshape(Bsz, T, S5_GROUPS, S5_GROUP_CH)
    bu_re = jnp.einsum('btgh,gph->btgp', ug, Bb_re)
    bu_im = jnp.einsum('btgh,gph->btgp', ug, Bb_im)
    a_re = jnp.broadcast_to(lb_re, bu_re.shape)
    a_im = jnp.broadcast_to(lb_im, bu_im.shape)

    def combine(e1, e2):
        a1r, a1i, b1r, b1i = e1
        a2r, a2i, b2r, b2i = e2
        return (a2r * a1r - a2i * a1i, a2r * a1i + a2i * a1r,
                a2r * b1r - a2i * b1i + b2r, a2r * b1i + a2i * b1r + b2i)

    p_re, p_im, h_re, h_im = lax.associative_scan(combine, (a_re, a_im, bu_re, bu_im), axis=1)
    s_re = h0_re.astype(f32)[:, None]
    s_im = h0_im.astype(f32)[:, None]
    h_re, h_im = h_re + p_re * s_re - p_im * s_im, h_im + p_re * s_im + p_im * s_re
    y = (jnp.einsum('btgp,ghp->btgh', h_re, C_re.astype(f32))
         - jnp.einsum('btgp,ghp->btgh', h_im, C_im.astype(f32)))
    y = y.reshape(Bsz, T, S5_CH) + D_skip.astype(f32) * uf
    zg = jax.nn.gelu(y)
    out = zg * jax.nn.sigmoid(zg @ w_glu.astype(f32))
    return out, h_re[:, -1], h_im[:, -1]


def _mlstm_chunked(q, k, v, log_i, log_f, C0, n0, m0):
    B, T, H, dk = q.shape
    L = math.gcd(T, CHUNK)
    qc = _to_chunks(q, L)
    kc = _to_chunks(k * dk ** -0.5, L)
    vc = _to_chunks(v, L)
    ic = _to_chunks(log_i, L)
    bc = jnp.cumsum(_to_chunks(log_f, L), axis=-1)
    incl = jnp.tril(jnp.ones((L, L), dtype=bool))

    def chunk_step(carry, inp):
        C, n, m = carry
        q_, k_, v_, li, bcum = inp
        d_mat = jnp.where(incl, bcum[..., :, None] - bcum[..., None, :] + li[..., None, :], -jnp.inf)
        inter = bcum + m[..., None]
        m_t = jnp.maximum(inter, jnp.max(d_mat, axis=-1))
        w_intra = jnp.exp(d_mat - m_t[..., None])
        e_inter = jnp.exp(inter - m_t)
        s = jnp.einsum('bhlk,bhmk->bhlm', q_, k_) * w_intra
        num = (e_inter[..., None] * jnp.einsum('bhlk,bhkv->bhlv', q_, C)
               + jnp.einsum('bhlm,bhmv->bhlv', s, v_))
        den = e_inter * jnp.einsum('bhlk,bhk->bhl', q_, n) + jnp.sum(s, axis=-1)
        h = num / jnp.maximum(jnp.abs(den), jnp.exp(-m_t))[..., None]
        m_new = m_t[..., -1]
        e_c = jnp.exp(bcum[..., -1] + m - m_new)
        w_k = jnp.exp(bcum[..., -1:] - bcum + li - m_new[..., None])
        kw = k_ * w_k[..., None]
        C = e_c[..., None, None] * C + jnp.einsum('bhlk,bhlv->bhkv', kw, v_)
        n = e_c[..., None] * n + jnp.sum(kw, axis=2)
        return (C, n, m_new), h

    (C, n, m), h = lax.scan(chunk_step, (C0, n0, m0), (qc, kc, vc, ic, bc))
    return _from_chunks(h), C, n, m


def _mlstm_mixer(qkv, i_pre, f_pre, o_pre, C0, n0, m0, ig_bias, fg_bias, norm_g):
    f32 = jnp.float32
    Bsz, T, _ = qkv.shape
    shape = (Bsz, T, ML_HEADS, ML_DH)
    q, k, v = _split(qkv.astype(f32), (ML_W, ML_W, ML_W))
    log_i = i_pre.astype(f32) + ig_bias.astype(f32)
    log_f = jax.nn.log_sigmoid(f_pre.astype(f32) + fg_bias.astype(f32))
    h, C, n, m = _mlstm_chunked(q.reshape(shape), k.reshape(shape), v.reshape(shape), log_i, log_f,
                                C0.astype(f32), n0.astype(f32), m0.astype(f32))
    h = _rmsnorm(h, norm_g) * jax.nn.sigmoid(o_pre.astype(f32)).reshape(shape)
    return h.reshape(Bsz, T, ML_W), C, n, m


def _hier_moe(x, w_rg, b_rg, w_re, b_re, w_gate, w_up, w_down):
    f32 = jnp.float32
    xf = x.astype(f32)
    p_grp = jax.nn.softmax(xf @ w_rg.astype(f32) + b_rg.astype(f32), axis=-1)
    g_idx = jnp.argmax(p_grp, axis=-1)
    g_prob = jnp.max(p_grp, axis=-1)
    logits_e = xf @ w_re.astype(f32) + b_re.astype(f32)
    expert_group = jnp.arange(N_EXPERTS) // EXPERTS_PER_GROUP
    logits_e = jnp.where(expert_group == g_idx[..., None], logits_e, -jnp.inf)
    top_logit, top_idx = lax.top_k(logits_e, EXPERT_TOP_K)
    top_w = jax.nn.softmax(top_logit, axis=-1) * g_prob[..., None]
    gates = jnp.sum(jax.nn.one_hot(top_idx, N_EXPERTS, dtype=f32) * top_w[..., None], axis=-2)
    hg = jnp.einsum('btd,edf->btef', x, w_gate)
    hu = jnp.einsum('btd,edf->btef', x, w_up)
    hidden = jax.nn.silu(hg) * hu * gates[..., None].astype(hg.dtype)
    return jnp.einsum('btef,efd->btd', hidden, w_down)


def _trunk(x, p, init_states, params):
    (conv0, gdn0, s5re0, s5im0, mC0, mn0, mm0) = init_states
    (norm_mix, w_in, w_out, gdn_conv_w, gdn_A_log, gdn_dt_bias, gdn_norm,
     s5_lam_re, s5_lam_im, s5_log_dt, s5_B_re, s5_B_im, s5_C_re, s5_C_im, s5_D, s5_w_glu,
     ml_ig_bias, ml_fg_bias, ml_norm,
     norm_ffn, w_router_group, b_router_group, w_router_expert, b_router_expert,
     w_exp_gate, w_exp_up, w_exp_down,
     norm_ple, w_ple_gate, w_ple_proj, final_norm) = params
    h = x
    conv_n, gdn_n, s5re_n, s5im_n, mC_n, mn_n, mm_n = [], [], [], [], [], [], []
    for l in range(DEPTH):
        u = _rmsnorm(h, norm_mix[l]).astype(h.dtype)
        (g_qkv, g_z, g_b, g_a, s_u, m_qkv, m_i, m_f, m_o) = _split(u @ w_in[l], IN_SPLITS)
        o_gdn, conv_l, gdn_l = _gdn_mixer(g_qkv, g_z, g_b, g_a, conv0[l], gdn0[l], gdn_conv_w[l],
                                          gdn_A_log[l], gdn_dt_bias[l], gdn_norm[l])
        o_s5, s5re_l, s5im_l = _s5_mixer(s_u, s5re0[l], s5im0[l], s5_lam_re[l], s5_lam_im[l], s5_log_dt[l],
                                         s5_B_re[l], s5_B_im[l], s5_C_re[l], s5_C_im[l], s5_D[l], s5_w_glu[l])
        o_ml, C_l, n_l, m_l = _mlstm_mixer(m_qkv, m_i, m_f, m_o, mC0[l], mn0[l], mm0[l],
                                           ml_ig_bias[l], ml_fg_bias[l], ml_norm[l])
        mix = jnp.concatenate([o_gdn, o_s5, o_ml], axis=-1).astype(h.dtype)
        h = h + mix @ w_out[l]
        ffn_in = _rmsnorm(h, norm_ffn[l]).astype(h.dtype)
        h = h + _hier_moe(ffn_in, w_router_group[l], b_router_group[l], w_router_expert[l],
                          b_router_expert[l], w_exp_gate[l], w_exp_up[l], w_exp_down[l])
        gate = jax.nn.sigmoid(_rmsnorm(h, norm_ple[l]).astype(h.dtype) @ w_ple_gate[l])
        h = h + (p[l] @ w_ple_proj[l]) * gate
        conv_n.append(conv_l)
        gdn_n.append(gdn_l)
        s5re_n.append(s5re_l)
        s5im_n.append(s5im_l)
        mC_n.append(C_l)
        mn_n.append(n_l)
        mm_n.append(m_l)
    y = _rmsnorm(h, final_norm).astype(x.dtype)
    return (y, jnp.stack(conv_n), jnp.stack(gdn_n), jnp.stack(s5re_n), jnp.stack(s5im_n),
            jnp.stack(mC_n), jnp.stack(mn_n), jnp.stack(mm_n))


def setup_inputs(seed: int = 0) -> dict:
    key = jax.random.key(seed)
    f32 = jnp.float32
    counter = [0]

    def nxt():
        counter[0] += 1
        return jax.random.fold_in(key, counter[0])

    def nrm(shape, scale):
        return jax.random.normal(nxt(), shape, f32) * scale

    def unif(shape, lo, hi):
        return jax.random.uniform(nxt(), shape, f32, lo, hi)

    def gain(shape):
        return 1.0 + nrm(shape, 0.02)

    gdn_dt = jnp.exp(unif((DEPTH, GDN_HEADS), math.log(1e-3), math.log(1e-1)))
    lam_im = jnp.broadcast_to(jnp.pi * jnp.arange(S5_STATE, dtype=f32), (DEPTH, S5_GROUPS, S5_STATE))
    return {
        'x_prompt': nrm((BATCH, SEQ, D_MODEL), 1.0),
        'x_sample': nrm((DEC_BATCH, DEC_SEQ, D_MODEL), 1.0),
        'p_prompt': nrm((DEPTH, BATCH, SEQ, PLE_DIM), 1.0),
        'p_sample': nrm((DEPTH, DEC_BATCH, DEC_SEQ, PLE_DIM), 1.0),
        'state_gdn_conv': nrm((DEPTH, DEC_BATCH, CONV_W - 1, GDN_CONV_CH), 1.0),
        'state_gdn': nrm((DEPTH, DEC_BATCH, GDN_HEADS, GDN_DK, GDN_DV), 0.1),
        'state_s5_re': nrm((DEPTH, DEC_BATCH, S5_GROUPS, S5_STATE), 0.1),
        'state_s5_im': nrm((DEPTH, DEC_BATCH, S5_GROUPS, S5_STATE), 0.1),
        'state_mlstm_C': nrm((DEPTH, DEC_BATCH, ML_HEADS, ML_DH, ML_DH), 0.1),
        'state_mlstm_n': nrm((DEPTH, DEC_BATCH, ML_HEADS, ML_DH), 0.1),
        'state_mlstm_m': nrm((DEPTH, DEC_BATCH, ML_HEADS), 1.0),
        'norm_mix': gain((DEPTH, D_MODEL)),
        'w_in': nrm((DEPTH, D_MODEL, IN_W), D_MODEL ** -0.5),
        'w_out': nrm((DEPTH, MIX_W, D_MODEL), MIX_W ** -0.5),
        'gdn_conv_w': nrm((DEPTH, CONV_W, GDN_CONV_CH), CONV_W ** -0.5),
        'gdn_A_log': jnp.log(unif((DEPTH, GDN_HEADS), 1.0, 16.0)),
        'gdn_dt_bias': gdn_dt + jnp.log(-jnp.expm1(-gdn_dt)),
        'gdn_norm': gain((DEPTH, GDN_DV)),
        's5_lam_re': -0.5 + nrm((DEPTH, S5_GROUPS, S5_STATE), 0.01),
        's5_lam_im': lam_im,
        's5_log_dt': unif((DEPTH, S5_GROUPS), math.log(1e-3), math.log(1e-1)),
        's5_B_re': nrm((DEPTH, S5_GROUPS, S5_STATE, S5_GROUP_CH), (2 * S5_GROUP_CH) ** -0.5),
        's5_B_im': nrm((DEPTH, S5_GROUPS, S5_STATE, S5_GROUP_CH), (2 * S5_GROUP_CH) ** -0.5),
        's5_C_re': nrm((DEPTH, S5_GROUPS, S5_GROUP_CH, S5_STATE), (0.5 * S5_STATE) ** -0.5),
        's5_C_im': nrm((DEPTH, S5_GROUPS, S5_GROUP_CH, S5_STATE), (0.5 * S5_STATE) ** -0.5),
        's5_D': nrm((DEPTH, S5_CH), 1.0),
        's5_w_glu': nrm((DEPTH, S5_CH, S5_CH), S5_CH ** -0.5),
        'ml_ig_bias': nrm((DEPTH, ML_HEADS), 0.1),
        'ml_fg_bias': jnp.linspace(3.0, 6.0, ML_HEADS, dtype=f32)[None] + nrm((DEPTH, ML_HEADS), 0.01),
        'ml_norm': gain((DEPTH, ML_DH)),
        'norm_ffn': gain((DEPTH, D_MODEL)),
        'w_router_group': nrm((DEPTH, D_MODEL, N_GROUPS), D_MODEL ** -0.5),
        'b_router_group': nrm((DEPTH, N_GROUPS), 0.01),
        'w_router_expert': nrm((DEPTH, D_MODEL, N_EXPERTS), D_MODEL ** -0.5),
        'b_router_expert': nrm((DEPTH, N_EXPERTS), 0.01),
        'w_exp_gate': nrm((DEPTH, N_EXPERTS, D_MODEL, D_EXPERT), D_MODEL ** -0.5),
        'w_exp_up': nrm((DEPTH, N_EXPERTS, D_MODEL, D_EXPERT), D_MODEL ** -0.5),
        'w_exp_down': nrm((DEPTH, N_EXPERTS, D_EXPERT, D_MODEL), D_EXPERT ** -0.5),
        'norm_ple': gain((DEPTH, D_MODEL)),
        'w_ple_gate': nrm((DEPTH, D_MODEL, D_MODEL), D_MODEL ** -0.5),
        'w_ple_proj': nrm((DEPTH, PLE_DIM, D_MODEL), PLE_DIM ** -0.5),
        'final_norm': gain((D_MODEL,)),
    }


def reference(x_prompt, x_sample, p_prompt, p_sample,
              state_gdn_conv, state_gdn, state_s5_re, state_s5_im,
              state_mlstm_C, state_mlstm_n, state_mlstm_m,
              norm_mix, w_in, w_out, gdn_conv_w, gdn_A_log, gdn_dt_bias, gdn_norm,
              s5_lam_re, s5_lam_im, s5_log_dt, s5_B_re, s5_B_im, s5_C_re, s5_C_im, s5_D, s5_w_glu,
              ml_ig_bias, ml_fg_bias, ml_norm,
              norm_ffn, w_router_group, b_router_group, w_router_expert, b_router_expert,
              w_exp_gate, w_exp_up, w_exp_down,
              norm_ple, w_ple_gate, w_ple_proj, final_norm):
    f32 = jnp.float32
    params = (norm_mix, w_in, w_out, gdn_conv_w, gdn_A_log, gdn_dt_bias, gdn_norm,
              s5_lam_re, s5_lam_im, s5_log_dt, s5_B_re, s5_B_im, s5_C_re, s5_C_im, s5_D, s5_w_glu,
              ml_ig_bias, ml_fg_bias, ml_norm,
              norm_ffn, w_router_group, b_router_group, w_router_expert, b_router_expert,
              w_exp_gate, w_exp_up, w_exp_down,
              norm_ple, w_ple_gate, w_ple_proj, final_norm)
    bp = x_prompt.shape[0]
    prompt_init = (jnp.zeros((DEPTH, bp, CONV_W - 1, GDN_CONV_CH), f32),
                   jnp.zeros((DEPTH, bp, GDN_HEADS, GDN_DK, GDN_DV), f32),
                   jnp.zeros((DEPTH, bp, S5_GROUPS, S5_STATE), f32),
                   jnp.zeros((DEPTH, bp, S5_GROUPS, S5_STATE), f32),
                   jnp.zeros((DEPTH, bp, ML_HEADS, ML_DH, ML_DH), f32),
                   jnp.zeros((DEPTH, bp, ML_HEADS, ML_DH), f32),
                   jnp.zeros((DEPTH, bp, ML_HEADS), f32))
    sample_init = (state_gdn_conv, state_gdn, state_s5_re, state_s5_im,
                   state_mlstm_C, state_mlstm_n, state_mlstm_m)
    (y_prompt, p_conv, p_gdn, p_s5_re, p_s5_im, p_mC, p_mn, p_mm) = _trunk(x_prompt, p_prompt, prompt_init, params)
    (y_sample, s_conv, s_gdn, s_s5_re, s_s5_im, s_mC, s_mn, s_mm) = _trunk(x_sample, p_sample, sample_init, params)
    return (y_prompt, y_sample,
            p_conv, p_gdn, p_s5_re, p_s5_im, p_mC, p_mn, p_mm,
            s_conv, s_gdn, s_s5_re, s_s5_im, s_mC, s_mn, s_mm)
```

```python
import functools
import math

import jax
import jax.numpy as jnp
from jax import lax
from jax.experimental import pallas as pl
from jax.experimental.pallas import tpu as pltpu

F32 = jnp.float32
BF16 = jnp.bfloat16

D_MODEL = 1024
DEPTH = 2
GDN_HEADS = 6
GDN_DK = 64
GDN_DV = 64
GDN_QK = GDN_HEADS * GDN_DK
GDN_VW = GDN_HEADS * GDN_DV
GDN_CONV_CH = 2 * GDN_QK + GDN_VW
CONV_W = 4
S5_GROUPS = 16
S5_GROUP_CH = 16
S5_CH = S5_GROUPS * S5_GROUP_CH
S5_STATE = 64
S5_N = S5_GROUPS * S5_STATE
ML_HEADS = 6
ML_DH = 64
ML_W = ML_HEADS * ML_DH
N_GROUPS = 4
EXPERTS_PER_GROUP = 4
N_EXPERTS = N_GROUPS * EXPERTS_PER_GROUP
D_EXPERT = 256
PLE_DIM = 256
EPS = 1e-6

LANES = 128
SUBLANES = 8
SLAB_W = GDN_CONV_CH + GDN_VW + LANES
GATE_OFF = GDN_CONV_CH + GDN_VW
NEG = -1e30
VMEM_LIMIT = 56 * 1024 * 1024

_HI = lax.Precision.HIGHEST


def _cparams(*sem):
    return pltpu.CompilerParams(dimension_semantics=sem, vmem_limit_bytes=VMEM_LIMIT)


def _dot(a, b):
    return jnp.dot(a.astype(BF16), b.astype(BF16), preferred_element_type=F32)


def _dot_nt(a, b):
    return lax.dot_general(a.astype(BF16), b.astype(BF16), (((1,), (1,)), ((), ())),
                           preferred_element_type=F32)


def _dot_tn(a, b):
    return lax.dot_general(a.astype(BF16), b.astype(BF16), (((0,), (0,)), ((), ())),
                           preferred_element_type=F32)


def _split_bf16(a):
    hi = a.astype(BF16)
    lo = (a - hi.astype(F32)).astype(BF16)
    return hi, lo


def _dot3(a, b):
    ah, al = _split_bf16(a)
    bh, bl = _split_bf16(b)
    d = functools.partial(jnp.dot, preferred_element_type=F32)
    return d(ah, bh) + (d(ah, bl) + d(al, bh))


def _rms(x, gain):
    return x * lax.rsqrt(jnp.mean(x * x, axis=-1, keepdims=True) + EPS) * gain


def _softplus(x):
    return jnp.maximum(x, 0.0) + jnp.log(1.0 + jnp.exp(-jnp.abs(x)))


def _sigmoid(x):
    return 1.0 / (1.0 + jnp.exp(-x))


def _silu(x):
    return x * _sigmoid(x)


def _unit_lower_inverse(n_mat, size):
    r = lax.broadcasted_iota(jnp.int32, (size, size), 0)
    c = lax.broadcasted_iota(jnp.int32, (size, size), 1)
    base = min(16, size)
    same = jnp.bitwise_xor(r, c) < base
    nd = jnp.where(same, n_mat, 0.0)
    eye = jnp.where(r == c, 1.0, 0.0).astype(F32)
    t = eye - nd
    x = nd
    p = 1
    while 2 * p < base:
        x = _dot3(x, x)
        t = t + _dot3(t, x)
        p *= 2
    blk = base
    while blk < size:
        pair = jnp.bitwise_xor(r, c)
        off = jnp.where((pair < 2 * blk) & (pair >= blk), n_mat, 0.0)
        t = t - _dot3(t, _dot3(off, t))
        blk *= 2
    return t


def _norm_inproj_kernel(x_ref, g_ref, w_ref, oa_ref, ob_ref, oc_ref):
    u = _rms(x_ref[...], g_ref[...]).astype(BF16)
    oa_ref[...] = jnp.dot(u, w_ref[:, 0:SLAB_W], preferred_element_type=F32)
    ob_ref[...] = jnp.dot(u, w_ref[:, SLAB_W:SLAB_W + S5_CH], preferred_element_type=F32)
    oc_ref[...] = jnp.dot(u, w_ref[:, SLAB_W + S5_CH:], preferred_element_type=F32)


def _norm_inproj(x, gain, w, tm):
    m = x.shape[0]
    n = w.shape[1]
    return pl.pallas_call(
        _norm_inproj_kernel,
        grid=(m // tm,),
        in_specs=[pl.BlockSpec((tm, D_MODEL), lambda i: (i, 0)),
                  pl.BlockSpec((1, D_MODEL), lambda i: (0, 0)),
                  pl.BlockSpec((D_MODEL, n), lambda i: (0, 0))],
        out_specs=[pl.BlockSpec((tm, SLAB_W), lambda i: (i, 0)),
                   pl.BlockSpec((tm, S5_CH), lambda i: (i, 0)),
                   pl.BlockSpec((tm, SLAB_W), lambda i: (i, 0))],
        out_shape=[jax.ShapeDtypeStruct((m, SLAB_W), F32),
                   jax.ShapeDtypeStruct((m, S5_CH), F32),
                   jax.ShapeDtypeStruct((m, SLAB_W), F32)],
        compiler_params=_cparams("parallel"),
        name="norm_inproj",
    )(x, gain, w)


def _outproj_router_kernel(og_ref, os_ref, om_ref, h_ref, wo_ref, nf_ref, wr_ref, br_ref,
                           h1_ref, f_ref, gates_ref):
    h1 = h_ref[...]
    h1 = h1 + jnp.dot(og_ref[...], wo_ref[0:GDN_VW, :], preferred_element_type=F32)
    h1 = h1 + jnp.dot(os_ref[...], wo_ref[GDN_VW:GDN_VW + S5_CH, :], preferred_element_type=F32)
    h1 = h1 + jnp.dot(om_ref[...], wo_ref[GDN_VW + S5_CH:, :], preferred_element_type=F32)
    h1_ref[...] = h1
    f = _rms(h1, nf_ref[...])
    f_ref[...] = f.astype(BF16)
    logits = jnp.dot(f, wr_ref[...], precision=_HI, preferred_element_type=F32) + br_ref[...]
    lane = lax.broadcasted_iota(jnp.int32, logits.shape, 1)
    is_g = lane < N_GROUPS
    gl = jnp.where(is_g, logits, NEG)
    gmax = jnp.max(gl, axis=-1, keepdims=True)
    ge = jnp.where(is_g, jnp.exp(gl - gmax), 0.0)
    p_grp = ge / jnp.sum(ge, axis=-1, keepdims=True)
    g_prob = jnp.max(p_grp, axis=-1, keepdims=True)
    g_idx = jnp.min(jnp.where(is_g & (gl == gmax), lane, LANES), axis=-1, keepdims=True)
    e_lane = lane - N_GROUPS
    is_e = (e_lane >= 0) & (e_lane < N_EXPERTS) & (jnp.right_shift(e_lane, 2) == g_idx)
    le = jnp.where(is_e, logits, NEG)
    m1 = jnp.max(le, axis=-1, keepdims=True)
    i1 = jnp.min(jnp.where(is_e & (le == m1), lane, LANES), axis=-1, keepdims=True)
    is_e2 = is_e & (lane != i1)
    le2 = jnp.where(is_e2, logits, NEG)
    m2 = jnp.max(le2, axis=-1, keepdims=True)
    i2 = jnp.min(jnp.where(is_e2 & (le2 == m2), lane, LANES), axis=-1, keepdims=True)
    e2 = jnp.exp(m2 - m1)
    w1 = g_prob / (1.0 + e2)
    w2 = g_prob * e2 / (1.0 + e2)
    gates_ref[...] = jnp.where(lane == i1, w1, 0.0) + jnp.where(lane == i2, w2, 0.0)


def _outproj_router(og, os_, om, h, wo, nf, wr, br, tm):
    m = h.shape[0]
    row = lambda i: (i, 0)
    fix = lambda i: (0, 0)
    return pl.pallas_call(
        _outproj_router_kernel,
        grid=(m // tm,),
        in_specs=[pl.BlockSpec((tm, GDN_VW), row), pl.BlockSpec((tm, S5_CH), row),
                  pl.BlockSpec((tm, ML_W), row), pl.BlockSpec((tm, D_MODEL), row),
                  pl.BlockSpec((D_MODEL, D_MODEL), fix), pl.BlockSpec((1, D_MODEL), fix),
                  pl.BlockSpec((D_MODEL, LANES), fix), pl.BlockSpec((1, LANES), fix)],
        out_specs=[pl.BlockSpec((tm, D_MODEL), row), pl.BlockSpec((tm, D_MODEL), row),
                   pl.BlockSpec((tm, LANES), row)],
        out_shape=[jax.ShapeDtypeStruct((m, D_MODEL), F32),
                   jax.ShapeDtypeStruct((m, D_MODEL), BF16),
                   jax.ShapeDtypeStruct((m, LANES), F32)],
        compiler_params=_cparams("parallel"),
        name="outproj_router",
    )(og, os_, om, h, wo, nf, wr, br)


def _moe_kernel(f_ref, gates_ref, h1_ref, wg_ref, wu_ref, wd_ref, out_ref):
    gi = pl.program_id(1)

    @pl.when(gi == 0)
    def _():
        out_ref[...] = h1_ref[...]

    x = f_ref[...]
    hg = jnp.dot(x, wg_ref[0], preferred_element_type=F32)
    hu = jnp.dot(x, wu_ref[0], preferred_element_type=F32)
    gates = gates_ref[...]
    lane = lax.broadcasted_iota(jnp.int32, gates.shape, 1)
    base = N_GROUPS + EXPERTS_PER_GROUP * gi
    parts = []
    for j in range(EXPERTS_PER_GROUP):
        gcol = jnp.sum(jnp.where(lane == base + j, gates, 0.0), axis=-1, keepdims=True)
        sl = slice(j * D_EXPERT, (j + 1) * D_EXPERT)
        parts.append((_silu(hg[:, sl]) * hu[:, sl] * gcol).astype(BF16))
    hidden = jnp.concatenate(parts, axis=-1)
    out_ref[...] += jnp.dot(hidden, wd_ref[0], preferred_element_type=F32)


def _moe(f, gates, h1, wg, wu, wd, tm):
    m = f.shape[0]
    row = lambda i, g: (i, 0)
    wsel = lambda i, g: (g, 0, 0)
    gw = EXPERTS_PER_GROUP * D_EXPERT
    return pl.pallas_call(
        _moe_kernel,
        grid=(m // tm, N_GROUPS),
        in_specs=[pl.BlockSpec((tm, D_MODEL), row), pl.BlockSpec((tm, LANES), row),
                  pl.BlockSpec((tm, D_MODEL), row),
                  pl.BlockSpec((1, D_MODEL, gw), wsel), pl.BlockSpec((1, D_MODEL, gw), wsel),
                  pl.BlockSpec((1, gw, D_MODEL), wsel)],
        out_specs=pl.BlockSpec((tm, D_MODEL), row),
        out_shape=jax.ShapeDtypeStruct((m, D_MODEL), F32),
        compiler_params=_cparams("parallel", "arbitrary"),
        name="moe",
    )(f, gates, h1, wg, wu, wd)


def _ple_kernel(h_ref, p_ref, np_ref, wg_ref, wp_ref, fn_ref, out_ref, *, final):
    h = h_ref[...]
    gate = _sigmoid(jnp.dot(_rms(h, np_ref[...]).astype(BF16), wg_ref[...],
                            preferred_element_type=F32))
    proj = jnp.dot(p_ref[...].astype(BF16), wp_ref[...], preferred_element_type=F32)
    h = h + proj * gate
    if final:
        h = _rms(h, fn_ref[...])
    out_ref[...] = h


def _ple(h, p, npl, wg, wp, fn, tm, final):
    m = h.shape[0]
    row = lambda i: (i, 0)
    fix = lambda i: (0, 0)
    return pl.pallas_call(
        functools.partial(_ple_kernel, final=final),
        grid=(m // tm,),
        in_specs=[pl.BlockSpec((tm, D_MODEL), row), pl.BlockSpec((tm, PLE_DIM), row),
                  pl.BlockSpec((1, D_MODEL), fix), pl.BlockSpec((D_MODEL, D_MODEL), fix),
                  pl.BlockSpec((PLE_DIM, D_MODEL), fix), pl.BlockSpec((1, D_MODEL), fix)],
        out_specs=pl.BlockSpec((tm, D_MODEL), row),
        out_shape=jax.ShapeDtypeStruct((m, D_MODEL), F32),
        compiler_params=_cparams("parallel"),
        name="ple",
    )(h, p, npl, wg, wp, fn)


def _gdn_kernel(a_ref, gt_ref, conv0_ref, s0_ref, cw_ref, alog_ref, dtb_ref, alogc_ref, dtbc_ref,
                ng_ref, o_ref, convn_ref, sn_ref, ext_ref, s_ref, *, L, tv):
    c = pl.program_id(1)
    nc = pl.num_programs(1)

    @pl.when(c == 0)
    def _():
        ext_ref[0:SUBLANES, :] = conv0_ref[0]
        s_ref[...] = s0_ref[0]

    raw = a_ref[0, :, 0:GDN_CONV_CH]
    ext_ref[SUBLANES:SUBLANES + L, :] = raw
    ext = ext_ref[...]
    cw = cw_ref[...]
    acc = raw * cw[CONV_W - 1:CONV_W, :]
    for j in range(CONV_W - 1):
        sh = pltpu.roll(ext, CONV_W - 1 - j, axis=0)[SUBLANES:SUBLANES + L]
        acc = acc + sh * cw[j:j + 1, :]
    qkv = _silu(acc)
    head = pltpu.roll(ext, (L + SUBLANES - tv) % (L + SUBLANES), axis=0)[0:SUBLANES]
    ext_ref[0:SUBLANES, :] = head

    ba = a_ref[0, :, GATE_OFF:GATE_OFF + LANES]
    baT = gt_ref[0, 0]
    beta = _sigmoid(ba)
    g = -jnp.exp(alog_ref[...]) * _softplus(ba + dtb_ref[...])
    gT = -jnp.exp(alogc_ref[...]) * _softplus(baT + dtbc_ref[...])
    if tv < L:
        beta = jnp.where(lax.broadcasted_iota(jnp.int32, ba.shape, 0) < tv, beta, 0.0)
        g = jnp.where(lax.broadcasted_iota(jnp.int32, ba.shape, 0) < tv, g, 0.0)
        gT = jnp.where(lax.broadcasted_iota(jnp.int32, baT.shape, 1) < tv, gT, 0.0)
    r = lax.broadcasted_iota(jnp.int32, (L, L), 0)
    cc = lax.broadcasted_iota(jnp.int32, (L, L), 1)
    incl = r >= cc
    strict = r > cc
    tri = jnp.where(incl, 1.0, 0.0).astype(F32)
    triu = jnp.where(r <= cc, 1.0, 0.0).astype(F32)
    gcum = jnp.dot(tri, g, precision=_HI, preferred_element_type=F32)
    gcumT = jnp.dot(gT, triu, precision=_HI, preferred_element_type=F32)

    gain = ng_ref[...]
    outs = []
    for h in range(GDN_HEADS):
        q = qkv[:, h * GDN_DK:(h + 1) * GDN_DK]
        k = qkv[:, GDN_QK + h * GDN_DK:GDN_QK + (h + 1) * GDN_DK]
        v = qkv[:, 2 * GDN_QK + h * GDN_DV:2 * GDN_QK + (h + 1) * GDN_DV]
        z = a_ref[0, :, GDN_CONV_CH + h * GDN_DV:GDN_CONV_CH + (h + 1) * GDN_DV]
        q = q * lax.rsqrt(jnp.sum(q * q, axis=-1, keepdims=True) + EPS) * (GDN_DK ** -0.5)
        k = k * lax.rsqrt(jnp.sum(k * k, axis=-1, keepdims=True) + EPS)
        b_c = beta[:, h:h + 1]
        gc = gcum[:, GDN_HEADS + h:GDN_HEADS + h + 1]
        gr = gcumT[GDN_HEADS + h:GDN_HEADS + h + 1, :]
        gamma = jnp.where(incl, jnp.exp(jnp.minimum(gc - gr, 0.0)), 0.0)
        eg = jnp.exp(gc)
        kb = k * b_c
        n_mat = jnp.where(strict, _dot_nt(kb, k) * gamma, 0.0)
        t_inv = _unit_lower_inverse(n_mat, L)
        rhs = jnp.concatenate([v * b_c, kb * eg], axis=-1)
        sol = _dot3(t_inv, rhs)
        u = sol[:, 0:GDN_DV]
        w = sol[:, GDN_DV:]
        s = s_ref[h]
        v_new = u - _dot(w, s)
        qk = _dot_nt(q, k) * gamma
        o = _dot(q * eg, s) + _dot(qk, v_new)
        g_last = gc[L - 1:L, :]
        s_ref[h] = s * jnp.exp(g_last) + _dot_tn(k * jnp.exp(g_last - gc), v_new)
        outs.append(_rms(o, gain) * _silu(z))
    o_ref[0] = jnp.concatenate(outs, axis=-1).astype(o_ref.dtype)

    @pl.when(c == nc - 1)
    def _():
        convn_ref[0] = head
        sn_ref[0] = s_ref[...]


def _gdn(slab, gt, conv0, s0, cw, alog, dtb, alogc, dtbc, ng, L, tv):
    bsz, t, _ = slab.shape
    nc = t // L
    fix2 = lambda b, c: (0, 0)
    return pl.pallas_call(
        functools.partial(_gdn_kernel, L=L, tv=tv),
        grid=(bsz, nc),
        in_specs=[pl.BlockSpec((1, L, SLAB_W), lambda b, c: (b, c, 0)),
                  pl.BlockSpec((1, 1, LANES, L), lambda b, c: (b, c, 0, 0)),
                  pl.BlockSpec((1, SUBLANES, GDN_CONV_CH), lambda b, c: (b, 0, 0)),
                  pl.BlockSpec((1, GDN_HEADS, GDN_DK, GDN_DV), lambda b, c: (b, 0, 0, 0)),
                  pl.BlockSpec((CONV_W, GDN_CONV_CH), fix2),
                  pl.BlockSpec((1, LANES), fix2), pl.BlockSpec((1, LANES), fix2),
                  pl.BlockSpec((LANES, 1), fix2), pl.BlockSpec((LANES, 1), fix2),
                  pl.BlockSpec((1, GDN_DV), fix2)],
        out_specs=[pl.BlockSpec((1, L, GDN_VW), lambda b, c: (b, c, 0)),
                   pl.BlockSpec((1, SUBLANES, GDN_CONV_CH), lambda b, c: (b, 0, 0)),
                   pl.BlockSpec((1, GDN_HEADS, GDN_DK, GDN_DV), lambda b, c: (b, 0, 0, 0))],
        out_shape=[jax.ShapeDtypeStruct((bsz, t, GDN_VW), BF16),
                   jax.ShapeDtypeStruct((bsz, SUBLANES, GDN_CONV_CH), F32),
                   jax.ShapeDtypeStruct((bsz, GDN_HEADS, GDN_DK, GDN_DV), F32)],
        scratch_shapes=[pltpu.VMEM((L + SUBLANES, GDN_CONV_CH), F32),
                        pltpu.VMEM((GDN_HEADS, GDN_DK, GDN_DV), F32)],
        compiler_params=_cparams("parallel", "arbitrary"),
        name="gdn",
    )(slab, gt, conv0, s0, cw, alog, dtb, alogc, dtbc, ng)


def _mlstm_kernel(a_ref, gt_ref, c0_ref, n0_ref, m0_ref, bias_ref, biasc_ref, ng_ref,
                  o_ref, cn_ref, nn_ref, mn_ref, c_ref, n_ref, m_ref, *, L, tv):
    cidx = pl.program_id(1)
    nc = pl.num_programs(1)

    @pl.when(cidx == 0)
    def _():
        c_ref[...] = c0_ref[0]
        n_ref[...] = n0_ref[0]
        m_ref[...] = m0_ref[0]

    gi = a_ref[0, :, GATE_OFF:GATE_OFF + LANES] + bias_ref[...]
    giT = gt_ref[0, 0] + biasc_ref[...]
    lf = -_softplus(-gi)
    lfT = -_softplus(-giT)
    li = gi
    liT = giT
    if tv < L:
        vc = lax.broadcasted_iota(jnp.int32, gi.shape, 0) < tv
        vr = lax.broadcasted_iota(jnp.int32, giT.shape, 1) < tv
        lf = jnp.where(vc, lf, 0.0)
        lfT = jnp.where(vr, lfT, 0.0)
        li = jnp.where(vc, li, NEG)
        liT = jnp.where(vr, liT, NEG)
    r = lax.broadcasted_iota(jnp.int32, (L, L), 0)
    cc = lax.broadcasted_iota(jnp.int32, (L, L), 1)
    incl = r >= cc
    tri = jnp.where(incl, 1.0, 0.0).astype(F32)
    triu = jnp.where(r <= cc, 1.0, 0.0).astype(F32)
    bcum = jnp.dot(tri, lf, precision=_HI, preferred_element_type=F32)
    bcumT = jnp.dot(lfT, triu, precision=_HI, preferred_element_type=F32)

    gain = ng_ref[...]
    m_all = m_ref[...]
    lane = lax.broadcasted_iota(jnp.int32, m_all.shape, 1)
    m_next = m_all
    outs = []
    for h in range(ML_HEADS):
        q = a_ref[0, :, h * ML_DH:(h + 1) * ML_DH]
        k = a_ref[0, :, ML_W + h * ML_DH:ML_W + (h + 1) * ML_DH] * (ML_DH ** -0.5)
        v = a_ref[0, :, 2 * ML_W + h * ML_DH:2 * ML_W + (h + 1) * ML_DH]
        og = a_ref[0, :, 3 * ML_W + h * ML_DH:3 * ML_W + (h + 1) * ML_DH]
        bc = bcum[:, ML_HEADS + h:ML_HEADS + h + 1]
        br = bcumT[ML_HEADS + h:ML_HEADS + h + 1, :]
        lic = li[:, h:h + 1]
        lir = liT[h:h + 1, :]
        m = m_all[:, h:h + 1]
        d_mat = jnp.where(incl, bc - br + lir, NEG)
        inter = bc + m
        m_t = jnp.maximum(inter, jnp.max(d_mat, axis=-1, keepdims=True))
        w_intra = jnp.exp(d_mat - m_t)
        e_inter = jnp.exp(inter - m_t)
        s = _dot_nt(q, k) * w_intra
        cmat = c_ref[h]
        nrow = n_ref[h:h + 1, :]
        num = e_inter * _dot(q, cmat) + _dot(s, v)
        den = e_inter * jnp.sum(q * nrow, axis=-1, keepdims=True) + jnp.sum(s, axis=-1, keepdims=True)
        hh = num / jnp.maximum(jnp.abs(den), jnp.exp(-m_t))
        m_new = m_t[L - 1:L, :]
        b_last = bc[L - 1:L, :]
        e_c = jnp.exp(b_last + m - m_new)
        w_k = jnp.exp(b_last - bc + lic - m_new)
        kw = k * w_k
        c_ref[h] = e_c * cmat + _dot_tn(kw, v)
        n_ref[h:h + 1, :] = e_c * nrow + jnp.sum(kw, axis=0, keepdims=True)
        m_next = jnp.where(lane == h, m_new, m_next)
        outs.append(_rms(hh, gain) * _sigmoid(og))
    m_ref[...] = m_next
    o_ref[0] = jnp.concatenate(outs, axis=-1).astype(o_ref.dtype)

    @pl.when(cidx == nc - 1)
    def _():
        cn_ref[0] = c_ref[...]
        nn_ref[0] = n_ref[...]
        mn_ref[0] = m_ref[...]


def _mlstm(slab, gt, c0, n0, m0, bias, biasc, ng, L, tv):
    bsz, t, _ = slab.shape
    nc = t // L
    fix2 = lambda b, c: (0, 0)
    return pl.pallas_call(
        functools.partial(_mlstm_kernel, L=L, tv=tv),
        grid=(bsz, nc),
        in_specs=[pl.BlockSpec((1, L, SLAB_W), lambda b, c: (b, c, 0)),
                  pl.BlockSpec((1, 1, LANES, L), lambda b, c: (b, c, 0, 0)),
                  pl.BlockSpec((1, ML_HEADS, ML_DH, ML_DH), lambda b, c: (b, 0, 0, 0)),
                  pl.BlockSpec((1, SUBLANES, ML_DH), lambda b, c: (b, 0, 0)),
                  pl.BlockSpec((1, 1, LANES), lambda b, c: (b, 0, 0)),
                  pl.BlockSpec((1, LANES), fix2), pl.BlockSpec((LANES, 1), fix2),
                  pl.BlockSpec((1, ML_DH), fix2)],
        out_specs=[pl.BlockSpec((1, L, ML_W), lambda b, c: (b, c, 0)),
                   pl.BlockSpec((1, ML_HEADS, ML_DH, ML_DH), lambda b, c: (b, 0, 0, 0)),
                   pl.BlockSpec((1, SUBLANES, ML_DH), lambda b, c: (b, 0, 0)),
                   pl.BlockSpec((1, 1, LANES), lambda b, c: (b, 0, 0))],
        out_shape=[jax.ShapeDtypeStruct((bsz, t, ML_W), BF16),
                   jax.ShapeDtypeStruct((bsz, ML_HEADS, ML_DH, ML_DH), F32),
                   jax.ShapeDtypeStruct((bsz, SUBLANES, ML_DH), F32),
                   jax.ShapeDtypeStruct((bsz, 1, LANES), F32)],
        scratch_shapes=[pltpu.VMEM((ML_HEADS, ML_DH, ML_DH), F32),
                        pltpu.VMEM((SUBLANES, ML_DH), F32),
                        pltpu.VMEM((1, LANES), F32)],
        compiler_params=_cparams("parallel", "arbitrary"),
        name="mlstm",
    )(slab, gt, c0, n0, m0, bias, biasc, ng)


def _s5_kernel(u_ref, h0_ref, lb_ref, bw_ref, cw_ref, dsk_ref, wglu_ref,
               o_ref, hn_ref, h_ref, *, L):
    c = pl.program_id(1)
    nc = pl.num_programs(1)

    @pl.when(c == 0)
    def _():
        h_ref[...] = h0_ref[0]

    u = u_ref[0]
    bu = jnp.dot(u.astype(BF16), bw_ref[...], preferred_element_type=F32)
    x_re = bu[:, 0:S5_N]
    x_im = bu[:, S5_N:]
    pr = lb_ref[0:1, :]
    pi = lb_ref[1:2, :]
    hin_re = h_ref[0:1, :]
    hin_im = h_ref[1:2, :]
    row = lax.broadcasted_iota(jnp.int32, x_re.shape, 0)
    first = row == 0
    x_re = x_re + jnp.where(first, pr * hin_re - pi * hin_im, 0.0)
    x_im = x_im + jnp.where(first, pr * hin_im + pi * hin_re, 0.0)
    d = 1
    while d < L:
        keep = row >= d
        s_re = jnp.where(keep, pltpu.roll(x_re, d, axis=0), 0.0)
        s_im = jnp.where(keep, pltpu.roll(x_im, d, axis=0), 0.0)
        x_re, x_im = x_re + (pr * s_re - pi * s_im), x_im + (pr * s_im + pi * s_re)
        pr, pi = pr * pr - pi * pi, 2.0 * pr * pi
        d *= 2
    tail_re = x_re[L - SUBLANES:L, :]
    tail_im = x_im[L - SUBLANES:L, :]
    h_ref[0:1, :] = tail_re[SUBLANES - 1:SUBLANES, :]
    h_ref[1:2, :] = tail_im[SUBLANES - 1:SUBLANES, :]
    hcat = jnp.concatenate([x_re, x_im], axis=-1).astype(BF16)
    y = jnp.dot(hcat, cw_ref[...], preferred_element_type=F32) + dsk_ref[...] * u
    zg = jax.nn.gelu(y)
    out = zg * _sigmoid(jnp.dot(zg.astype(BF16), wglu_ref[...], preferred_element_type=F32))
    o_ref[0] = out.astype(o_ref.dtype)

    @pl.when(c == nc - 1)
    def _():
        hn_ref[0, 0] = tail_re
        hn_ref[0, 1] = tail_im


def _s5(u, h0, lb, bw, cw, dsk, wglu, L):
    bsz, t, _ = u.shape
    nc = t // L
    fix2 = lambda b, c: (0, 0)
    return pl.pallas_call(
        functools.partial(_s5_kernel, L=L),
        grid=(bsz, nc),
        in_specs=[pl.BlockSpec((1, L, S5_CH), lambda b, c: (b, c, 0)),
                  pl.BlockSpec((1, SUBLANES, S5_N), lambda b, c: (b, 0, 0)),
                  pl.BlockSpec((SUBLANES, S5_N), fix2),
                  pl.BlockSpec((S5_CH, 2 * S5_N), fix2),
                  pl.BlockSpec((2 * S5_N, S5_CH), fix2),
                  pl.BlockSpec((1, S5_CH), fix2),
                  pl.BlockSpec((S5_CH, S5_CH), fix2)],
        out_specs=[pl.BlockSpec((1, L, S5_CH), lambda b, c: (b, c, 0)),
                   pl.BlockSpec((1, 2, SUBLANES, S5_N), lambda b, c: (b, 0, 0, 0))],
        out_shape=[jax.ShapeDtypeStruct((bsz, t, S5_CH), BF16),
                   jax.ShapeDtypeStruct((bsz, 2, SUBLANES, S5_N), F32)],
        scratch_shapes=[pltpu.VMEM((SUBLANES, S5_N), F32)],
        compiler_params=_cparams("parallel", "arbitrary"),
        name="s5",
    )(u, h0, lb, bw, cw, dsk, wglu)


def _pad_lanes(v, offset=0):
    return jnp.zeros((1, LANES), F32).at[0, offset:offset + v.shape[0]].set(v.astype(F32))


def _prep_layer(l, P):
    w_in = P["w_in"][l]
    sizes = (GDN_CONV_CH, GDN_VW, GDN_HEADS, GDN_HEADS, S5_CH, 3 * ML_W, ML_HEADS, ML_HEADS, ML_W)
    offs = [0]
    for s in sizes:
        offs.append(offs[-1] + s)
    g_qkv, g_z, g_b, g_a, s_u, m_qkv, m_i, m_f, m_o = [w_in[:, offs[i]:offs[i + 1]] for i in range(9)]
    zpad = jnp.zeros((D_MODEL, LANES - 2 * GDN_HEADS), F32)
    w_cat = jnp.concatenate([g_qkv, g_z, g_b, g_a, zpad, s_u, m_qkv, m_o, m_i, m_f, zpad],
                            axis=1).astype(BF16)

    lr = P["s5_lam_re"][l].astype(F32)
    li = P["s5_lam_im"][l].astype(F32)
    dt = jnp.exp(P["s5_log_dt"][l].astype(F32))[:, None]
    mag = jnp.exp(lr * dt)
    lb_re = mag * jnp.cos(li * dt)
    lb_im = mag * jnp.sin(li * dt)
    den = lr * lr + li * li
    c_re = ((lb_re - 1.0) * lr + lb_im * li) / den
    c_im = (lb_im * lr - (lb_re - 1.0) * li) / den
    b_r = P["s5_B_re"][l].astype(F32)
    b_i = P["s5_B_im"][l].astype(F32)
    bb_re = c_re[..., None] * b_r - c_im[..., None] * b_i
    bb_im = c_re[..., None] * b_i + c_im[..., None] * b_r
    eye_g = jnp.eye(S5_GROUPS, dtype=F32)
    bd = lambda m: jnp.einsum("gph,gk->ghkp", m, eye_g).reshape(S5_CH, S5_N)
    bw = jnp.concatenate([bd(bb_re), bd(bb_im)], axis=1).astype(BF16)
    cd = lambda m: jnp.einsum("ghp,gk->gpkh", m, eye_g).reshape(S5_N, S5_CH)
    cw = jnp.concatenate([cd(P["s5_C_re"][l].astype(F32)),
                          -cd(P["s5_C_im"][l].astype(F32))], axis=0).astype(BF16)
    lb = jnp.zeros((SUBLANES, S5_N), F32).at[0].set(lb_re.reshape(-1)).at[1].set(lb_im.reshape(-1))

    wr = jnp.zeros((D_MODEL, LANES), F32)
    wr = wr.at[:, 0:N_GROUPS].set(P["w_router_group"][l])
    wr = wr.at[:, N_GROUPS:N_GROUPS + N_EXPERTS].set(P["w_router_expert"][l])
    br = jnp.zeros((1, LANES), F32)
    br = br.at[0, 0:N_GROUPS].set(P["b_router_group"][l])
    br = br.at[0, N_GROUPS:N_GROUPS + N_EXPERTS].set(P["b_router_expert"][l])

    gw = EXPERTS_PER_GROUP * D_EXPERT
    regroup = lambda w: (w.reshape(N_GROUPS, EXPERTS_PER_GROUP, D_MODEL, D_EXPERT)
                         .transpose(0, 2, 1, 3).reshape(N_GROUPS, D_MODEL, gw).astype(BF16))
    ml_bias = (_pad_lanes(P["ml_ig_bias"][l]) + _pad_lanes(P["ml_fg_bias"][l], ML_HEADS))
    alog = _pad_lanes(P["gdn_A_log"][l], GDN_HEADS)
    dtb = _pad_lanes(P["gdn_dt_bias"][l], GDN_HEADS)
    return dict(
        norm_mix=P["norm_mix"][l].reshape(1, D_MODEL).astype(F32),
        w_cat=w_cat,
        w_out=P["w_out"][l].astype(BF16),
        conv_w=P["gdn_conv_w"][l].astype(F32),
        alog=alog, dtb=dtb, alogc=alog.reshape(LANES, 1), dtbc=dtb.reshape(LANES, 1),
        gdn_norm=P["gdn_norm"][l].reshape(1, GDN_DV).astype(F32),
        lb=lb, bw=bw, cw=cw,
        s5_d=P["s5_D"][l].reshape(1, S5_CH).astype(F32),
        w_glu=P["s5_w_glu"][l].astype(BF16),
        ml_bias=ml_bias, ml_biasc=ml_bias.reshape(LANES, 1),
        ml_norm=P["ml_norm"][l].reshape(1, ML_DH).astype(F32),
        norm_ffn=P["norm_ffn"][l].reshape(1, D_MODEL).astype(F32),
        wr=wr, br=br,
        wg=regroup(P["w_exp_gate"][l]), wu=regroup(P["w_exp_up"][l]),
        wd=P["w_exp_down"][l].reshape(N_GROUPS, gw, D_MODEL).astype(BF16),
        norm_ple=P["norm_ple"][l].reshape(1, D_MODEL).astype(F32),
        w_ple_gate=P["w_ple_gate"][l].astype(BF16),
        w_ple_proj=P["w_ple_proj"][l].astype(BF16),
    )


def _gate_rows(slab, bsz, t, L):
    g = slab[:, GATE_OFF:].reshape(bsz, t // L, L, LANES)
    return g.transpose(0, 1, 3, 2)


def _trunk(x, p, states, layers, final_norm, *, L, tv, Ls, tm, tm_moe):
    conv0, gdn0, s5re0, s5im0, mc0, mn0, mm0 = states
    bsz, t, _ = x.shape
    m = bsz * t
    h = x.reshape(m, D_MODEL)
    outs = [[] for _ in range(7)]
    last_row = (tv - 1) % SUBLANES
    for l, W in enumerate(layers):
        slab_g, s_u, slab_m = _norm_inproj(h, W["norm_mix"], W["w_cat"], tm)
        conv_in = jnp.pad(conv0[l], ((0, 0), (SUBLANES - (CONV_W - 1), 0), (0, 0)))
        o_gdn, conv_n, gdn_n = _gdn(slab_g.reshape(bsz, t, SLAB_W), _gate_rows(slab_g, bsz, t, L),
                                    conv_in, gdn0[l], W["conv_w"], W["alog"], W["dtb"],
                                    W["alogc"], W["dtbc"], W["gdn_norm"], L, tv)
        h0 = jnp.zeros((bsz, SUBLANES, S5_N), F32)
        h0 = h0.at[:, 0].set(s5re0[l].reshape(bsz, S5_N)).at[:, 1].set(s5im0[l].reshape(bsz, S5_N))
        o_s5, s5_n = _s5(s_u.reshape(bsz, t, S5_CH), h0, W["lb"], W["bw"], W["cw"], W["s5_d"],
                         W["w_glu"], Ls)
        n_in = jnp.pad(mn0[l], ((0, 0), (0, SUBLANES - ML_HEADS), (0, 0)))
        m_in = jnp.pad(mm0[l], ((0, 0), (0, LANES - ML_HEADS))).reshape(bsz, 1, LANES)
        o_ml, c_n, n_n, m_n = _mlstm(slab_m.reshape(bsz, t, SLAB_W), _gate_rows(slab_m, bsz, t, L),
                                     mc0[l], n_in, m_in, W["ml_bias"], W["ml_biasc"],
                                     W["ml_norm"], L, tv)
        h1, f, gates = _outproj_router(o_gdn.reshape(m, GDN_VW), o_s5.reshape(m, S5_CH),
                                       o_ml.reshape(m, ML_W), h, W["w_out"], W["norm_ffn"],
                                       W["wr"], W["br"], tm)
        h2 = _moe(f, gates, h1, W["wg"], W["wu"], W["wd"], tm_moe)
        h = _ple(h2, p[l].reshape(m, PLE_DIM), W["norm_ple"], W["w_ple_gate"], W["w_ple_proj"],
                 final_norm, tm, l == len(layers) - 1)
        outs[0].append(conv_n[:, SUBLANES - (CONV_W - 1):])
        outs[1].append(gdn_n)
        outs[2].append(s5_n[:, 0, last_row].reshape(bsz, S5_GROUPS, S5_STATE))
        outs[3].append(s5_n[:, 1, last_row].reshape(bsz, S5_GROUPS, S5_STATE))
        outs[4].append(c_n)
        outs[5].append(n_n[:, :ML_HEADS])
        outs[6].append(m_n[:, 0, :ML_HEADS])
    return (h.reshape(bsz, t, D_MODEL),) + tuple(jnp.stack(o) for o in outs)


def kernel(x_prompt, x_sample, p_prompt, p_sample, state_gdn_conv, state_gdn, state_s5_re, state_s5_im, state_mlstm_C, state_mlstm_n, state_mlstm_m, norm_mix, w_in, w_out, gdn_conv_w, gdn_A_log, gdn_dt_bias, gdn_norm, s5_lam_re, s5_lam_im, s5_log_dt, s5_B_re, s5_B_im, s5_C_re, s5_C_im, s5_D, s5_w_glu, ml_ig_bias, ml_fg_bias, ml_norm, norm_ffn, w_router_group, b_router_group, w_router_expert, b_router_expert, w_exp_gate, w_exp_up, w_exp_down, norm_ple, w_ple_gate, w_ple_proj, final_norm):
    P = dict(norm_mix=norm_mix, w_in=w_in, w_out=w_out, gdn_conv_w=gdn_conv_w, gdn_A_log=gdn_A_log,
             gdn_dt_bias=gdn_dt_bias, gdn_norm=gdn_norm, s5_lam_re=s5_lam_re, s5_lam_im=s5_lam_im,
             s5_log_dt=s5_log_dt, s5_B_re=s5_B_re, s5_B_im=s5_B_im, s5_C_re=s5_C_re, s5_C_im=s5_C_im,
             s5_D=s5_D, s5_w_glu=s5_w_glu, ml_ig_bias=ml_ig_bias, ml_fg_bias=ml_fg_bias,
             ml_norm=ml_norm, norm_ffn=norm_ffn, w_router_group=w_router_group,
             b_router_group=b_router_group, w_router_expert=w_router_expert,
             b_router_expert=b_router_expert, w_exp_gate=w_exp_gate, w_exp_up=w_exp_up,
             w_exp_down=w_exp_down, norm_ple=norm_ple, w_ple_gate=w_ple_gate, w_ple_proj=w_ple_proj)
    depth = norm_mix.shape[0]
    layers = [_prep_layer(l, P) for l in range(depth)]
    fnorm = final_norm.reshape(1, D_MODEL).astype(F32)

    bp, tp, _ = x_prompt.shape
    zeros = lambda *s: jnp.zeros((depth, bp) + s, F32)
    prompt_init = (zeros(CONV_W - 1, GDN_CONV_CH), zeros(GDN_HEADS, GDN_DK, GDN_DV),
                   zeros(S5_GROUPS, S5_STATE), zeros(S5_GROUPS, S5_STATE),
                   zeros(ML_HEADS, ML_DH, ML_DH), zeros(ML_HEADS, ML_DH), zeros(ML_HEADS))
    lp = math.gcd(tp, 64)
    lsp = math.gcd(tp, 256)
    res_p = _trunk(x_prompt, p_prompt, prompt_init, layers, fnorm,
                   L=lp, tv=lp, Ls=lsp, tm=512, tm_moe=1024)

    bs, ts, _ = x_sample.shape
    tpad = -(-ts // SUBLANES) * SUBLANES
    xs = jnp.pad(x_sample, ((0, 0), (0, tpad - ts), (0, 0)))
    ps = jnp.pad(p_sample, ((0, 0), (0, 0), (0, tpad - ts), (0, 0)))
    sample_init = (state_gdn_conv, state_gdn, state_s5_re, state_s5_im,
                   state_mlstm_C, state_mlstm_n, state_mlstm_m)
    res_s = _trunk(xs, ps, sample_init, layers, fnorm,
                   L=tpad, tv=ts, Ls=tpad, tm=512, tm_moe=1024)
    y_sample = res_s[0][:, :ts]
    return (res_p[0], y_sample) + res_p[1:] + res_s[1:]
```

```python
import functools
import math

import jax
import jax.numpy as jnp
from jax import lax
from jax.experimental import pallas as pl
from jax.experimental.pallas import tpu as pltpu

F32 = jnp.float32
BF16 = jnp.bfloat16

D_MODEL = 1024
DEPTH = 2
GDN_HEADS = 6
GDN_DK = 64
GDN_DV = 64
GDN_QK = GDN_HEADS * GDN_DK
GDN_VW = GDN_HEADS * GDN_DV
GDN_CONV_CH = 2 * GDN_QK + GDN_VW
CONV_W = 4
S5_GROUPS = 16
S5_GROUP_CH = 16
S5_CH = S5_GROUPS * S5_GROUP_CH
S5_STATE = 64
S5_N = S5_GROUPS * S5_STATE
ML_HEADS = 6
ML_DH = 64
ML_W = ML_HEADS * ML_DH
N_GROUPS = 4
EXPERTS_PER_GROUP = 4
N_EXPERTS = N_GROUPS * EXPERTS_PER_GROUP
D_EXPERT = 256
PLE_DIM = 256
EPS = 1e-6

LANES = 128
SUBLANES = 8
SLAB_W = GDN_CONV_CH + GDN_VW + LANES
GATE_OFF = GDN_CONV_CH + GDN_VW
NEG = -1e30
VMEM_LIMIT = 56 * 1024 * 1024

_HI = lax.Precision.HIGHEST


def _cparams(*sem):
    return pltpu.CompilerParams(dimension_semantics=sem, vmem_limit_bytes=VMEM_LIMIT)


def _dot(a, b):
    return jnp.dot(a.astype(BF16), b.astype(BF16), preferred_element_type=F32)


def _dot_nt(a, b):
    return lax.dot_general(a.astype(BF16), b.astype(BF16), (((1,), (1,)), ((), ())),
                           preferred_element_type=F32)


def _dot_tn(a, b):
    return lax.dot_general(a.astype(BF16), b.astype(BF16), (((0,), (0,)), ((), ())),
                           preferred_element_type=F32)


def _split_bf16(a):
    hi = a.astype(BF16)
    lo = (a - hi.astype(F32)).astype(BF16)
    return hi, lo


def _dot3(a, b):
    ah, al = _split_bf16(a)
    bh, bl = _split_bf16(b)
    d = functools.partial(jnp.dot, preferred_element_type=F32)
    return d(ah, bh) + (d(ah, bl) + d(al, bh))


def _rms(x, gain):
    return x * lax.rsqrt(jnp.mean(x * x, axis=-1, keepdims=True) + EPS) * gain


def _softplus(x):
    return jnp.maximum(x, 0.0) + jnp.log(1.0 + jnp.exp(-jnp.abs(x)))


def _sigmoid(x):
    return 1.0 / (1.0 + jnp.exp(-x))


def _silu(x):
    return x * _sigmoid(x)


def _interleave(gens):
    live = list(gens)
    while live:
        still = []
        for g in live:
            try:
                next(g)
                still.append(g)
            except StopIteration:
                pass
        live = still


def _unit_lower_inverse(n_mat, size):
    r = lax.broadcasted_iota(jnp.int32, (size, size), 0)
    c = lax.broadcasted_iota(jnp.int32, (size, size), 1)
    base = min(16, size)
    same = jnp.bitwise_xor(r, c) < base
    nd = jnp.where(same, n_mat, 0.0)
    eye = jnp.where(r == c, 1.0, 0.0).astype(F32)
    t = eye - nd
    x = nd
    p = 1
    while 2 * p < base:
        x = _dot3(x, x)
        yield
        t = t + _dot3(t, x)
        yield
        p *= 2
    blk = base
    while blk < size:
        pair = jnp.bitwise_xor(r, c)
        off = jnp.where((pair < 2 * blk) & (pair >= blk), n_mat, 0.0)
        ot = _dot3(off, t)
        yield
        t = t - _dot3(t, ot)
        yield
        blk *= 2
    return t


def _norm_inproj_kernel(x_ref, g_ref, w_ref, oa_ref, ob_ref, oc_ref):
    u = _rms(x_ref[...], g_ref[...]).astype(BF16)
    oa_ref[...] = jnp.dot(u, w_ref[:, 0:SLAB_W], preferred_element_type=F32)
    ob_ref[...] = jnp.dot(u, w_ref[:, SLAB_W:SLAB_W + S5_CH], preferred_element_type=F32)
    oc_ref[...] = jnp.dot(u, w_ref[:, SLAB_W + S5_CH:], preferred_element_type=F32)


def _norm_inproj(x, gain, w, tm):
    m = x.shape[0]
    n = w.shape[1]
    return pl.pallas_call(
        _norm_inproj_kernel,
        grid=(m // tm,),
        in_specs=[pl.BlockSpec((tm, D_MODEL), lambda i: (i, 0)),
                  pl.BlockSpec((1, D_MODEL), lambda i: (0, 0)),
                  pl.BlockSpec((D_MODEL, n), lambda i: (0, 0))],
        out_specs=[pl.BlockSpec((tm, SLAB_W), lambda i: (i, 0)),
                   pl.BlockSpec((tm, S5_CH), lambda i: (i, 0)),
                   pl.BlockSpec((tm, SLAB_W), lambda i: (i, 0))],
        out_shape=[jax.ShapeDtypeStruct((m, SLAB_W), F32),
                   jax.ShapeDtypeStruct((m, S5_CH), F32),
                   jax.ShapeDtypeStruct((m, SLAB_W), F32)],
        compiler_params=_cparams("parallel"),
        name="norm_inproj",
    )(x, gain, w)


def _outproj_router_kernel(og_ref, os_ref, om_ref, h_ref, wo_ref, nf_ref, wr_ref, br_ref,
                           h1_ref, f_ref, gates_ref):
    h1 = h_ref[...]
    h1 = h1 + jnp.dot(og_ref[...], wo_ref[0:GDN_VW, :], preferred_element_type=F32)
    h1 = h1 + jnp.dot(os_ref[...], wo_ref[GDN_VW:GDN_VW + S5_CH, :], preferred_element_type=F32)
    h1 = h1 + jnp.dot(om_ref[...], wo_ref[GDN_VW + S5_CH:, :], preferred_element_type=F32)
    h1_ref[...] = h1
    f = _rms(h1, nf_ref[...])
    f_ref[...] = f.astype(BF16)
    logits = jnp.dot(f, wr_ref[...], precision=_HI, preferred_element_type=F32) + br_ref[...]
    lane = lax.broadcasted_iota(jnp.int32, logits.shape, 1)
    is_g = lane < N_GROUPS
    gl = jnp.where(is_g, logits, NEG)
    gmax = jnp.max(gl, axis=-1, keepdims=True)
    ge = jnp.where(is_g, jnp.exp(gl - gmax), 0.0)
    p_grp = ge / jnp.sum(ge, axis=-1, keepdims=True)
    g_prob = jnp.max(p_grp, axis=-1, keepdims=True)
    g_idx = jnp.min(jnp.where(is_g & (gl == gmax), lane, LANES), axis=-1, keepdims=True)
    e_lane = lane - N_GROUPS
    is_e = (e_lane >= 0) & (e_lane < N_EXPERTS) & (jnp.right_shift(e_lane, 2) == g_idx)
    le = jnp.where(is_e, logits, NEG)
    m1 = jnp.max(le, axis=-1, keepdims=True)
    i1 = jnp.min(jnp.where(is_e & (le == m1), lane, LANES), axis=-1, keepdims=True)
    is_e2 = is_e & (lane != i1)
    le2 = jnp.where(is_e2, logits, NEG)
    m2 = jnp.max(le2, axis=-1, keepdims=True)
    i2 = jnp.min(jnp.where(is_e2 & (le2 == m2), lane, LANES), axis=-1, keepdims=True)
    e2 = jnp.exp(m2 - m1)
    w1 = g_prob / (1.0 + e2)
    w2 = g_prob * e2 / (1.0 + e2)
    gates_ref[...] = jnp.where(lane == i1, w1, 0.0) + jnp.where(lane == i2, w2, 0.0)


def _outproj_router(og, os_, om, h, wo, nf, wr, br, tm):
    m = h.shape[0]
    row = lambda i: (i, 0)
    fix = lambda i: (0, 0)
    return pl.pallas_call(
        _outproj_router_kernel,
        grid=(m // tm,),
        in_specs=[pl.BlockSpec((tm, GDN_VW), row), pl.BlockSpec((tm, S5_CH), row),
                  pl.BlockSpec((tm, ML_W), row), pl.BlockSpec((tm, D_MODEL), row),
                  pl.BlockSpec((D_MODEL, D_MODEL), fix), pl.BlockSpec((1, D_MODEL), fix),
                  pl.BlockSpec((D_MODEL, LANES), fix), pl.BlockSpec((1, LANES), fix)],
        out_specs=[pl.BlockSpec((tm, D_MODEL), row), pl.BlockSpec((tm, D_MODEL), row),
                   pl.BlockSpec((tm, LANES), row)],
        out_shape=[jax.ShapeDtypeStruct((m, D_MODEL), F32),
                   jax.ShapeDtypeStruct((m, D_MODEL), BF16),
                   jax.ShapeDtypeStruct((m, LANES), F32)],
        compiler_params=_cparams("parallel"),
        name="outproj_router",
    )(og, os_, om, h, wo, nf, wr, br)


def _moe_kernel(f_ref, gates_ref, h1_ref, wg_ref, wu_ref, wd_ref, out_ref):
    gi = pl.program_id(1)

    @pl.when(gi == 0)
    def _():
        out_ref[...] = h1_ref[...]

    x = f_ref[...]
    hg = jnp.dot(x, wg_ref[0], preferred_element_type=F32)
    hu = jnp.dot(x, wu_ref[0], preferred_element_type=F32)
    gates = gates_ref[...]
    lane = lax.broadcasted_iota(jnp.int32, gates.shape, 1)
    base = N_GROUPS + EXPERTS_PER_GROUP * gi
    parts = []
    for j in range(EXPERTS_PER_GROUP):
        gcol = jnp.sum(jnp.where(lane == base + j, gates, 0.0), axis=-1, keepdims=True)
        sl = slice(j * D_EXPERT, (j + 1) * D_EXPERT)
        parts.append((_silu(hg[:, sl]) * hu[:, sl] * gcol).astype(BF16))
    hidden = jnp.concatenate(parts, axis=-1)
    out_ref[...] += jnp.dot(hidden, wd_ref[0], preferred_element_type=F32)


def _moe(f, gates, h1, wg, wu, wd, tm):
    m = f.shape[0]
    row = lambda i, g: (i, 0)
    wsel = lambda i, g: (g, 0, 0)
    gw = EXPERTS_PER_GROUP * D_EXPERT
    return pl.pallas_call(
        _moe_kernel,
        grid=(m // tm, N_GROUPS),
        in_specs=[pl.BlockSpec((tm, D_MODEL), row), pl.BlockSpec((tm, LANES), row),
                  pl.BlockSpec((tm, D_MODEL), row),
                  pl.BlockSpec((1, D_MODEL, gw), wsel), pl.BlockSpec((1, D_MODEL, gw), wsel),
                  pl.BlockSpec((1, gw, D_MODEL), wsel)],
        out_specs=pl.BlockSpec((tm, D_MODEL), row),
        out_shape=jax.ShapeDtypeStruct((m, D_MODEL), F32),
        compiler_params=_cparams("parallel", "arbitrary"),
        name="moe",
    )(f, gates, h1, wg, wu, wd)


def _ple_kernel(h_ref, p_ref, np_ref, wg_ref, wp_ref, fn_ref, out_ref, *, final):
    h = h_ref[...]
    gate = _sigmoid(jnp.dot(_rms(h, np_ref[...]).astype(BF16), wg_ref[...],
                            preferred_element_type=F32))
    proj = jnp.dot(p_ref[...].astype(BF16), wp_ref[...], preferred_element_type=F32)
    h = h + proj * gate
    if final:
        h = _rms(h, fn_ref[...])
    out_ref[...] = h


def _ple(h, p, npl, wg, wp, fn, tm, final):
    m = h.shape[0]
    row = lambda i: (i, 0)
    fix = lambda i: (0, 0)
    return pl.pallas_call(
        functools.partial(_ple_kernel, final=final),
        grid=(m // tm,),
        in_specs=[pl.BlockSpec((tm, D_MODEL), row), pl.BlockSpec((tm, PLE_DIM), row),
                  pl.BlockSpec((1, D_MODEL), fix), pl.BlockSpec((D_MODEL, D_MODEL), fix),
                  pl.BlockSpec((PLE_DIM, D_MODEL), fix), pl.BlockSpec((1, D_MODEL), fix)],
        out_specs=pl.BlockSpec((tm, D_MODEL), row),
        out_shape=jax.ShapeDtypeStruct((m, D_MODEL), F32),
        compiler_params=_cparams("parallel"),
        name="ple",
    )(h, p, npl, wg, wp, fn)


def _gdn_kernel(a_ref, gt_ref, conv0_ref, s0_ref, cw_ref, alog_ref, dtb_ref, alogc_ref, dtbc_ref,
                ng_ref, o_ref, convn_ref, sn_ref, ext_ref, s_ref, *, L, tv, bb):
    c = pl.program_id(1)
    nc = pl.num_programs(1)

    @pl.when(c == 0)
    def _():
        ext_ref[:, 0:SUBLANES, :] = conv0_ref[...]
        s_ref[...] = s0_ref[...]

    r = lax.broadcasted_iota(jnp.int32, (L, L), 0)
    cc = lax.broadcasted_iota(jnp.int32, (L, L), 1)
    incl = r >= cc
    strict = r > cc
    tri = jnp.where(incl, 1.0, 0.0).astype(F32)
    triu = jnp.where(r <= cc, 1.0, 0.0).astype(F32)
    cw = cw_ref[...]
    gain = ng_ref[...]
    outs = [[None] * GDN_HEADS for _ in range(bb)]
    s_news = [[None] * GDN_HEADS for _ in range(bb)]
    carries = [None] * bb
    rows = []
    for i in range(bb):
        raw = a_ref[i, :, 0:GDN_CONV_CH]
        ext_ref[i, SUBLANES:SUBLANES + L, :] = raw
        ext = ext_ref[i]
        acc = raw * cw[CONV_W - 1:CONV_W, :]
        for j in range(CONV_W - 1):
            sh = pltpu.roll(ext, CONV_W - 1 - j, axis=0)[SUBLANES:SUBLANES + L]
            acc = acc + sh * cw[j:j + 1, :]
        qkv = _silu(acc)
        carries[i] = pltpu.roll(ext, (L + SUBLANES - tv) % (L + SUBLANES), axis=0)[0:SUBLANES]
        ext_ref[i, 0:SUBLANES, :] = carries[i]

        ba = a_ref[i, :, GATE_OFF:GATE_OFF + LANES]
        baT = gt_ref[i, 0]
        beta = _sigmoid(ba)
        g = -jnp.exp(alog_ref[...]) * _softplus(ba + dtb_ref[...])
        gT = -jnp.exp(alogc_ref[...]) * _softplus(baT + dtbc_ref[...])
        if tv < L:
            beta = jnp.where(lax.broadcasted_iota(jnp.int32, ba.shape, 0) < tv, beta, 0.0)
            g = jnp.where(lax.broadcasted_iota(jnp.int32, ba.shape, 0) < tv, g, 0.0)
            gT = jnp.where(lax.broadcasted_iota(jnp.int32, baT.shape, 1) < tv, gT, 0.0)
        gcum = jnp.dot(tri, g, precision=_HI, preferred_element_type=F32)
        gcumT = jnp.dot(gT, triu, precision=_HI, preferred_element_type=F32)
        rows.append((qkv, beta, gcum, gcumT))

    def head(i, h):
        qkv, beta, gcum, gcumT = rows[i]
        q = qkv[:, h * GDN_DK:(h + 1) * GDN_DK]
        k = qkv[:, GDN_QK + h * GDN_DK:GDN_QK + (h + 1) * GDN_DK]
        v = qkv[:, 2 * GDN_QK + h * GDN_DV:2 * GDN_QK + (h + 1) * GDN_DV]
        z = a_ref[i, :, GDN_CONV_CH + h * GDN_DV:GDN_CONV_CH + (h + 1) * GDN_DV]
        q = q * lax.rsqrt(jnp.sum(q * q, axis=-1, keepdims=True) + EPS) * (GDN_DK ** -0.5)
        k = k * lax.rsqrt(jnp.sum(k * k, axis=-1, keepdims=True) + EPS)
        b_c = beta[:, h:h + 1]
        gc = gcum[:, GDN_HEADS + h:GDN_HEADS + h + 1]
        gr = gcumT[GDN_HEADS + h:GDN_HEADS + h + 1, :]
        gamma = jnp.where(incl, jnp.exp(jnp.minimum(gc - gr, 0.0)), 0.0)
        eg = jnp.exp(gc)
        kb = k * b_c
        s = s_ref[i, h]
        kk = _dot_nt(kb, k)
        qk = _dot_nt(q, k) * gamma
        qs = _dot(q * eg, s)
        yield
        n_mat = jnp.where(strict, kk * gamma, 0.0)
        t_inv = yield from _unit_lower_inverse(n_mat, L)
        rhs = jnp.concatenate([v * b_c, kb * eg], axis=-1)
        sol = _dot3(t_inv, rhs)
        yield
        u = sol[:, 0:GDN_DV]
        w = sol[:, GDN_DV:]
        v_new = u - _dot(w, s)
        yield
        o = qs + _dot(qk, v_new)
        g_last = gc[L - 1:L, :]
        s_news[i][h] = s * jnp.exp(g_last) + _dot_tn(k * jnp.exp(g_last - gc), v_new)
        outs[i][h] = _rms(o, gain) * _silu(z)

    _interleave([head(i, h) for i in range(bb) for h in range(GDN_HEADS)])
    for i in range(bb):
        for h in range(GDN_HEADS):
            s_ref[i, h] = s_news[i][h]
        o_ref[i] = jnp.concatenate(outs[i], axis=-1).astype(o_ref.dtype)

    @pl.when(c == nc - 1)
    def _():
        for i in range(bb):
            convn_ref[i] = carries[i]
        sn_ref[...] = s_ref[...]


def _gdn(slab, gt, conv0, s0, cw, alog, dtb, alogc, dtbc, ng, L, tv, bb):
    bsz, t, _ = slab.shape
    nc = t // L
    fix2 = lambda b, c: (0, 0)
    return pl.pallas_call(
        functools.partial(_gdn_kernel, L=L, tv=tv, bb=bb),
        grid=(bsz // bb, nc),
        in_specs=[pl.BlockSpec((bb, L, SLAB_W), lambda b, c: (b, c, 0)),
                  pl.BlockSpec((bb, 1, LANES, L), lambda b, c: (b, c, 0, 0)),
                  pl.BlockSpec((bb, SUBLANES, GDN_CONV_CH), lambda b, c: (b, 0, 0)),
                  pl.BlockSpec((bb, GDN_HEADS, GDN_DK, GDN_DV), lambda b, c: (b, 0, 0, 0)),
                  pl.BlockSpec((CONV_W, GDN_CONV_CH), fix2),
                  pl.BlockSpec((1, LANES), fix2), pl.BlockSpec((1, LANES), fix2),
                  pl.BlockSpec((LANES, 1), fix2), pl.BlockSpec((LANES, 1), fix2),
                  pl.BlockSpec((1, GDN_DV), fix2)],
        out_specs=[pl.BlockSpec((bb, L, GDN_VW), lambda b, c: (b, c, 0)),
                   pl.BlockSpec((bb, SUBLANES, GDN_CONV_CH), lambda b, c: (b, 0, 0)),
                   pl.BlockSpec((bb, GDN_HEADS, GDN_DK, GDN_DV), lambda b, c: (b, 0, 0, 0))],
        out_shape=[jax.ShapeDtypeStruct((bsz, t, GDN_VW), BF16),
                   jax.ShapeDtypeStruct((bsz, SUBLANES, GDN_CONV_CH), F32),
                   jax.ShapeDtypeStruct((bsz, GDN_HEADS, GDN_DK, GDN_DV), F32)],
        scratch_shapes=[pltpu.VMEM((bb, L + SUBLANES, GDN_CONV_CH), F32),
                        pltpu.VMEM((bb, GDN_HEADS, GDN_DK, GDN_DV), F32)],
        compiler_params=_cparams("parallel", "arbitrary"),
        name="gdn",
    )(slab, gt, conv0, s0, cw, alog, dtb, alogc, dtbc, ng)


def _mlstm_kernel(a_ref, gt_ref, c0_ref, n0_ref, m0_ref, bias_ref, biasc_ref, ng_ref,
                  o_ref, cn_ref, nn_ref, mn_ref, c_ref, n_ref, m_ref, *, L, tv, bb):
    cidx = pl.program_id(1)
    nc = pl.num_programs(1)

    @pl.when(cidx == 0)
    def _():
        c_ref[...] = c0_ref[...]
        n_ref[...] = n0_ref[...]
        m_ref[...] = m0_ref[...]

    r = lax.broadcasted_iota(jnp.int32, (L, L), 0)
    cc = lax.broadcasted_iota(jnp.int32, (L, L), 1)
    incl = r >= cc
    tri = jnp.where(incl, 1.0, 0.0).astype(F32)
    triu = jnp.where(r <= cc, 1.0, 0.0).astype(F32)
    gain = ng_ref[...]
    lane = lax.broadcasted_iota(jnp.int32, (1, LANES), 1)
    outs = [[None] * ML_HEADS for _ in range(bb)]
    m_news = [[None] * ML_HEADS for _ in range(bb)]
    c_news = [[None] * ML_HEADS for _ in range(bb)]
    n_news = [[None] * ML_HEADS for _ in range(bb)]
    rows = []
    for i in range(bb):
        gi = a_ref[i, :, GATE_OFF:GATE_OFF + LANES] + bias_ref[...]
        giT = gt_ref[i, 0] + biasc_ref[...]
        lf = -_softplus(-gi)
        lfT = -_softplus(-giT)
        li = gi
        liT = giT
        if tv < L:
            vc = lax.broadcasted_iota(jnp.int32, gi.shape, 0) < tv
            vr = lax.broadcasted_iota(jnp.int32, giT.shape, 1) < tv
            lf = jnp.where(vc, lf, 0.0)
            lfT = jnp.where(vr, lfT, 0.0)
            li = jnp.where(vc, li, NEG)
            liT = jnp.where(vr, liT, NEG)
        bcum = jnp.dot(tri, lf, precision=_HI, preferred_element_type=F32)
        bcumT = jnp.dot(lfT, triu, precision=_HI, preferred_element_type=F32)
        rows.append((li, liT, bcum, bcumT, m_ref[i]))

    def head(i, h):
        li, liT, bcum, bcumT, m_all = rows[i]
        q = a_ref[i, :, h * ML_DH:(h + 1) * ML_DH]
        k = a_ref[i, :, ML_W + h * ML_DH:ML_W + (h + 1) * ML_DH] * (ML_DH ** -0.5)
        v = a_ref[i, :, 2 * ML_W + h * ML_DH:2 * ML_W + (h + 1) * ML_DH]
        og = a_ref[i, :, 3 * ML_W + h * ML_DH:3 * ML_W + (h + 1) * ML_DH]
        bc = bcum[:, ML_HEADS + h:ML_HEADS + h + 1]
        br = bcumT[ML_HEADS + h:ML_HEADS + h + 1, :]
        lic = li[:, h:h + 1]
        lir = liT[h:h + 1, :]
        m = m_all[:, h:h + 1]
        d_mat = jnp.where(incl, bc - br + lir, NEG)
        inter = bc + m
        m_t = jnp.maximum(inter, jnp.max(d_mat, axis=-1, keepdims=True))
        w_intra = jnp.exp(d_mat - m_t)
        e_inter = jnp.exp(inter - m_t)
        cmat = c_ref[i, h]
        nrow = n_ref[i, h:h + 1, :]
        m_new = m_t[L - 1:L, :]
        b_last = bc[L - 1:L, :]
        e_c = jnp.exp(b_last + m - m_new)
        w_k = jnp.exp(b_last - bc + lic - m_new)
        kw = k * w_k
        qk = _dot_nt(q, k)
        qc = _dot(q, cmat)
        c_news[i][h] = e_c * cmat + _dot_tn(kw, v)
        n_news[i][h] = e_c * nrow + jnp.sum(kw, axis=0, keepdims=True)
        m_news[i][h] = m_new
        yield
        s = qk * w_intra
        num = e_inter * qc + _dot(s, v)
        den = e_inter * jnp.sum(q * nrow, axis=-1, keepdims=True) + jnp.sum(s, axis=-1, keepdims=True)
        hh = num / jnp.maximum(jnp.abs(den), jnp.exp(-m_t))
        outs[i][h] = _rms(hh, gain) * _sigmoid(og)

    _interleave([head(i, h) for i in range(bb) for h in range(ML_HEADS)])
    for i in range(bb):
        m_next = rows[i][4]
        for h in range(ML_HEADS):
            m_next = jnp.where(lane == h, m_news[i][h], m_next)
            c_ref[i, h] = c_news[i][h]
            n_ref[i, h:h + 1, :] = n_news[i][h]
        m_ref[i] = m_next
        o_ref[i] = jnp.concatenate(outs[i], axis=-1).astype(o_ref.dtype)

    @pl.when(cidx == nc - 1)
    def _():
        cn_ref[...] = c_ref[...]
        nn_ref[...] = n_ref[...]
        mn_ref[...] = m_ref[...]


def _mlstm(slab, gt, c0, n0, m0, bias, biasc, ng, L, tv, bb):
    bsz, t, _ = slab.shape
    nc = t // L
    fix2 = lambda b, c: (0, 0)
    return pl.pallas_call(
        functools.partial(_mlstm_kernel, L=L, tv=tv, bb=bb),
        grid=(bsz // bb, nc),
        in_specs=[pl.BlockSpec((bb, L, SLAB_W), lambda b, c: (b, c, 0)),
                  pl.BlockSpec((bb, 1, LANES, L), lambda b, c: (b, c, 0, 0)),
                  pl.BlockSpec((bb, ML_HEADS, ML_DH, ML_DH), lambda b, c: (b, 0, 0, 0)),
                  pl.BlockSpec((bb, SUBLANES, ML_DH), lambda b, c: (b, 0, 0)),
                  pl.BlockSpec((bb, 1, LANES), lambda b, c: (b, 0, 0)),
                  pl.BlockSpec((1, LANES), fix2), pl.BlockSpec((LANES, 1), fix2),
                  pl.BlockSpec((1, ML_DH), fix2)],
        out_specs=[pl.BlockSpec((bb, L, ML_W), lambda b, c: (b, c, 0)),
                   pl.BlockSpec((bb, ML_HEADS, ML_DH, ML_DH), lambda b, c: (b, 0, 0, 0)),
                   pl.BlockSpec((bb, SUBLANES, ML_DH), lambda b, c: (b, 0, 0)),
                   pl.BlockSpec((bb, 1, LANES), lambda b, c: (b, 0, 0))],
        out_shape=[jax.ShapeDtypeStruct((bsz, t, ML_W), BF16),
                   jax.ShapeDtypeStruct((bsz, ML_HEADS, ML_DH, ML_DH), F32),
                   jax.ShapeDtypeStruct((bsz, SUBLANES, ML_DH), F32),
                   jax.ShapeDtypeStruct((bsz, 1, LANES), F32)],
        scratch_shapes=[pltpu.VMEM((bb, ML_HEADS, ML_DH, ML_DH), F32),
                        pltpu.VMEM((bb, SUBLANES, ML_DH), F32),
                        pltpu.VMEM((bb, 1, LANES), F32)],
        compiler_params=_cparams("parallel", "arbitrary"),
        name="mlstm",
    )(slab, gt, c0, n0, m0, bias, biasc, ng)


def _s5_kernel(u_ref, h0_ref, lb_ref, bw_ref, cw_ref, dsk_ref, wglu_ref,
               o_ref, hn_ref, h_ref, *, L):
    c = pl.program_id(1)
    nc = pl.num_programs(1)

    @pl.when(c == 0)
    def _():
        h_ref[...] = h0_ref[0]

    u = u_ref[0]
    bu = jnp.dot(u.astype(BF16), bw_ref[...], preferred_element_type=F32)
    x_re = bu[:, 0:S5_N]
    x_im = bu[:, S5_N:]
    pr = lb_ref[0:1, :]
    pi = lb_ref[1:2, :]
    hin_re = h_ref[0:1, :]
    hin_im = h_ref[1:2, :]
    row = lax.broadcasted_iota(jnp.int32, x_re.shape, 0)
    first = row == 0
    x_re = x_re + jnp.where(first, pr * hin_re - pi * hin_im, 0.0)
    x_im = x_im + jnp.where(first, pr * hin_im + pi * hin_re, 0.0)
    d = 1
    while d < L:
        keep = row >= d
        s_re = jnp.where(keep, pltpu.roll(x_re, d, axis=0), 0.0)
        s_im = jnp.where(keep, pltpu.roll(x_im, d, axis=0), 0.0)
        x_re, x_im = x_re + (pr * s_re - pi * s_im), x_im + (pr * s_im + pi * s_re)
        pr, pi = pr * pr - pi * pi, 2.0 * pr * pi
        d *= 2
    tail_re = x_re[L - SUBLANES:L, :]
    tail_im = x_im[L - SUBLANES:L, :]
    h_ref[0:1, :] = tail_re[SUBLANES - 1:SUBLANES, :]
    h_ref[1:2, :] = tail_im[SUBLANES - 1:SUBLANES, :]
    hcat = jnp.concatenate([x_re, x_im], axis=-1).astype(BF16)
    y = jnp.dot(hcat, cw_ref[...], preferred_element_type=F32) + dsk_ref[...] * u
    zg = jax.nn.gelu(y)
    out = zg * _sigmoid(jnp.dot(zg.astype(BF16), wglu_ref[...], preferred_element_type=F32))
    o_ref[0] = out.astype(o_ref.dtype)

    @pl.when(c == nc - 1)
    def _():
        hn_ref[0, 0] = tail_re
        hn_ref[0, 1] = tail_im


def _s5(u, h0, lb, bw, cw, dsk, wglu, L):
    bsz, t, _ = u.shape
    nc = t // L
    fix2 = lambda b, c: (0, 0)
    return pl.pallas_call(
        functools.partial(_s5_kernel, L=L),
        grid=(bsz, nc),
        in_specs=[pl.BlockSpec((1, L, S5_CH), lambda b, c: (b, c, 0)),
                  pl.BlockSpec((1, SUBLANES, S5_N), lambda b, c: (b, 0, 0)),
                  pl.BlockSpec((SUBLANES, S5_N), fix2),
                  pl.BlockSpec((S5_CH, 2 * S5_N), fix2),
                  pl.BlockSpec((2 * S5_N, S5_CH), fix2),
                  pl.BlockSpec((1, S5_CH), fix2),
                  pl.BlockSpec((S5_CH, S5_CH), fix2)],
        out_specs=[pl.BlockSpec((1, L, S5_CH), lambda b, c: (b, c, 0)),
                   pl.BlockSpec((1, 2, SUBLANES, S5_N), lambda b, c: (b, 0, 0, 0))],
        out_shape=[jax.ShapeDtypeStruct((bsz, t, S5_CH), BF16),
                   jax.ShapeDtypeStruct((bsz, 2, SUBLANES, S5_N), F32)],
        scratch_shapes=[pltpu.VMEM((SUBLANES, S5_N), F32)],
        compiler_params=_cparams("parallel", "arbitrary"),
        name="s5",
    )(u, h0, lb, bw, cw, dsk, wglu)


def _pad_lanes(v, offset=0):
    return jnp.zeros((1, LANES), F32).at[0, offset:offset + v.shape[0]].set(v.astype(F32))


def _prep_layer(l, P):
    w_in = P["w_in"][l]
    sizes = (GDN_CONV_CH, GDN_VW, GDN_HEADS, GDN_HEADS, S5_CH, 3 * ML_W, ML_HEADS, ML_HEADS, ML_W)
    offs = [0]
    for s in sizes:
        offs.append(offs[-1] + s)
    g_qkv, g_z, g_b, g_a, s_u, m_qkv, m_i, m_f, m_o = [w_in[:, offs[i]:offs[i + 1]] for i in range(9)]
    zpad = jnp.zeros((D_MODEL, LANES - 2 * GDN_HEADS), F32)
    w_cat = jnp.concatenate([g_qkv, g_z, g_b, g_a, zpad, s_u, m_qkv, m_o, m_i, m_f, zpad],
                            axis=1).astype(BF16)

    lr = P["s5_lam_re"][l].astype(F32)
    li = P["s5_lam_im"][l].astype(F32)
    dt = jnp.exp(P["s5_log_dt"][l].astype(F32))[:, None]
    mag = jnp.exp(lr * dt)
    lb_re = mag * jnp.cos(li * dt)
    lb_im = mag * jnp.sin(li * dt)
    den = lr * lr + li * li
    c_re = ((lb_re - 1.0) * lr + lb_im * li) / den
    c_im = (lb_im * lr - (lb_re - 1.0) * li) / den
    b_r = P["s5_B_re"][l].astype(F32)
    b_i = P["s5_B_im"][l].astype(F32)
    bb_re = c_re[..., None] * b_r - c_im[..., None] * b_i
    bb_im = c_re[..., None] * b_i + c_im[..., None] * b_r
    eye_g = jnp.eye(S5_GROUPS, dtype=F32)
    bd = lambda m: jnp.einsum("gph,gk->ghkp", m, eye_g).reshape(S5_CH, S5_N)
    bw = jnp.concatenate([bd(bb_re), bd(bb_im)], axis=1).astype(BF16)
    cd = lambda m: jnp.einsum("ghp,gk->gpkh", m, eye_g).reshape(S5_N, S5_CH)
    cw = jnp.concatenate([cd(P["s5_C_re"][l].astype(F32)),
                          -cd(P["s5_C_im"][l].astype(F32))], axis=0).astype(BF16)
    lb = jnp.zeros((SUBLANES, S5_N), F32).at[0].set(lb_re.reshape(-1)).at[1].set(lb_im.reshape(-1))

    wr = jnp.zeros((D_MODEL, LANES), F32)
    wr = wr.at[:, 0:N_GROUPS].set(P["w_router_group"][l])
    wr = wr.at[:, N_GROUPS:N_GROUPS + N_EXPERTS].set(P["w_router_expert"][l])
    br = jnp.zeros((1, LANES), F32)
    br = br.at[0, 0:N_GROUPS].set(P["b_router_group"][l])
    br = br.at[0, N_GROUPS:N_GROUPS + N_EXPERTS].set(P["b_router_expert"][l])

    gw = EXPERTS_PER_GROUP * D_EXPERT
    regroup = lambda w: (w.reshape(N_GROUPS, EXPERTS_PER_GROUP, D_MODEL, D_EXPERT)
                         .transpose(0, 2, 1, 3).reshape(N_GROUPS, D_MODEL, gw).astype(BF16))
    ml_bias = (_pad_lanes(P["ml_ig_bias"][l]) + _pad_lanes(P["ml_fg_bias"][l], ML_HEADS))
    alog = _pad_lanes(P["gdn_A_log"][l], GDN_HEADS)
    dtb = _pad_lanes(P["gdn_dt_bias"][l], GDN_HEADS)
    return dict(
        norm_mix=P["norm_mix"][l].reshape(1, D_MODEL).astype(F32),
        w_cat=w_cat,
        w_out=P["w_out"][l].astype(BF16),
        conv_w=P["gdn_conv_w"][l].astype(F32),
        alog=alog, dtb=dtb, alogc=alog.reshape(LANES, 1), dtbc=dtb.reshape(LANES, 1),
        gdn_norm=P["gdn_norm"][l].reshape(1, GDN_DV).astype(F32),
        lb=lb, bw=bw, cw=cw,
        s5_d=P["s5_D"][l].reshape(1, S5_CH).astype(F32),
        w_glu=P["s5_w_glu"][l].astype(BF16),
        ml_bias=ml_bias, ml_biasc=ml_bias.reshape(LANES, 1),
        ml_norm=P["ml_norm"][l].reshape(1, ML_DH).astype(F32),
        norm_ffn=P["norm_ffn"][l].reshape(1, D_MODEL).astype(F32),
        wr=wr, br=br,
        wg=regroup(P["w_exp_gate"][l]), wu=regroup(P["w_exp_up"][l]),
        wd=P["w_exp_down"][l].reshape(N_GROUPS, gw, D_MODEL).astype(BF16),
        norm_ple=P["norm_ple"][l].reshape(1, D_MODEL).astype(F32),
        w_ple_gate=P["w_ple_gate"][l].astype(BF16),
        w_ple_proj=P["w_ple_proj"][l].astype(BF16),
    )


def _gate_rows(slab, bsz, t, L):
    g = slab[:, GATE_OFF:].reshape(bsz, t // L, L, LANES)
    return g.transpose(0, 1, 3, 2)


def _trunk(x, p, states, layers, final_norm, *, L, tv, Ls, tm, tm_moe, bb):
    conv0, gdn0, s5re0, s5im0, mc0, mn0, mm0 = states
    bsz, t, _ = x.shape
    m = bsz * t
    h = x.reshape(m, D_MODEL)
    outs = [[] for _ in range(7)]
    last_row = (tv - 1) % SUBLANES
    for l, W in enumerate(layers):
        slab_g, s_u, slab_m = _norm_inproj(h, W["norm_mix"], W["w_cat"], tm)
        conv_in = jnp.pad(conv0[l], ((0, 0), (SUBLANES - (CONV_W - 1), 0), (0, 0)))
        o_gdn, conv_n, gdn_n = _gdn(slab_g.reshape(bsz, t, SLAB_W), _gate_rows(slab_g, bsz, t, L),
                                    conv_in, gdn0[l], W["conv_w"], W["alog"], W["dtb"],
                                    W["alogc"], W["dtbc"], W["gdn_norm"], L, tv, bb)
        h0 = jnp.zeros((bsz, SUBLANES, S5_N), F32)
        h0 = h0.at[:, 0].set(s5re0[l].reshape(bsz, S5_N)).at[:, 1].set(s5im0[l].reshape(bsz, S5_N))
        o_s5, s5_n = _s5(s_u.reshape(bsz, t, S5_CH), h0, W["lb"], W["bw"], W["cw"], W["s5_d"],
                         W["w_glu"], Ls)
        n_in = jnp.pad(mn0[l], ((0, 0), (0, SUBLANES - ML_HEADS), (0, 0)))
        m_in = jnp.pad(mm0[l], ((0, 0), (0, LANES - ML_HEADS))).reshape(bsz, 1, LANES)
        o_ml, c_n, n_n, m_n = _mlstm(slab_m.reshape(bsz, t, SLAB_W), _gate_rows(slab_m, bsz, t, L),
                                     mc0[l], n_in, m_in, W["ml_bias"], W["ml_biasc"],
                                     W["ml_norm"], L, tv, 1)
        h1, f, gates = _outproj_router(o_gdn.reshape(m, GDN_VW), o_s5.reshape(m, S5_CH),
                                       o_ml.reshape(m, ML_W), h, W["w_out"], W["norm_ffn"],
                                       W["wr"], W["br"], tm)
        h2 = _moe(f, gates, h1, W["wg"], W["wu"], W["wd"], tm_moe)
        h = _ple(h2, p[l].reshape(m, PLE_DIM), W["norm_ple"], W["w_ple_gate"], W["w_ple_proj"],
                 final_norm, tm, l == len(layers) - 1)
        outs[0].append(conv_n[:, SUBLANES - (CONV_W - 1):])
        outs[1].append(gdn_n)
        outs[2].append(s5_n[:, 0, last_row].reshape(bsz, S5_GROUPS, S5_STATE))
        outs[3].append(s5_n[:, 1, last_row].reshape(bsz, S5_GROUPS, S5_STATE))
        outs[4].append(c_n)
        outs[5].append(n_n[:, :ML_HEADS])
        outs[6].append(m_n[:, 0, :ML_HEADS])
    return (h.reshape(bsz, t, D_MODEL),) + tuple(jnp.stack(o) for o in outs)


def kernel(x_prompt, x_sample, p_prompt, p_sample, state_gdn_conv, state_gdn, state_s5_re, state_s5_im, state_mlstm_C, state_mlstm_n, state_mlstm_m, norm_mix, w_in, w_out, gdn_conv_w, gdn_A_log, gdn_dt_bias, gdn_norm, s5_lam_re, s5_lam_im, s5_log_dt, s5_B_re, s5_B_im, s5_C_re, s5_C_im, s5_D, s5_w_glu, ml_ig_bias, ml_fg_bias, ml_norm, norm_ffn, w_router_group, b_router_group, w_router_expert, b_router_expert, w_exp_gate, w_exp_up, w_exp_down, norm_ple, w_ple_gate, w_ple_proj, final_norm):
    P = dict(norm_mix=norm_mix, w_in=w_in, w_out=w_out, gdn_conv_w=gdn_conv_w, gdn_A_log=gdn_A_log,
             gdn_dt_bias=gdn_dt_bias, gdn_norm=gdn_norm, s5_lam_re=s5_lam_re, s5_lam_im=s5_lam_im,
             s5_log_dt=s5_log_dt, s5_B_re=s5_B_re, s5_B_im=s5_B_im, s5_C_re=s5_C_re, s5_C_im=s5_C_im,
             s5_D=s5_D, s5_w_glu=s5_w_glu, ml_ig_bias=ml_ig_bias, ml_fg_bias=ml_fg_bias,
             ml_norm=ml_norm, norm_ffn=norm_ffn, w_router_group=w_router_group,
             b_router_group=b_router_group, w_router_expert=w_router_expert,
             b_router_expert=b_router_expert, w_exp_gate=w_exp_gate, w_exp_up=w_exp_up,
             w_exp_down=w_exp_down, norm_ple=norm_ple, w_ple_gate=w_ple_gate, w_ple_proj=w_ple_proj)
    depth = norm_mix.shape[0]
    layers = [_prep_layer(l, P) for l in range(depth)]
    fnorm = final_norm.reshape(1, D_MODEL).astype(F32)

    bp, tp, _ = x_prompt.shape
    zeros = lambda *s: jnp.zeros((depth, bp) + s, F32)
    prompt_init = (zeros(CONV_W - 1, GDN_CONV_CH), zeros(GDN_HEADS, GDN_DK, GDN_DV),
                   zeros(S5_GROUPS, S5_STATE), zeros(S5_GROUPS, S5_STATE),
                   zeros(ML_HEADS, ML_DH, ML_DH), zeros(ML_HEADS, ML_DH), zeros(ML_HEADS))
    lp = math.gcd(tp, 64)
    lsp = math.gcd(tp, 256)
    res_p = _trunk(x_prompt, p_prompt, prompt_init, layers, fnorm,
                   L=lp, tv=lp, Ls=lsp, tm=512, tm_moe=1024, bb=2)

    bs, ts, _ = x_sample.shape
    tpad = -(-ts // SUBLANES) * SUBLANES
    xs = jnp.pad(x_sample, ((0, 0), (0, tpad - ts), (0, 0)))
    ps = jnp.pad(p_sample, ((0, 0), (0, 0), (0, tpad - ts), (0, 0)))
    sample_init = (state_gdn_conv, state_gdn, state_s5_re, state_s5_im,
                   state_mlstm_C, state_mlstm_n, state_mlstm_m)
    res_s = _trunk(xs, ps, sample_init, layers, fnorm,
                   L=tpad, tv=ts, Ls=tpad, tm=512, tm_moe=1024, bb=4)
    y_sample = res_s[0][:, :ts]
    return (res_p[0], y_sample) + res_p[1:] + res_s[1:]
```

```python
import functools
import math

import jax
import jax.numpy as jnp
from jax import lax
from jax.experimental import pallas as pl
from jax.experimental.pallas import tpu as pltpu

F32 = jnp.float32
BF16 = jnp.bfloat16

D_MODEL = 1024
DEPTH = 2
GDN_HEADS = 6
GDN_DK = 64
GDN_DV = 64
GDN_QK = GDN_HEADS * GDN_DK
GDN_VW = GDN_HEADS * GDN_DV
GDN_CONV_CH = 2 * GDN_QK + GDN_VW
CONV_W = 4
S5_GROUPS = 16
S5_GROUP_CH = 16
S5_CH = S5_GROUPS * S5_GROUP_CH
S5_STATE = 64
S5_N = S5_GROUPS * S5_STATE
ML_HEADS = 6
ML_DH = 64
ML_W = ML_HEADS * ML_DH
N_GROUPS = 4
EXPERTS_PER_GROUP = 4
N_EXPERTS = N_GROUPS * EXPERTS_PER_GROUP
D_EXPERT = 256
PLE_DIM = 256
EPS = 1e-6

LANES = 128
SUBLANES = 8
NEG = -1e30
VMEM_LIMIT = 56 * 1024 * 1024

_HI = lax.Precision.HIGHEST


def _cparams(*sem):
    return pltpu.CompilerParams(dimension_semantics=sem, vmem_limit_bytes=VMEM_LIMIT)


def _dot(a, b):
    return jnp.dot(a.astype(BF16), b.astype(BF16), preferred_element_type=F32)


def _dot_nt(a, b):
    return lax.dot_general(a.astype(BF16), b.astype(BF16), (((1,), (1,)), ((), ())),
                           preferred_element_type=F32)


def _dot_tn(a, b):
    return lax.dot_general(a.astype(BF16), b.astype(BF16), (((0,), (0,)), ((), ())),
                           preferred_element_type=F32)


def _split_bf16(a):
    hi = a.astype(BF16)
    lo = (a - hi.astype(F32)).astype(BF16)
    return hi, lo


def _rms(x, gain):
    return x * lax.rsqrt(jnp.mean(x * x, axis=-1, keepdims=True) + EPS) * gain


def _softplus(x):
    return jnp.maximum(x, 0.0) + jnp.log(1.0 + jnp.exp(-jnp.abs(x)))


def _sigmoid(x):
    return 1.0 / (1.0 + jnp.exp(-x))


def _silu(x):
    return x * _sigmoid(x)


def _interleave(gens):
    live = list(gens)
    while live:
        still = []
        for g in live:
            try:
                next(g)
                still.append(g)
            except StopIteration:
                pass
        live = still


def _unit_lower_inverse(n_mat, size, top=None):
    top = size if top is None else top
    r = lax.broadcasted_iota(jnp.int32, (size, size), 0)
    c = lax.broadcasted_iota(jnp.int32, (size, size), 1)
    base = min(16, top)
    same = jnp.bitwise_xor(r, c) < base
    nd = jnp.where(same, n_mat, 0.0)
    eye = jnp.where(r == c, 1.0, 0.0).astype(F32)
    t = eye - nd
    x = nd
    p = 1
    while 2 * p < base:
        x = _dot(x, x)
        yield
        t = t + _dot(t, x)
        yield
        p *= 2
    blk = base
    while blk < top:
        pair = jnp.bitwise_xor(r, c)
        off = jnp.where((pair < 2 * blk) & (pair >= blk), n_mat, 0.0)
        ot = _dot(off, t)
        yield
        t = t - _dot(t, ot)
        yield
        blk *= 2
    return t


def _norm_inproj_kernel(x_ref, g_ref, w_ref, *out_refs, widths):
    u = _rms(x_ref[...], g_ref[...]).astype(BF16)
    off = 0
    for o_ref, wd in zip(out_refs, widths):
        o_ref[...] = jnp.dot(u, w_ref[:, off:off + wd], preferred_element_type=F32)
        off += wd


def _norm_inproj(x, gain, w, widths, tm):
    m = x.shape[0]
    assert sum(widths) == w.shape[1]
    return pl.pallas_call(
        functools.partial(_norm_inproj_kernel, widths=widths),
        grid=(m // tm,),
        in_specs=[pl.BlockSpec((tm, D_MODEL), lambda i: (i, 0)),
                  pl.BlockSpec((1, D_MODEL), lambda i: (0, 0)),
                  pl.BlockSpec((D_MODEL, w.shape[1]), lambda i: (0, 0))],
        out_specs=[pl.BlockSpec((tm, wd), lambda i: (i, 0)) for wd in widths],
        out_shape=[jax.ShapeDtypeStruct((m, wd), F32) for wd in widths],
        compiler_params=_cparams("parallel"),
        name="norm_inproj",
    )(x, gain, w)


def _outproj_router_kernel(og_ref, os_ref, om_ref, h_ref, wo_ref, nf_ref, wr_ref, br_ref,
                           h1_ref, f_ref, gates_ref):
    h1 = h_ref[...]
    h1 = h1 + jnp.dot(og_ref[...], wo_ref[0:GDN_VW, :], preferred_element_type=F32)
    h1 = h1 + jnp.dot(os_ref[...], wo_ref[GDN_VW:GDN_VW + S5_CH, :], preferred_element_type=F32)
    h1 = h1 + jnp.dot(om_ref[...], wo_ref[GDN_VW + S5_CH:, :], preferred_element_type=F32)
    h1_ref[...] = h1
    f = _rms(h1, nf_ref[...])
    f_ref[...] = f.astype(BF16)
    logits = jnp.dot(f, wr_ref[...], precision=_HI, preferred_element_type=F32) + br_ref[...]
    lane = lax.broadcasted_iota(jnp.int32, logits.shape, 1)
    is_g = lane < N_GROUPS
    gl = jnp.where(is_g, logits, NEG)
    gmax = jnp.max(gl, axis=-1, keepdims=True)
    ge = jnp.where(is_g, jnp.exp(gl - gmax), 0.0)
    p_grp = ge / jnp.sum(ge, axis=-1, keepdims=True)
    g_prob = jnp.max(p_grp, axis=-1, keepdims=True)
    g_idx = jnp.min(jnp.where(is_g & (gl == gmax), lane, LANES), axis=-1, keepdims=True)
    e_lane = lane - N_GROUPS
    is_e = (e_lane >= 0) & (e_lane < N_EXPERTS) & (jnp.right_shift(e_lane, 2) == g_idx)
    le = jnp.where(is_e, logits, NEG)
    m1 = jnp.max(le, axis=-1, keepdims=True)
    i1 = jnp.min(jnp.where(is_e & (le == m1), lane, LANES), axis=-1, keepdims=True)
    is_e2 = is_e & (lane != i1)
    le2 = jnp.where(is_e2, logits, NEG)
    m2 = jnp.max(le2, axis=-1, keepdims=True)
    i2 = jnp.min(jnp.where(is_e2 & (le2 == m2), lane, LANES), axis=-1, keepdims=True)
    e2 = jnp.exp(m2 - m1)
    w1 = g_prob / (1.0 + e2)
    w2 = g_prob * e2 / (1.0 + e2)
    gates_ref[...] = jnp.where(lane == i1, w1, 0.0) + jnp.where(lane == i2, w2, 0.0)


def _outproj_router(og, os_, om, h, wo, nf, wr, br, tm):
    m = h.shape[0]
    row = lambda i: (i, 0)
    fix = lambda i: (0, 0)
    return pl.pallas_call(
        _outproj_router_kernel,
        grid=(m // tm,),
        in_specs=[pl.BlockSpec((tm, GDN_VW), row), pl.BlockSpec((tm, S5_CH), row),
                  pl.BlockSpec((tm, ML_W), row), pl.BlockSpec((tm, D_MODEL), row),
                  pl.BlockSpec((D_MODEL, D_MODEL), fix), pl.BlockSpec((1, D_MODEL), fix),
                  pl.BlockSpec((D_MODEL, LANES), fix), pl.BlockSpec((1, LANES), fix)],
        out_specs=[pl.BlockSpec((tm, D_MODEL), row), pl.BlockSpec((tm, D_MODEL), row),
                   pl.BlockSpec((tm, LANES), row)],
        out_shape=[jax.ShapeDtypeStruct((m, D_MODEL), F32),
                   jax.ShapeDtypeStruct((m, D_MODEL), BF16),
                   jax.ShapeDtypeStruct((m, LANES), F32)],
        compiler_params=_cparams("parallel"),
        name="outproj_router",
    )(og, os_, om, h, wo, nf, wr, br)


def _moe_kernel(f_ref, gates_ref, h1_ref, wg_ref, wu_ref, wd_ref, out_ref):
    gi = pl.program_id(1)

    @pl.when(gi == 0)
    def _():
        out_ref[...] = h1_ref[...]

    x = f_ref[...]
    hg = jnp.dot(x, wg_ref[0], preferred_element_type=F32)
    hu = jnp.dot(x, wu_ref[0], preferred_element_type=F32)
    gates = gates_ref[...]
    lane = lax.broadcasted_iota(jnp.int32, gates.shape, 1)
    base = N_GROUPS + EXPERTS_PER_GROUP * gi
    parts = []
    for j in range(EXPERTS_PER_GROUP):
        gcol = jnp.sum(jnp.where(lane == base + j, gates, 0.0), axis=-1, keepdims=True)
        sl = slice(j * D_EXPERT, (j + 1) * D_EXPERT)
        parts.append((_silu(hg[:, sl]) * hu[:, sl] * gcol).astype(BF16))
    hidden = jnp.concatenate(parts, axis=-1)
    out_ref[...] += jnp.dot(hidden, wd_ref[0], preferred_element_type=F32)


def _moe(f, gates, h1, wg, wu, wd, tm):
    m = f.shape[0]
    row = lambda i, g: (i, 0)
    wsel = lambda i, g: (g, 0, 0)
    gw = EXPERTS_PER_GROUP * D_EXPERT
    return pl.pallas_call(
        _moe_kernel,
        grid=(m // tm, N_GROUPS),
        in_specs=[pl.BlockSpec((tm, D_MODEL), row), pl.BlockSpec((tm, LANES), row),
                  pl.BlockSpec((tm, D_MODEL), row),
                  pl.BlockSpec((1, D_MODEL, gw), wsel), pl.BlockSpec((1, D_MODEL, gw), wsel),
                  pl.BlockSpec((1, gw, D_MODEL), wsel)],
        out_specs=pl.BlockSpec((tm, D_MODEL), row),
        out_shape=jax.ShapeDtypeStruct((m, D_MODEL), F32),
        compiler_params=_cparams("parallel", "arbitrary"),
        name="moe",
    )(f, gates, h1, wg, wu, wd)


def _ple_kernel(h_ref, p_ref, np_ref, wg_ref, wp_ref, fn_ref, out_ref, *, final):
    h = h_ref[...]
    gate = _sigmoid(jnp.dot(_rms(h, np_ref[...]).astype(BF16), wg_ref[...],
                            preferred_element_type=F32))
    proj = jnp.dot(p_ref[...].astype(BF16), wp_ref[...], preferred_element_type=F32)
    h = h + proj * gate
    if final:
        h = _rms(h, fn_ref[...])
    out_ref[...] = h


def _ple(h, p, npl, wg, wp, fn, tm, final):
    m = h.shape[0]
    row = lambda i: (i, 0)
    fix = lambda i: (0, 0)
    return pl.pallas_call(
        functools.partial(_ple_kernel, final=final),
        grid=(m // tm,),
        in_specs=[pl.BlockSpec((tm, D_MODEL), row), pl.BlockSpec((tm, PLE_DIM), row),
                  pl.BlockSpec((1, D_MODEL), fix), pl.BlockSpec((D_MODEL, D_MODEL), fix),
                  pl.BlockSpec((PLE_DIM, D_MODEL), fix), pl.BlockSpec((1, D_MODEL), fix)],
        out_specs=pl.BlockSpec((tm, D_MODEL), row),
        out_shape=jax.ShapeDtypeStruct((m, D_MODEL), F32),
        compiler_params=_cparams("parallel"),
        name="ple",
    )(h, p, npl, wg, wp, fn)


MIX_SLAB_W = 6 * ML_W
PAIRS = ML_HEADS // 2
HALF = LANES // 2


def _scan_rows(x, size, op, fill):
    row = lax.broadcasted_iota(jnp.int32, x.shape, 0)
    d = 1
    while d < size:
        x = op(x, jnp.where(row >= d, pltpu.roll(x, d, axis=0), fill))
        d *= 2
    return x


def _split3(a):
    hi = a.astype(BF16)
    r1 = a - hi.astype(F32)
    mid = r1.astype(BF16)
    lo = (r1 - mid.astype(F32)).astype(BF16)
    return hi, mid, lo


def _row_form(x_s, L):
    rr = lax.broadcasted_iota(jnp.int32, x_s.shape, 0)
    cc = lax.broadcasted_iota(jnp.int32, x_s.shape, 1)
    dg = jnp.where(rr == jnp.bitwise_and(cc, L - 1), x_s, 0.0)
    ones = jnp.ones((SUBLANES, L), BF16)
    acc = None
    for piece in _split3(dg):
        t = jnp.dot(ones, piece, preferred_element_type=F32)
        acc = t if acc is None else acc + t
    return acc[0:1, :]


def _score_cols(x, p, L):
    if 2 * L == LANES:
        return x[:, LANES * p:LANES * (p + 1)]
    return jnp.concatenate([x[:, LANES * p:LANES * p + L],
                            x[:, LANES * p + HALF:LANES * p + HALF + L]], axis=-1)


def _bd_stack(x2):
    lo = lax.broadcasted_iota(jnp.int32, x2.shape, 1) < HALF
    return jnp.concatenate([jnp.where(lo, x2, 0.0), jnp.where(lo, 0.0, x2)], axis=0)


def _bd_ones(rows_per_half):
    shape = (2 * rows_per_half, LANES)
    r = lax.broadcasted_iota(jnp.int32, shape, 0) < rows_per_half
    c = lax.broadcasted_iota(jnp.int32, shape, 1) < HALF
    return jnp.where(r == c, 1.0, 0.0).astype(F32)


def _half_sums(x2, ones_bd):
    hi, lo = _split_bf16(x2)
    ob = ones_bd.astype(BF16)
    return (jnp.dot(hi, ob, preferred_element_type=F32)
            + jnp.dot(lo, ob, preferred_element_type=F32))


def _mlstm_kernel(a_ref, st0_ref, m0_ref, bi_ref, bf_ref, ng_ref,
                  o_ref, stn_ref, mn_ref, st_ref, m_ref, *, L, tv, bb):
    cidx = pl.program_id(1)
    nc = pl.num_programs(1)

    @pl.when(cidx == 0)
    def _():
        st_ref[...] = st0_ref[...]
        m_ref[...] = m0_ref[...]

    W = 2 * L
    rr = lax.broadcasted_iota(jnp.int32, (L, W), 0)
    cc = lax.broadcasted_iota(jnp.int32, (L, W), 1)
    incl = rr >= jnp.bitwise_and(cc, L - 1)
    ones_keys = _bd_ones(L)
    ones_ch = _bd_ones(HALF)
    bd256 = jnp.concatenate([ones_ch, ones_ch], axis=-1) > 0.5
    gain2 = ng_ref[...]
    gain2 = jnp.concatenate([gain2, gain2], axis=-1)
    ones_l = jnp.ones((L, LANES), F32)
    outs = [[None] * PAIRS for _ in range(bb)]
    st_news = [[None] * PAIRS for _ in range(bb)]
    m_news = [None] * bb
    rows = []
    for i in range(bb):
        li = a_ref[i, :, 4 * ML_W:5 * ML_W] + bi_ref[...]
        lf = -_softplus(-(a_ref[i, :, 5 * ML_W:6 * ML_W] + bf_ref[...]))
        if tv < L:
            valid = lax.broadcasted_iota(jnp.int32, li.shape, 0) < tv
            li = jnp.where(valid, li, NEG)
            lf = jnp.where(valid, lf, 0.0)
        bcum = _scan_rows(lf, L, jnp.add, 0.0)
        a = li - bcum
        m0 = m_ref[i]
        m_t = bcum + jnp.maximum(m0, _scan_rows(a, L, jnp.maximum, NEG))
        e_inter = jnp.exp(bcum + m0 - m_t)
        m_new = m_t[L - 1:L, :]
        b_last = bcum[L - 1:L, :]
        e_c = jnp.exp(b_last + m0 - m_new)
        kw = a_ref[i, :, ML_W:2 * ML_W] * (ML_DH ** -0.5) * jnp.exp(b_last + a - m_new)
        m_news[i] = m_new
        rows.append((a, bcum, m_t, e_inter, e_c, kw))

    def pair(i, p):
        a, bcum, m_t, e_inter, e_c, kw = rows[i]
        sl = slice(LANES * p, LANES * (p + 1))
        q2 = a_ref[i, :, sl]
        k2 = a_ref[i, :, ML_W + LANES * p:ML_W + LANES * (p + 1)] * (ML_DH ** -0.5)
        v2 = a_ref[i, :, 2 * ML_W + LANES * p:2 * ML_W + LANES * (p + 1)]
        og2 = a_ref[i, :, 3 * ML_W + LANES * p:3 * ML_W + LANES * (p + 1)]
        st = st_ref[i, p]
        arow = _row_form(_score_cols(a, p, L), L)
        qk = _dot_nt(q2, _bd_stack(k2))
        qcn = _dot(q2, st)
        upd = _dot_tn(kw[:, sl], jnp.concatenate([v2, ones_l], axis=-1))
        ec2 = e_c[:, sl]
        st_news[i][p] = st * jnp.concatenate([ec2, ec2], axis=-1) + jnp.where(bd256, upd, 0.0)
        yield
        w_intra = jnp.where(incl, jnp.exp(_score_cols(bcum, p, L) + arow - _score_cols(m_t, p, L)), 0.0)
        s2 = qk * w_intra
        nd = _dot(s2, jnp.concatenate([_bd_stack(v2), ones_keys], axis=-1))
        yield
        e2 = e_inter[:, sl]
        num = e2 * qcn[:, 0:LANES] + nd[:, 0:LANES]
        den = e2 * qcn[:, LANES:] + nd[:, LANES:]
        hh = num / jnp.maximum(jnp.abs(den), jnp.exp(-m_t[:, sl]))
        ss = _half_sums(hh * hh, ones_ch)
        yield
        outs[i][p] = hh * lax.rsqrt(ss * (1.0 / ML_DH) + EPS) * gain2 * _sigmoid(og2)

    _interleave([pair(i, p) for i in range(bb) for p in range(PAIRS)])
    for i in range(bb):
        for p in range(PAIRS):
            st_ref[i, p] = st_news[i][p]
        m_ref[i] = m_news[i]
        o_ref[i] = jnp.concatenate(outs[i], axis=-1).astype(o_ref.dtype)

    @pl.when(cidx == nc - 1)
    def _():
        stn_ref[...] = st_ref[...]
        mn_ref[...] = m_ref[...]


def _mlstm(slab, st0, m0, bi, bf, ng, L, tv, bb):
    bsz, t, _ = slab.shape
    nc = t // L
    fix2 = lambda b, c: (0, 0)
    st_spec = pl.BlockSpec((bb, PAIRS, LANES, 2 * LANES), lambda b, c: (b, 0, 0, 0))
    m_spec = pl.BlockSpec((bb, 1, ML_W), lambda b, c: (b, 0, 0))
    return pl.pallas_call(
        functools.partial(_mlstm_kernel, L=L, tv=tv, bb=bb),
        grid=(bsz // bb, nc),
        in_specs=[pl.BlockSpec((bb, L, MIX_SLAB_W), lambda b, c: (b, c, 0)),
                  st_spec, m_spec,
                  pl.BlockSpec((1, ML_W), fix2), pl.BlockSpec((1, ML_W), fix2),
                  pl.BlockSpec((1, ML_DH), fix2)],
        out_specs=[pl.BlockSpec((bb, L, ML_W), lambda b, c: (b, c, 0)), st_spec, m_spec],
        out_shape=[jax.ShapeDtypeStruct((bsz, t, ML_W), BF16),
                   jax.ShapeDtypeStruct((bsz, PAIRS, LANES, 2 * LANES), F32),
                   jax.ShapeDtypeStruct((bsz, 1, ML_W), F32)],
        scratch_shapes=[pltpu.VMEM((bb, PAIRS, LANES, 2 * LANES), F32),
                        pltpu.VMEM((bb, 1, ML_W), F32)],
        compiler_params=_cparams("parallel", "arbitrary"),
        name="mlstm",
    )(slab, st0, m0, bi, bf, ng)


def _mlstm_state_pack(c, n, m):
    bsz = c.shape[0]
    eye2 = jnp.eye(2, dtype=F32)
    cbd = jnp.einsum("bpxkd,xy->bpxkyd", c.reshape(bsz, PAIRS, 2, ML_DH, ML_DH), eye2)
    nrep = jnp.einsum("bpxk,xy,d->bpxkyd", n.reshape(bsz, PAIRS, 2, ML_DH), eye2,
                      jnp.ones((ML_DH,), F32))
    st = jnp.concatenate([cbd.reshape(bsz, PAIRS, LANES, LANES),
                          nrep.reshape(bsz, PAIRS, LANES, LANES)], axis=-1)
    return st, jnp.repeat(m, ML_DH, axis=-1).reshape(bsz, 1, ML_W)


def _mlstm_state_unpack(st, m):
    bsz = st.shape[0]
    cb = st[..., :LANES].reshape(bsz, PAIRS, 2, ML_DH, 2, ML_DH)
    c = jnp.stack([cb[:, :, 0, :, 0, :], cb[:, :, 1, :, 1, :]], axis=2)
    nb = st[..., LANES:].reshape(bsz, PAIRS, 2, ML_DH, 2, ML_DH)[..., 0]
    n = jnp.stack([nb[:, :, 0, :, 0], nb[:, :, 1, :, 1]], axis=2)
    return (c.reshape(bsz, ML_HEADS, ML_DH, ML_DH), n.reshape(bsz, ML_HEADS, ML_DH),
            m[:, 0, ::ML_DH])


def _gdn_kernel(a_ref, conv0_ref, s0_ref, cw_ref, alog_ref, dtb_ref, ng_ref,
                o_ref, convn_ref, sn_ref, ext_ref, s_ref, *, L, tv, bb):
    c = pl.program_id(1)
    nc = pl.num_programs(1)

    @pl.when(c == 0)
    def _():
        ext_ref[:, 0:SUBLANES, :] = conv0_ref[...]
        s_ref[...] = s0_ref[...]

    W = 2 * L
    rr = lax.broadcasted_iota(jnp.int32, (L, W), 0)
    cc = lax.broadcasted_iota(jnp.int32, (L, W), 1)
    key = jnp.bitwise_and(cc, L - 1)
    incl = rr >= key
    strict = rr > key
    first_half = cc < L
    lo = lax.broadcasted_iota(jnp.int32, (L, LANES), 1) < HALF
    ones_ch = _bd_ones(HALF)
    bd128 = ones_ch > 0.5
    cw = cw_ref[...]
    gain2 = ng_ref[...]
    gain2 = jnp.concatenate([gain2, gain2], axis=-1)
    outs = [[None] * PAIRS for _ in range(bb)]
    s_news = [[None] * PAIRS for _ in range(bb)]
    carries = [None] * bb
    rows = []
    for i in range(bb):
        raw = a_ref[i, :, 0:GDN_CONV_CH]
        ext_ref[i, SUBLANES:SUBLANES + L, :] = raw
        ext = ext_ref[i]
        acc = raw * cw[CONV_W - 1:CONV_W, :]
        for j in range(CONV_W - 1):
            sh = pltpu.roll(ext, CONV_W - 1 - j, axis=0)[SUBLANES:SUBLANES + L]
            acc = acc + sh * cw[j:j + 1, :]
        qkv = _silu(acc)
        carries[i] = pltpu.roll(ext, (L + SUBLANES - tv) % (L + SUBLANES), axis=0)[0:SUBLANES]

        beta = _sigmoid(a_ref[i, :, 4 * GDN_VW:5 * GDN_VW])
        g = -jnp.exp(alog_ref[...]) * _softplus(a_ref[i, :, 5 * GDN_VW:6 * GDN_VW] + dtb_ref[...])
        if tv < L:
            valid = lax.broadcasted_iota(jnp.int32, g.shape, 0) < tv
            beta = jnp.where(valid, beta, 0.0)
            g = jnp.where(valid, g, 0.0)
        gcum = _scan_rows(g, L, jnp.add, 0.0)
        g_last = gcum[L - 1:L, :]
        rows.append((qkv, beta, gcum, jnp.exp(gcum), jnp.exp(g_last - gcum), jnp.exp(g_last)))
    for i in range(bb):
        ext_ref[i, 0:SUBLANES, :] = carries[i]

    def pair(i, p):
        qkv, beta, gcum, eg, kdec, sdec = rows[i]
        sl = slice(LANES * p, LANES * (p + 1))
        q2 = qkv[:, LANES * p:LANES * (p + 1)]
        k2 = qkv[:, GDN_QK + LANES * p:GDN_QK + LANES * (p + 1)]
        v2 = qkv[:, 2 * GDN_QK + LANES * p:2 * GDN_QK + LANES * (p + 1)]
        z2 = a_ref[i, :, GDN_CONV_CH + LANES * p:GDN_CONV_CH + LANES * (p + 1)]
        s2 = s_ref[i, p]
        ssq = _half_sums(q2 * q2, ones_ch)
        ssk = _half_sums(k2 * k2, ones_ch)
        grow = _row_form(_score_cols(gcum, p, L), L)
        yield
        q2 = q2 * lax.rsqrt(ssq + EPS) * (GDN_DK ** -0.5)
        k2 = k2 * lax.rsqrt(ssk + EPS)
        beta2 = beta[:, sl]
        eg2 = eg[:, sl]
        kb2 = k2 * beta2
        kq = _dot_nt(jnp.concatenate([_bd_stack(kb2), q2], axis=0), _bd_stack(k2))
        qs = _dot(q2 * eg2, s2)
        yield
        gam = jnp.where(incl, jnp.exp(jnp.minimum(_score_cols(gcum, p, L) - grow, 0.0)), 0.0)
        gam_s = jnp.where(strict, gam, 0.0)
        gam_bd = jnp.concatenate([jnp.where(first_half, gam_s, 0.0),
                                  jnp.where(first_half, 0.0, gam_s)], axis=0)
        t_inv = yield from _unit_lower_inverse(kq[0:W] * gam_bd, W, L)
        rhs = jnp.concatenate([v2 * beta2, kb2 * eg2], axis=-1)
        sol = _dot(t_inv, jnp.concatenate([rhs, rhs], axis=0))
        yield
        u2 = jnp.where(lo, sol[0:L, 0:LANES], sol[L:W, 0:LANES])
        w2 = jnp.where(lo, sol[0:L, LANES:], sol[L:W, LANES:])
        v_new = u2 - _dot(w2, s2)
        yield
        o2 = qs + _dot(kq[W:] * gam, _bd_stack(v_new))
        upd = _dot_tn(k2 * kdec[:, sl], v_new)
        s_news[i][p] = s2 * sdec[:, sl] + jnp.where(bd128, upd, 0.0)
        yield
        ss = _half_sums(o2 * o2, ones_ch)
        yield
        outs[i][p] = o2 * lax.rsqrt(ss * (1.0 / GDN_DV) + EPS) * gain2 * _silu(z2)

    _interleave([pair(i, p) for i in range(bb) for p in range(PAIRS)])
    for i in range(bb):
        for p in range(PAIRS):
            s_ref[i, p] = s_news[i][p]
        o_ref[i] = jnp.concatenate(outs[i], axis=-1).astype(o_ref.dtype)

    @pl.when(c == nc - 1)
    def _():
        for i in range(bb):
            convn_ref[i] = carries[i]
        sn_ref[...] = s_ref[...]


def _gdn(slab, conv0, s0, cw, alog, dtb, ng, L, tv, bb):
    bsz, t, _ = slab.shape
    nc = t // L
    fix2 = lambda b, c: (0, 0)
    s_spec = pl.BlockSpec((bb, PAIRS, LANES, LANES), lambda b, c: (b, 0, 0, 0))
    cv_spec = pl.BlockSpec((bb, SUBLANES, GDN_CONV_CH), lambda b, c: (b, 0, 0))
    return pl.pallas_call(
        functools.partial(_gdn_kernel, L=L, tv=tv, bb=bb),
        grid=(bsz // bb, nc),
        in_specs=[pl.BlockSpec((bb, L, MIX_SLAB_W), lambda b, c: (b, c, 0)),
                  cv_spec, s_spec,
                  pl.BlockSpec((CONV_W, GDN_CONV_CH), fix2),
                  pl.BlockSpec((1, GDN_VW), fix2), pl.BlockSpec((1, GDN_VW), fix2),
                  pl.BlockSpec((1, GDN_DV), fix2)],
        out_specs=[pl.BlockSpec((bb, L, GDN_VW), lambda b, c: (b, c, 0)), cv_spec, s_spec],
        out_shape=[jax.ShapeDtypeStruct((bsz, t, GDN_VW), BF16),
                   jax.ShapeDtypeStruct((bsz, SUBLANES, GDN_CONV_CH), F32),
                   jax.ShapeDtypeStruct((bsz, PAIRS, LANES, LANES), F32)],
        scratch_shapes=[pltpu.VMEM((bb, L + SUBLANES, GDN_CONV_CH), F32),
                        pltpu.VMEM((bb, PAIRS, LANES, LANES), F32)],
        compiler_params=_cparams("parallel", "arbitrary"),
        name="gdn",
    )(slab, conv0, s0, cw, alog, dtb, ng)


def _pair_pack(s):
    bsz = s.shape[0]
    bd = jnp.einsum("bpxkd,xy->bpxkyd", s.reshape(bsz, PAIRS, 2, GDN_DK, GDN_DV),
                    jnp.eye(2, dtype=F32))
    return bd.reshape(bsz, PAIRS, LANES, LANES)


def _pair_unpack(s):
    bsz = s.shape[0]
    sb = s.reshape(bsz, PAIRS, 2, GDN_DK, 2, GDN_DV)
    return jnp.stack([sb[:, :, 0, :, 0, :], sb[:, :, 1, :, 1, :]],
                     axis=2).reshape(bsz, GDN_HEADS, GDN_DK, GDN_DV)


def _s5_kernel(u_ref, h0_ref, lb_ref, bw_ref, cw_ref, dsk_ref, wglu_ref,
               o_ref, hn_ref, h_ref, *, L):
    c = pl.program_id(1)
    nc = pl.num_programs(1)

    @pl.when(c == 0)
    def _():
        h_ref[...] = h0_ref[0]

    u = u_ref[0]
    bu = jnp.dot(u.astype(BF16), bw_ref[...], preferred_element_type=F32)
    x_re = bu[:, 0:S5_N]
    x_im = bu[:, S5_N:]
    pr = lb_ref[0:1, :]
    pi = lb_ref[1:2, :]
    hin_re = h_ref[0:1, :]
    hin_im = h_ref[1:2, :]
    row = lax.broadcasted_iota(jnp.int32, x_re.shape, 0)
    first = row == 0
    x_re = x_re + jnp.where(first, pr * hin_re - pi * hin_im, 0.0)
    x_im = x_im + jnp.where(first, pr * hin_im + pi * hin_re, 0.0)
    d = 1
    while d < L:
        keep = row >= d
        s_re = jnp.where(keep, pltpu.roll(x_re, d, axis=0), 0.0)
        s_im = jnp.where(keep, pltpu.roll(x_im, d, axis=0), 0.0)
        x_re, x_im = x_re + (pr * s_re - pi * s_im), x_im + (pr * s_im + pi * s_re)
        pr, pi = pr * pr - pi * pi, 2.0 * pr * pi
        d *= 2
    tail_re = x_re[L - SUBLANES:L, :]
    tail_im = x_im[L - SUBLANES:L, :]
    h_ref[0:1, :] = tail_re[SUBLANES - 1:SUBLANES, :]
    h_ref[1:2, :] = tail_im[SUBLANES - 1:SUBLANES, :]
    hcat = jnp.concatenate([x_re, x_im], axis=-1).astype(BF16)
    y = jnp.dot(hcat, cw_ref[...], preferred_element_type=F32) + dsk_ref[...] * u
    zg = jax.nn.gelu(y)
    out = zg * _sigmoid(jnp.dot(zg.astype(BF16), wglu_ref[...], preferred_element_type=F32))
    o_ref[0] = out.astype(o_ref.dtype)

    @pl.when(c == nc - 1)
    def _():
        hn_ref[0, 0] = tail_re
        hn_ref[0, 1] = tail_im


def _s5(u, h0, lb, bw, cw, dsk, wglu, L):
    bsz, t, _ = u.shape
    nc = t // L
    fix2 = lambda b, c: (0, 0)
    return pl.pallas_call(
        functools.partial(_s5_kernel, L=L),
        grid=(bsz, nc),
        in_specs=[pl.BlockSpec((1, L, S5_CH), lambda b, c: (b, c, 0)),
                  pl.BlockSpec((1, SUBLANES, S5_N), lambda b, c: (b, 0, 0)),
                  pl.BlockSpec((SUBLANES, S5_N), fix2),
                  pl.BlockSpec((S5_CH, 2 * S5_N), fix2),
                  pl.BlockSpec((2 * S5_N, S5_CH), fix2),
                  pl.BlockSpec((1, S5_CH), fix2),
                  pl.BlockSpec((S5_CH, S5_CH), fix2)],
        out_specs=[pl.BlockSpec((1, L, S5_CH), lambda b, c: (b, c, 0)),
                   pl.BlockSpec((1, 2, SUBLANES, S5_N), lambda b, c: (b, 0, 0, 0))],
        out_shape=[jax.ShapeDtypeStruct((bsz, t, S5_CH), BF16),
                   jax.ShapeDtypeStruct((bsz, 2, SUBLANES, S5_N), F32)],
        scratch_shapes=[pltpu.VMEM((SUBLANES, S5_N), F32)],
        compiler_params=_cparams("parallel", "arbitrary"),
        name="s5",
    )(u, h0, lb, bw, cw, dsk, wglu)


def _prep_layer(l, P):
    w_in = P["w_in"][l]
    sizes = (GDN_CONV_CH, GDN_VW, GDN_HEADS, GDN_HEADS, S5_CH, 3 * ML_W, ML_HEADS, ML_HEADS, ML_W)
    offs = [0]
    for s in sizes:
        offs.append(offs[-1] + s)
    g_qkv, g_z, g_b, g_a, s_u, m_qkv, m_i, m_f, m_o = [w_in[:, offs[i]:offs[i + 1]] for i in range(9)]
    rep = lambda w: jnp.repeat(w, ML_DH, axis=1)
    w_cat = jnp.concatenate([g_qkv, g_z, rep(g_b), rep(g_a), s_u, m_qkv, m_o, rep(m_i), rep(m_f)],
                            axis=1).astype(BF16)

    lr = P["s5_lam_re"][l].astype(F32)
    li = P["s5_lam_im"][l].astype(F32)
    dt = jnp.exp(P["s5_log_dt"][l].astype(F32))[:, None]
    mag = jnp.exp(lr * dt)
    lb_re = mag * jnp.cos(li * dt)
    lb_im = mag * jnp.sin(li * dt)
    den = lr * lr + li * li
    c_re = ((lb_re - 1.0) * lr + lb_im * li) / den
    c_im = (lb_im * lr - (lb_re - 1.0) * li) / den
    b_r = P["s5_B_re"][l].astype(F32)
    b_i = P["s5_B_im"][l].astype(F32)
    bb_re = c_re[..., None] * b_r - c_im[..., None] * b_i
    bb_im = c_re[..., None] * b_i + c_im[..., None] * b_r
    eye_g = jnp.eye(S5_GROUPS, dtype=F32)
    bd = lambda m: jnp.einsum("gph,gk->ghkp", m, eye_g).reshape(S5_CH, S5_N)
    bw = jnp.concatenate([bd(bb_re), bd(bb_im)], axis=1).astype(BF16)
    cd = lambda m: jnp.einsum("ghp,gk->gpkh", m, eye_g).reshape(S5_N, S5_CH)
    cw = jnp.concatenate([cd(P["s5_C_re"][l].astype(F32)),
                          -cd(P["s5_C_im"][l].astype(F32))], axis=0).astype(BF16)
    lb = jnp.zeros((SUBLANES, S5_N), F32).at[0].set(lb_re.reshape(-1)).at[1].set(lb_im.reshape(-1))

    wr = jnp.zeros((D_MODEL, LANES), F32)
    wr = wr.at[:, 0:N_GROUPS].set(P["w_router_group"][l])
    wr = wr.at[:, N_GROUPS:N_GROUPS + N_EXPERTS].set(P["w_router_expert"][l])
    br = jnp.zeros((1, LANES), F32)
    br = br.at[0, 0:N_GROUPS].set(P["b_router_group"][l])
    br = br.at[0, N_GROUPS:N_GROUPS + N_EXPERTS].set(P["b_router_expert"][l])

    gw = EXPERTS_PER_GROUP * D_EXPERT
    regroup = lambda w: (w.reshape(N_GROUPS, EXPERTS_PER_GROUP, D_MODEL, D_EXPERT)
                         .transpose(0, 2, 1, 3).reshape(N_GROUPS, D_MODEL, gw).astype(BF16))
    rep_row = lambda v: jnp.repeat(v.astype(F32), ML_DH).reshape(1, ML_W)
    return dict(
        norm_mix=P["norm_mix"][l].reshape(1, D_MODEL).astype(F32),
        w_cat=w_cat,
        w_out=P["w_out"][l].astype(BF16),
        conv_w=P["gdn_conv_w"][l].astype(F32),
        alog=rep_row(P["gdn_A_log"][l]), dtb=rep_row(P["gdn_dt_bias"][l]),
        gdn_norm=P["gdn_norm"][l].reshape(1, GDN_DV).astype(F32),
        lb=lb, bw=bw, cw=cw,
        s5_d=P["s5_D"][l].reshape(1, S5_CH).astype(F32),
        w_glu=P["s5_w_glu"][l].astype(BF16),
        ml_bi=rep_row(P["ml_ig_bias"][l]), ml_bf=rep_row(P["ml_fg_bias"][l]),
        ml_norm=P["ml_norm"][l].reshape(1, ML_DH).astype(F32),
        norm_ffn=P["norm_ffn"][l].reshape(1, D_MODEL).astype(F32),
        wr=wr, br=br,
        wg=regroup(P["w_exp_gate"][l]), wu=regroup(P["w_exp_up"][l]),
        wd=P["w_exp_down"][l].reshape(N_GROUPS, gw, D_MODEL).astype(BF16),
        norm_ple=P["norm_ple"][l].reshape(1, D_MODEL).astype(F32),
        w_ple_gate=P["w_ple_gate"][l].astype(BF16),
        w_ple_proj=P["w_ple_proj"][l].astype(BF16),
    )


def _trunk(x, p, states, layers, final_norm, *, L, tv, Ls, tm, tm_moe, bb):
    conv0, gdn0, s5re0, s5im0, mc0, mn0, mm0 = states
    bsz, t, _ = x.shape
    m = bsz * t
    h = x.reshape(m, D_MODEL)
    outs = [[] for _ in range(7)]
    last_row = (tv - 1) % SUBLANES
    for l, W in enumerate(layers):
        slab_g, s_u, slab_m = _norm_inproj(h, W["norm_mix"], W["w_cat"],
                                           (MIX_SLAB_W, S5_CH, MIX_SLAB_W), tm)
        conv_in = jnp.pad(conv0[l], ((0, 0), (SUBLANES - (CONV_W - 1), 0), (0, 0)))
        o_gdn, conv_n, gdn_n = _gdn(slab_g.reshape(bsz, t, MIX_SLAB_W), conv_in,
                                    _pair_pack(gdn0[l]), W["conv_w"], W["alog"], W["dtb"],
                                    W["gdn_norm"], L, tv, bb)
        h0 = jnp.zeros((bsz, SUBLANES, S5_N), F32)
        h0 = h0.at[:, 0].set(s5re0[l].reshape(bsz, S5_N)).at[:, 1].set(s5im0[l].reshape(bsz, S5_N))
        o_s5, s5_n = _s5(s_u.reshape(bsz, t, S5_CH), h0, W["lb"], W["bw"], W["cw"], W["s5_d"],
                         W["w_glu"], Ls)
        st_in, m_in = _mlstm_state_pack(mc0[l], mn0[l], mm0[l])
        o_ml, st_n, m_n = _mlstm(slab_m.reshape(bsz, t, MIX_SLAB_W), st_in, m_in, W["ml_bi"],
                                 W["ml_bf"], W["ml_norm"], L, tv, bb)
        c_n, n_n, m_n = _mlstm_state_unpack(st_n, m_n)
        h1, f, gates = _outproj_router(o_gdn.reshape(m, GDN_VW), o_s5.reshape(m, S5_CH),
                                       o_ml.reshape(m, ML_W), h, W["w_out"], W["norm_ffn"],
                                       W["wr"], W["br"], tm)
        h2 = _moe(f, gates, h1, W["wg"], W["wu"], W["wd"], tm_moe)
        h = _ple(h2, p[l].reshape(m, PLE_DIM), W["norm_ple"], W["w_ple_gate"], W["w_ple_proj"],
                 final_norm, tm, l == len(layers) - 1)
        outs[0].append(conv_n[:, SUBLANES - (CONV_W - 1):])
        outs[1].append(_pair_unpack(gdn_n))
        outs[2].append(s5_n[:, 0, last_row].reshape(bsz, S5_GROUPS, S5_STATE))
        outs[3].append(s5_n[:, 1, last_row].reshape(bsz, S5_GROUPS, S5_STATE))
        outs[4].append(c_n)
        outs[5].append(n_n)
        outs[6].append(m_n)
    return (h.reshape(bsz, t, D_MODEL),) + tuple(jnp.stack(o) for o in outs)


def kernel(x_prompt, x_sample, p_prompt, p_sample, state_gdn_conv, state_gdn, state_s5_re, state_s5_im, state_mlstm_C, state_mlstm_n, state_mlstm_m, norm_mix, w_in, w_out, gdn_conv_w, gdn_A_log, gdn_dt_bias, gdn_norm, s5_lam_re, s5_lam_im, s5_log_dt, s5_B_re, s5_B_im, s5_C_re, s5_C_im, s5_D, s5_w_glu, ml_ig_bias, ml_fg_bias, ml_norm, norm_ffn, w_router_group, b_router_group, w_router_expert, b_router_expert, w_exp_gate, w_exp_up, w_exp_down, norm_ple, w_ple_gate, w_ple_proj, final_norm):
    P = dict(norm_mix=norm_mix, w_in=w_in, w_out=w_out, gdn_conv_w=gdn_conv_w, gdn_A_log=gdn_A_log,
             gdn_dt_bias=gdn_dt_bias, gdn_norm=gdn_norm, s5_lam_re=s5_lam_re, s5_lam_im=s5_lam_im,
             s5_log_dt=s5_log_dt, s5_B_re=s5_B_re, s5_B_im=s5_B_im, s5_C_re=s5_C_re, s5_C_im=s5_C_im,
             s5_D=s5_D, s5_w_glu=s5_w_glu, ml_ig_bias=ml_ig_bias, ml_fg_bias=ml_fg_bias,
             ml_norm=ml_norm, norm_ffn=norm_ffn, w_router_group=w_router_group,
             b_router_group=b_router_group, w_router_expert=w_router_expert,
             b_router_expert=b_router_expert, w_exp_gate=w_exp_gate, w_exp_up=w_exp_up,
             w_exp_down=w_exp_down, norm_ple=norm_ple, w_ple_gate=w_ple_gate, w_ple_proj=w_ple_proj)
    depth = norm_mix.shape[0]
    layers = [_prep_layer(l, P) for l in range(depth)]
    fnorm = final_norm.reshape(1, D_MODEL).astype(F32)

    bp, tp, _ = x_prompt.shape
    zeros = lambda *s: jnp.zeros((depth, bp) + s, F32)
    prompt_init = (zeros(CONV_W - 1, GDN_CONV_CH), zeros(GDN_HEADS, GDN_DK, GDN_DV),
                   zeros(S5_GROUPS, S5_STATE), zeros(S5_GROUPS, S5_STATE),
                   zeros(ML_HEADS, ML_DH, ML_DH), zeros(ML_HEADS, ML_DH), zeros(ML_HEADS))
    lp = math.gcd(tp, 64)
    lsp = math.gcd(tp, 256)
    res_p = _trunk(x_prompt, p_prompt, prompt_init, layers, fnorm,
                   L=lp, tv=lp, Ls=lsp, tm=512, tm_moe=1024, bb=2)

    bs, ts, _ = x_sample.shape
    tpad = -(-ts // SUBLANES) * SUBLANES
    xs = jnp.pad(x_sample, ((0, 0), (0, tpad - ts), (0, 0)))
    ps = jnp.pad(p_sample, ((0, 0), (0, 0), (0, tpad - ts), (0, 0)))
    sample_init = (state_gdn_conv, state_gdn, state_s5_re, state_s5_im,
                   state_mlstm_C, state_mlstm_n, state_mlstm_m)
    res_s = _trunk(xs, ps, sample_init, layers, fnorm,
                   L=tpad, tv=ts, Ls=tpad, tm=512, tm_moe=1024, bb=4)
    y_sample = res_s[0][:, :ts]
    return (res_p[0], y_sample) + res_p[1:] + res_s[1:]
```

```python
import functools
import math

import jax
import jax.numpy as jnp
from jax import lax
from jax.experimental import pallas as pl
from jax.experimental.pallas import tpu as pltpu

F32 = jnp.float32
BF16 = jnp.bfloat16

D_MODEL = 1024
DEPTH = 2
GDN_HEADS = 6
GDN_DK = 64
GDN_DV = 64
GDN_QK = GDN_HEADS * GDN_DK
GDN_VW = GDN_HEADS * GDN_DV
GDN_CONV_CH = 2 * GDN_QK + GDN_VW
CONV_W = 4
S5_GROUPS = 16
S5_GROUP_CH = 16
S5_CH = S5_GROUPS * S5_GROUP_CH
S5_STATE = 64
S5_N = S5_GROUPS * S5_STATE
ML_HEADS = 6
ML_DH = 64
ML_W = ML_HEADS * ML_DH
N_GROUPS = 4
EXPERTS_PER_GROUP = 4
N_EXPERTS = N_GROUPS * EXPERTS_PER_GROUP
D_EXPERT = 256
PLE_DIM = 256
EPS = 1e-6

LANES = 128
SUBLANES = 8
NEG = -1e30
VMEM_LIMIT = 56 * 1024 * 1024

def _cparams(*sem):
    return pltpu.CompilerParams(dimension_semantics=sem, vmem_limit_bytes=VMEM_LIMIT)


def _dot(a, b):
    return jnp.dot(a.astype(BF16), b.astype(BF16), preferred_element_type=F32)


def _dot_nt(a, b):
    return lax.dot_general(a.astype(BF16), b.astype(BF16), (((1,), (1,)), ((), ())),
                           preferred_element_type=F32)


def _dot_tn(a, b):
    return lax.dot_general(a.astype(BF16), b.astype(BF16), (((0,), (0,)), ((), ())),
                           preferred_element_type=F32)


def _split_bf16(a):
    hi = a.astype(BF16)
    lo = (a - hi.astype(F32)).astype(BF16)
    return hi, lo


def _rms(x, gain):
    return x * lax.rsqrt(jnp.mean(x * x, axis=-1, keepdims=True) + EPS) * gain


def _softplus(x):
    return jnp.maximum(x, 0.0) + jnp.log(1.0 + jnp.exp(-jnp.abs(x)))


def _sigmoid(x):
    return 1.0 / (1.0 + jnp.exp(-x))


def _silu(x):
    return x * _sigmoid(x)


def _interleave(gens):
    live = list(gens)
    while live:
        still = []
        for g in live:
            try:
                next(g)
                still.append(g)
            except StopIteration:
                pass
        live = still


def _unit_lower_inverse(n_mat, size, top=None):
    top = size if top is None else top
    r = lax.broadcasted_iota(jnp.int32, (size, size), 0)
    c = lax.broadcasted_iota(jnp.int32, (size, size), 1)
    base = min(16, top)
    same = jnp.bitwise_xor(r, c) < base
    nd = jnp.where(same, n_mat, 0.0)
    eye = jnp.where(r == c, 1.0, 0.0).astype(F32)
    t = eye - nd
    x = nd
    p = 1
    while 2 * p < base:
        x = _dot(x, x)
        yield
        t = t + _dot(t, x)
        yield
        p *= 2
    blk = base
    while blk < top:
        pair = jnp.bitwise_xor(r, c)
        off = jnp.where((pair < 2 * blk) & (pair >= blk), n_mat, 0.0)
        ot = _dot(off, t)
        yield
        t = t - _dot(t, ot)
        yield
        blk *= 2
    return t


def _norm_inproj_kernel(x_ref, g_ref, w_ref, *out_refs, widths):
    u = _rms(x_ref[...], g_ref[...]).astype(BF16)
    off = 0
    for o_ref, wd in zip(out_refs, widths):
        o_ref[...] = jnp.dot(u, w_ref[:, off:off + wd], preferred_element_type=F32)
        off += wd


def _norm_inproj(x, gain, w, widths, tm):
    m = x.shape[0]
    assert sum(widths) == w.shape[1]
    return pl.pallas_call(
        functools.partial(_norm_inproj_kernel, widths=widths),
        grid=(m // tm,),
        in_specs=[pl.BlockSpec((tm, D_MODEL), lambda i: (i, 0)),
                  pl.BlockSpec((1, D_MODEL), lambda i: (0, 0)),
                  pl.BlockSpec((D_MODEL, w.shape[1]), lambda i: (0, 0))],
        out_specs=[pl.BlockSpec((tm, wd), lambda i: (i, 0)) for wd in widths],
        out_shape=[jax.ShapeDtypeStruct((m, wd), F32) for wd in widths],
        compiler_params=_cparams("parallel"),
        name="norm_inproj",
    )(x, gain, w)


def _outproj_router_kernel(og_ref, os_ref, om_ref, h_ref, wo_ref, nf_ref, wr_ref, br_ref,
                           h1_ref, f_ref, gates_ref):
    h1 = h_ref[...]
    h1 = h1 + jnp.dot(og_ref[...], wo_ref[0:GDN_VW, :], preferred_element_type=F32)
    h1 = h1 + jnp.dot(os_ref[...], wo_ref[GDN_VW:GDN_VW + S5_CH, :], preferred_element_type=F32)
    h1 = h1 + jnp.dot(om_ref[...], wo_ref[GDN_VW + S5_CH:, :], preferred_element_type=F32)
    h1_ref[...] = h1
    f = _rms(h1, nf_ref[...])
    fh, fl = _split_bf16(f)
    f_ref[...] = fh
    wh, wl = _split_bf16(wr_ref[...])
    d = functools.partial(jnp.dot, preferred_element_type=F32)
    logits = d(fh, wh) + (d(fh, wl) + d(fl, wh)) + br_ref[...]
    lane = lax.broadcasted_iota(jnp.int32, logits.shape, 1)
    is_g = lane < N_GROUPS
    gl = jnp.where(is_g, logits, NEG)
    gmax = jnp.max(gl, axis=-1, keepdims=True)
    ge = jnp.where(is_g, jnp.exp(gl - gmax), 0.0)
    p_grp = ge / jnp.sum(ge, axis=-1, keepdims=True)
    g_prob = jnp.max(p_grp, axis=-1, keepdims=True)
    g_idx = jnp.min(jnp.where(is_g & (gl == gmax), lane, LANES), axis=-1, keepdims=True)
    e_lane = lane - N_GROUPS
    is_e = (e_lane >= 0) & (e_lane < N_EXPERTS) & (jnp.right_shift(e_lane, 2) == g_idx)
    le = jnp.where(is_e, logits, NEG)
    m1 = jnp.max(le, axis=-1, keepdims=True)
    i1 = jnp.min(jnp.where(is_e & (le == m1), lane, LANES), axis=-1, keepdims=True)
    is_e2 = is_e & (lane != i1)
    le2 = jnp.where(is_e2, logits, NEG)
    m2 = jnp.max(le2, axis=-1, keepdims=True)
    i2 = jnp.min(jnp.where(is_e2 & (le2 == m2), lane, LANES), axis=-1, keepdims=True)
    e2 = jnp.exp(m2 - m1)
    w1 = g_prob / (1.0 + e2)
    w2 = g_prob * e2 / (1.0 + e2)
    gates_ref[...] = jnp.where(lane == i1, w1, 0.0) + jnp.where(lane == i2, w2, 0.0)


def _outproj_router(og, os_, om, h, wo, nf, wr, br, tm):
    m = h.shape[0]
    row = lambda i: (i, 0)
    fix = lambda i: (0, 0)
    return pl.pallas_call(
        _outproj_router_kernel,
        grid=(m // tm,),
        in_specs=[pl.BlockSpec((tm, GDN_VW), row), pl.BlockSpec((tm, S5_CH), row),
                  pl.BlockSpec((tm, ML_W), row), pl.BlockSpec((tm, D_MODEL), row),
                  pl.BlockSpec((D_MODEL, D_MODEL), fix), pl.BlockSpec((1, D_MODEL), fix),
                  pl.BlockSpec((D_MODEL, LANES), fix), pl.BlockSpec((1, LANES), fix)],
        out_specs=[pl.BlockSpec((tm, D_MODEL), row), pl.BlockSpec((tm, D_MODEL), row),
                   pl.BlockSpec((tm, LANES), row)],
        out_shape=[jax.ShapeDtypeStruct((m, D_MODEL), F32),
                   jax.ShapeDtypeStruct((m, D_MODEL), BF16),
                   jax.ShapeDtypeStruct((m, LANES), F32)],
        compiler_params=_cparams("parallel"),
        name="outproj_router",
    )(og, os_, om, h, wo, nf, wr, br)


def _moe_kernel(f_ref, gates_ref, h1_ref, wg_ref, wu_ref, wd_ref, out_ref):
    gi = pl.program_id(1)

    @pl.when(gi == 0)
    def _():
        out_ref[...] = h1_ref[...]

    x = f_ref[...]
    gates = gates_ref[...]
    lane = lax.broadcasted_iota(jnp.int32, gates.shape, 1)
    base = N_GROUPS + EXPERTS_PER_GROUP * gi
    acc = None
    for j in range(EXPERTS_PER_GROUP):
        gcol = jnp.sum(jnp.where(lane == base + j, gates, 0.0), axis=-1, keepdims=True)
        hg = jnp.dot(x, wg_ref[j], preferred_element_type=F32)
        hu = jnp.dot(x, wu_ref[j], preferred_element_type=F32)
        hidden = (_silu(hg) * hu * gcol).astype(BF16)
        t = jnp.dot(hidden, wd_ref[j], preferred_element_type=F32)
        acc = t if acc is None else acc + t
    out_ref[...] += acc


def _moe(f, gates, h1, wg, wu, wd, tm):
    m = f.shape[0]
    row = lambda i, g: (i, 0)
    wsel = lambda i, g: (g, 0, 0)
    e = EXPERTS_PER_GROUP
    return pl.pallas_call(
        _moe_kernel,
        grid=(m // tm, N_GROUPS),
        in_specs=[pl.BlockSpec((tm, D_MODEL), row), pl.BlockSpec((tm, LANES), row),
                  pl.BlockSpec((tm, D_MODEL), row),
                  pl.BlockSpec((e, D_MODEL, D_EXPERT), wsel),
                  pl.BlockSpec((e, D_MODEL, D_EXPERT), wsel),
                  pl.BlockSpec((e, D_EXPERT, D_MODEL), wsel)],
        out_specs=pl.BlockSpec((tm, D_MODEL), row),
        out_shape=jax.ShapeDtypeStruct((m, D_MODEL), F32),
        compiler_params=_cparams("parallel", "arbitrary"),
        name="moe",
    )(f, gates, h1, wg, wu, wd)


def _ple_kernel(h_ref, p_ref, np_ref, wg_ref, wp_ref, fn_ref, out_ref, *, final):
    h = h_ref[...]
    gate = _sigmoid(jnp.dot(_rms(h, np_ref[...]).astype(BF16), wg_ref[...],
                            preferred_element_type=F32))
    proj = jnp.dot(p_ref[...].astype(BF16), wp_ref[...], preferred_element_type=F32)
    h = h + proj * gate
    if final:
        h = _rms(h, fn_ref[...])
    out_ref[...] = h


def _ple(h, p, npl, wg, wp, fn, tm, final):
    m = h.shape[0]
    row = lambda i: (i, 0)
    fix = lambda i: (0, 0)
    return pl.pallas_call(
        functools.partial(_ple_kernel, final=final),
        grid=(m // tm,),
        in_specs=[pl.BlockSpec((tm, D_MODEL), row), pl.BlockSpec((tm, PLE_DIM), row),
                  pl.BlockSpec((1, D_MODEL), fix), pl.BlockSpec((D_MODEL, D_MODEL), fix),
                  pl.BlockSpec((PLE_DIM, D_MODEL), fix), pl.BlockSpec((1, D_MODEL), fix)],
        out_specs=pl.BlockSpec((tm, D_MODEL), row),
        out_shape=jax.ShapeDtypeStruct((m, D_MODEL), F32),
        compiler_params=_cparams("parallel"),
        name="ple",
    )(h, p, npl, wg, wp, fn)


MIX_SLAB_W = 6 * ML_W
PAIRS = ML_HEADS // 2
HALF = LANES // 2


def _scan_rows(x, size, op, fill):
    row = lax.broadcasted_iota(jnp.int32, x.shape, 0)
    d = 1
    while d < size:
        x = op(x, jnp.where(row >= d, pltpu.roll(x, d, axis=0), fill))
        d *= 2
    return x


def _split3(a):
    hi = a.astype(BF16)
    r1 = a - hi.astype(F32)
    mid = r1.astype(BF16)
    lo = (r1 - mid.astype(F32)).astype(BF16)
    return hi, mid, lo


def _row_form(x_s, L):
    rr = lax.broadcasted_iota(jnp.int32, x_s.shape, 0)
    cc = lax.broadcasted_iota(jnp.int32, x_s.shape, 1)
    dg = jnp.where(rr == jnp.bitwise_and(cc, L - 1), x_s, 0.0)
    ones = jnp.ones((SUBLANES, L), BF16)
    acc = None
    for piece in _split3(dg):
        t = jnp.dot(ones, piece, preferred_element_type=F32)
        acc = t if acc is None else acc + t
    return acc[0:1, :]


def _score_cols(x, p, L):
    if 2 * L == LANES:
        return x[:, LANES * p:LANES * (p + 1)]
    return jnp.concatenate([x[:, LANES * p:LANES * p + L],
                            x[:, LANES * p + HALF:LANES * p + HALF + L]], axis=-1)


def _bd_stack(x2):
    lo = lax.broadcasted_iota(jnp.int32, x2.shape, 1) < HALF
    return jnp.concatenate([jnp.where(lo, x2, 0.0), jnp.where(lo, 0.0, x2)], axis=0)


def _bd_ones(rows_per_half):
    shape = (2 * rows_per_half, LANES)
    r = lax.broadcasted_iota(jnp.int32, shape, 0) < rows_per_half
    c = lax.broadcasted_iota(jnp.int32, shape, 1) < HALF
    return jnp.where(r == c, 1.0, 0.0).astype(F32)


def _half_sums(x2, ones_bd):
    hi, lo = _split_bf16(x2)
    ob = ones_bd.astype(BF16)
    return (jnp.dot(hi, ob, preferred_element_type=F32)
            + jnp.dot(lo, ob, preferred_element_type=F32))


def _mlstm_kernel(a_ref, c0_ref, n0_ref, m0_ref, bi_ref, bf_ref, ng_ref,
                  o_ref, cn_ref, nn_ref, mn_ref, st_ref, m_ref, *, L, tv, bb):
    cidx = pl.program_id(1)
    nc = pl.num_programs(1)
    ones_ch = _bd_ones(HALF)
    r128 = lax.broadcasted_iota(jnp.int32, (LANES, LANES), 0)
    c128 = lax.broadcasted_iota(jnp.int32, (LANES, LANES), 1)
    diag128 = r128 == c128

    @pl.when(cidx == 0)
    def _():
        st_ref[...] = jnp.zeros(st_ref.shape, F32)
        m_ref[...] = m0_ref[...]
        ob = ones_ch.astype(BF16)
        for i in range(bb):
            for p in range(PAIRS):
                st_ref[i, p, 0:HALF, 0:HALF] = c0_ref[i, 2 * p]
                st_ref[i, p, HALF:, HALF:LANES] = c0_ref[i, 2 * p + 1]
                dg = jnp.where(diag128, n0_ref[i, p], 0.0)
                acc = None
                for piece in _split3(dg):
                    t = jnp.dot(piece, ob, preferred_element_type=F32)
                    acc = t if acc is None else acc + t
                st_ref[i, p, :, LANES:] = acc

    W = 2 * L
    rr = lax.broadcasted_iota(jnp.int32, (L, W), 0)
    cc = lax.broadcasted_iota(jnp.int32, (L, W), 1)
    incl = rr >= jnp.bitwise_and(cc, L - 1)
    ones_keys = _bd_ones(L)
    bd256 = jnp.concatenate([ones_ch, ones_ch], axis=-1) > 0.5
    gain2 = ng_ref[...]
    gain2 = jnp.concatenate([gain2, gain2], axis=-1)
    ones_l = jnp.ones((L, LANES), F32)
    outs = [[None] * PAIRS for _ in range(bb)]
    st_news = [[None] * PAIRS for _ in range(bb)]
    m_news = [None] * bb
    rows = []
    for i in range(bb):
        li = a_ref[i, :, 4 * ML_W:5 * ML_W] + bi_ref[...]
        lf = -_softplus(-(a_ref[i, :, 5 * ML_W:6 * ML_W] + bf_ref[...]))
        if tv < L:
            valid = lax.broadcasted_iota(jnp.int32, li.shape, 0) < tv
            li = jnp.where(valid, li, NEG)
            lf = jnp.where(valid, lf, 0.0)
        bcum = _scan_rows(lf, L, jnp.add, 0.0)
        a = li - bcum
        m0 = m_ref[i]
        m_t = bcum + jnp.maximum(m0, _scan_rows(a, L, jnp.maximum, NEG))
        e_inter = jnp.exp(bcum + m0 - m_t)
        m_new = m_t[L - 1:L, :]
        b_last = bcum[L - 1:L, :]
        e_c = jnp.exp(b_last + m0 - m_new)
        kw = a_ref[i, :, ML_W:2 * ML_W] * (ML_DH ** -0.5) * jnp.exp(b_last + a - m_new)
        m_news[i] = m_new
        rows.append((a, bcum, m_t, e_inter, e_c, kw))

    def pair(i, p):
        a, bcum, m_t, e_inter, e_c, kw = rows[i]
        sl = slice(LANES * p, LANES * (p + 1))
        q2 = a_ref[i, :, sl]
        k2 = a_ref[i, :, ML_W + LANES * p:ML_W + LANES * (p + 1)] * (ML_DH ** -0.5)
        v2 = a_ref[i, :, 2 * ML_W + LANES * p:2 * ML_W + LANES * (p + 1)]
        og2 = a_ref[i, :, 3 * ML_W + LANES * p:3 * ML_W + LANES * (p + 1)]
        st = st_ref[i, p]
        arow = _row_form(_score_cols(a, p, L), L)
        qk = _dot_nt(q2, _bd_stack(k2))
        qcn = _dot(q2, st)
        upd = _dot_tn(kw[:, sl], jnp.concatenate([v2, ones_l], axis=-1))
        ec2 = e_c[:, sl]
        st_news[i][p] = st * jnp.concatenate([ec2, ec2], axis=-1) + jnp.where(bd256, upd, 0.0)
        yield
        w_intra = jnp.where(incl, jnp.exp(_score_cols(bcum, p, L) + arow - _score_cols(m_t, p, L)), 0.0)
        s2 = qk * w_intra
        nd = _dot(s2, jnp.concatenate([_bd_stack(v2), ones_keys], axis=-1))
        yield
        e2 = e_inter[:, sl]
        num = e2 * qcn[:, 0:LANES] + nd[:, 0:LANES]
        den = e2 * qcn[:, LANES:] + nd[:, LANES:]
        hh = num / jnp.maximum(jnp.abs(den), jnp.exp(-m_t[:, sl]))
        ss = _half_sums(hh * hh, ones_ch)
        yield
        outs[i][p] = hh * lax.rsqrt(ss * (1.0 / ML_DH) + EPS) * gain2 * _sigmoid(og2)

    _interleave([pair(i, p) for i in range(bb) for p in range(PAIRS)])
    for i in range(bb):
        for p in range(PAIRS):
            st_ref[i, p] = st_news[i][p]
        m_ref[i] = m_news[i]
        o_ref[i] = jnp.concatenate(outs[i], axis=-1).astype(o_ref.dtype)

    @pl.when(cidx == nc - 1)
    def _():
        ones8 = jnp.ones((SUBLANES, LANES), BF16)
        for i in range(bb):
            for p in range(PAIRS):
                st = st_news[i][p]
                cn_ref[i, 2 * p] = st[0:HALF, 0:HALF]
                cn_ref[i, 2 * p + 1] = st[HALF:, HALF:LANES]
                dg = jnp.where(diag128, st[:, LANES:], 0.0)
                acc = None
                for piece in _split3(dg):
                    t = jnp.dot(ones8, piece, preferred_element_type=F32)
                    acc = t if acc is None else acc + t
                nn_ref[i, p] = acc[0:1, :]
        mn_ref[...] = m_ref[...]


def _mlstm(slab, c0, n0, m0, bi, bf, ng, L, tv, bb):
    bsz, t, _ = slab.shape
    nc = t // L
    fix2 = lambda b, c: (0, 0)
    c_spec = pl.BlockSpec((bb, ML_HEADS, ML_DH, ML_DH), lambda b, c: (b, 0, 0, 0))
    n_spec = pl.BlockSpec((bb, PAIRS, 1, LANES), lambda b, c: (b, 0, 0, 0))
    m_spec = pl.BlockSpec((bb, 1, ML_W), lambda b, c: (b, 0, 0))
    return pl.pallas_call(
        functools.partial(_mlstm_kernel, L=L, tv=tv, bb=bb),
        grid=(bsz // bb, nc),
        in_specs=[pl.BlockSpec((bb, L, MIX_SLAB_W), lambda b, c: (b, c, 0)),
                  c_spec, n_spec, m_spec,
                  pl.BlockSpec((1, ML_W), fix2), pl.BlockSpec((1, ML_W), fix2),
                  pl.BlockSpec((1, ML_DH), fix2)],
        out_specs=[pl.BlockSpec((bb, L, ML_W), lambda b, c: (b, c, 0)), c_spec, n_spec, m_spec],
        out_shape=[jax.ShapeDtypeStruct((bsz, t, ML_W), BF16),
                   jax.ShapeDtypeStruct((bsz, ML_HEADS, ML_DH, ML_DH), F32),
                   jax.ShapeDtypeStruct((bsz, PAIRS, 1, LANES), F32),
                   jax.ShapeDtypeStruct((bsz, 1, ML_W), F32)],
        scratch_shapes=[pltpu.VMEM((bb, PAIRS, LANES, 2 * LANES), F32),
                        pltpu.VMEM((bb, 1, ML_W), F32)],
        compiler_params=_cparams("parallel", "arbitrary"),
        name="mlstm",
    )(slab, c0, n0, m0, bi, bf, ng)


def _gdn_kernel(a_ref, conv0_ref, s0_ref, cw_ref, alog_ref, dtb_ref, ng_ref,
                o_ref, convn_ref, sn_ref, ext_ref, s_ref, *, L, tv, bb):
    c = pl.program_id(1)
    nc = pl.num_programs(1)

    @pl.when(c == 0)
    def _():
        ext_ref[:, 0:SUBLANES, :] = conv0_ref[...]
        s_ref[...] = jnp.zeros(s_ref.shape, F32)
        for i in range(bb):
            for p in range(PAIRS):
                s_ref[i, p, 0:HALF, 0:HALF] = s0_ref[i, 2 * p]
                s_ref[i, p, HALF:, HALF:] = s0_ref[i, 2 * p + 1]

    W = 2 * L
    rr = lax.broadcasted_iota(jnp.int32, (L, W), 0)
    cc = lax.broadcasted_iota(jnp.int32, (L, W), 1)
    key = jnp.bitwise_and(cc, L - 1)
    incl = rr >= key
    strict = rr > key
    first_half = cc < L
    lo = lax.broadcasted_iota(jnp.int32, (L, LANES), 1) < HALF
    ones_ch = _bd_ones(HALF)
    bd128 = ones_ch > 0.5
    cw = cw_ref[...]
    gain2 = ng_ref[...]
    gain2 = jnp.concatenate([gain2, gain2], axis=-1)
    outs = [[None] * PAIRS for _ in range(bb)]
    s_news = [[None] * PAIRS for _ in range(bb)]
    carries = [None] * bb
    rows = []
    for i in range(bb):
        raw = a_ref[i, :, 0:GDN_CONV_CH]
        ext_ref[i, SUBLANES:SUBLANES + L, :] = raw
        ext = ext_ref[i]
        acc = raw * cw[CONV_W - 1:CONV_W, :]
        for j in range(CONV_W - 1):
            sh = pltpu.roll(ext, CONV_W - 1 - j, axis=0)[SUBLANES:SUBLANES + L]
            acc = acc + sh * cw[j:j + 1, :]
        qkv = _silu(acc)
        carries[i] = pltpu.roll(ext, (L + SUBLANES - tv) % (L + SUBLANES), axis=0)[0:SUBLANES]

        beta = _sigmoid(a_ref[i, :, 4 * GDN_VW:5 * GDN_VW])
        g = -jnp.exp(alog_ref[...]) * _softplus(a_ref[i, :, 5 * GDN_VW:6 * GDN_VW] + dtb_ref[...])
        if tv < L:
            valid = lax.broadcasted_iota(jnp.int32, g.shape, 0) < tv
            beta = jnp.where(valid, beta, 0.0)
            g = jnp.where(valid, g, 0.0)
        gcum = _scan_rows(g, L, jnp.add, 0.0)
        g_last = gcum[L - 1:L, :]
        rows.append((qkv, beta, gcum, jnp.exp(gcum), jnp.exp(g_last - gcum), jnp.exp(g_last)))
    for i in range(bb):
        ext_ref[i, 0:SUBLANES, :] = carries[i]

    def pair(i, p):
        qkv, beta, gcum, eg, kdec, sdec = rows[i]
        sl = slice(LANES * p, LANES * (p + 1))
        q2 = qkv[:, LANES * p:LANES * (p + 1)]
        k2 = qkv[:, GDN_QK + LANES * p:GDN_QK + LANES * (p + 1)]
        v2 = qkv[:, 2 * GDN_QK + LANES * p:2 * GDN_QK + LANES * (p + 1)]
        z2 = a_ref[i, :, GDN_CONV_CH + LANES * p:GDN_CONV_CH + LANES * (p + 1)]
        s2 = s_ref[i, p]
        ssq = _half_sums(q2 * q2, ones_ch)
        ssk = _half_sums(k2 * k2, ones_ch)
        grow = _row_form(_score_cols(gcum, p, L), L)
        yield
        q2 = q2 * lax.rsqrt(ssq + EPS) * (GDN_DK ** -0.5)
        k2 = k2 * lax.rsqrt(ssk + EPS)
        beta2 = beta[:, sl]
        eg2 = eg[:, sl]
        kb2 = k2 * beta2
        kq = _dot_nt(jnp.concatenate([_bd_stack(kb2), q2], axis=0), _bd_stack(k2))
        qs = _dot(q2 * eg2, s2)
        yield
        gam = jnp.where(incl, jnp.exp(jnp.minimum(_score_cols(gcum, p, L) - grow, 0.0)), 0.0)
        gam_s = jnp.where(strict, gam, 0.0)
        gam_bd = jnp.concatenate([jnp.where(first_half, gam_s, 0.0),
                                  jnp.where(first_half, 0.0, gam_s)], axis=0)
        t_inv = yield from _unit_lower_inverse(kq[0:W] * gam_bd, W, L)
        rhs = jnp.concatenate([v2 * beta2, kb2 * eg2], axis=-1)
        sol = _dot(t_inv, jnp.concatenate([rhs, rhs], axis=0))
        yield
        u2 = jnp.where(lo, sol[0:L, 0:LANES], sol[L:W, 0:LANES])
        w2 = jnp.where(lo, sol[0:L, LANES:], sol[L:W, LANES:])
        v_new = u2 - _dot(w2, s2)
        yield
        o2 = qs + _dot(kq[W:] * gam, _bd_stack(v_new))
        upd = _dot_tn(k2 * kdec[:, sl], v_new)
        s_news[i][p] = s2 * sdec[:, sl] + jnp.where(bd128, upd, 0.0)
        yield
        ss = _half_sums(o2 * o2, ones_ch)
        yield
        outs[i][p] = o2 * lax.rsqrt(ss * (1.0 / GDN_DV) + EPS) * gain2 * _silu(z2)

    _interleave([pair(i, p) for i in range(bb) for p in range(PAIRS)])
    for i in range(bb):
        for p in range(PAIRS):
            s_ref[i, p] = s_news[i][p]
        o_ref[i] = jnp.concatenate(outs[i], axis=-1).astype(o_ref.dtype)

    @pl.when(c == nc - 1)
    def _():
        for i in range(bb):
            convn_ref[i] = carries[i]
            for p in range(PAIRS):
                sn_ref[i, 2 * p] = s_news[i][p][0:HALF, 0:HALF]
                sn_ref[i, 2 * p + 1] = s_news[i][p][HALF:, HALF:]


def _gdn(slab, conv0, s0, cw, alog, dtb, ng, L, tv, bb):
    bsz, t, _ = slab.shape
    nc = t // L
    fix2 = lambda b, c: (0, 0)
    s_spec = pl.BlockSpec((bb, GDN_HEADS, GDN_DK, GDN_DV), lambda b, c: (b, 0, 0, 0))
    cv_spec = pl.BlockSpec((bb, SUBLANES, GDN_CONV_CH), lambda b, c: (b, 0, 0))
    return pl.pallas_call(
        functools.partial(_gdn_kernel, L=L, tv=tv, bb=bb),
        grid=(bsz // bb, nc),
        in_specs=[pl.BlockSpec((bb, L, MIX_SLAB_W), lambda b, c: (b, c, 0)),
                  cv_spec, s_spec,
                  pl.BlockSpec((CONV_W, GDN_CONV_CH), fix2),
                  pl.BlockSpec((1, GDN_VW), fix2), pl.BlockSpec((1, GDN_VW), fix2),
                  pl.BlockSpec((1, GDN_DV), fix2)],
        out_specs=[pl.BlockSpec((bb, L, GDN_VW), lambda b, c: (b, c, 0)), cv_spec, s_spec],
        out_shape=[jax.ShapeDtypeStruct((bsz, t, GDN_VW), BF16),
                   jax.ShapeDtypeStruct((bsz, SUBLANES, GDN_CONV_CH), F32),
                   jax.ShapeDtypeStruct((bsz, GDN_HEADS, GDN_DK, GDN_DV), F32)],
        scratch_shapes=[pltpu.VMEM((bb, L + SUBLANES, GDN_CONV_CH), F32),
                        pltpu.VMEM((bb, PAIRS, LANES, LANES), F32)],
        compiler_params=_cparams("parallel", "arbitrary"),
        name="gdn",
    )(slab, conv0, s0, cw, alog, dtb, ng)


def _cmul(ar, ai, br, bi):
    return ar * br - ai * bi, ar * bi + ai * br


def _s5_kernel(u_ref, h0_ref, lb_ref, bw_ref, cw_ref, dsk_ref, wglu_ref,
               o_ref, hn_ref, car_ref, h_ref, *, L, bb):
    c = pl.program_id(1)
    nc = pl.num_programs(1)

    @pl.when(c == 0)
    def _():
        car_ref[...] = h0_ref[...]

    rows = bb * L
    u = u_ref[...].reshape(rows, S5_CH)
    bu = jnp.dot(u.astype(BF16), bw_ref[...], preferred_element_type=F32)
    tiles = rows // SUBLANES
    x_re = bu[:, 0:S5_N].reshape(tiles, SUBLANES, S5_N)
    x_im = bu[:, S5_N:].reshape(tiles, SUBLANES, S5_N)
    p1 = (lb_ref[0:1, :], lb_ref[1:2, :])
    p2 = _cmul(*p1, *p1)
    p4 = _cmul(*p2, *p2)
    p8 = _cmul(*p4, *p4)
    sub = lax.broadcasted_iota(jnp.int32, (SUBLANES, S5_N), 0)
    for d, (pr, pi) in ((1, p1), (2, p2), (4, p4)):
        pr = jnp.where(sub >= d, pr, 0.0)
        pi = jnp.where(sub >= d, pi, 0.0)
        s_re = pltpu.roll(x_re, d, axis=1)
        s_im = pltpu.roll(x_im, d, axis=1)
        x_re, x_im = x_re + (pr * s_re - pi * s_im), x_im + (pr * s_im + pi * s_re)
    x_re = x_re.reshape(rows, S5_N)
    x_im = x_im.reshape(rows, S5_N)
    k = sub + 1
    pw_re = jnp.ones((SUBLANES, S5_N), F32)
    pw_im = jnp.zeros((SUBLANES, S5_N), F32)
    for bit, (pr, pi) in ((1, p1), (2, p2), (4, p4), (8, p8)):
        m_re, m_im = _cmul(pw_re, pw_im, pr, pi)
        on = jnp.bitwise_and(k, bit) != 0
        pw_re = jnp.where(on, m_re, pw_re)
        pw_im = jnp.where(on, m_im, pw_im)
    for i in range(bb):
        car_re = car_ref[i, 0]
        car_im = car_ref[i, 1]
        for t in range(L // SUBLANES):
            r0 = i * L + t * SUBLANES
            a_re, a_im = _cmul(pw_re, pw_im, car_re, car_im)
            t_re = x_re[r0:r0 + SUBLANES, :] + a_re
            t_im = x_im[r0:r0 + SUBLANES, :] + a_im
            h_ref[r0:r0 + SUBLANES, 0:S5_N] = t_re
            h_ref[r0:r0 + SUBLANES, S5_N:] = t_im
            car_re = jnp.broadcast_to(t_re[SUBLANES - 1:SUBLANES, :], (SUBLANES, S5_N))
            car_im = jnp.broadcast_to(t_im[SUBLANES - 1:SUBLANES, :], (SUBLANES, S5_N))
        car_ref[i, 0] = car_re
        car_ref[i, 1] = car_im
        hn_ref[i, 0] = t_re
        hn_ref[i, 1] = t_im
    y = jnp.dot(h_ref[...].astype(BF16), cw_ref[...], preferred_element_type=F32) + dsk_ref[...] * u
    zg = jax.nn.gelu(y)
    out = zg * _sigmoid(jnp.dot(zg.astype(BF16), wglu_ref[...], preferred_element_type=F32))
    o_ref[...] = out.reshape(bb, L, S5_CH).astype(o_ref.dtype)


def _s5(u, h0, lb, bw, cw, dsk, wglu, L, bb):
    bsz, t, _ = u.shape
    nc = t // L
    fix2 = lambda b, c: (0, 0)
    h_spec = pl.BlockSpec((bb, 2, SUBLANES, S5_N), lambda b, c: (b, 0, 0, 0))
    return pl.pallas_call(
        functools.partial(_s5_kernel, L=L, bb=bb),
        grid=(bsz // bb, nc),
        in_specs=[pl.BlockSpec((bb, L, S5_CH), lambda b, c: (b, c, 0)),
                  h_spec,
                  pl.BlockSpec((SUBLANES, S5_N), fix2),
                  pl.BlockSpec((S5_CH, 2 * S5_N), fix2),
                  pl.BlockSpec((2 * S5_N, S5_CH), fix2),
                  pl.BlockSpec((1, S5_CH), fix2),
                  pl.BlockSpec((S5_CH, S5_CH), fix2)],
        out_specs=[pl.BlockSpec((bb, L, S5_CH), lambda b, c: (b, c, 0)), h_spec],
        out_shape=[jax.ShapeDtypeStruct((bsz, t, S5_CH), BF16),
                   jax.ShapeDtypeStruct((bsz, 2, SUBLANES, S5_N), F32)],
        scratch_shapes=[pltpu.VMEM((bb, 2, SUBLANES, S5_N), F32),
                        pltpu.VMEM((bb * L, 2 * S5_N), F32)],
        compiler_params=_cparams("parallel", "arbitrary"),
        name="s5",
    )(u, h0, lb, bw, cw, dsk, wglu)


def _prep_layer(l, P):
    w_in = P["w_in"][l]
    sizes = (GDN_CONV_CH, GDN_VW, GDN_HEADS, GDN_HEADS, S5_CH, 3 * ML_W, ML_HEADS, ML_HEADS, ML_W)
    offs = [0]
    for s in sizes:
        offs.append(offs[-1] + s)
    g_qkv, g_z, g_b, g_a, s_u, m_qkv, m_i, m_f, m_o = [w_in[:, offs[i]:offs[i + 1]] for i in range(9)]
    rep = lambda w: jnp.repeat(w, ML_DH, axis=1)
    w_cat = jnp.concatenate([g_qkv, g_z, rep(g_b), rep(g_a), s_u, m_qkv, m_o, rep(m_i), rep(m_f)],
                            axis=1).astype(BF16)

    lr = P["s5_lam_re"][l].astype(F32)
    li = P["s5_lam_im"][l].astype(F32)
    dt = jnp.exp(P["s5_log_dt"][l].astype(F32))[:, None]
    mag = jnp.exp(lr * dt)
    lb_re = mag * jnp.cos(li * dt)
    lb_im = mag * jnp.sin(li * dt)
    den = lr * lr + li * li
    c_re = ((lb_re - 1.0) * lr + lb_im * li) / den
    c_im = (lb_im * lr - (lb_re - 1.0) * li) / den
    b_r = P["s5_B_re"][l].astype(F32)
    b_i = P["s5_B_im"][l].astype(F32)
    bb_re = c_re[..., None] * b_r - c_im[..., None] * b_i
    bb_im = c_re[..., None] * b_i + c_im[..., None] * b_r
    eye_g = jnp.eye(S5_GROUPS, dtype=F32)
    bd = lambda m: jnp.einsum("gph,gk->ghkp", m, eye_g).reshape(S5_CH, S5_N)
    bw = jnp.concatenate([bd(bb_re), bd(bb_im)], axis=1).astype(BF16)
    cd = lambda m: jnp.einsum("ghp,gk->gpkh", m, eye_g).reshape(S5_N, S5_CH)
    cw = jnp.concatenate([cd(P["s5_C_re"][l].astype(F32)),
                          -cd(P["s5_C_im"][l].astype(F32))], axis=0).astype(BF16)
    lb = jnp.zeros((SUBLANES, S5_N), F32).at[0].set(lb_re.reshape(-1)).at[1].set(lb_im.reshape(-1))

    wr = jnp.zeros((D_MODEL, LANES), F32)
    wr = wr.at[:, 0:N_GROUPS].set(P["w_router_group"][l])
    wr = wr.at[:, N_GROUPS:N_GROUPS + N_EXPERTS].set(P["w_router_expert"][l])
    br = jnp.zeros((1, LANES), F32)
    br = br.at[0, 0:N_GROUPS].set(P["b_router_group"][l])
    br = br.at[0, N_GROUPS:N_GROUPS + N_EXPERTS].set(P["b_router_expert"][l])

    rep_row = lambda v: jnp.repeat(v.astype(F32), ML_DH).reshape(1, ML_W)
    return dict(
        norm_mix=P["norm_mix"][l].reshape(1, D_MODEL).astype(F32),
        w_cat=w_cat,
        w_out=P["w_out"][l].astype(BF16),
        conv_w=P["gdn_conv_w"][l].astype(F32),
        alog=rep_row(P["gdn_A_log"][l]), dtb=rep_row(P["gdn_dt_bias"][l]),
        gdn_norm=P["gdn_norm"][l].reshape(1, GDN_DV).astype(F32),
        lb=lb, bw=bw, cw=cw,
        s5_d=P["s5_D"][l].reshape(1, S5_CH).astype(F32),
        w_glu=P["s5_w_glu"][l].astype(BF16),
        ml_bi=rep_row(P["ml_ig_bias"][l]), ml_bf=rep_row(P["ml_fg_bias"][l]),
        ml_norm=P["ml_norm"][l].reshape(1, ML_DH).astype(F32),
        norm_ffn=P["norm_ffn"][l].reshape(1, D_MODEL).astype(F32),
        wr=wr, br=br,
        wg=P["w_exp_gate"][l].astype(BF16), wu=P["w_exp_up"][l].astype(BF16),
        wd=P["w_exp_down"][l].astype(BF16),
        norm_ple=P["norm_ple"][l].reshape(1, D_MODEL).astype(F32),
        w_ple_gate=P["w_ple_gate"][l].astype(BF16),
        w_ple_proj=P["w_ple_proj"][l].astype(BF16),
    )


def _trunk(x, p, states, layers, final_norm, *, L, tv, Ls, tm, tm_moe, bb, bbs):
    conv0, gdn0, s5re0, s5im0, mc0, mn0, mm0 = states
    bsz, t, _ = x.shape
    m = bsz * t
    h = x.reshape(m, D_MODEL)
    outs = [[] for _ in range(7)]
    last_row = (tv - 1) % SUBLANES
    for l, W in enumerate(layers):
        slab_g, s_u, slab_m = _norm_inproj(h, W["norm_mix"], W["w_cat"],
                                           (MIX_SLAB_W, S5_CH, MIX_SLAB_W), tm)
        conv_in = jnp.pad(conv0[l], ((0, 0), (SUBLANES - (CONV_W - 1), 0), (0, 0)))
        o_gdn, conv_n, gdn_n = _gdn(slab_g.reshape(bsz, t, MIX_SLAB_W), conv_in, gdn0[l],
                                    W["conv_w"], W["alog"], W["dtb"], W["gdn_norm"], L, tv, bb)
        h0 = jnp.stack([s5re0[l].reshape(bsz, S5_N), s5im0[l].reshape(bsz, S5_N)], axis=1)
        h0 = jnp.broadcast_to(h0[:, :, None, :], (bsz, 2, SUBLANES, S5_N))
        o_s5, s5_n = _s5(s_u.reshape(bsz, t, S5_CH), h0, W["lb"], W["bw"], W["cw"], W["s5_d"],
                         W["w_glu"], Ls, bbs)
        o_ml, c_n, n_n, m_n = _mlstm(slab_m.reshape(bsz, t, MIX_SLAB_W), mc0[l],
                                     mn0[l].reshape(bsz, PAIRS, 1, LANES),
                                     jnp.repeat(mm0[l], ML_DH, axis=-1).reshape(bsz, 1, ML_W),
                                     W["ml_bi"], W["ml_bf"], W["ml_norm"], L, tv, bb)
        h1, f, gates = _outproj_router(o_gdn.reshape(m, GDN_VW), o_s5.reshape(m, S5_CH),
                                       o_ml.reshape(m, ML_W), h, W["w_out"], W["norm_ffn"],
                                       W["wr"], W["br"], tm)
        h2 = _moe(f, gates, h1, W["wg"], W["wu"], W["wd"], tm_moe)
        h = _ple(h2, p[l].reshape(m, PLE_DIM), W["norm_ple"], W["w_ple_gate"], W["w_ple_proj"],
                 final_norm, tm, l == len(layers) - 1)
        outs[0].append(conv_n[:, SUBLANES - (CONV_W - 1):])
        outs[1].append(gdn_n)
        outs[2].append(s5_n[:, 0, last_row].reshape(bsz, S5_GROUPS, S5_STATE))
        outs[3].append(s5_n[:, 1, last_row].reshape(bsz, S5_GROUPS, S5_STATE))
        outs[4].append(c_n)
        outs[5].append(n_n.reshape(bsz, ML_HEADS, ML_DH))
        outs[6].append(m_n[:, 0, ::ML_DH])
    return (h.reshape(bsz, t, D_MODEL),) + tuple(jnp.stack(o) for o in outs)


def kernel(x_prompt, x_sample, p_prompt, p_sample, state_gdn_conv, state_gdn, state_s5_re, state_s5_im, state_mlstm_C, state_mlstm_n, state_mlstm_m, norm_mix, w_in, w_out, gdn_conv_w, gdn_A_log, gdn_dt_bias, gdn_norm, s5_lam_re, s5_lam_im, s5_log_dt, s5_B_re, s5_B_im, s5_C_re, s5_C_im, s5_D, s5_w_glu, ml_ig_bias, ml_fg_bias, ml_norm, norm_ffn, w_router_group, b_router_group, w_router_expert, b_router_expert, w_exp_gate, w_exp_up, w_exp_down, norm_ple, w_ple_gate, w_ple_proj, final_norm):
    P = dict(norm_mix=norm_mix, w_in=w_in, w_out=w_out, gdn_conv_w=gdn_conv_w, gdn_A_log=gdn_A_log,
             gdn_dt_bias=gdn_dt_bias, gdn_norm=gdn_norm, s5_lam_re=s5_lam_re, s5_lam_im=s5_lam_im,
             s5_log_dt=s5_log_dt, s5_B_re=s5_B_re, s5_B_im=s5_B_im, s5_C_re=s5_C_re, s5_C_im=s5_C_im,
             s5_D=s5_D, s5_w_glu=s5_w_glu, ml_ig_bias=ml_ig_bias, ml_fg_bias=ml_fg_bias,
             ml_norm=ml_norm, norm_ffn=norm_ffn, w_router_group=w_router_group,
             b_router_group=b_router_group, w_router_expert=w_router_expert,
             b_router_expert=b_router_expert, w_exp_gate=w_exp_gate, w_exp_up=w_exp_up,
             w_exp_down=w_exp_down, norm_ple=norm_ple, w_ple_gate=w_ple_gate, w_ple_proj=w_ple_proj)
    depth = norm_mix.shape[0]
    layers = [_prep_layer(l, P) for l in range(depth)]
    fnorm = final_norm.reshape(1, D_MODEL).astype(F32)

    bp, tp, _ = x_prompt.shape
    zeros = lambda *s: jnp.zeros((depth, bp) + s, F32)
    prompt_init = (zeros(CONV_W - 1, GDN_CONV_CH), zeros(GDN_HEADS, GDN_DK, GDN_DV),
                   zeros(S5_GROUPS, S5_STATE), zeros(S5_GROUPS, S5_STATE),
                   zeros(ML_HEADS, ML_DH, ML_DH), zeros(ML_HEADS, ML_DH), zeros(ML_HEADS))
    lp = math.gcd(tp, 64)
    lsp = math.gcd(tp, 256)
    res_p = _trunk(x_prompt, p_prompt, prompt_init, layers, fnorm,
                   L=lp, tv=lp, Ls=lsp, tm=512, tm_moe=1024, bb=2, bbs=1)

    bs, ts, _ = x_sample.shape
    tpad = -(-ts // SUBLANES) * SUBLANES
    xs = jnp.pad(x_sample, ((0, 0), (0, tpad - ts), (0, 0)))
    ps = jnp.pad(p_sample, ((0, 0), (0, 0), (0, tpad - ts), (0, 0)))
    sample_init = (state_gdn_conv, state_gdn, state_s5_re, state_s5_im,
                   state_mlstm_C, state_mlstm_n, state_mlstm_m)
    res_s = _trunk(xs, ps, sample_init, layers, fnorm,
                   L=tpad, tv=ts, Ls=tpad, tm=512, tm_moe=1024, bb=4, bbs=8)
    y_sample = res_s[0][:, :ts]
    return (res_p[0], y_sample) + res_p[1:] + res_s[1:]
```

```python
import functools
import math

import jax
import jax.numpy as jnp
from jax import lax
from jax.experimental import pallas as pl
from jax.experimental.pallas import tpu as pltpu

F32 = jnp.float32
BF16 = jnp.bfloat16

D_MODEL = 1024
DEPTH = 2
GDN_HEADS = 6
GDN_DK = 64
GDN_DV = 64
GDN_QK = GDN_HEADS * GDN_DK
GDN_VW = GDN_HEADS * GDN_DV
GDN_CONV_CH = 2 * GDN_QK + GDN_VW
CONV_W = 4
S5_GROUPS = 16
S5_GROUP_CH = 16
S5_CH = S5_GROUPS * S5_GROUP_CH
S5_STATE = 64
S5_N = S5_GROUPS * S5_STATE
ML_HEADS = 6
ML_DH = 64
ML_W = ML_HEADS * ML_DH
N_GROUPS = 4
EXPERTS_PER_GROUP = 4
N_EXPERTS = N_GROUPS * EXPERTS_PER_GROUP
D_EXPERT = 256
PLE_DIM = 256
EPS = 1e-6

LANES = 128
SUBLANES = 8
NEG = -1e30
VMEM_LIMIT = 56 * 1024 * 1024

def _cparams(*sem):
    return pltpu.CompilerParams(dimension_semantics=sem, vmem_limit_bytes=VMEM_LIMIT)


def _dot(a, b):
    return jnp.dot(a.astype(BF16), b.astype(BF16), preferred_element_type=F32)


def _dot_nt(a, b):
    return lax.dot_general(a.astype(BF16), b.astype(BF16), (((1,), (1,)), ((), ())),
                           preferred_element_type=F32)


def _dot_tn(a, b):
    return lax.dot_general(a.astype(BF16), b.astype(BF16), (((0,), (0,)), ((), ())),
                           preferred_element_type=F32)


def _split_bf16(a):
    hi = a.astype(BF16)
    lo = (a - hi.astype(F32)).astype(BF16)
    return hi, lo


def _rms(x, gain):
    return x * lax.rsqrt(jnp.mean(x * x, axis=-1, keepdims=True) + EPS) * gain


def _softplus(x):
    return jnp.maximum(x, 0.0) + jnp.log(1.0 + jnp.exp(-jnp.abs(x)))


def _sigmoid(x):
    return 1.0 / (1.0 + jnp.exp(-x))


def _silu(x):
    return x * _sigmoid(x)


def _interleave(gens):
    live = list(gens)
    while live:
        still = []
        for g in live:
            try:
                next(g)
                still.append(g)
            except StopIteration:
                pass
        live = still


def _unit_lower_inverse(n_mat, size, top=None):
    top = size if top is None else top
    r = lax.broadcasted_iota(jnp.int32, (size, size), 0)
    c = lax.broadcasted_iota(jnp.int32, (size, size), 1)
    base = min(16, top)
    same = jnp.bitwise_xor(r, c) < base
    nd = jnp.where(same, n_mat, 0.0)
    eye = jnp.where(r == c, 1.0, 0.0).astype(F32)
    t = eye - nd
    x = nd
    p = 1
    while 2 * p < base:
        x = _dot(x, x)
        yield
        t = t + _dot(t, x)
        yield
        p *= 2
    blk = base
    while blk < top:
        pair = jnp.bitwise_xor(r, c)
        off = jnp.where((pair < 2 * blk) & (pair >= blk), n_mat, 0.0)
        ot = _dot(off, t)
        yield
        t = t - _dot(t, ot)
        yield
        blk *= 2
    return t


def _norm_inproj_kernel(x_ref, g_ref, w_ref, *out_refs, widths):
    u = _rms(x_ref[...], g_ref[...]).astype(BF16)
    off = 0
    for o_ref, wd in zip(out_refs, widths):
        o_ref[...] = jnp.dot(u, w_ref[:, off:off + wd], preferred_element_type=F32)
        off += wd


def _norm_inproj(x, gain, w, widths, tm):
    m = x.shape[0]
    assert sum(widths) == w.shape[1]
    return pl.pallas_call(
        functools.partial(_norm_inproj_kernel, widths=widths),
        grid=(m // tm,),
        in_specs=[pl.BlockSpec((tm, D_MODEL), lambda i: (i, 0)),
                  pl.BlockSpec((1, D_MODEL), lambda i: (0, 0)),
                  pl.BlockSpec((D_MODEL, w.shape[1]), lambda i: (0, 0))],
        out_specs=[pl.BlockSpec((tm, wd), lambda i: (i, 0)) for wd in widths],
        out_shape=[jax.ShapeDtypeStruct((m, wd), F32) for wd in widths],
        compiler_params=_cparams("parallel"),
        name="norm_inproj",
    )(x, gain, w)


def _route(f, wr, br):
    fh, fl = _split_bf16(f)
    wh, wl = _split_bf16(wr)
    d = functools.partial(jnp.dot, preferred_element_type=F32)
    logits = d(fh, wh) + (d(fh, wl) + d(fl, wh)) + br
    lane = lax.broadcasted_iota(jnp.int32, logits.shape, 1)
    is_g = lane < N_GROUPS
    gl = jnp.where(is_g, logits, NEG)
    gmax = jnp.max(gl, axis=-1, keepdims=True)
    ge = jnp.where(is_g, jnp.exp(gl - gmax), 0.0)
    p_grp = ge / jnp.sum(ge, axis=-1, keepdims=True)
    g_prob = jnp.max(p_grp, axis=-1, keepdims=True)
    g_idx = jnp.min(jnp.where(is_g & (gl == gmax), lane, LANES), axis=-1, keepdims=True)
    e_lane = lane - N_GROUPS
    is_e = (e_lane >= 0) & (e_lane < N_EXPERTS) & (jnp.right_shift(e_lane, 2) == g_idx)
    le = jnp.where(is_e, logits, NEG)
    m1 = jnp.max(le, axis=-1, keepdims=True)
    i1 = jnp.min(jnp.where(is_e & (le == m1), lane, LANES), axis=-1, keepdims=True)
    is_e2 = is_e & (lane != i1)
    le2 = jnp.where(is_e2, logits, NEG)
    m2 = jnp.max(le2, axis=-1, keepdims=True)
    i2 = jnp.min(jnp.where(is_e2 & (le2 == m2), lane, LANES), axis=-1, keepdims=True)
    e2 = jnp.exp(m2 - m1)
    w1 = g_prob / (1.0 + e2)
    w2 = g_prob * e2 / (1.0 + e2)
    return fh, jnp.where(lane == i1, w1, 0.0) + jnp.where(lane == i2, w2, 0.0)


def _ffn_kernel(og_ref, os_ref, om_ref, h_ref, p_ref, wo_ref, nf_ref, wr_ref, br_ref,
                wg_ref, wu_ref, wd_ref, np_ref, wpg_ref, wpp_ref, fn_ref,
                out_ref, f_ref, gates_ref, acc_ref, *, final):
    gi = pl.program_id(1)

    @pl.when(gi == 0)
    def _():
        h1 = h_ref[...]
        h1 = h1 + jnp.dot(og_ref[...], wo_ref[0:GDN_VW, :], preferred_element_type=F32)
        h1 = h1 + jnp.dot(os_ref[...], wo_ref[GDN_VW:GDN_VW + S5_CH, :], preferred_element_type=F32)
        h1 = h1 + jnp.dot(om_ref[...], wo_ref[GDN_VW + S5_CH:, :], preferred_element_type=F32)
        acc_ref[...] = h1
        fh, gates = _route(_rms(h1, nf_ref[...]), wr_ref[...], br_ref[...])
        f_ref[...] = fh
        gates_ref[...] = gates

    x = f_ref[...]
    gates = gates_ref[...]
    lane = lax.broadcasted_iota(jnp.int32, gates.shape, 1)
    base = N_GROUPS + EXPERTS_PER_GROUP * gi
    acc = None
    for j in range(EXPERTS_PER_GROUP):
        gcol = jnp.sum(jnp.where(lane == base + j, gates, 0.0), axis=-1, keepdims=True)
        hg = jnp.dot(x, wg_ref[j], preferred_element_type=F32)
        hu = jnp.dot(x, wu_ref[j], preferred_element_type=F32)
        hidden = (_silu(hg) * hu * gcol).astype(BF16)
        t = jnp.dot(hidden, wd_ref[j], preferred_element_type=F32)
        acc = t if acc is None else acc + t
    acc_ref[...] += acc

    @pl.when(gi == N_GROUPS - 1)
    def _():
        h = acc_ref[...]
        gate = _sigmoid(jnp.dot(_rms(h, np_ref[...]).astype(BF16), wpg_ref[...],
                                preferred_element_type=F32))
        proj = jnp.dot(p_ref[0].astype(BF16), wpp_ref[...], preferred_element_type=F32)
        h = h + proj * gate
        if final:
            h = _rms(h, fn_ref[...])
        out_ref[...] = h


def _ffn(og, os_, om, h, p_all, layer, W, fn, tm, final):
    m = h.shape[0]
    row = lambda i, g: (i, 0)
    fix = lambda i, g: (0, 0)
    wsel = lambda i, g: (g, 0, 0)
    e = EXPERTS_PER_GROUP
    return pl.pallas_call(
        functools.partial(_ffn_kernel, final=final),
        grid=(m // tm, N_GROUPS),
        in_specs=[pl.BlockSpec((tm, GDN_VW), row), pl.BlockSpec((tm, S5_CH), row),
                  pl.BlockSpec((tm, ML_W), row), pl.BlockSpec((tm, D_MODEL), row),
                  pl.BlockSpec((1, tm, PLE_DIM), lambda i, g: (layer, i, 0)),
                  pl.BlockSpec((D_MODEL, D_MODEL), fix), pl.BlockSpec((1, D_MODEL), fix),
                  pl.BlockSpec((D_MODEL, LANES), fix), pl.BlockSpec((1, LANES), fix),
                  pl.BlockSpec((e, D_MODEL, D_EXPERT), wsel),
                  pl.BlockSpec((e, D_MODEL, D_EXPERT), wsel),
                  pl.BlockSpec((e, D_EXPERT, D_MODEL), wsel),
                  pl.BlockSpec((1, D_MODEL), fix), pl.BlockSpec((D_MODEL, D_MODEL), fix),
                  pl.BlockSpec((PLE_DIM, D_MODEL), fix), pl.BlockSpec((1, D_MODEL), fix)],
        out_specs=pl.BlockSpec((tm, D_MODEL), row),
        out_shape=jax.ShapeDtypeStruct((m, D_MODEL), F32),
        scratch_shapes=[pltpu.VMEM((tm, D_MODEL), BF16), pltpu.VMEM((tm, LANES), F32),
                        pltpu.VMEM((tm, D_MODEL), F32)],
        compiler_params=_cparams("parallel", "arbitrary"),
        name="ffn",
    )(og, os_, om, h, p_all, W["w_out"], W["norm_ffn"], W["wr"], W["br"],
      W["wg"], W["wu"], W["wd"], W["norm_ple"], W["w_ple_gate"], W["w_ple_proj"], fn)


MIX_SLAB_W = 6 * ML_W
PAIRS = ML_HEADS // 2
HALF = LANES // 2


def _scan_rows(x, size, op, fill):
    row = lax.broadcasted_iota(jnp.int32, x.shape, 0)
    d = 1
    while d < size:
        x = op(x, jnp.where(row >= d, pltpu.roll(x, d, axis=0), fill))
        d *= 2
    return x


def _split3(a):
    hi = a.astype(BF16)
    r1 = a - hi.astype(F32)
    mid = r1.astype(BF16)
    lo = (r1 - mid.astype(F32)).astype(BF16)
    return hi, mid, lo


def _row_form(x_s, L):
    rr = lax.broadcasted_iota(jnp.int32, x_s.shape, 0)
    cc = lax.broadcasted_iota(jnp.int32, x_s.shape, 1)
    dg = jnp.where(rr == jnp.bitwise_and(cc, L - 1), x_s, 0.0)
    ones = jnp.ones((SUBLANES, L), BF16)
    acc = None
    for piece in _split3(dg):
        t = jnp.dot(ones, piece, preferred_element_type=F32)
        acc = t if acc is None else acc + t
    return acc[0:1, :]


def _score_cols(x, p, L):
    if 2 * L == LANES:
        return x[:, LANES * p:LANES * (p + 1)]
    return jnp.concatenate([x[:, LANES * p:LANES * p + L],
                            x[:, LANES * p + HALF:LANES * p + HALF + L]], axis=-1)


def _bd_stack(x2):
    lo = lax.broadcasted_iota(jnp.int32, x2.shape, 1) < HALF
    return jnp.concatenate([jnp.where(lo, x2, 0.0), jnp.where(lo, 0.0, x2)], axis=0)


def _bd_ones(rows_per_half):
    shape = (2 * rows_per_half, LANES)
    r = lax.broadcasted_iota(jnp.int32, shape, 0) < rows_per_half
    c = lax.broadcasted_iota(jnp.int32, shape, 1) < HALF
    return jnp.where(r == c, 1.0, 0.0).astype(F32)


def _half_sums(x2, ones_bd):
    return _dot(x2, ones_bd)


def _mlstm_kernel(a_ref, c0_ref, n0_ref, m0_ref, bi_ref, bf_ref, ng_ref,
                  o_ref, cn_ref, nn_ref, mn_ref, st_ref, m_ref, *, L, tv, bb):
    cidx = pl.program_id(1)
    nc = pl.num_programs(1)
    ones_ch = _bd_ones(HALF)
    r128 = lax.broadcasted_iota(jnp.int32, (LANES, LANES), 0)
    c128 = lax.broadcasted_iota(jnp.int32, (LANES, LANES), 1)
    diag128 = r128 == c128

    @pl.when(cidx == 0)
    def _():
        st_ref[...] = jnp.zeros(st_ref.shape, F32)
        m_ref[...] = m0_ref[...]
        ob = ones_ch.astype(BF16)
        for i in range(bb):
            for p in range(PAIRS):
                st_ref[i, p, 0:HALF, 0:HALF] = c0_ref[i, 2 * p]
                st_ref[i, p, HALF:, HALF:LANES] = c0_ref[i, 2 * p + 1]
                dg = jnp.where(diag128, n0_ref[i, p], 0.0)
                acc = None
                for piece in _split3(dg):
                    t = jnp.dot(piece, ob, preferred_element_type=F32)
                    acc = t if acc is None else acc + t
                st_ref[i, p, :, LANES:] = acc

    W = 2 * L
    rr = lax.broadcasted_iota(jnp.int32, (L, W), 0)
    cc = lax.broadcasted_iota(jnp.int32, (L, W), 1)
    incl = rr >= jnp.bitwise_and(cc, L - 1)
    ones_keys = _bd_ones(L)
    bd256 = jnp.concatenate([ones_ch, ones_ch], axis=-1) > 0.5
    gain2 = ng_ref[...]
    gain2 = jnp.concatenate([gain2, gain2], axis=-1)
    ones_l = jnp.ones((L, LANES), F32)
    outs = [[None] * PAIRS for _ in range(bb)]
    st_news = [[None] * PAIRS for _ in range(bb)]
    m_news = [None] * bb
    rows = []
    for i in range(bb):
        li = a_ref[i, :, 4 * ML_W:5 * ML_W] + bi_ref[...]
        lf = -_softplus(-(a_ref[i, :, 5 * ML_W:6 * ML_W] + bf_ref[...]))
        if tv < L:
            valid = lax.broadcasted_iota(jnp.int32, li.shape, 0) < tv
            li = jnp.where(valid, li, NEG)
            lf = jnp.where(valid, lf, 0.0)
        bcum = _scan_rows(lf, L, jnp.add, 0.0)
        a = li - bcum
        m0 = m_ref[i]
        m_t = bcum + jnp.maximum(m0, _scan_rows(a, L, jnp.maximum, NEG))
        e_inter = jnp.exp(bcum + m0 - m_t)
        m_new = m_t[L - 1:L, :]
        b_last = bcum[L - 1:L, :]
        e_c = jnp.exp(b_last + m0 - m_new)
        kw = a_ref[i, :, ML_W:2 * ML_W] * (ML_DH ** -0.5) * jnp.exp(b_last + a - m_new)
        m_news[i] = m_new
        rows.append((a, bcum, m_t, e_inter, e_c, kw))

    def pair(i, p):
        a, bcum, m_t, e_inter, e_c, kw = rows[i]
        sl = slice(LANES * p, LANES * (p + 1))
        q2 = a_ref[i, :, sl]
        k2 = a_ref[i, :, ML_W + LANES * p:ML_W + LANES * (p + 1)] * (ML_DH ** -0.5)
        v2 = a_ref[i, :, 2 * ML_W + LANES * p:2 * ML_W + LANES * (p + 1)]
        og2 = a_ref[i, :, 3 * ML_W + LANES * p:3 * ML_W + LANES * (p + 1)]
        st = st_ref[i, p]
        arow = _row_form(_score_cols(a, p, L), L)
        qk = _dot_nt(q2, _bd_stack(k2))
        qcn = _dot(q2, st)
        upd = _dot_tn(kw[:, sl], jnp.concatenate([v2, ones_l], axis=-1))
        ec2 = e_c[:, sl]
        st_news[i][p] = st * jnp.concatenate([ec2, ec2], axis=-1) + jnp.where(bd256, upd, 0.0)
        yield
        w_intra = jnp.where(incl, jnp.exp(_score_cols(bcum, p, L) + arow - _score_cols(m_t, p, L)), 0.0)
        s2 = qk * w_intra
        nd = _dot(s2, jnp.concatenate([_bd_stack(v2), ones_keys], axis=-1))
        yield
        e2 = e_inter[:, sl]
        num = e2 * qcn[:, 0:LANES] + nd[:, 0:LANES]
        den = e2 * qcn[:, LANES:] + nd[:, LANES:]
        hh = num / jnp.maximum(jnp.abs(den), jnp.exp(-m_t[:, sl]))
        ss = _half_sums(hh * hh, ones_ch)
        yield
        outs[i][p] = hh * lax.rsqrt(ss * (1.0 / ML_DH) + EPS) * gain2 * _sigmoid(og2)

    _interleave([pair(i, p) for i in range(bb) for p in range(PAIRS)])
    for i in range(bb):
        for p in range(PAIRS):
            st_ref[i, p] = st_news[i][p]
        m_ref[i] = m_news[i]
        o_ref[i] = jnp.concatenate(outs[i], axis=-1).astype(o_ref.dtype)

    @pl.when(cidx == nc - 1)
    def _():
        ones8 = jnp.ones((SUBLANES, LANES), BF16)
        for i in range(bb):
            for p in range(PAIRS):
                st = st_news[i][p]
                cn_ref[i, 2 * p] = st[0:HALF, 0:HALF]
                cn_ref[i, 2 * p + 1] = st[HALF:, HALF:LANES]
                dg = jnp.where(diag128, st[:, LANES:], 0.0)
                acc = None
                for piece in _split3(dg):
                    t = jnp.dot(ones8, piece, preferred_element_type=F32)
                    acc = t if acc is None else acc + t
                nn_ref[i, p] = acc[0:1, :]
        mn_ref[...] = m_ref[...]


def _mlstm(slab, c_all, layer, n0, m0, bi, bf, ng, L, tv, bb):
    bsz, t, _ = slab.shape
    nc = t // L
    fix2 = lambda b, c: (0, 0)
    c_spec = pl.BlockSpec((None, bb, ML_HEADS, ML_DH, ML_DH), lambda b, c: (layer, b, 0, 0, 0))
    n_spec = pl.BlockSpec((bb, PAIRS, 1, LANES), lambda b, c: (b, 0, 0, 0))
    m_spec = pl.BlockSpec((bb, 1, ML_W), lambda b, c: (b, 0, 0))
    return pl.pallas_call(
        functools.partial(_mlstm_kernel, L=L, tv=tv, bb=bb),
        grid=(bsz // bb, nc),
        input_output_aliases={1: 1},
        in_specs=[pl.BlockSpec((bb, L, MIX_SLAB_W), lambda b, c: (b, c, 0)),
                  c_spec, n_spec, m_spec,
                  pl.BlockSpec((1, ML_W), fix2), pl.BlockSpec((1, ML_W), fix2),
                  pl.BlockSpec((1, ML_DH), fix2)],
        out_specs=[pl.BlockSpec((bb, L, ML_W), lambda b, c: (b, c, 0)), c_spec, n_spec, m_spec],
        out_shape=[jax.ShapeDtypeStruct((bsz, t, ML_W), BF16),
                   jax.ShapeDtypeStruct(c_all.shape, F32),
                   jax.ShapeDtypeStruct((bsz, PAIRS, 1, LANES), F32),
                   jax.ShapeDtypeStruct((bsz, 1, ML_W), F32)],
        scratch_shapes=[pltpu.VMEM((bb, PAIRS, LANES, 2 * LANES), F32),
                        pltpu.VMEM((bb, 1, ML_W), F32)],
        compiler_params=_cparams("parallel", "arbitrary"),
        name="mlstm",
    )(slab, c_all, n0, m0, bi, bf, ng)


def _gdn_kernel(a_ref, conv0_ref, s0_ref, cw_ref, alog_ref, dtb_ref, ng_ref,
                o_ref, convn_ref, sn_ref, ext_ref, s_ref, *, L, tv, bb):
    c = pl.program_id(1)
    nc = pl.num_programs(1)

    @pl.when(c == 0)
    def _():
        ext_ref[:, 0:SUBLANES, :] = conv0_ref[...]
        s_ref[...] = jnp.zeros(s_ref.shape, F32)
        for i in range(bb):
            for p in range(PAIRS):
                s_ref[i, p, 0:HALF, 0:HALF] = s0_ref[i, 2 * p]
                s_ref[i, p, HALF:, HALF:] = s0_ref[i, 2 * p + 1]

    W = 2 * L
    rr = lax.broadcasted_iota(jnp.int32, (L, W), 0)
    cc = lax.broadcasted_iota(jnp.int32, (L, W), 1)
    key = jnp.bitwise_and(cc, L - 1)
    incl = rr >= key
    strict = rr > key
    first_half = cc < L
    lo = lax.broadcasted_iota(jnp.int32, (L, LANES), 1) < HALF
    ones_ch = _bd_ones(HALF)
    bd128 = ones_ch > 0.5
    cw = cw_ref[...]
    gain2 = ng_ref[...]
    gain2 = jnp.concatenate([gain2, gain2], axis=-1)
    outs = [[None] * PAIRS for _ in range(bb)]
    s_news = [[None] * PAIRS for _ in range(bb)]
    carries = [None] * bb
    rows = []
    for i in range(bb):
        raw = a_ref[i, :, 0:GDN_CONV_CH]
        ext_ref[i, SUBLANES:SUBLANES + L, :] = raw
        ext = ext_ref[i]
        acc = raw * cw[CONV_W - 1:CONV_W, :]
        for j in range(CONV_W - 1):
            sh = pltpu.roll(ext, CONV_W - 1 - j, axis=0)[SUBLANES:SUBLANES + L]
            acc = acc + sh * cw[j:j + 1, :]
        qkv = _silu(acc)
        carries[i] = pltpu.roll(ext, (L + SUBLANES - tv) % (L + SUBLANES), axis=0)[0:SUBLANES]

        beta = _sigmoid(a_ref[i, :, 4 * GDN_VW:5 * GDN_VW])
        g = -jnp.exp(alog_ref[...]) * _softplus(a_ref[i, :, 5 * GDN_VW:6 * GDN_VW] + dtb_ref[...])
        if tv < L:
            valid = lax.broadcasted_iota(jnp.int32, g.shape, 0) < tv
            beta = jnp.where(valid, beta, 0.0)
            g = jnp.where(valid, g, 0.0)
        gcum = _scan_rows(g, L, jnp.add, 0.0)
        g_last = gcum[L - 1:L, :]
        rows.append((qkv, beta, gcum, jnp.exp(gcum), jnp.exp(g_last - gcum), jnp.exp(g_last)))
    for i in range(bb):
        ext_ref[i, 0:SUBLANES, :] = carries[i]

    def pair(i, p):
        qkv, beta, gcum, eg, kdec, sdec = rows[i]
        sl = slice(LANES * p, LANES * (p + 1))
        q2 = qkv[:, LANES * p:LANES * (p + 1)]
        k2 = qkv[:, GDN_QK + LANES * p:GDN_QK + LANES * (p + 1)]
        v2 = qkv[:, 2 * GDN_QK + LANES * p:2 * GDN_QK + LANES * (p + 1)]
        z2 = a_ref[i, :, GDN_CONV_CH + LANES * p:GDN_CONV_CH + LANES * (p + 1)]
        s2 = s_ref[i, p]
        ssq = _half_sums(q2 * q2, ones_ch)
        ssk = _half_sums(k2 * k2, ones_ch)
        grow = _row_form(_score_cols(gcum, p, L), L)
        yield
        q2 = q2 * lax.rsqrt(ssq + EPS) * (GDN_DK ** -0.5)
        k2 = k2 * lax.rsqrt(ssk + EPS)
        beta2 = beta[:, sl]
        eg2 = eg[:, sl]
        kb2 = k2 * beta2
        kq = _dot_nt(jnp.concatenate([_bd_stack(kb2), q2], axis=0), _bd_stack(k2))
        qs = _dot(q2 * eg2, s2)
        yield
        gam = jnp.where(incl, jnp.exp(jnp.minimum(_score_cols(gcum, p, L) - grow, 0.0)), 0.0)
        gam_s = jnp.where(strict, gam, 0.0)
        gam_bd = jnp.concatenate([jnp.where(first_half, gam_s, 0.0),
                                  jnp.where(first_half, 0.0, gam_s)], axis=0)
        t_inv = yield from _unit_lower_inverse(kq[0:W] * gam_bd, W, L)
        rhs = jnp.concatenate([v2 * beta2, kb2 * eg2], axis=-1)
        sol = _dot(t_inv, jnp.concatenate([rhs, rhs], axis=0))
        yield
        u2 = jnp.where(lo, sol[0:L, 0:LANES], sol[L:W, 0:LANES])
        w2 = jnp.where(lo, sol[0:L, LANES:], sol[L:W, LANES:])
        v_new = u2 - _dot(w2, s2)
        yield
        o2 = qs + _dot(kq[W:] * gam, _bd_stack(v_new))
        upd = _dot_tn(k2 * kdec[:, sl], v_new)
        s_news[i][p] = s2 * sdec[:, sl] + jnp.where(bd128, upd, 0.0)
        yield
        ss = _half_sums(o2 * o2, ones_ch)
        yield
        outs[i][p] = o2 * lax.rsqrt(ss * (1.0 / GDN_DV) + EPS) * gain2 * _silu(z2)

    _interleave([pair(i, p) for i in range(bb) for p in range(PAIRS)])
    for i in range(bb):
        for p in range(PAIRS):
            s_ref[i, p] = s_news[i][p]
        o_ref[i] = jnp.concatenate(outs[i], axis=-1).astype(o_ref.dtype)

    @pl.when(c == nc - 1)
    def _():
        for i in range(bb):
            convn_ref[i] = carries[i]
            for p in range(PAIRS):
                sn_ref[i, 2 * p] = s_news[i][p][0:HALF, 0:HALF]
                sn_ref[i, 2 * p + 1] = s_news[i][p][HALF:, HALF:]


def _gdn(slab, conv0, s_all, layer, cw, alog, dtb, ng, L, tv, bb):
    bsz, t, _ = slab.shape
    nc = t // L
    fix2 = lambda b, c: (0, 0)
    s_spec = pl.BlockSpec((None, bb, GDN_HEADS, GDN_DK, GDN_DV), lambda b, c: (layer, b, 0, 0, 0))
    cv_spec = pl.BlockSpec((bb, SUBLANES, GDN_CONV_CH), lambda b, c: (b, 0, 0))
    return pl.pallas_call(
        functools.partial(_gdn_kernel, L=L, tv=tv, bb=bb),
        grid=(bsz // bb, nc),
        input_output_aliases={2: 2},
        in_specs=[pl.BlockSpec((bb, L, MIX_SLAB_W), lambda b, c: (b, c, 0)),
                  cv_spec, s_spec,
                  pl.BlockSpec((CONV_W, GDN_CONV_CH), fix2),
                  pl.BlockSpec((1, GDN_VW), fix2), pl.BlockSpec((1, GDN_VW), fix2),
                  pl.BlockSpec((1, GDN_DV), fix2)],
        out_specs=[pl.BlockSpec((bb, L, GDN_VW), lambda b, c: (b, c, 0)), cv_spec, s_spec],
        out_shape=[jax.ShapeDtypeStruct((bsz, t, GDN_VW), BF16),
                   jax.ShapeDtypeStruct((bsz, SUBLANES, GDN_CONV_CH), F32),
                   jax.ShapeDtypeStruct(s_all.shape, F32)],
        scratch_shapes=[pltpu.VMEM((bb, L + SUBLANES, GDN_CONV_CH), F32),
                        pltpu.VMEM((bb, PAIRS, LANES, LANES), F32)],
        compiler_params=_cparams("parallel", "arbitrary"),
        name="gdn",
    )(slab, conv0, s_all, cw, alog, dtb, ng)


def _cmul(ar, ai, br, bi):
    return ar * br - ai * bi, ar * bi + ai * br


def _s5_kernel(u_ref, h0_ref, lb_ref, bw_ref, cw_ref, dsk_ref, wglu_ref,
               o_ref, hn_ref, car_ref, h_ref, *, L, bb):
    c = pl.program_id(1)
    nc = pl.num_programs(1)

    @pl.when(c == 0)
    def _():
        car_ref[...] = h0_ref[...]

    rows = bb * L
    u = u_ref[...].reshape(rows, S5_CH)
    bu = jnp.dot(u.astype(BF16), bw_ref[...], preferred_element_type=F32)
    tiles = rows // SUBLANES
    x_re = bu[:, 0:S5_N].reshape(tiles, SUBLANES, S5_N)
    x_im = bu[:, S5_N:].reshape(tiles, SUBLANES, S5_N)
    p1 = (lb_ref[0:1, :], lb_ref[1:2, :])
    p2 = _cmul(*p1, *p1)
    p4 = _cmul(*p2, *p2)
    p8 = _cmul(*p4, *p4)
    sub = lax.broadcasted_iota(jnp.int32, (SUBLANES, S5_N), 0)
    for d, (pr, pi) in ((1, p1), (2, p2), (4, p4)):
        pr = jnp.where(sub >= d, pr, 0.0)
        pi = jnp.where(sub >= d, pi, 0.0)
        s_re = pltpu.roll(x_re, d, axis=1)
        s_im = pltpu.roll(x_im, d, axis=1)
        x_re, x_im = x_re + (pr * s_re - pi * s_im), x_im + (pr * s_im + pi * s_re)
    x_re = x_re.reshape(rows, S5_N)
    x_im = x_im.reshape(rows, S5_N)
    k = sub + 1
    pw_re = jnp.ones((SUBLANES, S5_N), F32)
    pw_im = jnp.zeros((SUBLANES, S5_N), F32)
    for bit, (pr, pi) in ((1, p1), (2, p2), (4, p4), (8, p8)):
        m_re, m_im = _cmul(pw_re, pw_im, pr, pi)
        on = jnp.bitwise_and(k, bit) != 0
        pw_re = jnp.where(on, m_re, pw_re)
        pw_im = jnp.where(on, m_im, pw_im)
    for i in range(bb):
        car_re = car_ref[i, 0]
        car_im = car_ref[i, 1]
        for t in range(L // SUBLANES):
            r0 = i * L + t * SUBLANES
            a_re, a_im = _cmul(pw_re, pw_im, car_re, car_im)
            t_re = x_re[r0:r0 + SUBLANES, :] + a_re
            t_im = x_im[r0:r0 + SUBLANES, :] + a_im
            h_ref[r0:r0 + SUBLANES, 0:S5_N] = t_re
            h_ref[r0:r0 + SUBLANES, S5_N:] = t_im
            car_re = jnp.broadcast_to(t_re[SUBLANES - 1:SUBLANES, :], (SUBLANES, S5_N))
            car_im = jnp.broadcast_to(t_im[SUBLANES - 1:SUBLANES, :], (SUBLANES, S5_N))
        car_ref[i, 0] = car_re
        car_ref[i, 1] = car_im
        hn_ref[i, 0] = t_re
        hn_ref[i, 1] = t_im
    y = jnp.dot(h_ref[...].astype(BF16), cw_ref[...], preferred_element_type=F32) + dsk_ref[...] * u
    zg = jax.nn.gelu(y)
    out = zg * _sigmoid(jnp.dot(zg.astype(BF16), wglu_ref[...], preferred_element_type=F32))
    o_ref[...] = out.reshape(bb, L, S5_CH).astype(o_ref.dtype)


def _s5(u, h0, lb, bw, cw, dsk, wglu, L, bb):
    bsz, t, _ = u.shape
    nc = t // L
    fix2 = lambda b, c: (0, 0)
    h_spec = pl.BlockSpec((bb, 2, SUBLANES, S5_N), lambda b, c: (b, 0, 0, 0))
    return pl.pallas_call(
        functools.partial(_s5_kernel, L=L, bb=bb),
        grid=(bsz // bb, nc),
        in_specs=[pl.BlockSpec((bb, L, S5_CH), lambda b, c: (b, c, 0)),
                  h_spec,
                  pl.BlockSpec((SUBLANES, S5_N), fix2),
                  pl.BlockSpec((S5_CH, 2 * S5_N), fix2),
                  pl.BlockSpec((2 * S5_N, S5_CH), fix2),
                  pl.BlockSpec((1, S5_CH), fix2),
                  pl.BlockSpec((S5_CH, S5_CH), fix2)],
        out_specs=[pl.BlockSpec((bb, L, S5_CH), lambda b, c: (b, c, 0)), h_spec],
        out_shape=[jax.ShapeDtypeStruct((bsz, t, S5_CH), BF16),
                   jax.ShapeDtypeStruct((bsz, 2, SUBLANES, S5_N), F32)],
        scratch_shapes=[pltpu.VMEM((bb, 2, SUBLANES, S5_N), F32),
                        pltpu.VMEM((bb * L, 2 * S5_N), F32)],
        compiler_params=_cparams("parallel", "arbitrary"),
        name="s5",
    )(u, h0, lb, bw, cw, dsk, wglu)


def _prep_layer(l, P):
    w_in = P["w_in"][l]
    sizes = (GDN_CONV_CH, GDN_VW, GDN_HEADS, GDN_HEADS, S5_CH, 3 * ML_W, ML_HEADS, ML_HEADS, ML_W)
    offs = [0]
    for s in sizes:
        offs.append(offs[-1] + s)
    g_qkv, g_z, g_b, g_a, s_u, m_qkv, m_i, m_f, m_o = [w_in[:, offs[i]:offs[i + 1]] for i in range(9)]
    rep = lambda w: jnp.repeat(w, ML_DH, axis=1)
    w_cat = jnp.concatenate([g_qkv, g_z, rep(g_b), rep(g_a), s_u, m_qkv, m_o, rep(m_i), rep(m_f)],
                            axis=1).astype(BF16)

    lr = P["s5_lam_re"][l].astype(F32)
    li = P["s5_lam_im"][l].astype(F32)
    dt = jnp.exp(P["s5_log_dt"][l].astype(F32))[:, None]
    mag = jnp.exp(lr * dt)
    lb_re = mag * jnp.cos(li * dt)
    lb_im = mag * jnp.sin(li * dt)
    den = lr * lr + li * li
    c_re = ((lb_re - 1.0) * lr + lb_im * li) / den
    c_im = (lb_im * lr - (lb_re - 1.0) * li) / den
    b_r = P["s5_B_re"][l].astype(F32)
    b_i = P["s5_B_im"][l].astype(F32)
    bb_re = c_re[..., None] * b_r - c_im[..., None] * b_i
    bb_im = c_re[..., None] * b_i + c_im[..., None] * b_r
    eye_g = jnp.eye(S5_GROUPS, dtype=F32)
    bd = lambda m: jnp.einsum("gph,gk->ghkp", m, eye_g).reshape(S5_CH, S5_N)
    bw = jnp.concatenate([bd(bb_re), bd(bb_im)], axis=1).astype(BF16)
    cd = lambda m: jnp.einsum("ghp,gk->gpkh", m, eye_g).reshape(S5_N, S5_CH)
    cw = jnp.concatenate([cd(P["s5_C_re"][l].astype(F32)),
                          -cd(P["s5_C_im"][l].astype(F32))], axis=0).astype(BF16)
    lb = jnp.zeros((SUBLANES, S5_N), F32).at[0].set(lb_re.reshape(-1)).at[1].set(lb_im.reshape(-1))

    wr = jnp.zeros((D_MODEL, LANES), F32)
    wr = wr.at[:, 0:N_GROUPS].set(P["w_router_group"][l])
    wr = wr.at[:, N_GROUPS:N_GROUPS + N_EXPERTS].set(P["w_router_expert"][l])
    br = jnp.zeros((1, LANES), F32)
    br = br.at[0, 0:N_GROUPS].set(P["b_router_group"][l])
    br = br.at[0, N_GROUPS:N_GROUPS + N_EXPERTS].set(P["b_router_expert"][l])

    rep_row = lambda v: jnp.repeat(v.astype(F32), ML_DH).reshape(1, ML_W)
    return dict(
        norm_mix=P["norm_mix"][l].reshape(1, D_MODEL).astype(F32),
        w_cat=w_cat,
        w_out=P["w_out"][l].astype(BF16),
        conv_w=P["gdn_conv_w"][l].astype(F32),
        alog=rep_row(P["gdn_A_log"][l]), dtb=rep_row(P["gdn_dt_bias"][l]),
        gdn_norm=P["gdn_norm"][l].reshape(1, GDN_DV).astype(F32),
        lb=lb, bw=bw, cw=cw,
        s5_d=P["s5_D"][l].reshape(1, S5_CH).astype(F32),
        w_glu=P["s5_w_glu"][l].astype(BF16),
        ml_bi=rep_row(P["ml_ig_bias"][l]), ml_bf=rep_row(P["ml_fg_bias"][l]),
        ml_norm=P["ml_norm"][l].reshape(1, ML_DH).astype(F32),
        norm_ffn=P["norm_ffn"][l].reshape(1, D_MODEL).astype(F32),
        wr=wr, br=br,
        wg=P["w_exp_gate"][l].astype(BF16), wu=P["w_exp_up"][l].astype(BF16),
        wd=P["w_exp_down"][l].astype(BF16),
        norm_ple=P["norm_ple"][l].reshape(1, D_MODEL).astype(F32),
        w_ple_gate=P["w_ple_gate"][l].astype(BF16),
        w_ple_proj=P["w_ple_proj"][l].astype(BF16),
    )


def _trunk(x, p, states, layers, final_norm, *, L, tv, Ls, tm, bb, bbs):
    conv0, gdn0, s5re0, s5im0, mc0, mn0, mm0 = states
    bsz, t, _ = x.shape
    m = bsz * t
    h = x.reshape(m, D_MODEL)
    outs = [[] for _ in range(5)]
    last_row = (tv - 1) % SUBLANES
    p_all = p.reshape(p.shape[0], m, PLE_DIM)
    gdn_all, mc_all = gdn0, mc0
    for l, W in enumerate(layers):
        slab_g, s_u, slab_m = _norm_inproj(h, W["norm_mix"], W["w_cat"],
                                           (MIX_SLAB_W, S5_CH, MIX_SLAB_W), tm)
        conv_in = jnp.pad(conv0[l], ((0, 0), (SUBLANES - (CONV_W - 1), 0), (0, 0)))
        o_gdn, conv_n, gdn_all = _gdn(slab_g.reshape(bsz, t, MIX_SLAB_W), conv_in, gdn_all, l,
                                      W["conv_w"], W["alog"], W["dtb"], W["gdn_norm"], L, tv, bb)
        h0 = jnp.stack([s5re0[l].reshape(bsz, S5_N), s5im0[l].reshape(bsz, S5_N)], axis=1)
        h0 = jnp.broadcast_to(h0[:, :, None, :], (bsz, 2, SUBLANES, S5_N))
        o_s5, s5_n = _s5(s_u.reshape(bsz, t, S5_CH), h0, W["lb"], W["bw"], W["cw"], W["s5_d"],
                         W["w_glu"], Ls, bbs)
        o_ml, mc_all, n_n, m_n = _mlstm(slab_m.reshape(bsz, t, MIX_SLAB_W), mc_all, l,
                                        mn0[l].reshape(bsz, PAIRS, 1, LANES),
                                        jnp.repeat(mm0[l], ML_DH, axis=-1).reshape(bsz, 1, ML_W),
                                        W["ml_bi"], W["ml_bf"], W["ml_norm"], L, tv, bb)
        h = _ffn(o_gdn.reshape(m, GDN_VW), o_s5.reshape(m, S5_CH), o_ml.reshape(m, ML_W), h,
                 p_all, l, W, final_norm, tm, l == len(layers) - 1)
        outs[0].append(conv_n[:, SUBLANES - (CONV_W - 1):])
        outs[1].append(s5_n[:, 0, last_row].reshape(bsz, S5_GROUPS, S5_STATE))
        outs[2].append(s5_n[:, 1, last_row].reshape(bsz, S5_GROUPS, S5_STATE))
        outs[3].append(n_n.reshape(bsz, ML_HEADS, ML_DH))
        outs[4].append(m_n[:, 0, ::ML_DH])
    conv_o, s5re_o, s5im_o, mn_o, mm_o = (jnp.stack(o) for o in outs)
    return (h.reshape(bsz, t, D_MODEL), conv_o, gdn_all, s5re_o, s5im_o, mc_all, mn_o, mm_o)


def kernel(x_prompt, x_sample, p_prompt, p_sample, state_gdn_conv, state_gdn, state_s5_re, state_s5_im, state_mlstm_C, state_mlstm_n, state_mlstm_m, norm_mix, w_in, w_out, gdn_conv_w, gdn_A_log, gdn_dt_bias, gdn_norm, s5_lam_re, s5_lam_im, s5_log_dt, s5_B_re, s5_B_im, s5_C_re, s5_C_im, s5_D, s5_w_glu, ml_ig_bias, ml_fg_bias, ml_norm, norm_ffn, w_router_group, b_router_group, w_router_expert, b_router_expert, w_exp_gate, w_exp_up, w_exp_down, norm_ple, w_ple_gate, w_ple_proj, final_norm):
    P = dict(norm_mix=norm_mix, w_in=w_in, w_out=w_out, gdn_conv_w=gdn_conv_w, gdn_A_log=gdn_A_log,
             gdn_dt_bias=gdn_dt_bias, gdn_norm=gdn_norm, s5_lam_re=s5_lam_re, s5_lam_im=s5_lam_im,
             s5_log_dt=s5_log_dt, s5_B_re=s5_B_re, s5_B_im=s5_B_im, s5_C_re=s5_C_re, s5_C_im=s5_C_im,
             s5_D=s5_D, s5_w_glu=s5_w_glu, ml_ig_bias=ml_ig_bias, ml_fg_bias=ml_fg_bias,
             ml_norm=ml_norm, norm_ffn=norm_ffn, w_router_group=w_router_group,
             b_router_group=b_router_group, w_router_expert=w_router_expert,
             b_router_expert=b_router_expert, w_exp_gate=w_exp_gate, w_exp_up=w_exp_up,
             w_exp_down=w_exp_down, norm_ple=norm_ple, w_ple_gate=w_ple_gate, w_ple_proj=w_ple_proj)
    depth = norm_mix.shape[0]
    layers = [_prep_layer(l, P) for l in range(depth)]
    fnorm = final_norm.reshape(1, D_MODEL).astype(F32)

    bp, tp, _ = x_prompt.shape
    zeros = lambda *s: jnp.zeros((depth, bp) + s, F32)
    prompt_init = (zeros(CONV_W - 1, GDN_CONV_CH), zeros(GDN_HEADS, GDN_DK, GDN_DV),
                   zeros(S5_GROUPS, S5_STATE), zeros(S5_GROUPS, S5_STATE),
                   zeros(ML_HEADS, ML_DH, ML_DH), zeros(ML_HEADS, ML_DH), zeros(ML_HEADS))
    lp = math.gcd(tp, 64)
    lsp = math.gcd(tp, 256)
    res_p = _trunk(x_prompt, p_prompt, prompt_init, layers, fnorm,
                   L=lp, tv=lp, Ls=lsp, tm=512, bb=2, bbs=1)

    bs, ts, _ = x_sample.shape
    tpad = -(-ts // SUBLANES) * SUBLANES
    xs = jnp.pad(x_sample, ((0, 0), (0, tpad - ts), (0, 0)))
    ps = jnp.pad(p_sample, ((0, 0), (0, 0), (0, tpad - ts), (0, 0)))
    sample_init = (state_gdn_conv, state_gdn, state_s5_re, state_s5_im,
                   state_mlstm_C, state_mlstm_n, state_mlstm_m)
    res_s = _trunk(xs, ps, sample_init, layers, fnorm,
                   L=tpad, tv=ts, Ls=tpad, tm=512, bb=8, bbs=8)
    y_sample = res_s[0][:, :ts]
    return (res_p[0], y_sample) + res_p[1:] + res_s[1:]
```

```python
import functools
import math

import jax
import jax.numpy as jnp
from jax import lax
from jax.experimental import pallas as pl
from jax.experimental.pallas import tpu as pltpu

F32 = jnp.float32
BF16 = jnp.bfloat16

D_MODEL = 1024
DEPTH = 2
GDN_HEADS = 6
GDN_DK = 64
GDN_DV = 64
GDN_QK = GDN_HEADS * GDN_DK
GDN_VW = GDN_HEADS * GDN_DV
GDN_CONV_CH = 2 * GDN_QK + GDN_VW
CONV_W = 4
S5_GROUPS = 16
S5_GROUP_CH = 16
S5_CH = S5_GROUPS * S5_GROUP_CH
S5_STATE = 64
S5_N = S5_GROUPS * S5_STATE
ML_HEADS = 6
ML_DH = 64
ML_W = ML_HEADS * ML_DH
N_GROUPS = 4
EXPERTS_PER_GROUP = 4
N_EXPERTS = N_GROUPS * EXPERTS_PER_GROUP
D_EXPERT = 256
PLE_DIM = 256
EPS = 1e-6

LANES = 128
SUBLANES = 8
NEG = -1e30
VMEM_LIMIT = 56 * 1024 * 1024

def _cparams(*sem):
    return pltpu.CompilerParams(dimension_semantics=sem, vmem_limit_bytes=VMEM_LIMIT)


def _dot(a, b):
    return jnp.dot(a.astype(BF16), b.astype(BF16), preferred_element_type=F32)


def _dot_nt(a, b):
    return lax.dot_general(a.astype(BF16), b.astype(BF16), (((1,), (1,)), ((), ())),
                           preferred_element_type=F32)


def _dot_tn(a, b):
    return lax.dot_general(a.astype(BF16), b.astype(BF16), (((0,), (0,)), ((), ())),
                           preferred_element_type=F32)


def _split_bf16(a):
    hi = a.astype(BF16)
    lo = (a - hi.astype(F32)).astype(BF16)
    return hi, lo


def _rms(x, gain):
    return x * lax.rsqrt(jnp.mean(x * x, axis=-1, keepdims=True) + EPS) * gain


def _softplus(x):
    return jnp.maximum(x, 0.0) + jnp.log(1.0 + jnp.exp(-jnp.abs(x)))


def _sigmoid(x):
    return 1.0 / (1.0 + jnp.exp(-x))


def _silu(x):
    return x * _sigmoid(x)


def _interleave(gens):
    live = list(gens)
    while live:
        still = []
        for g in live:
            try:
                next(g)
                still.append(g)
            except StopIteration:
                pass
        live = still


def _unit_lower_inverse(n_mat, size, top=None):
    top = size if top is None else top
    r = lax.broadcasted_iota(jnp.int32, (size, size), 0)
    c = lax.broadcasted_iota(jnp.int32, (size, size), 1)
    base = min(16, top)
    same = jnp.bitwise_xor(r, c) < base
    nd = jnp.where(same, n_mat, 0.0)
    eye = jnp.where(r == c, 1.0, 0.0).astype(F32)
    t = eye - nd
    x = nd
    p = 1
    while 2 * p < base:
        x = _dot(x, x)
        yield
        t = t + _dot(t, x)
        yield
        p *= 2
    blk = base
    while blk < top:
        pair = jnp.bitwise_xor(r, c)
        off = jnp.where((pair < 2 * blk) & (pair >= blk), n_mat, 0.0)
        ot = _dot(off, t)
        yield
        t = t - _dot(t, ot)
        yield
        blk *= 2
    return t


def _norm_inproj_kernel(x_ref, g_ref, w_ref, *out_refs, widths):
    u = _rms(x_ref[...], g_ref[...]).astype(BF16)
    off = 0
    for o_ref, wd in zip(out_refs, widths):
        o_ref[...] = jnp.dot(u, w_ref[:, off:off + wd], preferred_element_type=F32)
        off += wd


def _norm_inproj(x, gain, w, widths, tm):
    m = x.shape[0]
    assert sum(widths) == w.shape[1]
    return pl.pallas_call(
        functools.partial(_norm_inproj_kernel, widths=widths),
        grid=(m // tm,),
        in_specs=[pl.BlockSpec((tm, D_MODEL), lambda i: (i, 0)),
                  pl.BlockSpec((1, D_MODEL), lambda i: (0, 0)),
                  pl.BlockSpec((D_MODEL, w.shape[1]), lambda i: (0, 0))],
        out_specs=[pl.BlockSpec((tm, wd), lambda i: (i, 0)) for wd in widths],
        out_shape=[jax.ShapeDtypeStruct((m, wd), F32) for wd in widths],
        compiler_params=_cparams("parallel"),
        name="norm_inproj",
    )(x, gain, w)


def _route(f, wr, br):
    fh, fl = _split_bf16(f)
    wh, wl = _split_bf16(wr)
    d = functools.partial(jnp.dot, preferred_element_type=F32)
    logits = d(fh, wh) + (d(fh, wl) + d(fl, wh)) + br
    lane = lax.broadcasted_iota(jnp.int32, logits.shape, 1)
    is_g = lane < N_GROUPS
    gl = jnp.where(is_g, logits, NEG)
    gmax = jnp.max(gl, axis=-1, keepdims=True)
    ge = jnp.where(is_g, jnp.exp(gl - gmax), 0.0)
    p_grp = ge / jnp.sum(ge, axis=-1, keepdims=True)
    g_prob = jnp.max(p_grp, axis=-1, keepdims=True)
    g_idx = jnp.min(jnp.where(is_g & (gl == gmax), lane, LANES), axis=-1, keepdims=True)
    e_lane = lane - N_GROUPS
    is_e = (e_lane >= 0) & (e_lane < N_EXPERTS) & (jnp.right_shift(e_lane, 2) == g_idx)
    le = jnp.where(is_e, logits, NEG)
    m1 = jnp.max(le, axis=-1, keepdims=True)
    i1 = jnp.min(jnp.where(is_e & (le == m1), lane, LANES), axis=-1, keepdims=True)
    is_e2 = is_e & (lane != i1)
    le2 = jnp.where(is_e2, logits, NEG)
    m2 = jnp.max(le2, axis=-1, keepdims=True)
    i2 = jnp.min(jnp.where(is_e2 & (le2 == m2), lane, LANES), axis=-1, keepdims=True)
    e2 = jnp.exp(m2 - m1)
    w1 = g_prob / (1.0 + e2)
    w2 = g_prob * e2 / (1.0 + e2)
    return fh, jnp.where(lane == i1, w1, 0.0) + jnp.where(lane == i2, w2, 0.0)


def _ffn_kernel(og_ref, os_ref, om_ref, h_ref, p_ref, wo_ref, nf_ref, wr_ref, br_ref,
                wg_hbm, wu_hbm, wd_hbm, np_ref, wpg_ref, wpp_ref, fn_ref, out_ref,
                wg_ref, wu_ref, wd_ref, acc_ref, *, final):
    @pl.when(pl.program_id(0) == 0)
    def _():
        pltpu.sync_copy(wg_hbm, wg_ref)
        pltpu.sync_copy(wu_hbm, wu_ref)
        pltpu.sync_copy(wd_hbm, wd_ref)

    h1 = h_ref[...]
    h1 = h1 + jnp.dot(og_ref[...], wo_ref[0:GDN_VW, :], preferred_element_type=F32)
    h1 = h1 + jnp.dot(os_ref[...], wo_ref[GDN_VW:GDN_VW + S5_CH, :], preferred_element_type=F32)
    h1 = h1 + jnp.dot(om_ref[...], wo_ref[GDN_VW + S5_CH:, :], preferred_element_type=F32)
    x, gates = _route(_rms(h1, nf_ref[...]), wr_ref[...], br_ref[...])
    lane = lax.broadcasted_iota(jnp.int32, gates.shape, 1)
    acc_ref[...] = h1

    def group(g, carry):
        for j in range(EXPERTS_PER_GROUP):
            e = g * EXPERTS_PER_GROUP + j
            gcol = jnp.sum(jnp.where(lane == N_GROUPS + e, gates, 0.0), axis=-1, keepdims=True)
            hg = jnp.dot(x, wg_ref[e], preferred_element_type=F32)
            hu = jnp.dot(x, wu_ref[e], preferred_element_type=F32)
            hidden = (_silu(hg) * hu * gcol).astype(BF16)
            acc_ref[...] += jnp.dot(hidden, wd_ref[e], preferred_element_type=F32)
        return carry

    lax.fori_loop(0, N_GROUPS, group, 0)
    acc = acc_ref[...]
    gate = _sigmoid(jnp.dot(_rms(acc, np_ref[...]).astype(BF16), wpg_ref[...],
                            preferred_element_type=F32))
    proj = jnp.dot(p_ref[0].astype(BF16), wpp_ref[...], preferred_element_type=F32)
    h = acc + proj * gate
    if final:
        h = _rms(h, fn_ref[...])
    out_ref[...] = h


def _ffn(og, os_, om, h, p_all, layer, W, fn, tm, final):
    m = h.shape[0]
    row = lambda i: (i, 0)
    const = lambda shape: pl.BlockSpec(shape, lambda i: (0,) * len(shape))
    hbm = pl.BlockSpec(memory_space=pl.ANY)
    return pl.pallas_call(
        functools.partial(_ffn_kernel, final=final),
        grid=(m // tm,),
        in_specs=[pl.BlockSpec((tm, GDN_VW), row), pl.BlockSpec((tm, S5_CH), row),
                  pl.BlockSpec((tm, ML_W), row), pl.BlockSpec((tm, D_MODEL), row),
                  pl.BlockSpec((1, tm, PLE_DIM), lambda i: (layer, i, 0)),
                  const((D_MODEL, D_MODEL)), const((1, D_MODEL)),
                  const((D_MODEL, LANES)), const((1, LANES)),
                  hbm, hbm, hbm,
                  const((1, D_MODEL)), const((D_MODEL, D_MODEL)),
                  const((PLE_DIM, D_MODEL)), const((1, D_MODEL))],
        out_specs=pl.BlockSpec((tm, D_MODEL), row),
        out_shape=jax.ShapeDtypeStruct((m, D_MODEL), F32),
        scratch_shapes=[pltpu.VMEM((N_EXPERTS, D_MODEL, D_EXPERT), BF16),
                        pltpu.VMEM((N_EXPERTS, D_MODEL, D_EXPERT), BF16),
                        pltpu.VMEM((N_EXPERTS, D_EXPERT, D_MODEL), BF16),
                        pltpu.VMEM((tm, D_MODEL), F32)],
        compiler_params=_cparams("arbitrary"),
        name="ffn",
    )(og, os_, om, h, p_all, W["w_out"], W["norm_ffn"], W["wr"], W["br"],
      W["wg"], W["wu"], W["wd"], W["norm_ple"], W["w_ple_gate"], W["w_ple_proj"], fn)


MIX_SLAB_W = 6 * ML_W
PAIRS = ML_HEADS // 2
HALF = LANES // 2


def _scan_rows(x, size, op, fill):
    row = lax.broadcasted_iota(jnp.int32, x.shape, 0)
    d = 1
    while d < size:
        x = op(x, jnp.where(row >= d, pltpu.roll(x, d, axis=0), fill))
        d *= 2
    return x


def _split3(a):
    hi = a.astype(BF16)
    r1 = a - hi.astype(F32)
    mid = r1.astype(BF16)
    lo = (r1 - mid.astype(F32)).astype(BF16)
    return hi, mid, lo


def _row_form(x_s, L):
    rr = lax.broadcasted_iota(jnp.int32, x_s.shape, 0)
    cc = lax.broadcasted_iota(jnp.int32, x_s.shape, 1)
    dg = jnp.where(rr == jnp.bitwise_and(cc, L - 1), x_s, 0.0)
    ones = jnp.ones((SUBLANES, L), BF16)
    acc = None
    for piece in _split3(dg):
        t = jnp.dot(ones, piece, preferred_element_type=F32)
        acc = t if acc is None else acc + t
    return acc[0:1, :]


def _score_cols(x, p, L):
    if 2 * L == LANES:
        return x[:, LANES * p:LANES * (p + 1)]
    return jnp.concatenate([x[:, LANES * p:LANES * p + L],
                            x[:, LANES * p + HALF:LANES * p + HALF + L]], axis=-1)


def _bd_stack(x2):
    lo = lax.broadcasted_iota(jnp.int32, x2.shape, 1) < HALF
    return jnp.concatenate([jnp.where(lo, x2, 0.0), jnp.where(lo, 0.0, x2)], axis=0)


def _bd_ones(rows_per_half):
    shape = (2 * rows_per_half, LANES)
    r = lax.broadcasted_iota(jnp.int32, shape, 0) < rows_per_half
    c = lax.broadcasted_iota(jnp.int32, shape, 1) < HALF
    return jnp.where(r == c, 1.0, 0.0).astype(F32)


def _half_sums(x2, ones_bd):
    return _dot(x2, ones_bd)


def _mlstm_kernel(a_ref, c0_ref, n0_ref, m0_ref, bi_ref, bf_ref, ng_ref,
                  o_ref, cn_ref, nn_ref, mn_ref, st_ref, m_ref, *, L, tv, bb):
    cidx = pl.program_id(1)
    nc = pl.num_programs(1)
    ones_ch = _bd_ones(HALF)
    r128 = lax.broadcasted_iota(jnp.int32, (LANES, LANES), 0)
    c128 = lax.broadcasted_iota(jnp.int32, (LANES, LANES), 1)
    diag128 = r128 == c128

    @pl.when(cidx == 0)
    def _():
        st_ref[...] = jnp.zeros(st_ref.shape, F32)
        m_ref[...] = m0_ref[...]
        ob = ones_ch.astype(BF16)
        for i in range(bb):
            for p in range(PAIRS):
                st_ref[i, p, 0:HALF, 0:HALF] = c0_ref[i, 2 * p]
                st_ref[i, p, HALF:, HALF:LANES] = c0_ref[i, 2 * p + 1]
                dg = jnp.where(diag128, n0_ref[i, p], 0.0)
                acc = None
                for piece in _split3(dg):
                    t = jnp.dot(piece, ob, preferred_element_type=F32)
                    acc = t if acc is None else acc + t
                st_ref[i, p, :, LANES:] = acc

    W = 2 * L
    rr = lax.broadcasted_iota(jnp.int32, (L, W), 0)
    cc = lax.broadcasted_iota(jnp.int32, (L, W), 1)
    incl = rr >= jnp.bitwise_and(cc, L - 1)
    ones_keys = _bd_ones(L)
    bd256 = jnp.concatenate([ones_ch, ones_ch], axis=-1) > 0.5
    gain2 = ng_ref[...]
    gain2 = jnp.concatenate([gain2, gain2], axis=-1)
    ones_l = jnp.ones((L, LANES), F32)
    outs = [[None] * PAIRS for _ in range(bb)]
    st_news = [[None] * PAIRS for _ in range(bb)]
    m_news = [None] * bb
    rows = []
    for i in range(bb):
        li = a_ref[i, :, 4 * ML_W:5 * ML_W] + bi_ref[...]
        lf = -_softplus(-(a_ref[i, :, 5 * ML_W:6 * ML_W] + bf_ref[...]))
        if tv < L:
            valid = lax.broadcasted_iota(jnp.int32, li.shape, 0) < tv
            li = jnp.where(valid, li, NEG)
            lf = jnp.where(valid, lf, 0.0)
        bcum = _scan_rows(lf, L, jnp.add, 0.0)
        a = li - bcum
        m0 = m_ref[i]
        m_t = bcum + jnp.maximum(m0, _scan_rows(a, L, jnp.maximum, NEG))
        e_inter = jnp.exp(bcum + m0 - m_t)
        m_new = m_t[L - 1:L, :]
        b_last = bcum[L - 1:L, :]
        e_c = jnp.exp(b_last + m0 - m_new)
        kw = a_ref[i, :, ML_W:2 * ML_W] * (ML_DH ** -0.5) * jnp.exp(b_last + a - m_new)
        m_news[i] = m_new
        rows.append((a, bcum, m_t, e_inter, e_c, kw))

    def pair(i, p):
        a, bcum, m_t, e_inter, e_c, kw = rows[i]
        sl = slice(LANES * p, LANES * (p + 1))
        q2 = a_ref[i, :, sl]
        k2 = a_ref[i, :, ML_W + LANES * p:ML_W + LANES * (p + 1)] * (ML_DH ** -0.5)
        v2 = a_ref[i, :, 2 * ML_W + LANES * p:2 * ML_W + LANES * (p + 1)]
        og2 = a_ref[i, :, 3 * ML_W + LANES * p:3 * ML_W + LANES * (p + 1)]
        st = st_ref[i, p]
        arow = _row_form(_score_cols(a, p, L), L)
        qk = _dot_nt(q2, _bd_stack(k2))
        qcn = _dot(q2, st)
        upd = _dot_tn(kw[:, sl], jnp.concatenate([v2, ones_l], axis=-1))
        ec2 = e_c[:, sl]
        st_news[i][p] = st * jnp.concatenate([ec2, ec2], axis=-1) + jnp.where(bd256, upd, 0.0)
        yield
        w_intra = jnp.where(incl, jnp.exp(_score_cols(bcum, p, L) + arow - _score_cols(m_t, p, L)), 0.0)
        s2 = qk * w_intra
        nd = _dot(s2, jnp.concatenate([_bd_stack(v2), ones_keys], axis=-1))
        yield
        e2 = e_inter[:, sl]
        num = e2 * qcn[:, 0:LANES] + nd[:, 0:LANES]
        den = e2 * qcn[:, LANES:] + nd[:, LANES:]
        hh = num / jnp.maximum(jnp.abs(den), jnp.exp(-m_t[:, sl]))
        ss = _half_sums(hh * hh, ones_ch)
        yield
        outs[i][p] = hh * lax.rsqrt(ss * (1.0 / ML_DH) + EPS) * gain2 * _sigmoid(og2)

    _interleave([pair(i, p) for i in range(bb) for p in range(PAIRS)])
    for i in range(bb):
        for p in range(PAIRS):
            st_ref[i, p] = st_news[i][p]
        m_ref[i] = m_news[i]
        o_ref[i] = jnp.concatenate(outs[i], axis=-1).astype(o_ref.dtype)

    @pl.when(cidx == nc - 1)
    def _():
        ones8 = jnp.ones((SUBLANES, LANES), BF16)
        for i in range(bb):
            for p in range(PAIRS):
                st = st_news[i][p]
                cn_ref[i, 2 * p] = st[0:HALF, 0:HALF]
                cn_ref[i, 2 * p + 1] = st[HALF:, HALF:LANES]
                dg = jnp.where(diag128, st[:, LANES:], 0.0)
                acc = None
                for piece in _split3(dg):
                    t = jnp.dot(ones8, piece, preferred_element_type=F32)
                    acc = t if acc is None else acc + t
                nn_ref[i, p] = acc[0:1, :]
        mn_ref[...] = m_ref[...]


def _mlstm(slab, c_all, layer, n0, m0, bi, bf, ng, L, tv, bb):
    bsz, t, _ = slab.shape
    nc = t // L
    fix2 = lambda b, c: (0, 0)
    c_spec = pl.BlockSpec((None, bb, ML_HEADS, ML_DH, ML_DH), lambda b, c: (layer, b, 0, 0, 0))
    n_spec = pl.BlockSpec((bb, PAIRS, 1, LANES), lambda b, c: (b, 0, 0, 0))
    m_spec = pl.BlockSpec((bb, 1, ML_W), lambda b, c: (b, 0, 0))
    return pl.pallas_call(
        functools.partial(_mlstm_kernel, L=L, tv=tv, bb=bb),
        grid=(bsz // bb, nc),
        input_output_aliases={1: 1},
        in_specs=[pl.BlockSpec((bb, L, MIX_SLAB_W), lambda b, c: (b, c, 0)),
                  c_spec, n_spec, m_spec,
                  pl.BlockSpec((1, ML_W), fix2), pl.BlockSpec((1, ML_W), fix2),
                  pl.BlockSpec((1, ML_DH), fix2)],
        out_specs=[pl.BlockSpec((bb, L, ML_W), lambda b, c: (b, c, 0)), c_spec, n_spec, m_spec],
        out_shape=[jax.ShapeDtypeStruct((bsz, t, ML_W), BF16),
                   jax.ShapeDtypeStruct(c_all.shape, F32),
                   jax.ShapeDtypeStruct((bsz, PAIRS, 1, LANES), F32),
                   jax.ShapeDtypeStruct((bsz, 1, ML_W), F32)],
        scratch_shapes=[pltpu.VMEM((bb, PAIRS, LANES, 2 * LANES), F32),
                        pltpu.VMEM((bb, 1, ML_W), F32)],
        compiler_params=_cparams("parallel", "arbitrary"),
        name="mlstm",
    )(slab, c_all, n0, m0, bi, bf, ng)


def _gdn_kernel(a_ref, conv0_ref, s0_ref, cw_ref, alog_ref, dtb_ref, ng_ref,
                o_ref, convn_ref, sn_ref, ext_ref, s_ref, *, L, tv, bb):
    c = pl.program_id(1)
    nc = pl.num_programs(1)

    @pl.when(c == 0)
    def _():
        ext_ref[:, 0:SUBLANES, :] = conv0_ref[...]
        s_ref[...] = jnp.zeros(s_ref.shape, F32)
        for i in range(bb):
            for p in range(PAIRS):
                s_ref[i, p, 0:HALF, 0:HALF] = s0_ref[i, 2 * p]
                s_ref[i, p, HALF:, HALF:] = s0_ref[i, 2 * p + 1]

    W = 2 * L
    rr = lax.broadcasted_iota(jnp.int32, (L, W), 0)
    cc = lax.broadcasted_iota(jnp.int32, (L, W), 1)
    key = jnp.bitwise_and(cc, L - 1)
    incl = rr >= key
    strict = rr > key
    first_half = cc < L
    lo = lax.broadcasted_iota(jnp.int32, (L, LANES), 1) < HALF
    ones_ch = _bd_ones(HALF)
    bd128 = ones_ch > 0.5
    cw = cw_ref[...]
    gain2 = ng_ref[...]
    gain2 = jnp.concatenate([gain2, gain2], axis=-1)
    outs = [[None] * PAIRS for _ in range(bb)]
    s_news = [[None] * PAIRS for _ in range(bb)]
    carries = [None] * bb
    rows = []
    for i in range(bb):
        raw = a_ref[i, :, 0:GDN_CONV_CH]
        ext_ref[i, SUBLANES:SUBLANES + L, :] = raw
        ext = ext_ref[i]
        acc = raw * cw[CONV_W - 1:CONV_W, :]
        for j in range(CONV_W - 1):
            sh = pltpu.roll(ext, CONV_W - 1 - j, axis=0)[SUBLANES:SUBLANES + L]
            acc = acc + sh * cw[j:j + 1, :]
        qkv = _silu(acc)
        carries[i] = pltpu.roll(ext, (L + SUBLANES - tv) % (L + SUBLANES), axis=0)[0:SUBLANES]

        beta = _sigmoid(a_ref[i, :, 4 * GDN_VW:5 * GDN_VW])
        g = -jnp.exp(alog_ref[...]) * _softplus(a_ref[i, :, 5 * GDN_VW:6 * GDN_VW] + dtb_ref[...])
        if tv < L:
            valid = lax.broadcasted_iota(jnp.int32, g.shape, 0) < tv
            beta = jnp.where(valid, beta, 0.0)
            g = jnp.where(valid, g, 0.0)
        gcum = _scan_rows(g, L, jnp.add, 0.0)
        g_last = gcum[L - 1:L, :]
        rows.append((qkv, beta, gcum, jnp.exp(gcum), jnp.exp(g_last - gcum), jnp.exp(g_last)))
    for i in range(bb):
        ext_ref[i, 0:SUBLANES, :] = carries[i]

    def pair(i, p):
        qkv, beta, gcum, eg, kdec, sdec = rows[i]
        sl = slice(LANES * p, LANES * (p + 1))
        q2 = qkv[:, LANES * p:LANES * (p + 1)]
        k2 = qkv[:, GDN_QK + LANES * p:GDN_QK + LANES * (p + 1)]
        v2 = qkv[:, 2 * GDN_QK + LANES * p:2 * GDN_QK + LANES * (p + 1)]
        z2 = a_ref[i, :, GDN_CONV_CH + LANES * p:GDN_CONV_CH + LANES * (p + 1)]
        s2 = s_ref[i, p]
        ssq = _half_sums(q2 * q2, ones_ch)
        ssk = _half_sums(k2 * k2, ones_ch)
        grow = _row_form(_score_cols(gcum, p, L), L)
        yield
        q2 = q2 * lax.rsqrt(ssq + EPS) * (GDN_DK ** -0.5)
        k2 = k2 * lax.rsqrt(ssk + EPS)
        beta2 = beta[:, sl]
        eg2 = eg[:, sl]
        kb2 = k2 * beta2
        kq = _dot_nt(jnp.concatenate([_bd_stack(kb2), q2], axis=0), _bd_stack(k2))
        qs = _dot(q2 * eg2, s2)
        yield
        gam = jnp.where(incl, jnp.exp(jnp.minimum(_score_cols(gcum, p, L) - grow, 0.0)), 0.0)
        gam_s = jnp.where(strict, gam, 0.0)
        gam_bd = jnp.concatenate([jnp.where(first_half, gam_s, 0.0),
                                  jnp.where(first_half, 0.0, gam_s)], axis=0)
        t_inv = yield from _unit_lower_inverse(kq[0:W] * gam_bd, W, L)
        rhs = jnp.concatenate([v2 * beta2, kb2 * eg2], axis=-1)
        sol = _dot(t_inv, jnp.concatenate([rhs, rhs], axis=0))
        yield
        u2 = jnp.where(lo, sol[0:L, 0:LANES], sol[L:W, 0:LANES])
        w2 = jnp.where(lo, sol[0:L, LANES:], sol[L:W, LANES:])
        v_new = u2 - _dot(w2, s2)
        yield
        o2 = qs + _dot(kq[W:] * gam, _bd_stack(v_new))
        upd = _dot_tn(k2 * kdec[:, sl], v_new)
        s_news[i][p] = s2 * sdec[:, sl] + jnp.where(bd128, upd, 0.0)
        yield
        ss = _half_sums(o2 * o2, ones_ch)
        yield
        outs[i][p] = o2 * lax.rsqrt(ss * (1.0 / GDN_DV) + EPS) * gain2 * _silu(z2)

    _interleave([pair(i, p) for i in range(bb) for p in range(PAIRS)])
    for i in range(bb):
        for p in range(PAIRS):
            s_ref[i, p] = s_news[i][p]
        o_ref[i] = jnp.concatenate(outs[i], axis=-1).astype(o_ref.dtype)

    @pl.when(c == nc - 1)
    def _():
        for i in range(bb):
            convn_ref[i] = carries[i]
            for p in range(PAIRS):
                sn_ref[i, 2 * p] = s_news[i][p][0:HALF, 0:HALF]
                sn_ref[i, 2 * p + 1] = s_news[i][p][HALF:, HALF:]


def _gdn(slab, conv0, s_all, layer, cw, alog, dtb, ng, L, tv, bb):
    bsz, t, _ = slab.shape
    nc = t // L
    fix2 = lambda b, c: (0, 0)
    s_spec = pl.BlockSpec((None, bb, GDN_HEADS, GDN_DK, GDN_DV), lambda b, c: (layer, b, 0, 0, 0))
    cv_spec = pl.BlockSpec((bb, SUBLANES, GDN_CONV_CH), lambda b, c: (b, 0, 0))
    return pl.pallas_call(
        functools.partial(_gdn_kernel, L=L, tv=tv, bb=bb),
        grid=(bsz // bb, nc),
        input_output_aliases={2: 2},
        in_specs=[pl.BlockSpec((bb, L, MIX_SLAB_W), lambda b, c: (b, c, 0)),
                  cv_spec, s_spec,
                  pl.BlockSpec((CONV_W, GDN_CONV_CH), fix2),
                  pl.BlockSpec((1, GDN_VW), fix2), pl.BlockSpec((1, GDN_VW), fix2),
                  pl.BlockSpec((1, GDN_DV), fix2)],
        out_specs=[pl.BlockSpec((bb, L, GDN_VW), lambda b, c: (b, c, 0)), cv_spec, s_spec],
        out_shape=[jax.ShapeDtypeStruct((bsz, t, GDN_VW), BF16),
                   jax.ShapeDtypeStruct((bsz, SUBLANES, GDN_CONV_CH), F32),
                   jax.ShapeDtypeStruct(s_all.shape, F32)],
        scratch_shapes=[pltpu.VMEM((bb, L + SUBLANES, GDN_CONV_CH), F32),
                        pltpu.VMEM((bb, PAIRS, LANES, LANES), F32)],
        compiler_params=_cparams("parallel", "arbitrary"),
        name="gdn",
    )(slab, conv0, s_all, cw, alog, dtb, ng)


def _cmul(ar, ai, br, bi):
    return ar * br - ai * bi, ar * bi + ai * br


def _s5_kernel(u_ref, h0_ref, lb_ref, bw_ref, cw_ref, dsk_ref, wglu_ref,
               o_ref, hn_ref, car_ref, h_ref, *, L, bb):
    c = pl.program_id(1)
    nc = pl.num_programs(1)

    @pl.when(c == 0)
    def _():
        car_ref[...] = h0_ref[...]

    rows = bb * L
    u = u_ref[...].reshape(rows, S5_CH)
    bu = jnp.dot(u.astype(BF16), bw_ref[...], preferred_element_type=F32)
    tiles = rows // SUBLANES
    x_re = bu[:, 0:S5_N].reshape(tiles, SUBLANES, S5_N)
    x_im = bu[:, S5_N:].reshape(tiles, SUBLANES, S5_N)
    p1 = (lb_ref[0:1, :], lb_ref[1:2, :])
    p2 = _cmul(*p1, *p1)
    p4 = _cmul(*p2, *p2)
    p8 = _cmul(*p4, *p4)
    sub = lax.broadcasted_iota(jnp.int32, (SUBLANES, S5_N), 0)
    for d, (pr, pi) in ((1, p1), (2, p2), (4, p4)):
        pr = jnp.where(sub >= d, pr, 0.0)
        pi = jnp.where(sub >= d, pi, 0.0)
        s_re = pltpu.roll(x_re, d, axis=1)
        s_im = pltpu.roll(x_im, d, axis=1)
        x_re, x_im = x_re + (pr * s_re - pi * s_im), x_im + (pr * s_im + pi * s_re)
    x_re = x_re.reshape(rows, S5_N)
    x_im = x_im.reshape(rows, S5_N)
    k = sub + 1
    pw_re = jnp.ones((SUBLANES, S5_N), F32)
    pw_im = jnp.zeros((SUBLANES, S5_N), F32)
    for bit, (pr, pi) in ((1, p1), (2, p2), (4, p4), (8, p8)):
        m_re, m_im = _cmul(pw_re, pw_im, pr, pi)
        on = jnp.bitwise_and(k, bit) != 0
        pw_re = jnp.where(on, m_re, pw_re)
        pw_im = jnp.where(on, m_im, pw_im)
    for i in range(bb):
        car_re = car_ref[i, 0]
        car_im = car_ref[i, 1]
        for t in range(L // SUBLANES):
            r0 = i * L + t * SUBLANES
            a_re, a_im = _cmul(pw_re, pw_im, car_re, car_im)
            t_re = x_re[r0:r0 + SUBLANES, :] + a_re
            t_im = x_im[r0:r0 + SUBLANES, :] + a_im
            h_ref[r0:r0 + SUBLANES, 0:S5_N] = t_re
            h_ref[r0:r0 + SUBLANES, S5_N:] = t_im
            car_re = jnp.broadcast_to(t_re[SUBLANES - 1:SUBLANES, :], (SUBLANES, S5_N))
            car_im = jnp.broadcast_to(t_im[SUBLANES - 1:SUBLANES, :], (SUBLANES, S5_N))
        car_ref[i, 0] = car_re
        car_ref[i, 1] = car_im
        hn_ref[i, 0] = t_re
        hn_ref[i, 1] = t_im
    y = jnp.dot(h_ref[...].astype(BF16), cw_ref[...], preferred_element_type=F32) + dsk_ref[...] * u
    zg = jax.nn.gelu(y)
    out = zg * _sigmoid(jnp.dot(zg.astype(BF16), wglu_ref[...], preferred_element_type=F32))
    o_ref[...] = out.reshape(bb, L, S5_CH).astype(o_ref.dtype)


def _s5(u, h0, lb, bw, cw, dsk, wglu, L, bb):
    bsz, t, _ = u.shape
    nc = t // L
    fix2 = lambda b, c: (0, 0)
    h_spec = pl.BlockSpec((bb, 2, SUBLANES, S5_N), lambda b, c: (b, 0, 0, 0))
    return pl.pallas_call(
        functools.partial(_s5_kernel, L=L, bb=bb),
        grid=(bsz // bb, nc),
        in_specs=[pl.BlockSpec((bb, L, S5_CH), lambda b, c: (b, c, 0)),
                  h_spec,
                  pl.BlockSpec((SUBLANES, S5_N), fix2),
                  pl.BlockSpec((S5_CH, 2 * S5_N), fix2),
                  pl.BlockSpec((2 * S5_N, S5_CH), fix2),
                  pl.BlockSpec((1, S5_CH), fix2),
                  pl.BlockSpec((S5_CH, S5_CH), fix2)],
        out_specs=[pl.BlockSpec((bb, L, S5_CH), lambda b, c: (b, c, 0)), h_spec],
        out_shape=[jax.ShapeDtypeStruct((bsz, t, S5_CH), BF16),
                   jax.ShapeDtypeStruct((bsz, 2, SUBLANES, S5_N), F32)],
        scratch_shapes=[pltpu.VMEM((bb, 2, SUBLANES, S5_N), F32),
                        pltpu.VMEM((bb * L, 2 * S5_N), F32)],
        compiler_params=_cparams("parallel", "arbitrary"),
        name="s5",
    )(u, h0, lb, bw, cw, dsk, wglu)


def _prep_layer(l, P):
    w_in = P["w_in"][l]
    sizes = (GDN_CONV_CH, GDN_VW, GDN_HEADS, GDN_HEADS, S5_CH, 3 * ML_W, ML_HEADS, ML_HEADS, ML_W)
    offs = [0]
    for s in sizes:
        offs.append(offs[-1] + s)
    g_qkv, g_z, g_b, g_a, s_u, m_qkv, m_i, m_f, m_o = [w_in[:, offs[i]:offs[i + 1]] for i in range(9)]
    rep = lambda w: jnp.repeat(w, ML_DH, axis=1)
    w_cat = jnp.concatenate([g_qkv, g_z, rep(g_b), rep(g_a), s_u, m_qkv, m_o, rep(m_i), rep(m_f)],
                            axis=1).astype(BF16)

    lr = P["s5_lam_re"][l].astype(F32)
    li = P["s5_lam_im"][l].astype(F32)
    dt = jnp.exp(P["s5_log_dt"][l].astype(F32))[:, None]
    mag = jnp.exp(lr * dt)
    lb_re = mag * jnp.cos(li * dt)
    lb_im = mag * jnp.sin(li * dt)
    den = lr * lr + li * li
    c_re = ((lb_re - 1.0) * lr + lb_im * li) / den
    c_im = (lb_im * lr - (lb_re - 1.0) * li) / den
    b_r = P["s5_B_re"][l].astype(F32)
    b_i = P["s5_B_im"][l].astype(F32)
    bb_re = c_re[..., None] * b_r - c_im[..., None] * b_i
    bb_im = c_re[..., None] * b_i + c_im[..., None] * b_r
    eye_g = jnp.eye(S5_GROUPS, dtype=F32)
    bd = lambda m: jnp.einsum("gph,gk->ghkp", m, eye_g).reshape(S5_CH, S5_N)
    bw = jnp.concatenate([bd(bb_re), bd(bb_im)], axis=1).astype(BF16)
    cd = lambda m: jnp.einsum("ghp,gk->gpkh", m, eye_g).reshape(S5_N, S5_CH)
    cw = jnp.concatenate([cd(P["s5_C_re"][l].astype(F32)),
                          -cd(P["s5_C_im"][l].astype(F32))], axis=0).astype(BF16)
    lb = jnp.zeros((SUBLANES, S5_N), F32).at[0].set(lb_re.reshape(-1)).at[1].set(lb_im.reshape(-1))

    wr = jnp.zeros((D_MODEL, LANES), F32)
    wr = wr.at[:, 0:N_GROUPS].set(P["w_router_group"][l])
    wr = wr.at[:, N_GROUPS:N_GROUPS + N_EXPERTS].set(P["w_router_expert"][l])
    br = jnp.zeros((1, LANES), F32)
    br = br.at[0, 0:N_GROUPS].set(P["b_router_group"][l])
    br = br.at[0, N_GROUPS:N_GROUPS + N_EXPERTS].set(P["b_router_expert"][l])

    rep_row = lambda v: jnp.repeat(v.astype(F32), ML_DH).reshape(1, ML_W)
    return dict(
        norm_mix=P["norm_mix"][l].reshape(1, D_MODEL).astype(F32),
        w_cat=w_cat,
        w_out=P["w_out"][l].astype(BF16),
        conv_w=P["gdn_conv_w"][l].astype(F32),
        alog=rep_row(P["gdn_A_log"][l]), dtb=rep_row(P["gdn_dt_bias"][l]),
        gdn_norm=P["gdn_norm"][l].reshape(1, GDN_DV).astype(F32),
        lb=lb, bw=bw, cw=cw,
        s5_d=P["s5_D"][l].reshape(1, S5_CH).astype(F32),
        w_glu=P["s5_w_glu"][l].astype(BF16),
        ml_bi=rep_row(P["ml_ig_bias"][l]), ml_bf=rep_row(P["ml_fg_bias"][l]),
        ml_norm=P["ml_norm"][l].reshape(1, ML_DH).astype(F32),
        norm_ffn=P["norm_ffn"][l].reshape(1, D_MODEL).astype(F32),
        wr=wr, br=br,
        wg=P["w_exp_gate"][l].astype(BF16), wu=P["w_exp_up"][l].astype(BF16),
        wd=P["w_exp_down"][l].astype(BF16),
        norm_ple=P["norm_ple"][l].reshape(1, D_MODEL).astype(F32),
        w_ple_gate=P["w_ple_gate"][l].astype(BF16),
        w_ple_proj=P["w_ple_proj"][l].astype(BF16),
    )


def _trunk(x, p, states, layers, final_norm, *, L, tv, Ls, tm, bb, bbs):
    conv0, gdn0, s5re0, s5im0, mc0, mn0, mm0 = states
    bsz, t, _ = x.shape
    m = bsz * t
    h = x.reshape(m, D_MODEL)
    outs = [[] for _ in range(5)]
    last_row = (tv - 1) % SUBLANES
    p_all = p.reshape(p.shape[0], m, PLE_DIM)
    gdn_all, mc_all = gdn0, mc0
    for l, W in enumerate(layers):
        slab_g, s_u, slab_m = _norm_inproj(h, W["norm_mix"], W["w_cat"],
                                           (MIX_SLAB_W, S5_CH, MIX_SLAB_W), tm)
        conv_in = jnp.pad(conv0[l], ((0, 0), (SUBLANES - (CONV_W - 1), 0), (0, 0)))
        o_gdn, conv_n, gdn_all = _gdn(slab_g.reshape(bsz, t, MIX_SLAB_W), conv_in, gdn_all, l,
                                      W["conv_w"], W["alog"], W["dtb"], W["gdn_norm"], L, tv, bb)
        h0 = jnp.stack([s5re0[l].reshape(bsz, S5_N), s5im0[l].reshape(bsz, S5_N)], axis=1)
        h0 = jnp.broadcast_to(h0[:, :, None, :], (bsz, 2, SUBLANES, S5_N))
        o_s5, s5_n = _s5(s_u.reshape(bsz, t, S5_CH), h0, W["lb"], W["bw"], W["cw"], W["s5_d"],
                         W["w_glu"], Ls, bbs)
        o_ml, mc_all, n_n, m_n = _mlstm(slab_m.reshape(bsz, t, MIX_SLAB_W), mc_all, l,
                                        mn0[l].reshape(bsz, PAIRS, 1, LANES),
                                        jnp.repeat(mm0[l], ML_DH, axis=-1).reshape(bsz, 1, ML_W),
                                        W["ml_bi"], W["ml_bf"], W["ml_norm"], L, tv, bb)
        h = _ffn(o_gdn.reshape(m, GDN_VW), o_s5.reshape(m, S5_CH), o_ml.reshape(m, ML_W), h,
                 p_all, l, W, final_norm, tm, l == len(layers) - 1)
        outs[0].append(conv_n[:, SUBLANES - (CONV_W - 1):])
        outs[1].append(s5_n[:, 0, last_row].reshape(bsz, S5_GROUPS, S5_STATE))
        outs[2].append(s5_n[:, 1, last_row].reshape(bsz, S5_GROUPS, S5_STATE))
        outs[3].append(n_n.reshape(bsz, ML_HEADS, ML_DH))
        outs[4].append(m_n[:, 0, ::ML_DH])
    conv_o, s5re_o, s5im_o, mn_o, mm_o = (jnp.stack(o) for o in outs)
    return (h.reshape(bsz, t, D_MODEL), conv_o, gdn_all, s5re_o, s5im_o, mc_all, mn_o, mm_o)


def kernel(x_prompt, x_sample, p_prompt, p_sample, state_gdn_conv, state_gdn, state_s5_re, state_s5_im, state_mlstm_C, state_mlstm_n, state_mlstm_m, norm_mix, w_in, w_out, gdn_conv_w, gdn_A_log, gdn_dt_bias, gdn_norm, s5_lam_re, s5_lam_im, s5_log_dt, s5_B_re, s5_B_im, s5_C_re, s5_C_im, s5_D, s5_w_glu, ml_ig_bias, ml_fg_bias, ml_norm, norm_ffn, w_router_group, b_router_group, w_router_expert, b_router_expert, w_exp_gate, w_exp_up, w_exp_down, norm_ple, w_ple_gate, w_ple_proj, final_norm):
    P = dict(norm_mix=norm_mix, w_in=w_in, w_out=w_out, gdn_conv_w=gdn_conv_w, gdn_A_log=gdn_A_log,
             gdn_dt_bias=gdn_dt_bias, gdn_norm=gdn_norm, s5_lam_re=s5_lam_re, s5_lam_im=s5_lam_im,
             s5_log_dt=s5_log_dt, s5_B_re=s5_B_re, s5_B_im=s5_B_im, s5_C_re=s5_C_re, s5_C_im=s5_C_im,
             s5_D=s5_D, s5_w_glu=s5_w_glu, ml_ig_bias=ml_ig_bias, ml_fg_bias=ml_fg_bias,
             ml_norm=ml_norm, norm_ffn=norm_ffn, w_router_group=w_router_group,
             b_router_group=b_router_group, w_router_expert=w_router_expert,
             b_router_expert=b_router_expert, w_exp_gate=w_exp_gate, w_exp_up=w_exp_up,
             w_exp_down=w_exp_down, norm_ple=norm_ple, w_ple_gate=w_ple_gate, w_ple_proj=w_ple_proj)
    depth = norm_mix.shape[0]
    layers = [_prep_layer(l, P) for l in range(depth)]
    fnorm = final_norm.reshape(1, D_MODEL).astype(F32)

    bp, tp, _ = x_prompt.shape
    zeros = lambda *s: jnp.zeros((depth, bp) + s, F32)
    prompt_init = (zeros(CONV_W - 1, GDN_CONV_CH), zeros(GDN_HEADS, GDN_DK, GDN_DV),
                   zeros(S5_GROUPS, S5_STATE), zeros(S5_GROUPS, S5_STATE),
                   zeros(ML_HEADS, ML_DH, ML_DH), zeros(ML_HEADS, ML_DH), zeros(ML_HEADS))
    lp = math.gcd(tp, 64)
    lsp = math.gcd(tp, 256)
    res_p = _trunk(x_prompt, p_prompt, prompt_init, layers, fnorm,
                   L=lp, tv=lp, Ls=lsp, tm=512, bb=2, bbs=1)

    bs, ts, _ = x_sample.shape
    tpad = -(-ts // SUBLANES) * SUBLANES
    xs = jnp.pad(x_sample, ((0, 0), (0, tpad - ts), (0, 0)))
    ps = jnp.pad(p_sample, ((0, 0), (0, 0), (0, tpad - ts), (0, 0)))
    sample_init = (state_gdn_conv, state_gdn, state_s5_re, state_s5_im,
                   state_mlstm_C, state_mlstm_n, state_mlstm_m)
    res_s = _trunk(xs, ps, sample_init, layers, fnorm,
                   L=tpad, tv=ts, Ls=tpad, tm=512, bb=8, bbs=8)
    y_sample = res_s[0][:, :ts]
    return (res_p[0], y_sample) + res_p[1:] + res_s[1:]
```

```python
import functools
import math

import jax
import jax.numpy as jnp
from jax import lax
from jax.experimental import pallas as pl
from jax.experimental.pallas import tpu as pltpu

F32 = jnp.float32
BF16 = jnp.bfloat16

D_MODEL = 1024
DEPTH = 2
GDN_HEADS = 6
GDN_DK = 64
GDN_DV = 64
GDN_QK = GDN_HEADS * GDN_DK
GDN_VW = GDN_HEADS * GDN_DV
GDN_CONV_CH = 2 * GDN_QK + GDN_VW
CONV_W = 4
S5_GROUPS = 16
S5_GROUP_CH = 16
S5_CH = S5_GROUPS * S5_GROUP_CH
S5_STATE = 64
S5_N = S5_GROUPS * S5_STATE
ML_HEADS = 6
ML_DH = 64
ML_W = ML_HEADS * ML_DH
N_GROUPS = 4
EXPERTS_PER_GROUP = 4
N_EXPERTS = N_GROUPS * EXPERTS_PER_GROUP
D_EXPERT = 256
PLE_DIM = 256
EPS = 1e-6

LANES = 128
SUBLANES = 8
NEG = -1e30
VMEM_LIMIT = 56 * 1024 * 1024

def _cparams(*sem):
    return pltpu.CompilerParams(dimension_semantics=sem, vmem_limit_bytes=VMEM_LIMIT)


def _dot(a, b):
    return jnp.dot(a.astype(BF16), b.astype(BF16), preferred_element_type=F32)


def _dot_nt(a, b):
    return lax.dot_general(a.astype(BF16), b.astype(BF16), (((1,), (1,)), ((), ())),
                           preferred_element_type=F32)


def _dot_tn(a, b):
    return lax.dot_general(a.astype(BF16), b.astype(BF16), (((0,), (0,)), ((), ())),
                           preferred_element_type=F32)


def _split_bf16(a):
    hi = a.astype(BF16)
    lo = (a - hi.astype(F32)).astype(BF16)
    return hi, lo


def _rms(x, gain):
    return x * lax.rsqrt(jnp.mean(x * x, axis=-1, keepdims=True) + EPS) * gain


def _softplus(x):
    return jnp.maximum(x, 0.0) + jnp.log(1.0 + jnp.exp(-jnp.abs(x)))


def _sigmoid(x):
    return 1.0 / (1.0 + jnp.exp(-x))


def _silu(x):
    return x * _sigmoid(x)


def _interleave(gens):
    live = list(gens)
    while live:
        still = []
        for g in live:
            try:
                next(g)
                still.append(g)
            except StopIteration:
                pass
        live = still


def _unit_lower_inverse(n_mat, size, top=None):
    top = size if top is None else top
    r = lax.broadcasted_iota(jnp.int32, (size, size), 0)
    c = lax.broadcasted_iota(jnp.int32, (size, size), 1)
    base = min(16, top)
    same = jnp.bitwise_xor(r, c) < base
    nd = jnp.where(same, n_mat, 0.0)
    eye = jnp.where(r == c, 1.0, 0.0).astype(F32)
    t = eye - nd
    p = 1
    if 2 * p < base:
        x = _dot(nd, nd)
        yield
    while 2 * p < base:
        t_next = t + _dot(t, x)
        if 4 * p < base:
            x = _dot(x, x)
        t = t_next
        yield
        p *= 2
    blk = base
    while blk < top:
        pair = jnp.bitwise_xor(r, c)
        off = jnp.where((pair < 2 * blk) & (pair >= blk), n_mat, 0.0)
        ot = _dot(off, t)
        yield
        t = t - _dot(t, ot)
        yield
        blk *= 2
    return t


def _norm_inproj_kernel(x_ref, g_ref, w_ref, *out_refs, widths):
    u = _rms(x_ref[...], g_ref[...]).astype(BF16)
    off = 0
    for o_ref, wd in zip(out_refs, widths):
        o_ref[...] = jnp.dot(u, w_ref[:, off:off + wd], preferred_element_type=F32)
        off += wd


def _norm_inproj(x, gain, w, widths, tm):
    m = x.shape[0]
    assert sum(widths) == w.shape[1]
    return pl.pallas_call(
        functools.partial(_norm_inproj_kernel, widths=widths),
        grid=(m // tm,),
        in_specs=[pl.BlockSpec((tm, D_MODEL), lambda i: (i, 0)),
                  pl.BlockSpec((1, D_MODEL), lambda i: (0, 0)),
                  pl.BlockSpec((D_MODEL, w.shape[1]), lambda i: (0, 0))],
        out_specs=[pl.BlockSpec((tm, wd), lambda i: (i, 0)) for wd in widths],
        out_shape=[jax.ShapeDtypeStruct((m, wd), F32) for wd in widths],
        compiler_params=_cparams("parallel"),
        name="norm_inproj",
    )(x, gain, w)


def _route(f, wr, br):
    fh, fl = _split_bf16(f)
    wh, wl = _split_bf16(wr)
    d = functools.partial(jnp.dot, preferred_element_type=F32)
    logits = d(fh, wh) + (d(fh, wl) + d(fl, wh)) + br
    lane = lax.broadcasted_iota(jnp.int32, logits.shape, 1)
    is_g = lane < N_GROUPS
    gl = jnp.where(is_g, logits, NEG)
    gmax = jnp.max(gl, axis=-1, keepdims=True)
    ge = jnp.where(is_g, jnp.exp(gl - gmax), 0.0)
    p_grp = ge / jnp.sum(ge, axis=-1, keepdims=True)
    g_prob = jnp.max(p_grp, axis=-1, keepdims=True)
    g_idx = jnp.min(jnp.where(is_g & (gl == gmax), lane, LANES), axis=-1, keepdims=True)
    e_lane = lane - N_GROUPS
    is_e = (e_lane >= 0) & (e_lane < N_EXPERTS) & (jnp.right_shift(e_lane, 2) == g_idx)
    le = jnp.where(is_e, logits, NEG)
    m1 = jnp.max(le, axis=-1, keepdims=True)
    i1 = jnp.min(jnp.where(is_e & (le == m1), lane, LANES), axis=-1, keepdims=True)
    is_e2 = is_e & (lane != i1)
    le2 = jnp.where(is_e2, logits, NEG)
    m2 = jnp.max(le2, axis=-1, keepdims=True)
    i2 = jnp.min(jnp.where(is_e2 & (le2 == m2), lane, LANES), axis=-1, keepdims=True)
    e2 = jnp.exp(m2 - m1)
    w1 = g_prob / (1.0 + e2)
    w2 = g_prob * e2 / (1.0 + e2)
    return fh, jnp.where(lane == i1, w1, 0.0) + jnp.where(lane == i2, w2, 0.0)


def _ffn_kernel(og_ref, os_ref, om_ref, h_ref, p_ref, wo_ref, nf_ref, wr_ref, br_ref,
                wg_ref, wu_ref, wd_ref, np_ref, wpg_ref, wpp_ref, fn_ref,
                out_ref, f_ref, gates_ref, acc_ref, *, final):
    gi = pl.program_id(1)

    @pl.when(gi == 0)
    def _():
        h1 = h_ref[...]
        h1 = h1 + jnp.dot(og_ref[...], wo_ref[0:GDN_VW, :], preferred_element_type=F32)
        h1 = h1 + jnp.dot(os_ref[...], wo_ref[GDN_VW:GDN_VW + S5_CH, :], preferred_element_type=F32)
        h1 = h1 + jnp.dot(om_ref[...], wo_ref[GDN_VW + S5_CH:, :], preferred_element_type=F32)
        acc_ref[...] = h1
        fh, gates = _route(_rms(h1, nf_ref[...]), wr_ref[...], br_ref[...])
        f_ref[...] = fh
        gates_ref[...] = gates

    x = f_ref[...]
    gates = gates_ref[...]
    lane = lax.broadcasted_iota(jnp.int32, gates.shape, 1)
    base = N_GROUPS + EXPERTS_PER_GROUP * gi
    acc = None
    for j in range(EXPERTS_PER_GROUP):
        gcol = jnp.sum(jnp.where(lane == base + j, gates, 0.0), axis=-1, keepdims=True)
        hg = jnp.dot(x, wg_ref[j], preferred_element_type=F32)
        hu = jnp.dot(x, wu_ref[j], preferred_element_type=F32)
        hidden = (_silu(hg) * hu * gcol).astype(BF16)
        t = jnp.dot(hidden, wd_ref[j], preferred_element_type=F32)
        acc = t if acc is None else acc + t
    acc_ref[...] += acc

    @pl.when(gi == N_GROUPS - 1)
    def _():
        h = acc_ref[...]
        gate = _sigmoid(jnp.dot(_rms(h, np_ref[...]).astype(BF16), wpg_ref[...],
                                preferred_element_type=F32))
        proj = jnp.dot(p_ref[0].astype(BF16), wpp_ref[...], preferred_element_type=F32)
        h = h + proj * gate
        if final:
            h = _rms(h, fn_ref[...])
        out_ref[...] = h


def _ffn(og, os_, om, h, p_all, layer, W, fn, tm, final):
    m = h.shape[0]
    row = lambda i, g: (i, 0)
    fix = lambda i, g: (0, 0)
    wsel = lambda i, g: (g, 0, 0)
    e = EXPERTS_PER_GROUP
    return pl.pallas_call(
        functools.partial(_ffn_kernel, final=final),
        grid=(m // tm, N_GROUPS),
        in_specs=[pl.BlockSpec((tm, GDN_VW), row), pl.BlockSpec((tm, S5_CH), row),
                  pl.BlockSpec((tm, ML_W), row), pl.BlockSpec((tm, D_MODEL), row),
                  pl.BlockSpec((1, tm, PLE_DIM), lambda i, g: (layer, i, 0)),
                  pl.BlockSpec((D_MODEL, D_MODEL), fix), pl.BlockSpec((1, D_MODEL), fix),
                  pl.BlockSpec((D_MODEL, LANES), fix), pl.BlockSpec((1, LANES), fix),
                  pl.BlockSpec((e, D_MODEL, D_EXPERT), wsel),
                  pl.BlockSpec((e, D_MODEL, D_EXPERT), wsel),
                  pl.BlockSpec((e, D_EXPERT, D_MODEL), wsel),
                  pl.BlockSpec((1, D_MODEL), fix), pl.BlockSpec((D_MODEL, D_MODEL), fix),
                  pl.BlockSpec((PLE_DIM, D_MODEL), fix), pl.BlockSpec((1, D_MODEL), fix)],
        out_specs=pl.BlockSpec((tm, D_MODEL), row),
        out_shape=jax.ShapeDtypeStruct((m, D_MODEL), F32),
        scratch_shapes=[pltpu.VMEM((tm, D_MODEL), BF16), pltpu.VMEM((tm, LANES), F32),
                        pltpu.VMEM((tm, D_MODEL), F32)],
        compiler_params=_cparams("parallel", "arbitrary"),
        name="ffn",
    )(og, os_, om, h, p_all, W["w_out"], W["norm_ffn"], W["wr"], W["br"],
      W["wg"], W["wu"], W["wd"], W["norm_ple"], W["w_ple_gate"], W["w_ple_proj"], fn)


MIX_SLAB_W = 6 * ML_W
PAIRS = ML_HEADS // 2
HALF = LANES // 2


def _scan_rows(x, size, op, fill):
    row = lax.broadcasted_iota(jnp.int32, x.shape, 0)
    d = 1
    while d < size:
        x = op(x, jnp.where(row >= d, pltpu.roll(x, d, axis=0), fill))
        d *= 2
    return x


def _split3(a):
    hi = a.astype(BF16)
    r1 = a - hi.astype(F32)
    mid = r1.astype(BF16)
    lo = (r1 - mid.astype(F32)).astype(BF16)
    return hi, mid, lo


def _row_form(x_s, L):
    rr = lax.broadcasted_iota(jnp.int32, x_s.shape, 0)
    cc = lax.broadcasted_iota(jnp.int32, x_s.shape, 1)
    dg = jnp.where(rr == jnp.bitwise_and(cc, L - 1), x_s, 0.0)
    ones = jnp.ones((SUBLANES, L), BF16)
    acc = None
    for piece in _split3(dg):
        t = jnp.dot(ones, piece, preferred_element_type=F32)
        acc = t if acc is None else acc + t
    return acc[0:1, :]


def _score_cols(x, p, L):
    if 2 * L == LANES:
        return x[:, LANES * p:LANES * (p + 1)]
    return jnp.concatenate([x[:, LANES * p:LANES * p + L],
                            x[:, LANES * p + HALF:LANES * p + HALF + L]], axis=-1)


def _bd_stack(x2):
    lo = lax.broadcasted_iota(jnp.int32, x2.shape, 1) < HALF
    return jnp.concatenate([jnp.where(lo, x2, 0.0), jnp.where(lo, 0.0, x2)], axis=0)


def _bd_ones(rows_per_half):
    shape = (2 * rows_per_half, LANES)
    r = lax.broadcasted_iota(jnp.int32, shape, 0) < rows_per_half
    c = lax.broadcasted_iota(jnp.int32, shape, 1) < HALF
    return jnp.where(r == c, 1.0, 0.0).astype(F32)


def _half_sums(x2, ones_bd):
    return _dot(x2, ones_bd)


def _mlstm_kernel(a_ref, c0_ref, n0_ref, m0_ref, bi_ref, bf_ref, ng_ref,
                  o_ref, cn_ref, nn_ref, mn_ref, st_ref, m_ref, *, L, tv, bb):
    cidx = pl.program_id(1)
    nc = pl.num_programs(1)
    ones_ch = _bd_ones(HALF)
    r128 = lax.broadcasted_iota(jnp.int32, (LANES, LANES), 0)
    c128 = lax.broadcasted_iota(jnp.int32, (LANES, LANES), 1)
    diag128 = r128 == c128

    @pl.when(cidx == 0)
    def _():
        st_ref[...] = jnp.zeros(st_ref.shape, F32)
        m_ref[...] = m0_ref[...]
        ob = ones_ch.astype(BF16)
        for i in range(bb):
            for p in range(PAIRS):
                st_ref[i, p, 0:HALF, 0:HALF] = c0_ref[i, 2 * p]
                st_ref[i, p, HALF:, HALF:LANES] = c0_ref[i, 2 * p + 1]
                dg = jnp.where(diag128, n0_ref[i, p], 0.0)
                acc = None
                for piece in _split3(dg):
                    t = jnp.dot(piece, ob, preferred_element_type=F32)
                    acc = t if acc is None else acc + t
                st_ref[i, p, :, LANES:] = acc

    W = 2 * L
    rr = lax.broadcasted_iota(jnp.int32, (L, W), 0)
    cc = lax.broadcasted_iota(jnp.int32, (L, W), 1)
    incl = rr >= jnp.bitwise_and(cc, L - 1)
    ones_keys = _bd_ones(L)
    bd256 = jnp.concatenate([ones_ch, ones_ch], axis=-1) > 0.5
    gain2 = ng_ref[...]
    gain2 = jnp.concatenate([gain2, gain2], axis=-1)
    ones_l = jnp.ones((L, LANES), F32)
    outs = [[None] * PAIRS for _ in range(bb)]
    st_news = [[None] * PAIRS for _ in range(bb)]
    m_news = [None] * bb
    rows = []
    for i in range(bb):
        li = a_ref[i, :, 4 * ML_W:5 * ML_W] + bi_ref[...]
        lf = -_softplus(-(a_ref[i, :, 5 * ML_W:6 * ML_W] + bf_ref[...]))
        if tv < L:
            valid = lax.broadcasted_iota(jnp.int32, li.shape, 0) < tv
            li = jnp.where(valid, li, NEG)
            lf = jnp.where(valid, lf, 0.0)
        bcum = _scan_rows(lf, L, jnp.add, 0.0)
        a = li - bcum
        m0 = m_ref[i]
        m_t = bcum + jnp.maximum(m0, _scan_rows(a, L, jnp.maximum, NEG))
        e_inter = jnp.exp(bcum + m0 - m_t)
        m_new = m_t[L - 1:L, :]
        b_last = bcum[L - 1:L, :]
        e_c = jnp.exp(b_last + m0 - m_new)
        kw = a_ref[i, :, ML_W:2 * ML_W] * (ML_DH ** -0.5) * jnp.exp(b_last + a - m_new)
        m_news[i] = m_new
        rows.append((a, bcum, m_t, e_inter, e_c, kw))

    def pair(i, p):
        a, bcum, m_t, e_inter, e_c, kw = rows[i]
        sl = slice(LANES * p, LANES * (p + 1))
        q2 = a_ref[i, :, sl]
        k2 = a_ref[i, :, ML_W + LANES * p:ML_W + LANES * (p + 1)] * (ML_DH ** -0.5)
        v2 = a_ref[i, :, 2 * ML_W + LANES * p:2 * ML_W + LANES * (p + 1)]
        og2 = a_ref[i, :, 3 * ML_W + LANES * p:3 * ML_W + LANES * (p + 1)]
        st = st_ref[i, p]
        arow = _row_form(_score_cols(a, p, L), L)
        qk = _dot_nt(q2, _bd_stack(k2))
        qcn = _dot(q2, st)
        upd = _dot_tn(kw[:, sl], jnp.concatenate([v2, ones_l], axis=-1))
        ec2 = e_c[:, sl]
        st_news[i][p] = st * jnp.concatenate([ec2, ec2], axis=-1) + jnp.where(bd256, upd, 0.0)
        yield
        w_intra = jnp.where(incl, jnp.exp(_score_cols(bcum, p, L) + arow - _score_cols(m_t, p, L)), 0.0)
        s2 = qk * w_intra
        nd = _dot(s2, jnp.concatenate([_bd_stack(v2), ones_keys], axis=-1))
        yield
        e2 = e_inter[:, sl]
        num = e2 * qcn[:, 0:LANES] + nd[:, 0:LANES]
        den = e2 * qcn[:, LANES:] + nd[:, LANES:]
        hh = num / jnp.maximum(jnp.abs(den), jnp.exp(-m_t[:, sl]))
        ss = _half_sums(hh * hh, ones_ch)
        yield
        outs[i][p] = hh * lax.rsqrt(ss * (1.0 / ML_DH) + EPS) * gain2 * _sigmoid(og2)

    _interleave([pair(i, p) for i in range(bb) for p in range(PAIRS)])
    for i in range(bb):
        for p in range(PAIRS):
            st_ref[i, p] = st_news[i][p]
        m_ref[i] = m_news[i]
        o_ref[i] = jnp.concatenate(outs[i], axis=-1).astype(o_ref.dtype)

    @pl.when(cidx == nc - 1)
    def _():
        ones8 = jnp.ones((SUBLANES, LANES), BF16)
        for i in range(bb):
            for p in range(PAIRS):
                st = st_news[i][p]
                cn_ref[i, 2 * p] = st[0:HALF, 0:HALF]
                cn_ref[i, 2 * p + 1] = st[HALF:, HALF:LANES]
                dg = jnp.where(diag128, st[:, LANES:], 0.0)
                acc = None
                for piece in _split3(dg):
                    t = jnp.dot(ones8, piece, preferred_element_type=F32)
                    acc = t if acc is None else acc + t
                nn_ref[i, p] = acc[0:1, :]
        mn_ref[...] = m_ref[...]


def _mlstm(slab, c_all, layer, n0, m0, bi, bf, ng, L, tv, bb):
    bsz, t, _ = slab.shape
    nc = t // L
    fix2 = lambda b, c: (0, 0)
    c_spec = pl.BlockSpec((None, bb, ML_HEADS, ML_DH, ML_DH), lambda b, c: (layer, b, 0, 0, 0))
    n_spec = pl.BlockSpec((bb, PAIRS, 1, LANES), lambda b, c: (b, 0, 0, 0))
    m_spec = pl.BlockSpec((bb, 1, ML_W), lambda b, c: (b, 0, 0))
    return pl.pallas_call(
        functools.partial(_mlstm_kernel, L=L, tv=tv, bb=bb),
        grid=(bsz // bb, nc),
        input_output_aliases={1: 1},
        in_specs=[pl.BlockSpec((bb, L, MIX_SLAB_W), lambda b, c: (b, c, 0)),
                  c_spec, n_spec, m_spec,
                  pl.BlockSpec((1, ML_W), fix2), pl.BlockSpec((1, ML_W), fix2),
                  pl.BlockSpec((1, ML_DH), fix2)],
        out_specs=[pl.BlockSpec((bb, L, ML_W), lambda b, c: (b, c, 0)), c_spec, n_spec, m_spec],
        out_shape=[jax.ShapeDtypeStruct((bsz, t, ML_W), BF16),
                   jax.ShapeDtypeStruct(c_all.shape, F32),
                   jax.ShapeDtypeStruct((bsz, PAIRS, 1, LANES), F32),
                   jax.ShapeDtypeStruct((bsz, 1, ML_W), F32)],
        scratch_shapes=[pltpu.VMEM((bb, PAIRS, LANES, 2 * LANES), F32),
                        pltpu.VMEM((bb, 1, ML_W), F32)],
        compiler_params=_cparams("parallel", "arbitrary"),
        name="mlstm",
    )(slab, c_all, n0, m0, bi, bf, ng)


def _gdn_kernel(a_ref, conv0_ref, s0_ref, cw_ref, alog_ref, dtb_ref, ng_ref,
                o_ref, convn_ref, sn_ref, ext_ref, s_ref, *, L, tv, bb):
    c = pl.program_id(1)
    nc = pl.num_programs(1)

    @pl.when(c == 0)
    def _():
        ext_ref[:, 0:SUBLANES, :] = conv0_ref[...]
        s_ref[...] = jnp.zeros(s_ref.shape, F32)
        for i in range(bb):
            for p in range(PAIRS):
                s_ref[i, p, 0:HALF, 0:HALF] = s0_ref[i, 2 * p]
                s_ref[i, p, HALF:, HALF:] = s0_ref[i, 2 * p + 1]

    W = 2 * L
    rr = lax.broadcasted_iota(jnp.int32, (L, W), 0)
    cc = lax.broadcasted_iota(jnp.int32, (L, W), 1)
    key = jnp.bitwise_and(cc, L - 1)
    incl = rr >= key
    strict = rr > key
    first_half = cc < L
    lo = lax.broadcasted_iota(jnp.int32, (L, LANES), 1) < HALF
    ones_ch = _bd_ones(HALF)
    bd128 = ones_ch > 0.5
    cw = cw_ref[...]
    gain2 = ng_ref[...]
    gain2 = jnp.concatenate([gain2, gain2], axis=-1)
    outs = [[None] * PAIRS for _ in range(bb)]
    s_news = [[None] * PAIRS for _ in range(bb)]
    carries = [None] * bb
    rows = []
    for i in range(bb):
        raw = a_ref[i, :, 0:GDN_CONV_CH]
        ext_ref[i, SUBLANES:SUBLANES + L, :] = raw
        ext = ext_ref[i]
        acc = raw * cw[CONV_W - 1:CONV_W, :]
        for j in range(CONV_W - 1):
            sh = pltpu.roll(ext, CONV_W - 1 - j, axis=0)[SUBLANES:SUBLANES + L]
            acc = acc + sh * cw[j:j + 1, :]
        qkv = _silu(acc)
        carries[i] = pltpu.roll(ext, (L + SUBLANES - tv) % (L + SUBLANES), axis=0)[0:SUBLANES]

        beta = _sigmoid(a_ref[i, :, 4 * GDN_VW:5 * GDN_VW])
        g = -jnp.exp(alog_ref[...]) * _softplus(a_ref[i, :, 5 * GDN_VW:6 * GDN_VW] + dtb_ref[...])
        if tv < L:
            valid = lax.broadcasted_iota(jnp.int32, g.shape, 0) < tv
            beta = jnp.where(valid, beta, 0.0)
            g = jnp.where(valid, g, 0.0)
        gcum = _scan_rows(g, L, jnp.add, 0.0)
        g_last = gcum[L - 1:L, :]
        rows.append((qkv, beta, gcum, jnp.exp(gcum), jnp.exp(g_last - gcum), jnp.exp(g_last)))
    for i in range(bb):
        ext_ref[i, 0:SUBLANES, :] = carries[i]

    def pair(i, p):
        qkv, beta, gcum, eg, kdec, sdec = rows[i]
        sl = slice(LANES * p, LANES * (p + 1))
        q2 = qkv[:, LANES * p:LANES * (p + 1)]
        k2 = qkv[:, GDN_QK + LANES * p:GDN_QK + LANES * (p + 1)]
        v2 = qkv[:, 2 * GDN_QK + LANES * p:2 * GDN_QK + LANES * (p + 1)]
        z2 = a_ref[i, :, GDN_CONV_CH + LANES * p:GDN_CONV_CH + LANES * (p + 1)]
        s2 = s_ref[i, p]
        ssq = _half_sums(q2 * q2, ones_ch)
        ssk = _half_sums(k2 * k2, ones_ch)
        grow = _row_form(_score_cols(gcum, p, L), L)
        yield
        q2 = q2 * lax.rsqrt(ssq + EPS) * (GDN_DK ** -0.5)
        k2 = k2 * lax.rsqrt(ssk + EPS)
        beta2 = beta[:, sl]
        eg2 = eg[:, sl]
        kb2 = k2 * beta2
        kq = _dot_nt(jnp.concatenate([_bd_stack(kb2), q2], axis=0), _bd_stack(k2))
        qs = _dot(q2 * eg2, s2)
        yield
        gam = jnp.where(incl, jnp.exp(jnp.minimum(_score_cols(gcum, p, L) - grow, 0.0)), 0.0)
        gam_s = jnp.where(strict, gam, 0.0)
        gam_bd = jnp.concatenate([jnp.where(first_half, gam_s, 0.0),
                                  jnp.where(first_half, 0.0, gam_s)], axis=0)
        t_inv = yield from _unit_lower_inverse(kq[0:W] * gam_bd, W, L)
        rhs = jnp.concatenate([v2 * beta2, kb2 * eg2], axis=-1)
        sol = _dot(t_inv, jnp.concatenate([rhs, rhs], axis=0))
        yield
        u2 = jnp.where(lo, sol[0:L, 0:LANES], sol[L:W, 0:LANES])
        w2 = jnp.where(lo, sol[0:L, LANES:], sol[L:W, LANES:])
        v_new = u2 - _dot(w2, s2)
        yield
        o2 = qs + _dot(kq[W:] * gam, _bd_stack(v_new))
        upd = _dot_tn(k2 * kdec[:, sl], v_new)
        s_news[i][p] = s2 * sdec[:, sl] + jnp.where(bd128, upd, 0.0)
        yield
        ss = _half_sums(o2 * o2, ones_ch)
        yield
        outs[i][p] = o2 * lax.rsqrt(ss * (1.0 / GDN_DV) + EPS) * gain2 * _silu(z2)

    _interleave([pair(i, p) for i in range(bb) for p in range(PAIRS)])
    for i in range(bb):
        for p in range(PAIRS):
            s_ref[i, p] = s_news[i][p]
        o_ref[i] = jnp.concatenate(outs[i], axis=-1).astype(o_ref.dtype)

    @pl.when(c == nc - 1)
    def _():
        for i in range(bb):
            convn_ref[i] = carries[i]
            for p in range(PAIRS):
                sn_ref[i, 2 * p] = s_news[i][p][0:HALF, 0:HALF]
                sn_ref[i, 2 * p + 1] = s_news[i][p][HALF:, HALF:]


def _gdn(slab, conv0, s_all, layer, cw, alog, dtb, ng, L, tv, bb):
    bsz, t, _ = slab.shape
    nc = t // L
    fix2 = lambda b, c: (0, 0)
    s_spec = pl.BlockSpec((None, bb, GDN_HEADS, GDN_DK, GDN_DV), lambda b, c: (layer, b, 0, 0, 0))
    cv_spec = pl.BlockSpec((bb, SUBLANES, GDN_CONV_CH), lambda b, c: (b, 0, 0))
    return pl.pallas_call(
        functools.partial(_gdn_kernel, L=L, tv=tv, bb=bb),
        grid=(bsz // bb, nc),
        input_output_aliases={2: 2},
        in_specs=[pl.BlockSpec((bb, L, MIX_SLAB_W), lambda b, c: (b, c, 0)),
                  cv_spec, s_spec,
                  pl.BlockSpec((CONV_W, GDN_CONV_CH), fix2),
                  pl.BlockSpec((1, GDN_VW), fix2), pl.BlockSpec((1, GDN_VW), fix2),
                  pl.BlockSpec((1, GDN_DV), fix2)],
        out_specs=[pl.BlockSpec((bb, L, GDN_VW), lambda b, c: (b, c, 0)), cv_spec, s_spec],
        out_shape=[jax.ShapeDtypeStruct((bsz, t, GDN_VW), BF16),
                   jax.ShapeDtypeStruct((bsz, SUBLANES, GDN_CONV_CH), F32),
                   jax.ShapeDtypeStruct(s_all.shape, F32)],
        scratch_shapes=[pltpu.VMEM((bb, L + SUBLANES, GDN_CONV_CH), F32),
                        pltpu.VMEM((bb, PAIRS, LANES, LANES), F32)],
        compiler_params=_cparams("parallel", "arbitrary"),
        name="gdn",
    )(slab, conv0, s_all, cw, alog, dtb, ng)


def _cmul(ar, ai, br, bi):
    return ar * br - ai * bi, ar * bi + ai * br


def _s5_kernel(u_ref, h0_ref, lb_ref, bw_ref, cw_ref, dsk_ref, wglu_ref,
               o_ref, hn_ref, car_ref, h_ref, *, L, bb):
    c = pl.program_id(1)
    nc = pl.num_programs(1)

    @pl.when(c == 0)
    def _():
        car_ref[...] = h0_ref[...]

    rows = bb * L
    u = u_ref[...].reshape(rows, S5_CH)
    bu = jnp.dot(u.astype(BF16), bw_ref[...], preferred_element_type=F32)
    tiles = rows // SUBLANES
    x_re = bu[:, 0:S5_N].reshape(tiles, SUBLANES, S5_N)
    x_im = bu[:, S5_N:].reshape(tiles, SUBLANES, S5_N)
    p1 = (lb_ref[0:1, :], lb_ref[1:2, :])
    p2 = _cmul(*p1, *p1)
    p4 = _cmul(*p2, *p2)
    p8 = _cmul(*p4, *p4)
    sub = lax.broadcasted_iota(jnp.int32, (SUBLANES, S5_N), 0)
    for d, (pr, pi) in ((1, p1), (2, p2), (4, p4)):
        pr = jnp.where(sub >= d, pr, 0.0)
        pi = jnp.where(sub >= d, pi, 0.0)
        s_re = pltpu.roll(x_re, d, axis=1)
        s_im = pltpu.roll(x_im, d, axis=1)
        x_re, x_im = x_re + (pr * s_re - pi * s_im), x_im + (pr * s_im + pi * s_re)
    x_re = x_re.reshape(rows, S5_N)
    x_im = x_im.reshape(rows, S5_N)
    k = sub + 1
    pw_re = jnp.ones((SUBLANES, S5_N), F32)
    pw_im = jnp.zeros((SUBLANES, S5_N), F32)
    for bit, (pr, pi) in ((1, p1), (2, p2), (4, p4), (8, p8)):
        m_re, m_im = _cmul(pw_re, pw_im, pr, pi)
        on = jnp.bitwise_and(k, bit) != 0
        pw_re = jnp.where(on, m_re, pw_re)
        pw_im = jnp.where(on, m_im, pw_im)
    for i in range(bb):
        car_re = car_ref[i, 0]
        car_im = car_ref[i, 1]
        for t in range(L // SUBLANES):
            r0 = i * L + t * SUBLANES
            a_re, a_im = _cmul(pw_re, pw_im, car_re, car_im)
            t_re = x_re[r0:r0 + SUBLANES, :] + a_re
            t_im = x_im[r0:r0 + SUBLANES, :] + a_im
            h_ref[r0:r0 + SUBLANES, 0:S5_N] = t_re
            h_ref[r0:r0 + SUBLANES, S5_N:] = t_im
            car_re = jnp.broadcast_to(t_re[SUBLANES - 1:SUBLANES, :], (SUBLANES, S5_N))
            car_im = jnp.broadcast_to(t_im[SUBLANES - 1:SUBLANES, :], (SUBLANES, S5_N))
        car_ref[i, 0] = car_re
        car_ref[i, 1] = car_im
        hn_ref[i, 0] = t_re
        hn_ref[i, 1] = t_im
    y = jnp.dot(h_ref[...].astype(BF16), cw_ref[...], preferred_element_type=F32) + dsk_ref[...] * u
    zg = jax.nn.gelu(y)
    out = zg * _sigmoid(jnp.dot(zg.astype(BF16), wglu_ref[...], preferred_element_type=F32))
    o_ref[...] = out.reshape(bb, L, S5_CH).astype(o_ref.dtype)


def _s5(u, h0, lb, bw, cw, dsk, wglu, L, bb):
    bsz, t, _ = u.shape
    nc = t // L
    fix2 = lambda b, c: (0, 0)
    h_spec = pl.BlockSpec((bb, 2, SUBLANES, S5_N), lambda b, c: (b, 0, 0, 0))
    return pl.pallas_call(
        functools.partial(_s5_kernel, L=L, bb=bb),
        grid=(bsz // bb, nc),
        in_specs=[pl.BlockSpec((bb, L, S5_CH), lambda b, c: (b, c, 0)),
                  h_spec,
                  pl.BlockSpec((SUBLANES, S5_N), fix2),
                  pl.BlockSpec((S5_CH, 2 * S5_N), fix2),
                  pl.BlockSpec((2 * S5_N, S5_CH), fix2),
                  pl.BlockSpec((1, S5_CH), fix2),
                  pl.BlockSpec((S5_CH, S5_CH), fix2)],
        out_specs=[pl.BlockSpec((bb, L, S5_CH), lambda b, c: (b, c, 0)), h_spec],
        out_shape=[jax.ShapeDtypeStruct((bsz, t, S5_CH), BF16),
                   jax.ShapeDtypeStruct((bsz, 2, SUBLANES, S5_N), F32)],
        scratch_shapes=[pltpu.VMEM((bb, 2, SUBLANES, S5_N), F32),
                        pltpu.VMEM((bb * L, 2 * S5_N), F32)],
        compiler_params=_cparams("parallel", "arbitrary"),
        name="s5",
    )(u, h0, lb, bw, cw, dsk, wglu)


def _prep_layer(l, P):
    w_in = P["w_in"][l]
    sizes = (GDN_CONV_CH, GDN_VW, GDN_HEADS, GDN_HEADS, S5_CH, 3 * ML_W, ML_HEADS, ML_HEADS, ML_W)
    offs = [0]
    for s in sizes:
        offs.append(offs[-1] + s)
    g_qkv, g_z, g_b, g_a, s_u, m_qkv, m_i, m_f, m_o = [w_in[:, offs[i]:offs[i + 1]] for i in range(9)]
    rep = lambda w: jnp.repeat(w, ML_DH, axis=1)
    w_cat = jnp.concatenate([g_qkv, g_z, rep(g_b), rep(g_a), s_u, m_qkv, m_o, rep(m_i), rep(m_f)],
                            axis=1).astype(BF16)

    lr = P["s5_lam_re"][l].astype(F32)
    li = P["s5_lam_im"][l].astype(F32)
    dt = jnp.exp(P["s5_log_dt"][l].astype(F32))[:, None]
    mag = jnp.exp(lr * dt)
    lb_re = mag * jnp.cos(li * dt)
    lb_im = mag * jnp.sin(li * dt)
    den = lr * lr + li * li
    c_re = ((lb_re - 1.0) * lr + lb_im * li) / den
    c_im = (lb_im * lr - (lb_re - 1.0) * li) / den
    b_r = P["s5_B_re"][l].astype(F32)
    b_i = P["s5_B_im"][l].astype(F32)
    bb_re = c_re[..., None] * b_r - c_im[..., None] * b_i
    bb_im = c_re[..., None] * b_i + c_im[..., None] * b_r
    eye_g = jnp.eye(S5_GROUPS, dtype=F32)
    bd = lambda m: jnp.einsum("gph,gk->ghkp", m, eye_g).reshape(S5_CH, S5_N)
    bw = jnp.concatenate([bd(bb_re), bd(bb_im)], axis=1).astype(BF16)
    cd = lambda m: jnp.einsum("ghp,gk->gpkh", m, eye_g).reshape(S5_N, S5_CH)
    cw = jnp.concatenate([cd(P["s5_C_re"][l].astype(F32)),
                          -cd(P["s5_C_im"][l].astype(F32))], axis=0).astype(BF16)
    lb = jnp.zeros((SUBLANES, S5_N), F32).at[0].set(lb_re.reshape(-1)).at[1].set(lb_im.reshape(-1))

    wr = jnp.zeros((D_MODEL, LANES), F32)
    wr = wr.at[:, 0:N_GROUPS].set(P["w_router_group"][l])
    wr = wr.at[:, N_GROUPS:N_GROUPS + N_EXPERTS].set(P["w_router_expert"][l])
    br = jnp.zeros((1, LANES), F32)
    br = br.at[0, 0:N_GROUPS].set(P["b_router_group"][l])
    br = br.at[0, N_GROUPS:N_GROUPS + N_EXPERTS].set(P["b_router_expert"][l])

    rep_row = lambda v: jnp.repeat(v.astype(F32), ML_DH).reshape(1, ML_W)
    return dict(
        norm_mix=P["norm_mix"][l].reshape(1, D_MODEL).astype(F32),
        w_cat=w_cat,
        w_out=P["w_out"][l].astype(BF16),
        conv_w=P["gdn_conv_w"][l].astype(F32),
        alog=rep_row(P["gdn_A_log"][l]), dtb=rep_row(P["gdn_dt_bias"][l]),
        gdn_norm=P["gdn_norm"][l].reshape(1, GDN_DV).astype(F32),
        lb=lb, bw=bw, cw=cw,
        s5_d=P["s5_D"][l].reshape(1, S5_CH).astype(F32),
        w_glu=P["s5_w_glu"][l].astype(BF16),
        ml_bi=rep_row(P["ml_ig_bias"][l]), ml_bf=rep_row(P["ml_fg_bias"][l]),
        ml_norm=P["ml_norm"][l].reshape(1, ML_DH).astype(F32),
        norm_ffn=P["norm_ffn"][l].reshape(1, D_MODEL).astype(F32),
        wr=wr, br=br,
        wg=P["w_exp_gate"][l].astype(BF16), wu=P["w_exp_up"][l].astype(BF16),
        wd=P["w_exp_down"][l].astype(BF16),
        norm_ple=P["norm_ple"][l].reshape(1, D_MODEL).astype(F32),
        w_ple_gate=P["w_ple_gate"][l].astype(BF16),
        w_ple_proj=P["w_ple_proj"][l].astype(BF16),
    )


def _trunk(x, p, states, layers, final_norm, *, L, tv, Ls, tm, bb, bbs):
    conv0, gdn0, s5re0, s5im0, mc0, mn0, mm0 = states
    bsz, t, _ = x.shape
    m = bsz * t
    h = x.reshape(m, D_MODEL)
    outs = [[] for _ in range(5)]
    last_row = (tv - 1) % SUBLANES
    p_all = p.reshape(p.shape[0], m, PLE_DIM)
    gdn_all, mc_all = gdn0, mc0
    for l, W in enumerate(layers):
        slab_g, s_u, slab_m = _norm_inproj(h, W["norm_mix"], W["w_cat"],
                                           (MIX_SLAB_W, S5_CH, MIX_SLAB_W), tm)
        conv_in = jnp.pad(conv0[l], ((0, 0), (SUBLANES - (CONV_W - 1), 0), (0, 0)))
        o_gdn, conv_n, gdn_all = _gdn(slab_g.reshape(bsz, t, MIX_SLAB_W), conv_in, gdn_all, l,
                                      W["conv_w"], W["alog"], W["dtb"], W["gdn_norm"], L, tv, bb)
        h0 = jnp.stack([s5re0[l].reshape(bsz, S5_N), s5im0[l].reshape(bsz, S5_N)], axis=1)
        h0 = jnp.broadcast_to(h0[:, :, None, :], (bsz, 2, SUBLANES, S5_N))
        o_s5, s5_n = _s5(s_u.reshape(bsz, t, S5_CH), h0, W["lb"], W["bw"], W["cw"], W["s5_d"],
                         W["w_glu"], Ls, bbs)
        o_ml, mc_all, n_n, m_n = _mlstm(slab_m.reshape(bsz, t, MIX_SLAB_W), mc_all, l,
                                        mn0[l].reshape(bsz, PAIRS, 1, LANES),
                                        jnp.repeat(mm0[l], ML_DH, axis=-1).reshape(bsz, 1, ML_W),
                                        W["ml_bi"], W["ml_bf"], W["ml_norm"], L, tv, bb)
        h = _ffn(o_gdn.reshape(m, GDN_VW), o_s5.reshape(m, S5_CH), o_ml.reshape(m, ML_W), h,
                 p_all, l, W, final_norm, tm, l == len(layers) - 1)
        outs[0].append(conv_n[:, SUBLANES - (CONV_W - 1):])
        outs[1].append(s5_n[:, 0, last_row].reshape(bsz, S5_GROUPS, S5_STATE))
        outs[2].append(s5_n[:, 1, last_row].reshape(bsz, S5_GROUPS, S5_STATE))
        outs[3].append(n_n.reshape(bsz, ML_HEADS, ML_DH))
        outs[4].append(m_n[:, 0, ::ML_DH])
    conv_o, s5re_o, s5im_o, mn_o, mm_o = (jnp.stack(o) for o in outs)
    return (h.reshape(bsz, t, D_MODEL), conv_o, gdn_all, s5re_o, s5im_o, mc_all, mn_o, mm_o)


def kernel(x_prompt, x_sample, p_prompt, p_sample, state_gdn_conv, state_gdn, state_s5_re, state_s5_im, state_mlstm_C, state_mlstm_n, state_mlstm_m, norm_mix, w_in, w_out, gdn_conv_w, gdn_A_log, gdn_dt_bias, gdn_norm, s5_lam_re, s5_lam_im, s5_log_dt, s5_B_re, s5_B_im, s5_C_re, s5_C_im, s5_D, s5_w_glu, ml_ig_bias, ml_fg_bias, ml_norm, norm_ffn, w_router_group, b_router_group, w_router_expert, b_router_expert, w_exp_gate, w_exp_up, w_exp_down, norm_ple, w_ple_gate, w_ple_proj, final_norm):
    P = dict(norm_mix=norm_mix, w_in=w_in, w_out=w_out, gdn_conv_w=gdn_conv_w, gdn_A_log=gdn_A_log,
             gdn_dt_bias=gdn_dt_bias, gdn_norm=gdn_norm, s5_lam_re=s5_lam_re, s5_lam_im=s5_lam_im,
             s5_log_dt=s5_log_dt, s5_B_re=s5_B_re, s5_B_im=s5_B_im, s5_C_re=s5_C_re, s5_C_im=s5_C_im,
             s5_D=s5_D, s5_w_glu=s5_w_glu, ml_ig_bias=ml_ig_bias, ml_fg_bias=ml_fg_bias,
             ml_norm=ml_norm, norm_ffn=norm_ffn, w_router_group=w_router_group,
             b_router_group=b_router_group, w_router_expert=w_router_expert,
             b_router_expert=b_router_expert, w_exp_gate=w_exp_gate, w_exp_up=w_exp_up,
             w_exp_down=w_exp_down, norm_ple=norm_ple, w_ple_gate=w_ple_gate, w_ple_proj=w_ple_proj)
    depth = norm_mix.shape[0]
    layers = [_prep_layer(l, P) for l in range(depth)]
    fnorm = final_norm.reshape(1, D_MODEL).astype(F32)

    bp, tp, _ = x_prompt.shape
    zeros = lambda *s: jnp.zeros((depth, bp) + s, F32)
    prompt_init = (zeros(CONV_W - 1, GDN_CONV_CH), zeros(GDN_HEADS, GDN_DK, GDN_DV),
                   zeros(S5_GROUPS, S5_STATE), zeros(S5_GROUPS, S5_STATE),
                   zeros(ML_HEADS, ML_DH, ML_DH), zeros(ML_HEADS, ML_DH), zeros(ML_HEADS))
    lp = math.gcd(tp, 64)
    lsp = math.gcd(tp, 256)
    res_p = _trunk(x_prompt, p_prompt, prompt_init, layers, fnorm,
                   L=lp, tv=lp, Ls=lsp, tm=512, bb=8, bbs=1)

    bs, ts, _ = x_sample.shape
    tpad = -(-ts // SUBLANES) * SUBLANES
    xs = jnp.pad(x_sample, ((0, 0), (0, tpad - ts), (0, 0)))
    ps = jnp.pad(p_sample, ((0, 0), (0, 0), (0, tpad - ts), (0, 0)))
    sample_init = (state_gdn_conv, state_gdn, state_s5_re, state_s5_im,
                   state_mlstm_C, state_mlstm_n, state_mlstm_m)
    res_s = _trunk(xs, ps, sample_init, layers, fnorm,
                   L=tpad, tv=ts, Ls=tpad, tm=512, bb=8, bbs=8)
    y_sample = res_s[0][:, :ts]
    return (res_p[0], y_sample) + res_p[1:] + res_s[1:]
```

```python
import functools
import math

import jax
import jax.numpy as jnp
from jax import lax
from jax.experimental import pallas as pl
from jax.experimental.pallas import tpu as pltpu

F32 = jnp.float32
BF16 = jnp.bfloat16

D_MODEL = 1024
DEPTH = 2
GDN_HEADS = 6
GDN_DK = 64
GDN_DV = 64
GDN_QK = GDN_HEADS * GDN_DK
GDN_VW = GDN_HEADS * GDN_DV
GDN_CONV_CH = 2 * GDN_QK + GDN_VW
CONV_W = 4
S5_GROUPS = 16
S5_GROUP_CH = 16
S5_CH = S5_GROUPS * S5_GROUP_CH
S5_STATE = 64
S5_N = S5_GROUPS * S5_STATE
ML_HEADS = 6
ML_DH = 64
ML_W = ML_HEADS * ML_DH
N_GROUPS = 4
EXPERTS_PER_GROUP = 4
N_EXPERTS = N_GROUPS * EXPERTS_PER_GROUP
D_EXPERT = 256
PLE_DIM = 256
EPS = 1e-6

LANES = 128
SUBLANES = 8
NEG = -1e30
VMEM_LIMIT = 56 * 1024 * 1024

def _cparams(*sem):
    return pltpu.CompilerParams(dimension_semantics=sem, vmem_limit_bytes=VMEM_LIMIT)


def _dot(a, b):
    return jnp.dot(a.astype(BF16), b.astype(BF16), preferred_element_type=F32)


def _dot_nt(a, b):
    return lax.dot_general(a.astype(BF16), b.astype(BF16), (((1,), (1,)), ((), ())),
                           preferred_element_type=F32)


def _dot_tn(a, b):
    return lax.dot_general(a.astype(BF16), b.astype(BF16), (((0,), (0,)), ((), ())),
                           preferred_element_type=F32)


def _split_bf16(a):
    hi = a.astype(BF16)
    lo = (a - hi.astype(F32)).astype(BF16)
    return hi, lo


def _rms(x, gain):
    return x * lax.rsqrt(jnp.mean(x * x, axis=-1, keepdims=True) + EPS) * gain


def _softplus(x):
    return jnp.maximum(x, 0.0) + jnp.log(1.0 + jnp.exp(-jnp.abs(x)))


def _sigmoid(x):
    return 1.0 / (1.0 + jnp.exp(-x))


def _silu(x):
    return x * _sigmoid(x)


def _interleave(gens):
    live = list(gens)
    while live:
        still = []
        for g in live:
            try:
                next(g)
                still.append(g)
            except StopIteration:
                pass
        live = still


def _unit_lower_inverse(n_mat, size, top=None):
    top = size if top is None else top
    r = lax.broadcasted_iota(jnp.int32, (size, size), 0)
    c = lax.broadcasted_iota(jnp.int32, (size, size), 1)
    base = min(16, top)
    same = jnp.bitwise_xor(r, c) < base
    nd = jnp.where(same, n_mat, 0.0)
    eye = jnp.where(r == c, 1.0, 0.0).astype(F32)
    t = eye - nd
    p = 1
    if 2 * p < base:
        x = _dot(nd, nd)
        yield
    while 2 * p < base:
        t_next = t + _dot(t, x)
        if 4 * p < base:
            x = _dot(x, x)
        t = t_next
        yield
        p *= 2
    blk = base
    while blk < top:
        pair = jnp.bitwise_xor(r, c)
        off = jnp.where((pair < 2 * blk) & (pair >= blk), n_mat, 0.0)
        ot = _dot(off, t)
        yield
        t = t - _dot(t, ot)
        yield
        blk *= 2
    return t


MIX_W = 4 * ML_W
GATE_W = 2 * ML_W


def _norm_inproj_kernel(x_ref, g_ref, w_ref, e_ref, og_ref, os_ref, om_ref):
    u = _rms(x_ref[...], g_ref[...]).astype(BF16)
    dot = functools.partial(jnp.dot, preferred_element_type=F32)
    og_ref[:, 0:MIX_W] = dot(u, w_ref[:, 0:MIX_W])
    os_ref[...] = dot(u, w_ref[:, MIX_W:MIX_W + S5_CH])
    om_ref[:, 0:MIX_W] = dot(u, w_ref[:, MIX_W + S5_CH:2 * MIX_W + S5_CH])
    small = dot(u, w_ref[:, 2 * MIX_W + S5_CH:])
    rep = None
    for piece in _split3(small):
        t = dot(piece, e_ref[...])
        rep = t if rep is None else rep + t
    og_ref[:, MIX_W:] = rep[:, 0:GATE_W]
    om_ref[:, MIX_W:] = rep[:, GATE_W:]


def _norm_inproj(x, gain, w, e_rep, tm):
    m = x.shape[0]
    fix = lambda i: (0, 0)
    widths = (MIX_SLAB_W, S5_CH, MIX_SLAB_W)
    return pl.pallas_call(
        _norm_inproj_kernel,
        grid=(m // tm,),
        in_specs=[pl.BlockSpec((tm, D_MODEL), lambda i: (i, 0)),
                  pl.BlockSpec((1, D_MODEL), fix),
                  pl.BlockSpec((D_MODEL, w.shape[1]), fix),
                  pl.BlockSpec((LANES, 2 * GATE_W), fix)],
        out_specs=[pl.BlockSpec((tm, wd), lambda i: (i, 0)) for wd in widths],
        out_shape=[jax.ShapeDtypeStruct((m, wd), F32) for wd in widths],
        compiler_params=_cparams("parallel"),
        name="norm_inproj",
    )(x, gain, w, e_rep)


def _route(f, wr, br):
    fh, fl = _split_bf16(f)
    wh, wl = _split_bf16(wr)
    d = functools.partial(jnp.dot, preferred_element_type=F32)
    logits = d(fh, wh) + (d(fh, wl) + d(fl, wh)) + br
    lane = lax.broadcasted_iota(jnp.int32, logits.shape, 1)
    is_g = lane < N_GROUPS
    gl = jnp.where(is_g, logits, NEG)
    gmax = jnp.max(gl, axis=-1, keepdims=True)
    ge = jnp.where(is_g, jnp.exp(gl - gmax), 0.0)
    p_grp = ge / jnp.sum(ge, axis=-1, keepdims=True)
    g_prob = jnp.max(p_grp, axis=-1, keepdims=True)
    g_idx = jnp.min(jnp.where(is_g & (gl == gmax), lane, LANES), axis=-1, keepdims=True)
    e_lane = lane - N_GROUPS
    is_e = (e_lane >= 0) & (e_lane < N_EXPERTS) & (jnp.right_shift(e_lane, 2) == g_idx)
    le = jnp.where(is_e, logits, NEG)
    m1 = jnp.max(le, axis=-1, keepdims=True)
    i1 = jnp.min(jnp.where(is_e & (le == m1), lane, LANES), axis=-1, keepdims=True)
    is_e2 = is_e & (lane != i1)
    le2 = jnp.where(is_e2, logits, NEG)
    m2 = jnp.max(le2, axis=-1, keepdims=True)
    i2 = jnp.min(jnp.where(is_e2 & (le2 == m2), lane, LANES), axis=-1, keepdims=True)
    e2 = jnp.exp(m2 - m1)
    w1 = g_prob / (1.0 + e2)
    w2 = g_prob * e2 / (1.0 + e2)
    return fh, jnp.where(lane == i1, w1, 0.0) + jnp.where(lane == i2, w2, 0.0)


def _ffn_kernel(og_ref, os_ref, om_ref, h_ref, p_ref, wo_ref, nf_ref, wr_ref, br_ref,
                wg_ref, wu_ref, wd_ref, np_ref, wpg_ref, wpp_ref, fn_ref,
                out_ref, f_ref, gates_ref, acc_ref, *, final):
    gi = pl.program_id(1)

    @pl.when(gi == 0)
    def _():
        h1 = h_ref[...]
        h1 = h1 + jnp.dot(og_ref[...], wo_ref[0:GDN_VW, :], preferred_element_type=F32)
        h1 = h1 + jnp.dot(os_ref[...], wo_ref[GDN_VW:GDN_VW + S5_CH, :], preferred_element_type=F32)
        h1 = h1 + jnp.dot(om_ref[...], wo_ref[GDN_VW + S5_CH:, :], preferred_element_type=F32)
        acc_ref[...] = h1
        fh, gates = _route(_rms(h1, nf_ref[...]), wr_ref[...], br_ref[...])
        f_ref[...] = fh
        gates_ref[...] = gates

    x = f_ref[...]
    gates = gates_ref[...]
    lane = lax.broadcasted_iota(jnp.int32, gates.shape, 1)
    base = N_GROUPS + EXPERTS_PER_GROUP * gi
    acc = None
    for j in range(EXPERTS_PER_GROUP):
        gcol = jnp.sum(jnp.where(lane == base + j, gates, 0.0), axis=-1, keepdims=True)
        hg = jnp.dot(x, wg_ref[j], preferred_element_type=F32)
        hu = jnp.dot(x, wu_ref[j], preferred_element_type=F32)
        hidden = (_silu(hg) * hu * gcol).astype(BF16)
        t = jnp.dot(hidden, wd_ref[j], preferred_element_type=F32)
        acc = t if acc is None else acc + t
    acc_ref[...] += acc

    @pl.when(gi == N_GROUPS - 1)
    def _():
        h = acc_ref[...]
        gate = _sigmoid(jnp.dot(_rms(h, np_ref[...]).astype(BF16), wpg_ref[...],
                                preferred_element_type=F32))
        proj = jnp.dot(p_ref[0].astype(BF16), wpp_ref[...], preferred_element_type=F32)
        h = h + proj * gate
        if final:
            h = _rms(h, fn_ref[...])
        out_ref[...] = h


def _ffn(og, os_, om, h, p_all, layer, W, fn, tm, final):
    m = h.shape[0]
    row = lambda i, g: (i, 0)
    fix = lambda i, g: (0, 0)
    wsel = lambda i, g: (g, 0, 0)
    e = EXPERTS_PER_GROUP
    return pl.pallas_call(
        functools.partial(_ffn_kernel, final=final),
        grid=(m // tm, N_GROUPS),
        in_specs=[pl.BlockSpec((tm, GDN_VW), row), pl.BlockSpec((tm, S5_CH), row),
                  pl.BlockSpec((tm, ML_W), row), pl.BlockSpec((tm, D_MODEL), row),
                  pl.BlockSpec((1, tm, PLE_DIM), lambda i, g: (layer, i, 0)),
                  pl.BlockSpec((D_MODEL, D_MODEL), fix), pl.BlockSpec((1, D_MODEL), fix),
                  pl.BlockSpec((D_MODEL, LANES), fix), pl.BlockSpec((1, LANES), fix),
                  pl.BlockSpec((e, D_MODEL, D_EXPERT), wsel),
                  pl.BlockSpec((e, D_MODEL, D_EXPERT), wsel),
                  pl.BlockSpec((e, D_EXPERT, D_MODEL), wsel),
                  pl.BlockSpec((1, D_MODEL), fix), pl.BlockSpec((D_MODEL, D_MODEL), fix),
                  pl.BlockSpec((PLE_DIM, D_MODEL), fix), pl.BlockSpec((1, D_MODEL), fix)],
        out_specs=pl.BlockSpec((tm, D_MODEL), row),
        out_shape=jax.ShapeDtypeStruct((m, D_MODEL), F32),
        scratch_shapes=[pltpu.VMEM((tm, D_MODEL), BF16), pltpu.VMEM((tm, LANES), F32),
                        pltpu.VMEM((tm, D_MODEL), F32)],
        compiler_params=_cparams("parallel", "arbitrary"),
        name="ffn",
    )(og, os_, om, h, p_all, W["w_out"], W["norm_ffn"], W["wr"], W["br"],
      W["wg"], W["wu"], W["wd"], W["norm_ple"], W["w_ple_gate"], W["w_ple_proj"], fn)


MIX_SLAB_W = 6 * ML_W
PAIRS = ML_HEADS // 2
HALF = LANES // 2


def _scan_rows(x, size, op, fill):
    row = lax.broadcasted_iota(jnp.int32, x.shape, 0)
    d = 1
    while d < size:
        x = op(x, jnp.where(row >= d, pltpu.roll(x, d, axis=0), fill))
        d *= 2
    return x


def _split3(a):
    hi = a.astype(BF16)
    r1 = a - hi.astype(F32)
    mid = r1.astype(BF16)
    lo = (r1 - mid.astype(F32)).astype(BF16)
    return hi, mid, lo


def _row_form(x_s, L):
    rr = lax.broadcasted_iota(jnp.int32, x_s.shape, 0)
    cc = lax.broadcasted_iota(jnp.int32, x_s.shape, 1)
    dg = jnp.where(rr == jnp.bitwise_and(cc, L - 1), x_s, 0.0)
    ones = jnp.ones((SUBLANES, L), BF16)
    acc = None
    for piece in _split3(dg):
        t = jnp.dot(ones, piece, preferred_element_type=F32)
        acc = t if acc is None else acc + t
    return acc[0:1, :]


def _score_cols(x, p, L):
    if 2 * L == LANES:
        return x[:, LANES * p:LANES * (p + 1)]
    return jnp.concatenate([x[:, LANES * p:LANES * p + L],
                            x[:, LANES * p + HALF:LANES * p + HALF + L]], axis=-1)


def _bd_stack(x2):
    lo = lax.broadcasted_iota(jnp.int32, x2.shape, 1) < HALF
    return jnp.concatenate([jnp.where(lo, x2, 0.0), jnp.where(lo, 0.0, x2)], axis=0)


def _bd_ones(rows_per_half):
    shape = (2 * rows_per_half, LANES)
    r = lax.broadcasted_iota(jnp.int32, shape, 0) < rows_per_half
    c = lax.broadcasted_iota(jnp.int32, shape, 1) < HALF
    return jnp.where(r == c, 1.0, 0.0).astype(F32)


def _layer_state_spec(s_all, layer, bb, in_idx, out_idx):
    depth = s_all.shape[0]
    tail = s_all.shape[2:]
    if layer == 0 and depth > 1:
        return pl.BlockSpec((depth, bb) + tail, lambda b, c: (0, b, 0, 0, 0)), 0, {}
    return (pl.BlockSpec((1, bb) + tail, lambda b, c: (layer, b, 0, 0, 0)), 0,
            {in_idx: out_idx})


def _half_sums(x2, ones_bd):
    return _dot(x2, ones_bd)


def _mlstm_kernel(a_ref, c0_ref, n0_ref, m0_ref, bi_ref, bf_ref, ng_ref,
                  o_ref, cn_ref, nn_ref, mn_ref, st_ref, m_ref, *, L, tv, bb, lsel):
    cidx = pl.program_id(1)
    nc = pl.num_programs(1)
    ones_ch = _bd_ones(HALF)
    r128 = lax.broadcasted_iota(jnp.int32, (LANES, LANES), 0)
    c128 = lax.broadcasted_iota(jnp.int32, (LANES, LANES), 1)
    diag128 = r128 == c128

    @pl.when(cidx == 0)
    def _():
        st_ref[...] = jnp.zeros(st_ref.shape, F32)
        m_ref[...] = m0_ref[...]
        ob = ones_ch.astype(BF16)
        for i in range(bb):
            for p in range(PAIRS):
                st_ref[i, p, 0:HALF, 0:HALF] = c0_ref[lsel, i, 2 * p]
                st_ref[i, p, HALF:, HALF:LANES] = c0_ref[lsel, i, 2 * p + 1]
                dg = jnp.where(diag128, n0_ref[i, p], 0.0)
                acc = None
                for piece in _split3(dg):
                    t = jnp.dot(piece, ob, preferred_element_type=F32)
                    acc = t if acc is None else acc + t
                st_ref[i, p, :, LANES:] = acc

    W = 2 * L
    rr = lax.broadcasted_iota(jnp.int32, (L, W), 0)
    cc = lax.broadcasted_iota(jnp.int32, (L, W), 1)
    incl = rr >= jnp.bitwise_and(cc, L - 1)
    ones_keys = _bd_ones(L)
    bd256 = jnp.concatenate([ones_ch, ones_ch], axis=-1) > 0.5
    gain2 = ng_ref[...]
    gain2 = jnp.concatenate([gain2, gain2], axis=-1)
    ones_l = jnp.ones((L, LANES), F32)
    outs = [[None] * PAIRS for _ in range(bb)]
    st_news = [[None] * PAIRS for _ in range(bb)]
    m_news = [None] * bb
    rows = []
    for i in range(bb):
        li = a_ref[i, :, 4 * ML_W:5 * ML_W] + bi_ref[...]
        lf = -_softplus(-(a_ref[i, :, 5 * ML_W:6 * ML_W] + bf_ref[...]))
        if tv < L:
            valid = lax.broadcasted_iota(jnp.int32, li.shape, 0) < tv
            li = jnp.where(valid, li, NEG)
            lf = jnp.where(valid, lf, 0.0)
        bcum = _scan_rows(lf, L, jnp.add, 0.0)
        a = li - bcum
        m0 = m_ref[i]
        m_t = bcum + jnp.maximum(m0, _scan_rows(a, L, jnp.maximum, NEG))
        e_inter = jnp.exp(bcum + m0 - m_t)
        m_new = m_t[L - 1:L, :]
        b_last = bcum[L - 1:L, :]
        e_c = jnp.exp(b_last + m0 - m_new)
        kw = a_ref[i, :, ML_W:2 * ML_W] * (ML_DH ** -0.5) * jnp.exp(b_last + a - m_new)
        m_news[i] = m_new
        rows.append((a, bcum, m_t, e_inter, e_c, kw))

    def pair(i, p):
        a, bcum, m_t, e_inter, e_c, kw = rows[i]
        sl = slice(LANES * p, LANES * (p + 1))
        q2 = a_ref[i, :, sl]
        k2 = a_ref[i, :, ML_W + LANES * p:ML_W + LANES * (p + 1)] * (ML_DH ** -0.5)
        v2 = a_ref[i, :, 2 * ML_W + LANES * p:2 * ML_W + LANES * (p + 1)]
        og2 = a_ref[i, :, 3 * ML_W + LANES * p:3 * ML_W + LANES * (p + 1)]
        st = st_ref[i, p]
        arow = _row_form(_score_cols(a, p, L), L)
        qk = _dot_nt(q2, _bd_stack(k2))
        qcn = _dot(q2, st)
        upd = _dot_tn(kw[:, sl], jnp.concatenate([v2, ones_l], axis=-1))
        ec2 = e_c[:, sl]
        st_news[i][p] = st * jnp.concatenate([ec2, ec2], axis=-1) + jnp.where(bd256, upd, 0.0)
        yield
        w_intra = jnp.where(incl, jnp.exp(_score_cols(bcum, p, L) + arow - _score_cols(m_t, p, L)), 0.0)
        s2 = qk * w_intra
        nd = _dot(s2, jnp.concatenate([_bd_stack(v2), ones_keys], axis=-1))
        yield
        e2 = e_inter[:, sl]
        num = e2 * qcn[:, 0:LANES] + nd[:, 0:LANES]
        den = e2 * qcn[:, LANES:] + nd[:, LANES:]
        hh = num / jnp.maximum(jnp.abs(den), jnp.exp(-m_t[:, sl]))
        ss = _half_sums(hh * hh, ones_ch)
        yield
        outs[i][p] = hh * lax.rsqrt(ss * (1.0 / ML_DH) + EPS) * gain2 * _sigmoid(og2)

    _interleave([pair(i, p) for i in range(bb) for p in range(PAIRS)])
    for i in range(bb):
        for p in range(PAIRS):
            st_ref[i, p] = st_news[i][p]
        m_ref[i] = m_news[i]
        o_ref[i] = jnp.concatenate(outs[i], axis=-1).astype(o_ref.dtype)

    @pl.when(cidx == nc - 1)
    def _():
        ones8 = jnp.ones((SUBLANES, LANES), BF16)
        for k in range(cn_ref.shape[0]):
            if k != lsel:
                cn_ref[k] = c0_ref[k]
        for i in range(bb):
            for p in range(PAIRS):
                st = st_news[i][p]
                cn_ref[lsel, i, 2 * p] = st[0:HALF, 0:HALF]
                cn_ref[lsel, i, 2 * p + 1] = st[HALF:, HALF:LANES]
                dg = jnp.where(diag128, st[:, LANES:], 0.0)
                acc = None
                for piece in _split3(dg):
                    t = jnp.dot(ones8, piece, preferred_element_type=F32)
                    acc = t if acc is None else acc + t
                nn_ref[i, p] = acc[0:1, :]
        mn_ref[...] = m_ref[...]


def _mlstm(slab, c_all, layer, n0, m0, bi, bf, ng, L, tv, bb):
    bsz, t, _ = slab.shape
    nc = t // L
    fix2 = lambda b, c: (0, 0)
    c_spec, lsel, aliases = _layer_state_spec(c_all, layer, bb, 1, 1)
    n_spec = pl.BlockSpec((bb, PAIRS, 1, LANES), lambda b, c: (b, 0, 0, 0))
    m_spec = pl.BlockSpec((bb, 1, ML_W), lambda b, c: (b, 0, 0))
    return pl.pallas_call(
        functools.partial(_mlstm_kernel, L=L, tv=tv, bb=bb, lsel=lsel),
        grid=(bsz // bb, nc),
        input_output_aliases=aliases,
        in_specs=[pl.BlockSpec((bb, L, MIX_SLAB_W), lambda b, c: (b, c, 0)),
                  c_spec, n_spec, m_spec,
                  pl.BlockSpec((1, ML_W), fix2), pl.BlockSpec((1, ML_W), fix2),
                  pl.BlockSpec((1, ML_DH), fix2)],
        out_specs=[pl.BlockSpec((bb, L, ML_W), lambda b, c: (b, c, 0)), c_spec, n_spec, m_spec],
        out_shape=[jax.ShapeDtypeStruct((bsz, t, ML_W), BF16),
                   jax.ShapeDtypeStruct(c_all.shape, F32),
                   jax.ShapeDtypeStruct((bsz, PAIRS, 1, LANES), F32),
                   jax.ShapeDtypeStruct((bsz, 1, ML_W), F32)],
        scratch_shapes=[pltpu.VMEM((bb, PAIRS, LANES, 2 * LANES), F32),
                        pltpu.VMEM((bb, 1, ML_W), F32)],
        compiler_params=_cparams("parallel", "arbitrary"),
        name="mlstm",
    )(slab, c_all, n0, m0, bi, bf, ng)


def _gdn_kernel(a_ref, conv0_ref, s0_ref, cw_ref, alog_ref, dtb_ref, ng_ref,
                o_ref, convn_ref, sn_ref, ext_ref, s_ref, *, L, tv, bb, lsel):
    c = pl.program_id(1)
    nc = pl.num_programs(1)

    @pl.when(c == 0)
    def _():
        ext_ref[:, 0:SUBLANES, :] = conv0_ref[...]
        s_ref[...] = jnp.zeros(s_ref.shape, F32)
        for i in range(bb):
            for p in range(PAIRS):
                s_ref[i, p, 0:HALF, 0:HALF] = s0_ref[lsel, i, 2 * p]
                s_ref[i, p, HALF:, HALF:] = s0_ref[lsel, i, 2 * p + 1]

    W = 2 * L
    rr = lax.broadcasted_iota(jnp.int32, (L, W), 0)
    cc = lax.broadcasted_iota(jnp.int32, (L, W), 1)
    key = jnp.bitwise_and(cc, L - 1)
    incl = rr >= key
    strict = rr > key
    first_half = cc < L
    lo = lax.broadcasted_iota(jnp.int32, (L, LANES), 1) < HALF
    ones_ch = _bd_ones(HALF)
    bd128 = ones_ch > 0.5
    cw = cw_ref[...]
    gain2 = ng_ref[...]
    gain2 = jnp.concatenate([gain2, gain2], axis=-1)
    outs = [[None] * PAIRS for _ in range(bb)]
    s_news = [[None] * PAIRS for _ in range(bb)]
    carries = [None] * bb
    rows = []
    for i in range(bb):
        raw = a_ref[i, :, 0:GDN_CONV_CH]
        ext_ref[i, SUBLANES:SUBLANES + L, :] = raw
        ext = ext_ref[i]
        acc = raw * cw[CONV_W - 1:CONV_W, :]
        for j in range(CONV_W - 1):
            sh = pltpu.roll(ext, CONV_W - 1 - j, axis=0)[SUBLANES:SUBLANES + L]
            acc = acc + sh * cw[j:j + 1, :]
        qkv = _silu(acc)
        carries[i] = pltpu.roll(ext, (L + SUBLANES - tv) % (L + SUBLANES), axis=0)[0:SUBLANES]

        beta = _sigmoid(a_ref[i, :, 4 * GDN_VW:5 * GDN_VW])
        g = -jnp.exp(alog_ref[...]) * _softplus(a_ref[i, :, 5 * GDN_VW:6 * GDN_VW] + dtb_ref[...])
        if tv < L:
            valid = lax.broadcasted_iota(jnp.int32, g.shape, 0) < tv
            beta = jnp.where(valid, beta, 0.0)
            g = jnp.where(valid, g, 0.0)
        gcum = _scan_rows(g, L, jnp.add, 0.0)
        g_last = gcum[L - 1:L, :]
        rows.append((qkv, beta, gcum, jnp.exp(gcum), jnp.exp(g_last - gcum), jnp.exp(g_last)))
    for i in range(bb):
        ext_ref[i, 0:SUBLANES, :] = carries[i]

    def pair(i, p):
        qkv, beta, gcum, eg, kdec, sdec = rows[i]
        sl = slice(LANES * p, LANES * (p + 1))
        q2 = qkv[:, LANES * p:LANES * (p + 1)]
        k2 = qkv[:, GDN_QK + LANES * p:GDN_QK + LANES * (p + 1)]
        v2 = qkv[:, 2 * GDN_QK + LANES * p:2 * GDN_QK + LANES * (p + 1)]
        z2 = a_ref[i, :, GDN_CONV_CH + LANES * p:GDN_CONV_CH + LANES * (p + 1)]
        s2 = s_ref[i, p]
        ssq = _half_sums(q2 * q2, ones_ch)
        ssk = _half_sums(k2 * k2, ones_ch)
        grow = _row_form(_score_cols(gcum, p, L), L)
        yield
        q2 = q2 * lax.rsqrt(ssq + EPS) * (GDN_DK ** -0.5)
        k2 = k2 * lax.rsqrt(ssk + EPS)
        beta2 = beta[:, sl]
        eg2 = eg[:, sl]
        kb2 = k2 * beta2
        kq = _dot_nt(jnp.concatenate([_bd_stack(kb2), q2], axis=0), _bd_stack(k2))
        qs = _dot(q2 * eg2, s2)
        yield
        gam = jnp.where(incl, jnp.exp(jnp.minimum(_score_cols(gcum, p, L) - grow, 0.0)), 0.0)
        gam_s = jnp.where(strict, gam, 0.0)
        gam_bd = jnp.concatenate([jnp.where(first_half, gam_s, 0.0),
                                  jnp.where(first_half, 0.0, gam_s)], axis=0)
        t_inv = yield from _unit_lower_inverse(kq[0:W] * gam_bd, W, L)
        rhs = jnp.concatenate([v2 * beta2, kb2 * eg2], axis=-1)
        sol = _dot(t_inv, jnp.concatenate([rhs, rhs], axis=0))
        yield
        u2 = jnp.where(lo, sol[0:L, 0:LANES], sol[L:W, 0:LANES])
        w2 = jnp.where(lo, sol[0:L, LANES:], sol[L:W, LANES:])
        v_new = u2 - _dot(w2, s2)
        yield
        o2 = qs + _dot(kq[W:] * gam, _bd_stack(v_new))
        upd = _dot_tn(k2 * kdec[:, sl], v_new)
        s_news[i][p] = s2 * sdec[:, sl] + jnp.where(bd128, upd, 0.0)
        yield
        ss = _half_sums(o2 * o2, ones_ch)
        yield
        outs[i][p] = o2 * lax.rsqrt(ss * (1.0 / GDN_DV) + EPS) * gain2 * _silu(z2)

    _interleave([pair(i, p) for i in range(bb) for p in range(PAIRS)])
    for i in range(bb):
        for p in range(PAIRS):
            s_ref[i, p] = s_news[i][p]
        o_ref[i] = jnp.concatenate(outs[i], axis=-1).astype(o_ref.dtype)

    @pl.when(c == nc - 1)
    def _():
        for k in range(sn_ref.shape[0]):
            if k != lsel:
                sn_ref[k] = s0_ref[k]
        for i in range(bb):
            convn_ref[i] = carries[i]
            for p in range(PAIRS):
                sn_ref[lsel, i, 2 * p] = s_news[i][p][0:HALF, 0:HALF]
                sn_ref[lsel, i, 2 * p + 1] = s_news[i][p][HALF:, HALF:]


def _gdn(slab, conv0, s_all, layer, cw, alog, dtb, ng, L, tv, bb):
    bsz, t, _ = slab.shape
    nc = t // L
    fix2 = lambda b, c: (0, 0)
    s_spec, lsel, aliases = _layer_state_spec(s_all, layer, bb, 2, 2)
    cv_spec = pl.BlockSpec((bb, SUBLANES, GDN_CONV_CH), lambda b, c: (b, 0, 0))
    return pl.pallas_call(
        functools.partial(_gdn_kernel, L=L, tv=tv, bb=bb, lsel=lsel),
        grid=(bsz // bb, nc),
        input_output_aliases=aliases,
        in_specs=[pl.BlockSpec((bb, L, MIX_SLAB_W), lambda b, c: (b, c, 0)),
                  cv_spec, s_spec,
                  pl.BlockSpec((CONV_W, GDN_CONV_CH), fix2),
                  pl.BlockSpec((1, GDN_VW), fix2), pl.BlockSpec((1, GDN_VW), fix2),
                  pl.BlockSpec((1, GDN_DV), fix2)],
        out_specs=[pl.BlockSpec((bb, L, GDN_VW), lambda b, c: (b, c, 0)), cv_spec, s_spec],
        out_shape=[jax.ShapeDtypeStruct((bsz, t, GDN_VW), BF16),
                   jax.ShapeDtypeStruct((bsz, SUBLANES, GDN_CONV_CH), F32),
                   jax.ShapeDtypeStruct(s_all.shape, F32)],
        scratch_shapes=[pltpu.VMEM((bb, L + SUBLANES, GDN_CONV_CH), F32),
                        pltpu.VMEM((bb, PAIRS, LANES, LANES), F32)],
        compiler_params=_cparams("parallel", "arbitrary"),
        name="gdn",
    )(slab, conv0, s_all, cw, alog, dtb, ng)


def _cmul(ar, ai, br, bi):
    return ar * br - ai * bi, ar * bi + ai * br


def _s5_kernel(u_ref, h0_ref, lb_ref, bw_ref, cw_ref, dsk_ref, wglu_ref,
               o_ref, hn_ref, car_ref, h_ref, *, L, bb):
    c = pl.program_id(1)
    nc = pl.num_programs(1)

    @pl.when(c == 0)
    def _():
        car_ref[...] = h0_ref[...]

    rows = bb * L
    u = u_ref[...].reshape(rows, S5_CH)
    bu = jnp.dot(u.astype(BF16), bw_ref[...], preferred_element_type=F32)
    tiles = rows // SUBLANES
    x_re = bu[:, 0:S5_N].reshape(tiles, SUBLANES, S5_N)
    x_im = bu[:, S5_N:].reshape(tiles, SUBLANES, S5_N)
    p1 = (lb_ref[0:1, :], lb_ref[1:2, :])
    p2 = _cmul(*p1, *p1)
    p4 = _cmul(*p2, *p2)
    p8 = _cmul(*p4, *p4)
    sub = lax.broadcasted_iota(jnp.int32, (SUBLANES, S5_N), 0)
    for d, (pr, pi) in ((1, p1), (2, p2), (4, p4)):
        pr = jnp.where(sub >= d, pr, 0.0)
        pi = jnp.where(sub >= d, pi, 0.0)
        s_re = pltpu.roll(x_re, d, axis=1)
        s_im = pltpu.roll(x_im, d, axis=1)
        x_re, x_im = x_re + (pr * s_re - pi * s_im), x_im + (pr * s_im + pi * s_re)
    x_re = x_re.reshape(rows, S5_N)
    x_im = x_im.reshape(rows, S5_N)
    k = sub + 1
    pw_re = jnp.ones((SUBLANES, S5_N), F32)
    pw_im = jnp.zeros((SUBLANES, S5_N), F32)
    for bit, (pr, pi) in ((1, p1), (2, p2), (4, p4), (8, p8)):
        m_re, m_im = _cmul(pw_re, pw_im, pr, pi)
        on = jnp.bitwise_and(k, bit) != 0
        pw_re = jnp.where(on, m_re, pw_re)
        pw_im = jnp.where(on, m_im, pw_im)
    for i in range(bb):
        car_re = car_ref[i, 0]
        car_im = car_ref[i, 1]
        for t in range(L // SUBLANES):
            r0 = i * L + t * SUBLANES
            a_re, a_im = _cmul(pw_re, pw_im, car_re, car_im)
            t_re = x_re[r0:r0 + SUBLANES, :] + a_re
            t_im = x_im[r0:r0 + SUBLANES, :] + a_im
            h_ref[r0:r0 + SUBLANES, 0:S5_N] = t_re
            h_ref[r0:r0 + SUBLANES, S5_N:] = t_im
            car_re = jnp.broadcast_to(t_re[SUBLANES - 1:SUBLANES, :], (SUBLANES, S5_N))
            car_im = jnp.broadcast_to(t_im[SUBLANES - 1:SUBLANES, :], (SUBLANES, S5_N))
        car_ref[i, 0] = car_re
        car_ref[i, 1] = car_im
        hn_ref[i, 0] = t_re
        hn_ref[i, 1] = t_im
    y = jnp.dot(h_ref[...].astype(BF16), cw_ref[...], preferred_element_type=F32) + dsk_ref[...] * u
    zg = jax.nn.gelu(y)
    out = zg * _sigmoid(jnp.dot(zg.astype(BF16), wglu_ref[...], preferred_element_type=F32))
    o_ref[...] = out.reshape(bb, L, S5_CH).astype(o_ref.dtype)


def _s5(u, h0, lb, bw, cw, dsk, wglu, L, bb):
    bsz, t, _ = u.shape
    nc = t // L
    fix2 = lambda b, c: (0, 0)
    h_spec = pl.BlockSpec((bb, 2, SUBLANES, S5_N), lambda b, c: (b, 0, 0, 0))
    return pl.pallas_call(
        functools.partial(_s5_kernel, L=L, bb=bb),
        grid=(bsz // bb, nc),
        in_specs=[pl.BlockSpec((bb, L, S5_CH), lambda b, c: (b, c, 0)),
                  h_spec,
                  pl.BlockSpec((SUBLANES, S5_N), fix2),
                  pl.BlockSpec((S5_CH, 2 * S5_N), fix2),
                  pl.BlockSpec((2 * S5_N, S5_CH), fix2),
                  pl.BlockSpec((1, S5_CH), fix2),
                  pl.BlockSpec((S5_CH, S5_CH), fix2)],
        out_specs=[pl.BlockSpec((bb, L, S5_CH), lambda b, c: (b, c, 0)), h_spec],
        out_shape=[jax.ShapeDtypeStruct((bsz, t, S5_CH), BF16),
                   jax.ShapeDtypeStruct((bsz, 2, SUBLANES, S5_N), F32)],
        scratch_shapes=[pltpu.VMEM((bb, 2, SUBLANES, S5_N), F32),
                        pltpu.VMEM((bb * L, 2 * S5_N), F32)],
        compiler_params=_cparams("parallel", "arbitrary"),
        name="s5",
    )(u, h0, lb, bw, cw, dsk, wglu)


def _prep_layer(l, P):
    w_in = P["w_in"][l]
    sizes = (GDN_CONV_CH, GDN_VW, GDN_HEADS, GDN_HEADS, S5_CH, 3 * ML_W, ML_HEADS, ML_HEADS, ML_W)
    offs = [0]
    for s in sizes:
        offs.append(offs[-1] + s)
    g_qkv, g_z, g_b, g_a, s_u, m_qkv, m_i, m_f, m_o = [w_in[:, offs[i]:offs[i + 1]] for i in range(9)]
    zpad = jnp.zeros((D_MODEL, LANES - 4 * GDN_HEADS), F32)
    w_cat = jnp.concatenate([g_qkv, g_z, s_u, m_qkv, m_o, g_b, g_a, m_i, m_f, zpad],
                            axis=1).astype(BF16)

    lr = P["s5_lam_re"][l].astype(F32)
    li = P["s5_lam_im"][l].astype(F32)
    dt = jnp.exp(P["s5_log_dt"][l].astype(F32))[:, None]
    mag = jnp.exp(lr * dt)
    lb_re = mag * jnp.cos(li * dt)
    lb_im = mag * jnp.sin(li * dt)
    den = lr * lr + li * li
    c_re = ((lb_re - 1.0) * lr + lb_im * li) / den
    c_im = (lb_im * lr - (lb_re - 1.0) * li) / den
    b_r = P["s5_B_re"][l].astype(F32)
    b_i = P["s5_B_im"][l].astype(F32)
    bb_re = c_re[..., None] * b_r - c_im[..., None] * b_i
    bb_im = c_re[..., None] * b_i + c_im[..., None] * b_r
    eye_g = jnp.eye(S5_GROUPS, dtype=F32)
    bd = lambda m: jnp.einsum("gph,gk->ghkp", m, eye_g).reshape(S5_CH, S5_N)
    bw = jnp.concatenate([bd(bb_re), bd(bb_im)], axis=1).astype(BF16)
    cd = lambda m: jnp.einsum("ghp,gk->gpkh", m, eye_g).reshape(S5_N, S5_CH)
    cw = jnp.concatenate([cd(P["s5_C_re"][l].astype(F32)),
                          -cd(P["s5_C_im"][l].astype(F32))], axis=0).astype(BF16)
    lb = jnp.zeros((SUBLANES, S5_N), F32).at[0].set(lb_re.reshape(-1)).at[1].set(lb_im.reshape(-1))

    wr = jnp.zeros((D_MODEL, LANES), F32)
    wr = wr.at[:, 0:N_GROUPS].set(P["w_router_group"][l])
    wr = wr.at[:, N_GROUPS:N_GROUPS + N_EXPERTS].set(P["w_router_expert"][l])
    br = jnp.zeros((1, LANES), F32)
    br = br.at[0, 0:N_GROUPS].set(P["b_router_group"][l])
    br = br.at[0, N_GROUPS:N_GROUPS + N_EXPERTS].set(P["b_router_expert"][l])

    rep_row = lambda v: jnp.repeat(v.astype(F32), ML_DH).reshape(1, ML_W)
    return dict(
        norm_mix=P["norm_mix"][l].reshape(1, D_MODEL).astype(F32),
        w_cat=w_cat,
        w_out=P["w_out"][l].astype(BF16),
        conv_w=P["gdn_conv_w"][l].astype(F32),
        alog=rep_row(P["gdn_A_log"][l]), dtb=rep_row(P["gdn_dt_bias"][l]),
        gdn_norm=P["gdn_norm"][l].reshape(1, GDN_DV).astype(F32),
        lb=lb, bw=bw, cw=cw,
        s5_d=P["s5_D"][l].reshape(1, S5_CH).astype(F32),
        w_glu=P["s5_w_glu"][l].astype(BF16),
        ml_bi=rep_row(P["ml_ig_bias"][l]), ml_bf=rep_row(P["ml_fg_bias"][l]),
        ml_norm=P["ml_norm"][l].reshape(1, ML_DH).astype(F32),
        norm_ffn=P["norm_ffn"][l].reshape(1, D_MODEL).astype(F32),
        wr=wr, br=br,
        wg=P["w_exp_gate"][l].astype(BF16), wu=P["w_exp_up"][l].astype(BF16),
        wd=P["w_exp_down"][l].astype(BF16),
        norm_ple=P["norm_ple"][l].reshape(1, D_MODEL).astype(F32),
        w_ple_gate=P["w_ple_gate"][l].astype(BF16),
        w_ple_proj=P["w_ple_proj"][l].astype(BF16),
    )


def _trunk(x, p, states, layers, final_norm, *, L, tv, Ls, tm, bb, bbs):
    conv0, gdn0, s5re0, s5im0, mc0, mn0, mm0 = states
    bsz, t, _ = x.shape
    m = bsz * t
    h = x.reshape(m, D_MODEL)
    outs = [[] for _ in range(5)]
    last_row = (tv - 1) % SUBLANES
    p_all = p.reshape(p.shape[0], m, PLE_DIM)
    e_rep = (jnp.arange(2 * GATE_W)[None, :] // ML_DH == jnp.arange(LANES)[:, None]).astype(BF16)
    gdn_all, mc_all = gdn0, mc0
    for l, W in enumerate(layers):
        slab_g, s_u, slab_m = _norm_inproj(h, W["norm_mix"], W["w_cat"], e_rep, tm)
        conv_in = jnp.pad(conv0[l], ((0, 0), (SUBLANES - (CONV_W - 1), 0), (0, 0)))
        o_gdn, conv_n, gdn_all = _gdn(slab_g.reshape(bsz, t, MIX_SLAB_W), conv_in, gdn_all, l,
                                      W["conv_w"], W["alog"], W["dtb"], W["gdn_norm"], L, tv, bb)
        h0 = jnp.stack([s5re0[l].reshape(bsz, S5_N), s5im0[l].reshape(bsz, S5_N)], axis=1)
        h0 = jnp.broadcast_to(h0[:, :, None, :], (bsz, 2, SUBLANES, S5_N))
        o_s5, s5_n = _s5(s_u.reshape(bsz, t, S5_CH), h0, W["lb"], W["bw"], W["cw"], W["s5_d"],
                         W["w_glu"], Ls, bbs)
        o_ml, mc_all, n_n, m_n = _mlstm(slab_m.reshape(bsz, t, MIX_SLAB_W), mc_all, l,
                                        mn0[l].reshape(bsz, PAIRS, 1, LANES),
                                        jnp.repeat(mm0[l], ML_DH, axis=-1).reshape(bsz, 1, ML_W),
                                        W["ml_bi"], W["ml_bf"], W["ml_norm"], L, tv, bb)
        h = _ffn(o_gdn.reshape(m, GDN_VW), o_s5.reshape(m, S5_CH), o_ml.reshape(m, ML_W), h,
                 p_all, l, W, final_norm, tm, l == len(layers) - 1)
        outs[0].append(conv_n[:, SUBLANES - (CONV_W - 1):])
        outs[1].append(s5_n[:, 0, last_row].reshape(bsz, S5_GROUPS, S5_STATE))
        outs[2].append(s5_n[:, 1, last_row].reshape(bsz, S5_GROUPS, S5_STATE))
        outs[3].append(n_n.reshape(bsz, ML_HEADS, ML_DH))
        outs[4].append(m_n[:, 0, ::ML_DH])
    conv_o, s5re_o, s5im_o, mn_o, mm_o = (jnp.stack(o) for o in outs)
    return (h.reshape(bsz, t, D_MODEL), conv_o, gdn_all, s5re_o, s5im_o, mc_all, mn_o, mm_o)


def kernel(x_prompt, x_sample, p_prompt, p_sample, state_gdn_conv, state_gdn, state_s5_re, state_s5_im, state_mlstm_C, state_mlstm_n, state_mlstm_m, norm_mix, w_in, w_out, gdn_conv_w, gdn_A_log, gdn_dt_bias, gdn_norm, s5_lam_re, s5_lam_im, s5_log_dt, s5_B_re, s5_B_im, s5_C_re, s5_C_im, s5_D, s5_w_glu, ml_ig_bias, ml_fg_bias, ml_norm, norm_ffn, w_router_group, b_router_group, w_router_expert, b_router_expert, w_exp_gate, w_exp_up, w_exp_down, norm_ple, w_ple_gate, w_ple_proj, final_norm):
    P = dict(norm_mix=norm_mix, w_in=w_in, w_out=w_out, gdn_conv_w=gdn_conv_w, gdn_A_log=gdn_A_log,
             gdn_dt_bias=gdn_dt_bias, gdn_norm=gdn_norm, s5_lam_re=s5_lam_re, s5_lam_im=s5_lam_im,
             s5_log_dt=s5_log_dt, s5_B_re=s5_B_re, s5_B_im=s5_B_im, s5_C_re=s5_C_re, s5_C_im=s5_C_im,
             s5_D=s5_D, s5_w_glu=s5_w_glu, ml_ig_bias=ml_ig_bias, ml_fg_bias=ml_fg_bias,
             ml_norm=ml_norm, norm_ffn=norm_ffn, w_router_group=w_router_group,
             b_router_group=b_router_group, w_router_expert=w_router_expert,
             b_router_expert=b_router_expert, w_exp_gate=w_exp_gate, w_exp_up=w_exp_up,
             w_exp_down=w_exp_down, norm_ple=norm_ple, w_ple_gate=w_ple_gate, w_ple_proj=w_ple_proj)
    depth = norm_mix.shape[0]
    layers = [_prep_layer(l, P) for l in range(depth)]
    fnorm = final_norm.reshape(1, D_MODEL).astype(F32)

    bp, tp, _ = x_prompt.shape
    zeros = lambda *s: jnp.zeros((depth, bp) + s, F32)
    prompt_init = (zeros(CONV_W - 1, GDN_CONV_CH), zeros(GDN_HEADS, GDN_DK, GDN_DV),
                   zeros(S5_GROUPS, S5_STATE), zeros(S5_GROUPS, S5_STATE),
                   zeros(ML_HEADS, ML_DH, ML_DH), zeros(ML_HEADS, ML_DH), zeros(ML_HEADS))
    lp = math.gcd(tp, 64)
    lsp = math.gcd(tp, 256)
    res_p = _trunk(x_prompt, p_prompt, prompt_init, layers, fnorm,
                   L=lp, tv=lp, Ls=lsp, tm=512, bb=8, bbs=1)

    bs, ts, _ = x_sample.shape
    tpad = -(-ts // SUBLANES) * SUBLANES
    xs = jnp.pad(x_sample, ((0, 0), (0, tpad - ts), (0, 0)))
    ps = jnp.pad(p_sample, ((0, 0), (0, 0), (0, tpad - ts), (0, 0)))
    sample_init = (state_gdn_conv, state_gdn, state_s5_re, state_s5_im,
                   state_mlstm_C, state_mlstm_n, state_mlstm_m)
    res_s = _trunk(xs, ps, sample_init, layers, fnorm,
                   L=tpad, tv=ts, Ls=tpad, tm=512, bb=8, bbs=8)
    y_sample = res_s[0][:, :ts]
    return (res_p[0], y_sample) + res_p[1:] + res_s[1:]
```

```python
import functools
import math

import jax
import jax.numpy as jnp
from jax import lax
from jax.experimental import pallas as pl
from jax.experimental.pallas import tpu as pltpu

F32 = jnp.float32
BF16 = jnp.bfloat16

D_MODEL = 1024
DEPTH = 2
GDN_HEADS = 6
GDN_DK = 64
GDN_DV = 64
GDN_QK = GDN_HEADS * GDN_DK
GDN_VW = GDN_HEADS * GDN_DV
GDN_CONV_CH = 2 * GDN_QK + GDN_VW
CONV_W = 4
S5_GROUPS = 16
S5_GROUP_CH = 16
S5_CH = S5_GROUPS * S5_GROUP_CH
S5_STATE = 64
S5_N = S5_GROUPS * S5_STATE
ML_HEADS = 6
ML_DH = 64
ML_W = ML_HEADS * ML_DH
N_GROUPS = 4
EXPERTS_PER_GROUP = 4
N_EXPERTS = N_GROUPS * EXPERTS_PER_GROUP
D_EXPERT = 256
PLE_DIM = 256
EPS = 1e-6

LANES = 128
SUBLANES = 8
NEG = -1e30
VMEM_LIMIT = 56 * 1024 * 1024

def _cparams(*sem):
    return pltpu.CompilerParams(dimension_semantics=sem, vmem_limit_bytes=VMEM_LIMIT)


def _dot(a, b):
    return jnp.dot(a.astype(BF16), b.astype(BF16), preferred_element_type=F32)


def _dot_nt(a, b):
    return lax.dot_general(a.astype(BF16), b.astype(BF16), (((1,), (1,)), ((), ())),
                           preferred_element_type=F32)


def _dot_tn(a, b):
    return lax.dot_general(a.astype(BF16), b.astype(BF16), (((0,), (0,)), ((), ())),
                           preferred_element_type=F32)


def _split_bf16(a):
    hi = a.astype(BF16)
    lo = (a - hi.astype(F32)).astype(BF16)
    return hi, lo


def _rms(x, gain):
    return x * lax.rsqrt(jnp.mean(x * x, axis=-1, keepdims=True) + EPS) * gain


def _softplus(x):
    return jnp.maximum(x, 0.0) + jnp.log(1.0 + jnp.exp(-jnp.abs(x)))


def _sigmoid(x):
    return 1.0 / (1.0 + jnp.exp(-x))


def _silu(x):
    return x * _sigmoid(x)


def _interleave(gens):
    live = list(gens)
    while live:
        still = []
        for g in live:
            try:
                next(g)
                still.append(g)
            except StopIteration:
                pass
        live = still


def _unit_lower_inverse(n_mat, size, top=None):
    top = size if top is None else top
    r = lax.broadcasted_iota(jnp.int32, (size, size), 0)
    c = lax.broadcasted_iota(jnp.int32, (size, size), 1)
    base = min(16, top)
    same = jnp.bitwise_xor(r, c) < base
    nd = jnp.where(same, n_mat, 0.0)
    eye = jnp.where(r == c, 1.0, 0.0).astype(F32)
    t = eye - nd
    p = 1
    if 2 * p < base:
        x = _dot(nd, nd)
        yield
    while 2 * p < base:
        t_next = t + _dot(t, x)
        if 4 * p < base:
            x = _dot(x, x)
        t = t_next
        yield
        p *= 2
    blk = base
    while blk < top:
        pair = jnp.bitwise_xor(r, c)
        off = jnp.where((pair < 2 * blk) & (pair >= blk), n_mat, 0.0)
        ot = _dot(off, t)
        yield
        t = t - _dot(t, ot)
        yield
        blk *= 2
    return t


MIX_W = 4 * ML_W
GATE_W = 2 * ML_W


def _norm_inproj_kernel(x_ref, g_ref, w_ref, e_ref, og_ref, os_ref, om_ref):
    u = _rms(x_ref[...], g_ref[...]).astype(BF16)
    dot = functools.partial(jnp.dot, preferred_element_type=F32)
    og_ref[:, 0:MIX_W] = dot(u, w_ref[:, 0:MIX_W])
    os_ref[...] = dot(u, w_ref[:, MIX_W:MIX_W + S5_CH])
    om_ref[:, 0:MIX_W] = dot(u, w_ref[:, MIX_W + S5_CH:2 * MIX_W + S5_CH])
    small = dot(u, w_ref[:, 2 * MIX_W + S5_CH:])
    rep = None
    for piece in _split3(small):
        t = dot(piece, e_ref[...])
        rep = t if rep is None else rep + t
    og_ref[:, MIX_W:] = rep[:, 0:GATE_W]
    om_ref[:, MIX_W:] = rep[:, GATE_W:]


def _norm_inproj(x, gain, w, e_rep, tm):
    m = x.shape[0]
    fix = lambda i: (0, 0)
    widths = (MIX_SLAB_W, S5_CH, MIX_SLAB_W)
    return pl.pallas_call(
        _norm_inproj_kernel,
        grid=(m // tm,),
        in_specs=[pl.BlockSpec((tm, D_MODEL), lambda i: (i, 0)),
                  pl.BlockSpec((1, D_MODEL), fix),
                  pl.BlockSpec((D_MODEL, w.shape[1]), fix),
                  pl.BlockSpec((LANES, 2 * GATE_W), fix)],
        out_specs=[pl.BlockSpec((tm, wd), lambda i: (i, 0)) for wd in widths],
        out_shape=[jax.ShapeDtypeStruct((m, wd), F32) for wd in widths],
        compiler_params=_cparams("parallel"),
        name="norm_inproj",
    )(x, gain, w, e_rep)


def _route(f, wr, br):
    fh, fl = _split_bf16(f)
    wh, wl = _split_bf16(wr)
    d = functools.partial(jnp.dot, preferred_element_type=F32)
    logits = d(fh, wh) + (d(fh, wl) + d(fl, wh)) + br
    lane = lax.broadcasted_iota(jnp.int32, logits.shape, 1)
    is_g = lane < N_GROUPS
    gl = jnp.where(is_g, logits, NEG)
    gmax = jnp.max(gl, axis=-1, keepdims=True)
    ge = jnp.where(is_g, jnp.exp(gl - gmax), 0.0)
    p_grp = ge / jnp.sum(ge, axis=-1, keepdims=True)
    g_prob = jnp.max(p_grp, axis=-1, keepdims=True)
    g_idx = jnp.min(jnp.where(is_g & (gl == gmax), lane, LANES), axis=-1, keepdims=True)
    e_lane = lane - N_GROUPS
    is_e = (e_lane >= 0) & (e_lane < N_EXPERTS) & (jnp.right_shift(e_lane, 2) == g_idx)
    le = jnp.where(is_e, logits, NEG)
    m1 = jnp.max(le, axis=-1, keepdims=True)
    i1 = jnp.min(jnp.where(is_e & (le == m1), lane, LANES), axis=-1, keepdims=True)
    is_e2 = is_e & (lane != i1)
    le2 = jnp.where(is_e2, logits, NEG)
    m2 = jnp.max(le2, axis=-1, keepdims=True)
    i2 = jnp.min(jnp.where(is_e2 & (le2 == m2), lane, LANES), axis=-1, keepdims=True)
    e2 = jnp.exp(m2 - m1)
    w1 = g_prob / (1.0 + e2)
    w2 = g_prob * e2 / (1.0 + e2)
    return fh, jnp.where(lane == i1, w1, 0.0) + jnp.where(lane == i2, w2, 0.0)


def _ffn_kernel(og_ref, os_ref, om_ref, h_ref, p_ref, wo_ref, nf_ref, wr_ref, br_ref,
                wg_ref, wu_ref, wd_ref, np_ref, wpg_ref, wpp_ref, fn_ref,
                out_ref, f_ref, gates_ref, *, final):
    acc_ref = out_ref
    gi = pl.program_id(1)

    @pl.when(gi == 0)
    def _():
        h1 = h_ref[...]
        h1 = h1 + jnp.dot(og_ref[...], wo_ref[0:GDN_VW, :], preferred_element_type=F32)
        h1 = h1 + jnp.dot(os_ref[...], wo_ref[GDN_VW:GDN_VW + S5_CH, :], preferred_element_type=F32)
        h1 = h1 + jnp.dot(om_ref[...], wo_ref[GDN_VW + S5_CH:, :], preferred_element_type=F32)
        acc_ref[...] = h1
        fh, gates = _route(_rms(h1, nf_ref[...]), wr_ref[...], br_ref[...])
        f_ref[...] = fh
        gates_ref[...] = gates

    x = f_ref[...]
    gates = gates_ref[...]
    lane = lax.broadcasted_iota(jnp.int32, gates.shape, 1)
    base = N_GROUPS + EXPERTS_PER_GROUP * gi
    acc = None
    for j in range(EXPERTS_PER_GROUP):
        gcol = jnp.sum(jnp.where(lane == base + j, gates, 0.0), axis=-1, keepdims=True)
        hg = jnp.dot(x, wg_ref[j], preferred_element_type=F32)
        hu = jnp.dot(x, wu_ref[j], preferred_element_type=F32)
        hidden = (_silu(hg) * hu * gcol).astype(BF16)
        t = jnp.dot(hidden, wd_ref[j], preferred_element_type=F32)
        acc = t if acc is None else acc + t
    acc_ref[...] += acc

    @pl.when(gi == N_GROUPS - 1)
    def _():
        h = acc_ref[...]
        gate = _sigmoid(jnp.dot(_rms(h, np_ref[...]).astype(BF16), wpg_ref[...],
                                preferred_element_type=F32))
        proj = jnp.dot(p_ref[0].astype(BF16), wpp_ref[...], preferred_element_type=F32)
        h = h + proj * gate
        if final:
            h = _rms(h, fn_ref[...])
        out_ref[...] = h


def _ffn(og, os_, om, h, p_all, layer, W, fn, tm, final):
    m = h.shape[0]
    row = lambda i, g: (i, 0)
    wsel = lambda i, g: (g, 0, 0)
    e = EXPERTS_PER_GROUP

    def fix(shape):
        return pl.BlockSpec(shape, lambda i, g: (0, 0), pipeline_mode=pl.Buffered(1))

    return pl.pallas_call(
        functools.partial(_ffn_kernel, final=final),
        grid=(m // tm, N_GROUPS),
        in_specs=[pl.BlockSpec((tm, GDN_VW), row), pl.BlockSpec((tm, S5_CH), row),
                  pl.BlockSpec((tm, ML_W), row), pl.BlockSpec((tm, D_MODEL), row),
                  pl.BlockSpec((1, tm, PLE_DIM), lambda i, g: (layer, i, 0)),
                  fix((D_MODEL, D_MODEL)), fix((1, D_MODEL)),
                  fix((D_MODEL, LANES)), fix((1, LANES)),
                  pl.BlockSpec((e, D_MODEL, D_EXPERT), wsel),
                  pl.BlockSpec((e, D_MODEL, D_EXPERT), wsel),
                  pl.BlockSpec((e, D_EXPERT, D_MODEL), wsel),
                  fix((1, D_MODEL)), fix((D_MODEL, D_MODEL)),
                  fix((PLE_DIM, D_MODEL)), fix((1, D_MODEL))],
        out_specs=pl.BlockSpec((tm, D_MODEL), row),
        out_shape=jax.ShapeDtypeStruct((m, D_MODEL), F32),
        scratch_shapes=[pltpu.VMEM((tm, D_MODEL), BF16), pltpu.VMEM((tm, LANES), F32)],
        compiler_params=_cparams("parallel", "arbitrary"),
        name="ffn",
    )(og, os_, om, h, p_all, W["w_out"], W["norm_ffn"], W["wr"], W["br"],
      W["wg"], W["wu"], W["wd"], W["norm_ple"], W["w_ple_gate"], W["w_ple_proj"], fn)


MIX_SLAB_W = 6 * ML_W
PAIRS = ML_HEADS // 2
HALF = LANES // 2


def _scan_rows(x, size, op, fill):
    row = lax.broadcasted_iota(jnp.int32, x.shape, 0)
    d = 1
    while d < size:
        x = op(x, jnp.where(row >= d, pltpu.roll(x, d, axis=0), fill))
        d *= 2
    return x


def _split3(a):
    hi = a.astype(BF16)
    r1 = a - hi.astype(F32)
    mid = r1.astype(BF16)
    lo = (r1 - mid.astype(F32)).astype(BF16)
    return hi, mid, lo


def _row_form(x_s, L):
    rr = lax.broadcasted_iota(jnp.int32, x_s.shape, 0)
    cc = lax.broadcasted_iota(jnp.int32, x_s.shape, 1)
    dg = jnp.where(rr == jnp.bitwise_and(cc, L - 1), x_s, 0.0)
    ones = jnp.ones((SUBLANES, L), BF16)
    acc = None
    for piece in _split3(dg):
        t = jnp.dot(ones, piece, preferred_element_type=F32)
        acc = t if acc is None else acc + t
    return acc[0:1, :]


def _score_cols(x, p, L):
    if 2 * L == LANES:
        return x[:, LANES * p:LANES * (p + 1)]
    return jnp.concatenate([x[:, LANES * p:LANES * p + L],
                            x[:, LANES * p + HALF:LANES * p + HALF + L]], axis=-1)


def _bd_stack(x2):
    lo = lax.broadcasted_iota(jnp.int32, x2.shape, 1) < HALF
    return jnp.concatenate([jnp.where(lo, x2, 0.0), jnp.where(lo, 0.0, x2)], axis=0)


def _bd_ones(rows_per_half):
    shape = (2 * rows_per_half, LANES)
    r = lax.broadcasted_iota(jnp.int32, shape, 0) < rows_per_half
    c = lax.broadcasted_iota(jnp.int32, shape, 1) < HALF
    return jnp.where(r == c, 1.0, 0.0).astype(F32)


def _layer_state_spec(s_all, layer, bb, in_idx, out_idx):
    depth = s_all.shape[0]
    tail = s_all.shape[2:]
    if layer == 0 and depth > 1:
        return pl.BlockSpec((depth, bb) + tail, lambda b, c: (0, b, 0, 0, 0)), 0, {}
    return (pl.BlockSpec((1, bb) + tail, lambda b, c: (layer, b, 0, 0, 0)), 0,
            {in_idx: out_idx})


def _half_sums(x2, ones_bd):
    return _dot(x2, ones_bd)


def _mlstm_kernel(a_ref, c0_ref, n0_ref, m0_ref, bi_ref, bf_ref, ng_ref,
                  o_ref, cn_ref, nn_ref, mn_ref, st_ref, m_ref, *, L, tv, bb, lsel):
    cidx = pl.program_id(1)
    nc = pl.num_programs(1)
    ones_ch = _bd_ones(HALF)
    r128 = lax.broadcasted_iota(jnp.int32, (LANES, LANES), 0)
    c128 = lax.broadcasted_iota(jnp.int32, (LANES, LANES), 1)
    diag128 = r128 == c128

    @pl.when(cidx == 0)
    def _():
        st_ref[...] = jnp.zeros(st_ref.shape, F32)
        m_ref[...] = m0_ref[...]
        ob = ones_ch.astype(BF16)
        for i in range(bb):
            for p in range(PAIRS):
                st_ref[i, p, 0:HALF, 0:HALF] = c0_ref[lsel, i, 2 * p]
                st_ref[i, p, HALF:, HALF:LANES] = c0_ref[lsel, i, 2 * p + 1]
                dg = jnp.where(diag128, n0_ref[i, p], 0.0)
                acc = None
                for piece in _split3(dg):
                    t = jnp.dot(piece, ob, preferred_element_type=F32)
                    acc = t if acc is None else acc + t
                st_ref[i, p, :, LANES:] = acc

    W = 2 * L
    rr = lax.broadcasted_iota(jnp.int32, (L, W), 0)
    cc = lax.broadcasted_iota(jnp.int32, (L, W), 1)
    incl = rr >= jnp.bitwise_and(cc, L - 1)
    ones_keys = _bd_ones(L)
    bd256 = jnp.concatenate([ones_ch, ones_ch], axis=-1) > 0.5
    gain2 = ng_ref[...]
    gain2 = jnp.concatenate([gain2, gain2], axis=-1)
    ones_l = jnp.ones((L, LANES), F32)
    outs = [[None] * PAIRS for _ in range(bb)]
    st_news = [[None] * PAIRS for _ in range(bb)]
    m_news = [None] * bb
    rows = []
    for i in range(bb):
        li = a_ref[i, :, 4 * ML_W:5 * ML_W] + bi_ref[...]
        lf = -_softplus(-(a_ref[i, :, 5 * ML_W:6 * ML_W] + bf_ref[...]))
        if tv < L:
            valid = lax.broadcasted_iota(jnp.int32, li.shape, 0) < tv
            li = jnp.where(valid, li, NEG)
            lf = jnp.where(valid, lf, 0.0)
        bcum = _scan_rows(lf, L, jnp.add, 0.0)
        a = li - bcum
        m0 = m_ref[i]
        m_t = bcum + jnp.maximum(m0, _scan_rows(a, L, jnp.maximum, NEG))
        e_inter = jnp.exp(bcum + m0 - m_t)
        m_new = m_t[L - 1:L, :]
        b_last = bcum[L - 1:L, :]
        e_c = jnp.exp(b_last + m0 - m_new)
        kw = a_ref[i, :, ML_W:2 * ML_W] * (ML_DH ** -0.5) * jnp.exp(b_last + a - m_new)
        m_news[i] = m_new
        rows.append((a, bcum, m_t, e_inter, e_c, kw))

    def pair(i, p):
        a, bcum, m_t, e_inter, e_c, kw = rows[i]
        sl = slice(LANES * p, LANES * (p + 1))
        q2 = a_ref[i, :, sl]
        k2 = a_ref[i, :, ML_W + LANES * p:ML_W + LANES * (p + 1)] * (ML_DH ** -0.5)
        v2 = a_ref[i, :, 2 * ML_W + LANES * p:2 * ML_W + LANES * (p + 1)]
        og2 = a_ref[i, :, 3 * ML_W + LANES * p:3 * ML_W + LANES * (p + 1)]
        st = st_ref[i, p]
        arow = _row_form(_score_cols(a, p, L), L)
        qk = _dot_nt(q2, _bd_stack(k2))
        qcn = _dot(q2, st)
        upd = _dot_tn(kw[:, sl], jnp.concatenate([v2, ones_l], axis=-1))
        ec2 = e_c[:, sl]
        st_news[i][p] = st * jnp.concatenate([ec2, ec2], axis=-1) + jnp.where(bd256, upd, 0.0)
        yield
        w_intra = jnp.where(incl, jnp.exp(_score_cols(bcum, p, L) + arow - _score_cols(m_t, p, L)), 0.0)
        s2 = qk * w_intra
        nd = _dot(s2, jnp.concatenate([_bd_stack(v2), ones_keys], axis=-1))
        yield
        e2 = e_inter[:, sl]
        num = e2 * qcn[:, 0:LANES] + nd[:, 0:LANES]
        den = e2 * qcn[:, LANES:] + nd[:, LANES:]
        hh = num / jnp.maximum(jnp.abs(den), jnp.exp(-m_t[:, sl]))
        ss = _half_sums(hh * hh, ones_ch)
        yield
        outs[i][p] = hh * lax.rsqrt(ss * (1.0 / ML_DH) + EPS) * gain2 * _sigmoid(og2)

    _interleave([pair(i, p) for i in range(bb) for p in range(PAIRS)])
    for i in range(bb):
        for p in range(PAIRS):
            st_ref[i, p] = st_news[i][p]
        m_ref[i] = m_news[i]
        o_ref[i] = jnp.concatenate(outs[i], axis=-1).astype(o_ref.dtype)

    @pl.when(cidx == nc - 1)
    def _():
        ones8 = jnp.ones((SUBLANES, LANES), BF16)
        for k in range(cn_ref.shape[0]):
            if k != lsel:
                cn_ref[k] = c0_ref[k]
        for i in range(bb):
            for p in range(PAIRS):
                st = st_news[i][p]
                cn_ref[lsel, i, 2 * p] = st[0:HALF, 0:HALF]
                cn_ref[lsel, i, 2 * p + 1] = st[HALF:, HALF:LANES]
                dg = jnp.where(diag128, st[:, LANES:], 0.0)
                acc = None
                for piece in _split3(dg):
                    t = jnp.dot(ones8, piece, preferred_element_type=F32)
                    acc = t if acc is None else acc + t
                nn_ref[i, p] = acc[0:1, :]
        mn_ref[...] = m_ref[...]


def _mlstm(slab, c_all, layer, n0, m0, bi, bf, ng, L, tv, bb):
    bsz, t, _ = slab.shape
    nc = t // L
    fix2 = lambda b, c: (0, 0)
    c_spec, lsel, aliases = _layer_state_spec(c_all, layer, bb, 1, 1)
    n_spec = pl.BlockSpec((bb, PAIRS, 1, LANES), lambda b, c: (b, 0, 0, 0))
    m_spec = pl.BlockSpec((bb, 1, ML_W), lambda b, c: (b, 0, 0))
    return pl.pallas_call(
        functools.partial(_mlstm_kernel, L=L, tv=tv, bb=bb, lsel=lsel),
        grid=(bsz // bb, nc),
        input_output_aliases=aliases,
        in_specs=[pl.BlockSpec((bb, L, MIX_SLAB_W), lambda b, c: (b, c, 0)),
                  c_spec, n_spec, m_spec,
                  pl.BlockSpec((1, ML_W), fix2), pl.BlockSpec((1, ML_W), fix2),
                  pl.BlockSpec((1, ML_DH), fix2)],
        out_specs=[pl.BlockSpec((bb, L, ML_W), lambda b, c: (b, c, 0)), c_spec, n_spec, m_spec],
        out_shape=[jax.ShapeDtypeStruct((bsz, t, ML_W), BF16),
                   jax.ShapeDtypeStruct(c_all.shape, F32),
                   jax.ShapeDtypeStruct((bsz, PAIRS, 1, LANES), F32),
                   jax.ShapeDtypeStruct((bsz, 1, ML_W), F32)],
        scratch_shapes=[pltpu.VMEM((bb, PAIRS, LANES, 2 * LANES), F32),
                        pltpu.VMEM((bb, 1, ML_W), F32)],
        compiler_params=_cparams("parallel", "arbitrary"),
        name="mlstm",
    )(slab, c_all, n0, m0, bi, bf, ng)


def _gdn_kernel(a_ref, conv0_ref, s0_ref, cw_ref, alog_ref, dtb_ref, ng_ref,
                o_ref, convn_ref, sn_ref, ext_ref, s_ref, *, L, tv, bb, lsel):
    c = pl.program_id(1)
    nc = pl.num_programs(1)

    @pl.when(c == 0)
    def _():
        ext_ref[:, 0:SUBLANES, :] = conv0_ref[...]
        s_ref[...] = jnp.zeros(s_ref.shape, F32)
        for i in range(bb):
            for p in range(PAIRS):
                s_ref[i, p, 0:HALF, 0:HALF] = s0_ref[lsel, i, 2 * p]
                s_ref[i, p, HALF:, HALF:] = s0_ref[lsel, i, 2 * p + 1]

    W = 2 * L
    rr = lax.broadcasted_iota(jnp.int32, (L, W), 0)
    cc = lax.broadcasted_iota(jnp.int32, (L, W), 1)
    key = jnp.bitwise_and(cc, L - 1)
    incl = rr >= key
    strict = rr > key
    first_half = cc < L
    lo = lax.broadcasted_iota(jnp.int32, (L, LANES), 1) < HALF
    ones_ch = _bd_ones(HALF)
    bd128 = ones_ch > 0.5
    cw = cw_ref[...]
    gain2 = ng_ref[...]
    gain2 = jnp.concatenate([gain2, gain2], axis=-1)
    outs = [[None] * PAIRS for _ in range(bb)]
    s_news = [[None] * PAIRS for _ in range(bb)]
    carries = [None] * bb
    rows = []
    for i in range(bb):
        raw = a_ref[i, :, 0:GDN_CONV_CH]
        ext_ref[i, SUBLANES:SUBLANES + L, :] = raw
        ext = ext_ref[i]
        acc = raw * cw[CONV_W - 1:CONV_W, :]
        for j in range(CONV_W - 1):
            sh = pltpu.roll(ext, CONV_W - 1 - j, axis=0)[SUBLANES:SUBLANES + L]
            acc = acc + sh * cw[j:j + 1, :]
        qkv = _silu(acc)
        carries[i] = pltpu.roll(ext, (L + SUBLANES - tv) % (L + SUBLANES), axis=0)[0:SUBLANES]

        beta = _sigmoid(a_ref[i, :, 4 * GDN_VW:5 * GDN_VW])
        g = -jnp.exp(alog_ref[...]) * _softplus(a_ref[i, :, 5 * GDN_VW:6 * GDN_VW] + dtb_ref[...])
        if tv < L:
            valid = lax.broadcasted_iota(jnp.int32, g.shape, 0) < tv
            beta = jnp.where(valid, beta, 0.0)
            g = jnp.where(valid, g, 0.0)
        gcum = _scan_rows(g, L, jnp.add, 0.0)
        g_last = gcum[L - 1:L, :]
        rows.append((qkv, beta, gcum, jnp.exp(gcum), jnp.exp(g_last - gcum), jnp.exp(g_last)))
    for i in range(bb):
        ext_ref[i, 0:SUBLANES, :] = carries[i]

    def pair(i, p):
        qkv, beta, gcum, eg, kdec, sdec = rows[i]
        sl = slice(LANES * p, LANES * (p + 1))
        q2 = qkv[:, LANES * p:LANES * (p + 1)]
        k2 = qkv[:, GDN_QK + LANES * p:GDN_QK + LANES * (p + 1)]
        v2 = qkv[:, 2 * GDN_QK + LANES * p:2 * GDN_QK + LANES * (p + 1)]
        z2 = a_ref[i, :, GDN_CONV_CH + LANES * p:GDN_CONV_CH + LANES * (p + 1)]
        s2 = s_ref[i, p]
        ssq = _half_sums(q2 * q2, ones_ch)
        ssk = _half_sums(k2 * k2, ones_ch)
        grow = _row_form(_score_cols(gcum, p, L), L)
        yield
        q2 = q2 * lax.rsqrt(ssq + EPS) * (GDN_DK ** -0.5)
        k2 = k2 * lax.rsqrt(ssk + EPS)
        beta2 = beta[:, sl]
        eg2 = eg[:, sl]
        kb2 = k2 * beta2
        kq = _dot_nt(jnp.concatenate([_bd_stack(kb2), q2], axis=0), _bd_stack(k2))
        qs = _dot(q2 * eg2, s2)
        yield
        gam = jnp.where(incl, jnp.exp(jnp.minimum(_score_cols(gcum, p, L) - grow, 0.0)), 0.0)
        gam_s = jnp.where(strict, gam, 0.0)
        gam_bd = jnp.concatenate([jnp.where(first_half, gam_s, 0.0),
                                  jnp.where(first_half, 0.0, gam_s)], axis=0)
        t_inv = yield from _unit_lower_inverse(kq[0:W] * gam_bd, W, L)
        rhs = jnp.concatenate([v2 * beta2, kb2 * eg2], axis=-1)
        sol = _dot(t_inv, jnp.concatenate([rhs, rhs], axis=0))
        yield
        u2 = jnp.where(lo, sol[0:L, 0:LANES], sol[L:W, 0:LANES])
        w2 = jnp.where(lo, sol[0:L, LANES:], sol[L:W, LANES:])
        v_new = u2 - _dot(w2, s2)
        yield
        o2 = qs + _dot(kq[W:] * gam, _bd_stack(v_new))
        upd = _dot_tn(k2 * kdec[:, sl], v_new)
        s_news[i][p] = s2 * sdec[:, sl] + jnp.where(bd128, upd, 0.0)
        yield
        ss = _half_sums(o2 * o2, ones_ch)
        yield
        outs[i][p] = o2 * lax.rsqrt(ss * (1.0 / GDN_DV) + EPS) * gain2 * _silu(z2)

    _interleave([pair(i, p) for i in range(bb) for p in range(PAIRS)])
    for i in range(bb):
        for p in range(PAIRS):
            s_ref[i, p] = s_news[i][p]
        o_ref[i] = jnp.concatenate(outs[i], axis=-1).astype(o_ref.dtype)

    @pl.when(c == nc - 1)
    def _():
        for k in range(sn_ref.shape[0]):
            if k != lsel:
                sn_ref[k] = s0_ref[k]
        for i in range(bb):
            convn_ref[i] = carries[i]
            for p in range(PAIRS):
                sn_ref[lsel, i, 2 * p] = s_news[i][p][0:HALF, 0:HALF]
                sn_ref[lsel, i, 2 * p + 1] = s_news[i][p][HALF:, HALF:]


def _gdn(slab, conv0, s_all, layer, cw, alog, dtb, ng, L, tv, bb):
    bsz, t, _ = slab.shape
    nc = t // L
    fix2 = lambda b, c: (0, 0)
    s_spec, lsel, aliases = _layer_state_spec(s_all, layer, bb, 2, 2)
    cv_spec = pl.BlockSpec((bb, SUBLANES, GDN_CONV_CH), lambda b, c: (b, 0, 0))
    return pl.pallas_call(
        functools.partial(_gdn_kernel, L=L, tv=tv, bb=bb, lsel=lsel),
        grid=(bsz // bb, nc),
        input_output_aliases=aliases,
        in_specs=[pl.BlockSpec((bb, L, MIX_SLAB_W), lambda b, c: (b, c, 0)),
                  cv_spec, s_spec,
                  pl.BlockSpec((CONV_W, GDN_CONV_CH), fix2),
                  pl.BlockSpec((1, GDN_VW), fix2), pl.BlockSpec((1, GDN_VW), fix2),
                  pl.BlockSpec((1, GDN_DV), fix2)],
        out_specs=[pl.BlockSpec((bb, L, GDN_VW), lambda b, c: (b, c, 0)), cv_spec, s_spec],
        out_shape=[jax.ShapeDtypeStruct((bsz, t, GDN_VW), BF16),
                   jax.ShapeDtypeStruct((bsz, SUBLANES, GDN_CONV_CH), F32),
                   jax.ShapeDtypeStruct(s_all.shape, F32)],
        scratch_shapes=[pltpu.VMEM((bb, L + SUBLANES, GDN_CONV_CH), F32),
                        pltpu.VMEM((bb, PAIRS, LANES, LANES), F32)],
        compiler_params=_cparams("parallel", "arbitrary"),
        name="gdn",
    )(slab, conv0, s_all, cw, alog, dtb, ng)


def _cmul(ar, ai, br, bi):
    return ar * br - ai * bi, ar * bi + ai * br


def _s5_kernel(u_ref, h0_ref, lb_ref, bw_ref, cw_ref, dsk_ref, wglu_ref,
               o_ref, hn_ref, car_ref, h_ref, *, L, bb):
    c = pl.program_id(1)
    nc = pl.num_programs(1)

    @pl.when(c == 0)
    def _():
        car_ref[...] = h0_ref[...]

    rows = bb * L
    u = u_ref[...].reshape(rows, S5_CH)
    bu = jnp.dot(u.astype(BF16), bw_ref[...], preferred_element_type=F32)
    tiles = rows // SUBLANES
    x_re = bu[:, 0:S5_N].reshape(tiles, SUBLANES, S5_N)
    x_im = bu[:, S5_N:].reshape(tiles, SUBLANES, S5_N)
    p1 = (lb_ref[0:1, :], lb_ref[1:2, :])
    p2 = _cmul(*p1, *p1)
    p4 = _cmul(*p2, *p2)
    p8 = _cmul(*p4, *p4)
    sub = lax.broadcasted_iota(jnp.int32, (SUBLANES, S5_N), 0)
    for d, (pr, pi) in ((1, p1), (2, p2), (4, p4)):
        pr = jnp.where(sub >= d, pr, 0.0)
        pi = jnp.where(sub >= d, pi, 0.0)
        s_re = pltpu.roll(x_re, d, axis=1)
        s_im = pltpu.roll(x_im, d, axis=1)
        x_re, x_im = x_re + (pr * s_re - pi * s_im), x_im + (pr * s_im + pi * s_re)
    x_re = x_re.reshape(rows, S5_N)
    x_im = x_im.reshape(rows, S5_N)
    k = sub + 1
    pw_re = jnp.ones((SUBLANES, S5_N), F32)
    pw_im = jnp.zeros((SUBLANES, S5_N), F32)
    for bit, (pr, pi) in ((1, p1), (2, p2), (4, p4), (8, p8)):
        m_re, m_im = _cmul(pw_re, pw_im, pr, pi)
        on = jnp.bitwise_and(k, bit) != 0
        pw_re = jnp.where(on, m_re, pw_re)
        pw_im = jnp.where(on, m_im, pw_im)
    for i in range(bb):
        car_re = car_ref[i, 0]
        car_im = car_ref[i, 1]
        for t in range(L // SUBLANES):
            r0 = i * L + t * SUBLANES
            a_re, a_im = _cmul(pw_re, pw_im, car_re, car_im)
            t_re = x_re[r0:r0 + SUBLANES, :] + a_re
            t_im = x_im[r0:r0 + SUBLANES, :] + a_im
            h_ref[r0:r0 + SUBLANES, 0:S5_N] = t_re
            h_ref[r0:r0 + SUBLANES, S5_N:] = t_im
            car_re = jnp.broadcast_to(t_re[SUBLANES - 1:SUBLANES, :], (SUBLANES, S5_N))
            car_im = jnp.broadcast_to(t_im[SUBLANES - 1:SUBLANES, :], (SUBLANES, S5_N))
        car_ref[i, 0] = car_re
        car_ref[i, 1] = car_im
        hn_ref[i, 0] = t_re
        hn_ref[i, 1] = t_im
    y = jnp.dot(h_ref[...].astype(BF16), cw_ref[...], preferred_element_type=F32) + dsk_ref[...] * u
    zg = jax.nn.gelu(y)
    out = zg * _sigmoid(jnp.dot(zg.astype(BF16), wglu_ref[...], preferred_element_type=F32))
    o_ref[...] = out.reshape(bb, L, S5_CH).astype(o_ref.dtype)


def _s5(u, h0, lb, bw, cw, dsk, wglu, L, bb):
    bsz, t, _ = u.shape
    nc = t // L
    fix2 = lambda b, c: (0, 0)
    h_spec = pl.BlockSpec((bb, 2, SUBLANES, S5_N), lambda b, c: (b, 0, 0, 0))
    return pl.pallas_call(
        functools.partial(_s5_kernel, L=L, bb=bb),
        grid=(bsz // bb, nc),
        in_specs=[pl.BlockSpec((bb, L, S5_CH), lambda b, c: (b, c, 0)),
                  h_spec,
                  pl.BlockSpec((SUBLANES, S5_N), fix2),
                  pl.BlockSpec((S5_CH, 2 * S5_N), fix2),
                  pl.BlockSpec((2 * S5_N, S5_CH), fix2),
                  pl.BlockSpec((1, S5_CH), fix2),
                  pl.BlockSpec((S5_CH, S5_CH), fix2)],
        out_specs=[pl.BlockSpec((bb, L, S5_CH), lambda b, c: (b, c, 0)), h_spec],
        out_shape=[jax.ShapeDtypeStruct((bsz, t, S5_CH), BF16),
                   jax.ShapeDtypeStruct((bsz, 2, SUBLANES, S5_N), F32)],
        scratch_shapes=[pltpu.VMEM((bb, 2, SUBLANES, S5_N), F32),
                        pltpu.VMEM((bb * L, 2 * S5_N), F32)],
        compiler_params=_cparams("parallel", "arbitrary"),
        name="s5",
    )(u, h0, lb, bw, cw, dsk, wglu)


def _prep_layer(l, P):
    w_in = P["w_in"][l]
    sizes = (GDN_CONV_CH, GDN_VW, GDN_HEADS, GDN_HEADS, S5_CH, 3 * ML_W, ML_HEADS, ML_HEADS, ML_W)
    offs = [0]
    for s in sizes:
        offs.append(offs[-1] + s)
    g_qkv, g_z, g_b, g_a, s_u, m_qkv, m_i, m_f, m_o = [w_in[:, offs[i]:offs[i + 1]] for i in range(9)]
    zpad = jnp.zeros((D_MODEL, LANES - 4 * GDN_HEADS), F32)
    w_cat = jnp.concatenate([g_qkv, g_z, s_u, m_qkv, m_o, g_b, g_a, m_i, m_f, zpad],
                            axis=1).astype(BF16)

    lr = P["s5_lam_re"][l].astype(F32)
    li = P["s5_lam_im"][l].astype(F32)
    dt = jnp.exp(P["s5_log_dt"][l].astype(F32))[:, None]
    mag = jnp.exp(lr * dt)
    lb_re = mag * jnp.cos(li * dt)
    lb_im = mag * jnp.sin(li * dt)
    den = lr * lr + li * li
    c_re = ((lb_re - 1.0) * lr + lb_im * li) / den
    c_im = (lb_im * lr - (lb_re - 1.0) * li) / den
    b_r = P["s5_B_re"][l].astype(F32)
    b_i = P["s5_B_im"][l].astype(F32)
    bb_re = c_re[..., None] * b_r - c_im[..., None] * b_i
    bb_im = c_re[..., None] * b_i + c_im[..., None] * b_r
    eye_g = jnp.eye(S5_GROUPS, dtype=F32)
    bd = lambda m: jnp.einsum("gph,gk->ghkp", m, eye_g).reshape(S5_CH, S5_N)
    bw = jnp.concatenate([bd(bb_re), bd(bb_im)], axis=1).astype(BF16)
    cd = lambda m: jnp.einsum("ghp,gk->gpkh", m, eye_g).reshape(S5_N, S5_CH)
    cw = jnp.concatenate([cd(P["s5_C_re"][l].astype(F32)),
                          -cd(P["s5_C_im"][l].astype(F32))], axis=0).astype(BF16)
    lb = jnp.zeros((SUBLANES, S5_N), F32).at[0].set(lb_re.reshape(-1)).at[1].set(lb_im.reshape(-1))

    wr = jnp.zeros((D_MODEL, LANES), F32)
    wr = wr.at[:, 0:N_GROUPS].set(P["w_router_group"][l])
    wr = wr.at[:, N_GROUPS:N_GROUPS + N_EXPERTS].set(P["w_router_expert"][l])
    br = jnp.zeros((1, LANES), F32)
    br = br.at[0, 0:N_GROUPS].set(P["b_router_group"][l])
    br = br.at[0, N_GROUPS:N_GROUPS + N_EXPERTS].set(P["b_router_expert"][l])

    rep_row = lambda v: jnp.repeat(v.astype(F32), ML_DH).reshape(1, ML_W)
    return dict(
        norm_mix=P["norm_mix"][l].reshape(1, D_MODEL).astype(F32),
        w_cat=w_cat,
        w_out=P["w_out"][l].astype(BF16),
        conv_w=P["gdn_conv_w"][l].astype(F32),
        alog=rep_row(P["gdn_A_log"][l]), dtb=rep_row(P["gdn_dt_bias"][l]),
        gdn_norm=P["gdn_norm"][l].reshape(1, GDN_DV).astype(F32),
        lb=lb, bw=bw, cw=cw,
        s5_d=P["s5_D"][l].reshape(1, S5_CH).astype(F32),
        w_glu=P["s5_w_glu"][l].astype(BF16),
        ml_bi=rep_row(P["ml_ig_bias"][l]), ml_bf=rep_row(P["ml_fg_bias"][l]),
        ml_norm=P["ml_norm"][l].reshape(1, ML_DH).astype(F32),
        norm_ffn=P["norm_ffn"][l].reshape(1, D_MODEL).astype(F32),
        wr=wr, br=br,
        wg=P["w_exp_gate"][l].astype(BF16), wu=P["w_exp_up"][l].astype(BF16),
        wd=P["w_exp_down"][l].astype(BF16),
        norm_ple=P["norm_ple"][l].reshape(1, D_MODEL).astype(F32),
        w_ple_gate=P["w_ple_gate"][l].astype(BF16),
        w_ple_proj=P["w_ple_proj"][l].astype(BF16),
    )


def _trunk(x, p, states, layers, final_norm, *, L, tv, Ls, tm, bb, bbs):
    conv0, gdn0, s5re0, s5im0, mc0, mn0, mm0 = states
    bsz, t, _ = x.shape
    m = bsz * t
    h = x.reshape(m, D_MODEL)
    outs = [[] for _ in range(5)]
    last_row = (tv - 1) % SUBLANES
    p_all = p.reshape(p.shape[0], m, PLE_DIM)
    e_rep = (jnp.arange(2 * GATE_W)[None, :] // ML_DH == jnp.arange(LANES)[:, None]).astype(BF16)
    gdn_all, mc_all = gdn0, mc0
    for l, W in enumerate(layers):
        slab_g, s_u, slab_m = _norm_inproj(h, W["norm_mix"], W["w_cat"], e_rep, tm)
        conv_in = jnp.pad(conv0[l], ((0, 0), (SUBLANES - (CONV_W - 1), 0), (0, 0)))
        o_gdn, conv_n, gdn_all = _gdn(slab_g.reshape(bsz, t, MIX_SLAB_W), conv_in, gdn_all, l,
                                      W["conv_w"], W["alog"], W["dtb"], W["gdn_norm"], L, tv, bb)
        h0 = jnp.stack([s5re0[l].reshape(bsz, S5_N), s5im0[l].reshape(bsz, S5_N)], axis=1)
        h0 = jnp.broadcast_to(h0[:, :, None, :], (bsz, 2, SUBLANES, S5_N))
        o_s5, s5_n = _s5(s_u.reshape(bsz, t, S5_CH), h0, W["lb"], W["bw"], W["cw"], W["s5_d"],
                         W["w_glu"], Ls, bbs)
        o_ml, mc_all, n_n, m_n = _mlstm(slab_m.reshape(bsz, t, MIX_SLAB_W), mc_all, l,
                                        mn0[l].reshape(bsz, PAIRS, 1, LANES),
                                        jnp.repeat(mm0[l], ML_DH, axis=-1).reshape(bsz, 1, ML_W),
                                        W["ml_bi"], W["ml_bf"], W["ml_norm"], L, tv, bb)
        h = _ffn(o_gdn.reshape(m, GDN_VW), o_s5.reshape(m, S5_CH), o_ml.reshape(m, ML_W), h,
                 p_all, l, W, final_norm, 2 * tm, l == len(layers) - 1)
        outs[0].append(conv_n[:, SUBLANES - (CONV_W - 1):])
        outs[1].append(s5_n[:, 0, last_row].reshape(bsz, S5_GROUPS, S5_STATE))
        outs[2].append(s5_n[:, 1, last_row].reshape(bsz, S5_GROUPS, S5_STATE))
        outs[3].append(n_n.reshape(bsz, ML_HEADS, ML_DH))
        outs[4].append(m_n[:, 0, ::ML_DH])
    conv_o, s5re_o, s5im_o, mn_o, mm_o = (jnp.stack(o) for o in outs)
    return (h.reshape(bsz, t, D_MODEL), conv_o, gdn_all, s5re_o, s5im_o, mc_all, mn_o, mm_o)


def kernel(x_prompt, x_sample, p_prompt, p_sample, state_gdn_conv, state_gdn, state_s5_re, state_s5_im, state_mlstm_C, state_mlstm_n, state_mlstm_m, norm_mix, w_in, w_out, gdn_conv_w, gdn_A_log, gdn_dt_bias, gdn_norm, s5_lam_re, s5_lam_im, s5_log_dt, s5_B_re, s5_B_im, s5_C_re, s5_C_im, s5_D, s5_w_glu, ml_ig_bias, ml_fg_bias, ml_norm, norm_ffn, w_router_group, b_router_group, w_router_expert, b_router_expert, w_exp_gate, w_exp_up, w_exp_down, norm_ple, w_ple_gate, w_ple_proj, final_norm):
    P = dict(norm_mix=norm_mix, w_in=w_in, w_out=w_out, gdn_conv_w=gdn_conv_w, gdn_A_log=gdn_A_log,
             gdn_dt_bias=gdn_dt_bias, gdn_norm=gdn_norm, s5_lam_re=s5_lam_re, s5_lam_im=s5_lam_im,
             s5_log_dt=s5_log_dt, s5_B_re=s5_B_re, s5_B_im=s5_B_im, s5_C_re=s5_C_re, s5_C_im=s5_C_im,
             s5_D=s5_D, s5_w_glu=s5_w_glu, ml_ig_bias=ml_ig_bias, ml_fg_bias=ml_fg_bias,
             ml_norm=ml_norm, norm_ffn=norm_ffn, w_router_group=w_router_group,
             b_router_group=b_router_group, w_router_expert=w_router_expert,
             b_router_expert=b_router_expert, w_exp_gate=w_exp_gate, w_exp_up=w_exp_up,
             w_exp_down=w_exp_down, norm_ple=norm_ple, w_ple_gate=w_ple_gate, w_ple_proj=w_ple_proj)
    depth = norm_mix.shape[0]
    layers = [_prep_layer(l, P) for l in range(depth)]
    fnorm = final_norm.reshape(1, D_MODEL).astype(F32)

    bp, tp, _ = x_prompt.shape
    zeros = lambda *s: jnp.zeros((depth, bp) + s, F32)
    prompt_init = (zeros(CONV_W - 1, GDN_CONV_CH), zeros(GDN_HEADS, GDN_DK, GDN_DV),
                   zeros(S5_GROUPS, S5_STATE), zeros(S5_GROUPS, S5_STATE),
                   zeros(ML_HEADS, ML_DH, ML_DH), zeros(ML_HEADS, ML_DH), zeros(ML_HEADS))
    lp = math.gcd(tp, 64)
    lsp = math.gcd(tp, 256)
    res_p = _trunk(x_prompt, p_prompt, prompt_init, layers, fnorm,
                   L=lp, tv=lp, Ls=lsp, tm=512, bb=8, bbs=1)

    bs, ts, _ = x_sample.shape
    tpad = -(-ts // SUBLANES) * SUBLANES
    xs = jnp.pad(x_sample, ((0, 0), (0, tpad - ts), (0, 0)))
    ps = jnp.pad(p_sample, ((0, 0), (0, 0), (0, tpad - ts), (0, 0)))
    sample_init = (state_gdn_conv, state_gdn, state_s5_re, state_s5_im,
                   state_mlstm_C, state_mlstm_n, state_mlstm_m)
    res_s = _trunk(xs, ps, sample_init, layers, fnorm,
                   L=tpad, tv=ts, Ls=tpad, tm=512, bb=8, bbs=8)
    y_sample = res_s[0][:, :ts]
    return (res_p[0], y_sample) + res_p[1:] + res_s[1:]
```

```python
import functools
import math

import jax
import jax.numpy as jnp
from jax import lax
from jax.experimental import pallas as pl
from jax.experimental.pallas import tpu as pltpu

F32 = jnp.float32
BF16 = jnp.bfloat16

D_MODEL = 1024
DEPTH = 2
GDN_HEADS = 6
GDN_DK = 64
GDN_DV = 64
GDN_QK = GDN_HEADS * GDN_DK
GDN_VW = GDN_HEADS * GDN_DV
GDN_CONV_CH = 2 * GDN_QK + GDN_VW
CONV_W = 4
S5_GROUPS = 16
S5_GROUP_CH = 16
S5_CH = S5_GROUPS * S5_GROUP_CH
S5_STATE = 64
S5_N = S5_GROUPS * S5_STATE
ML_HEADS = 6
ML_DH = 64
ML_W = ML_HEADS * ML_DH
N_GROUPS = 4
EXPERTS_PER_GROUP = 4
N_EXPERTS = N_GROUPS * EXPERTS_PER_GROUP
D_EXPERT = 256
PLE_DIM = 256
EPS = 1e-6

LANES = 128
SUBLANES = 8
NEG = -1e30
VMEM_LIMIT = 56 * 1024 * 1024

def _cparams(*sem):
    return pltpu.CompilerParams(dimension_semantics=sem, vmem_limit_bytes=VMEM_LIMIT)


def _dot(a, b):
    return jnp.dot(a.astype(BF16), b.astype(BF16), preferred_element_type=F32)


def _dot_nt(a, b):
    return lax.dot_general(a.astype(BF16), b.astype(BF16), (((1,), (1,)), ((), ())),
                           preferred_element_type=F32)


def _dot_tn(a, b):
    return lax.dot_general(a.astype(BF16), b.astype(BF16), (((0,), (0,)), ((), ())),
                           preferred_element_type=F32)


def _split_bf16(a):
    hi = a.astype(BF16)
    lo = (a - hi.astype(F32)).astype(BF16)
    return hi, lo


def _rms(x, gain):
    return x * lax.rsqrt(jnp.mean(x * x, axis=-1, keepdims=True) + EPS) * gain


def _softplus(x):
    return jnp.maximum(x, 0.0) + jnp.log(1.0 + jnp.exp(-jnp.abs(x)))


def _sigmoid(x):
    return 1.0 / (1.0 + jnp.exp(-x))


def _silu(x):
    return x * _sigmoid(x)


def _interleave(gens):
    live = list(gens)
    while live:
        still = []
        for g in live:
            try:
                next(g)
                still.append(g)
            except StopIteration:
                pass
        live = still


def _unit_lower_inverse(n_mat, size, top=None):
    top = size if top is None else top
    r = lax.broadcasted_iota(jnp.int32, (size, size), 0)
    c = lax.broadcasted_iota(jnp.int32, (size, size), 1)
    base = min(16, top)
    same = jnp.bitwise_xor(r, c) < base
    nd = jnp.where(same, n_mat, 0.0)
    eye = jnp.where(r == c, 1.0, 0.0).astype(F32)
    t = eye - nd
    p = 1
    if 2 * p < base:
        x = _dot(nd, nd)
        yield
    while 2 * p < base:
        t_next = t + _dot(t, x)
        if 4 * p < base:
            x = _dot(x, x)
        t = t_next
        yield
        p *= 2
    blk = base
    while blk < top:
        pair = jnp.bitwise_xor(r, c)
        off = jnp.where((pair < 2 * blk) & (pair >= blk), n_mat, 0.0)
        ot = _dot(off, t)
        yield
        t = t - _dot(t, ot)
        yield
        blk *= 2
    return t


MIX_W = 4 * ML_W
GATE_W = 2 * ML_W


def _norm_inproj_kernel(x_ref, g_ref, w_ref, e_ref, og_ref, os_ref, om_ref):
    u = _rms(x_ref[...], g_ref[...]).astype(BF16)
    dot = functools.partial(jnp.dot, preferred_element_type=F32)
    og_ref[:, 0:MIX_W] = dot(u, w_ref[:, 0:MIX_W])
    os_ref[...] = dot(u, w_ref[:, MIX_W:MIX_W + S5_CH])
    om_ref[:, 0:MIX_W] = dot(u, w_ref[:, MIX_W + S5_CH:2 * MIX_W + S5_CH])
    small = dot(u, w_ref[:, 2 * MIX_W + S5_CH:])
    rep = None
    for piece in _split3(small):
        t = dot(piece, e_ref[...])
        rep = t if rep is None else rep + t
    og_ref[:, MIX_W:] = rep[:, 0:GATE_W]
    om_ref[:, MIX_W:] = rep[:, GATE_W:]


def _norm_inproj(x, gain, w, e_rep, tm):
    m = x.shape[0]
    fix = lambda i: (0, 0)
    widths = (MIX_SLAB_W, S5_CH, MIX_SLAB_W)
    return pl.pallas_call(
        _norm_inproj_kernel,
        grid=(m // tm,),
        in_specs=[pl.BlockSpec((tm, D_MODEL), lambda i: (i, 0)),
                  pl.BlockSpec((1, D_MODEL), fix),
                  pl.BlockSpec((D_MODEL, w.shape[1]), fix),
                  pl.BlockSpec((LANES, 2 * GATE_W), fix)],
        out_specs=[pl.BlockSpec((tm, wd), lambda i: (i, 0)) for wd in widths],
        out_shape=[jax.ShapeDtypeStruct((m, wd), F32) for wd in widths],
        compiler_params=_cparams("parallel"),
        name="norm_inproj",
    )(x, gain, w, e_rep)


def _route(f, wr, br):
    fh, fl = _split_bf16(f)
    wh, wl = _split_bf16(wr)
    d = functools.partial(jnp.dot, preferred_element_type=F32)
    logits = d(fh, wh) + (d(fh, wl) + d(fl, wh)) + br
    lane = lax.broadcasted_iota(jnp.int32, logits.shape, 1)
    is_g = lane < N_GROUPS
    gl = jnp.where(is_g, logits, NEG)
    gmax = jnp.max(gl, axis=-1, keepdims=True)
    ge = jnp.where(is_g, jnp.exp(gl - gmax), 0.0)
    p_grp = ge / jnp.sum(ge, axis=-1, keepdims=True)
    g_prob = jnp.max(p_grp, axis=-1, keepdims=True)
    g_idx = jnp.min(jnp.where(is_g & (gl == gmax), lane, LANES), axis=-1, keepdims=True)
    e_lane = lane - N_GROUPS
    is_e = (e_lane >= 0) & (e_lane < N_EXPERTS) & (jnp.right_shift(e_lane, 2) == g_idx)
    le = jnp.where(is_e, logits, NEG)
    m1 = jnp.max(le, axis=-1, keepdims=True)
    i1 = jnp.min(jnp.where(is_e & (le == m1), lane, LANES), axis=-1, keepdims=True)
    is_e2 = is_e & (lane != i1)
    le2 = jnp.where(is_e2, logits, NEG)
    m2 = jnp.max(le2, axis=-1, keepdims=True)
    i2 = jnp.min(jnp.where(is_e2 & (le2 == m2), lane, LANES), axis=-1, keepdims=True)
    e2 = jnp.exp(m2 - m1)
    w1 = g_prob / (1.0 + e2)
    w2 = g_prob * e2 / (1.0 + e2)
    return fh, jnp.where(lane == i1, w1, 0.0) + jnp.where(lane == i2, w2, 0.0)


def _ffn_kernel(og_ref, os_ref, om_ref, h_ref, p_ref, wo_ref, nf_ref, wr_ref, br_ref,
                wg_ref, wu_ref, wd_ref, np_ref, wpg_ref, wpp_ref, fn_ref,
                out_ref, f_ref, gates_ref, *, final):
    acc_ref = out_ref
    gi = pl.program_id(1)

    @pl.when(gi == 0)
    def _():
        mix = jnp.concatenate([og_ref[...], os_ref[...], om_ref[...]], axis=-1)
        h1 = h_ref[...] + jnp.dot(mix, wo_ref[...], preferred_element_type=F32)
        acc_ref[...] = h1
        fh, gates = _route(_rms(h1, nf_ref[...]), wr_ref[...], br_ref[...])
        f_ref[...] = fh
        gates_ref[...] = gates

    x = f_ref[...]
    gates = gates_ref[...]
    lane = lax.broadcasted_iota(jnp.int32, gates.shape, 1)
    base = N_GROUPS + EXPERTS_PER_GROUP * gi
    acc = None
    for j in range(EXPERTS_PER_GROUP):
        gcol = jnp.sum(jnp.where(lane == base + j, gates, 0.0), axis=-1, keepdims=True)
        hg = jnp.dot(x, wg_ref[j], preferred_element_type=F32)
        hu = jnp.dot(x, wu_ref[j], preferred_element_type=F32)
        hidden = (_silu(hg) * hu * gcol).astype(BF16)
        t = jnp.dot(hidden, wd_ref[j], preferred_element_type=F32)
        acc = t if acc is None else acc + t
    acc_ref[...] += acc

    @pl.when(gi == N_GROUPS - 1)
    def _():
        h = acc_ref[...]
        gate = _sigmoid(jnp.dot(_rms(h, np_ref[...]).astype(BF16), wpg_ref[...],
                                preferred_element_type=F32))
        proj = jnp.dot(p_ref[0].astype(BF16), wpp_ref[...], preferred_element_type=F32)
        h = h + proj * gate
        if final:
            h = _rms(h, fn_ref[...])
        out_ref[...] = h


def _ffn(og, os_, om, h, p_all, layer, W, fn, tm, final):
    m = h.shape[0]
    row = lambda i, g: (i, 0)
    wsel = lambda i, g: (g, 0, 0)
    e = EXPERTS_PER_GROUP

    def fix(shape):
        return pl.BlockSpec(shape, lambda i, g: (0, 0), pipeline_mode=pl.Buffered(1))

    return pl.pallas_call(
        functools.partial(_ffn_kernel, final=final),
        grid=(m // tm, N_GROUPS),
        in_specs=[pl.BlockSpec((tm, GDN_VW), row), pl.BlockSpec((tm, S5_CH), row),
                  pl.BlockSpec((tm, ML_W), row), pl.BlockSpec((tm, D_MODEL), row),
                  pl.BlockSpec((1, tm, PLE_DIM), lambda i, g: (layer, i, 0)),
                  fix((D_MODEL, D_MODEL)), fix((1, D_MODEL)),
                  fix((D_MODEL, LANES)), fix((1, LANES)),
                  pl.BlockSpec((e, D_MODEL, D_EXPERT), wsel),
                  pl.BlockSpec((e, D_MODEL, D_EXPERT), wsel),
                  pl.BlockSpec((e, D_EXPERT, D_MODEL), wsel),
                  fix((1, D_MODEL)), fix((D_MODEL, D_MODEL)),
                  fix((PLE_DIM, D_MODEL)), fix((1, D_MODEL))],
        out_specs=pl.BlockSpec((tm, D_MODEL), row),
        out_shape=jax.ShapeDtypeStruct((m, D_MODEL), F32),
        scratch_shapes=[pltpu.VMEM((tm, D_MODEL), BF16), pltpu.VMEM((tm, LANES), F32)],
        compiler_params=_cparams("parallel", "arbitrary"),
        name="ffn",
    )(og, os_, om, h, p_all, W["w_out"], W["norm_ffn"], W["wr"], W["br"],
      W["wg"], W["wu"], W["wd"], W["norm_ple"], W["w_ple_gate"], W["w_ple_proj"], fn)


MIX_SLAB_W = 6 * ML_W
PAIRS = ML_HEADS // 2
HALF = LANES // 2


def _scan_rows(x, size, op, fill):
    row = lax.broadcasted_iota(jnp.int32, x.shape, 0)
    d = 1
    while d < size:
        x = op(x, jnp.where(row >= d, pltpu.roll(x, d, axis=0), fill))
        d *= 2
    return x


def _split3(a):
    hi = a.astype(BF16)
    r1 = a - hi.astype(F32)
    mid = r1.astype(BF16)
    lo = (r1 - mid.astype(F32)).astype(BF16)
    return hi, mid, lo


def _row_form(x_s, L):
    rr = lax.broadcasted_iota(jnp.int32, x_s.shape, 0)
    cc = lax.broadcasted_iota(jnp.int32, x_s.shape, 1)
    dg = jnp.where(rr == jnp.bitwise_and(cc, L - 1), x_s, 0.0)
    ones = jnp.ones((SUBLANES, L), BF16)
    acc = None
    for piece in _split3(dg):
        t = jnp.dot(ones, piece, preferred_element_type=F32)
        acc = t if acc is None else acc + t
    return acc[0:1, :]


def _score_cols(x, p, L):
    if 2 * L == LANES:
        return x[:, LANES * p:LANES * (p + 1)]
    return jnp.concatenate([x[:, LANES * p:LANES * p + L],
                            x[:, LANES * p + HALF:LANES * p + HALF + L]], axis=-1)


def _bd_stack(x2):
    lo = lax.broadcasted_iota(jnp.int32, x2.shape, 1) < HALF
    return jnp.concatenate([jnp.where(lo, x2, 0.0), jnp.where(lo, 0.0, x2)], axis=0)


def _bd_ones(rows_per_half):
    shape = (2 * rows_per_half, LANES)
    r = lax.broadcasted_iota(jnp.int32, shape, 0) < rows_per_half
    c = lax.broadcasted_iota(jnp.int32, shape, 1) < HALF
    return jnp.where(r == c, 1.0, 0.0).astype(F32)


def _layer_state_spec(s_all, layer, bb, in_idx, out_idx):
    depth = s_all.shape[0]
    tail = s_all.shape[2:]
    if layer == 0 and depth > 1:
        return pl.BlockSpec((depth, bb) + tail, lambda b, c: (0, b, 0, 0, 0)), 0, {}
    return (pl.BlockSpec((1, bb) + tail, lambda b, c: (layer, b, 0, 0, 0)), 0,
            {in_idx: out_idx})


def _half_sums(x2, ones_bd):
    return _dot(x2, ones_bd)


def _mlstm_kernel(a_ref, c0_ref, n0_ref, m0_ref, bi_ref, bf_ref, ng_ref,
                  o_ref, cn_ref, nn_ref, mn_ref, st_ref, m_ref, *, L, tv, bb, lsel):
    cidx = pl.program_id(1)
    nc = pl.num_programs(1)
    ones_ch = _bd_ones(HALF)
    r128 = lax.broadcasted_iota(jnp.int32, (LANES, LANES), 0)
    c128 = lax.broadcasted_iota(jnp.int32, (LANES, LANES), 1)
    diag128 = r128 == c128

    @pl.when(cidx == 0)
    def _():
        st_ref[...] = jnp.zeros(st_ref.shape, F32)
        m_ref[...] = m0_ref[...]
        ob = ones_ch.astype(BF16)
        for i in range(bb):
            for p in range(PAIRS):
                st_ref[i, p, 0:HALF, 0:HALF] = c0_ref[lsel, i, 2 * p]
                st_ref[i, p, HALF:, HALF:LANES] = c0_ref[lsel, i, 2 * p + 1]
                dg = jnp.where(diag128, n0_ref[i, p], 0.0)
                acc = None
                for piece in _split3(dg):
                    t = jnp.dot(piece, ob, preferred_element_type=F32)
                    acc = t if acc is None else acc + t
                st_ref[i, p, :, LANES:] = acc

    W = 2 * L
    rr = lax.broadcasted_iota(jnp.int32, (L, W), 0)
    cc = lax.broadcasted_iota(jnp.int32, (L, W), 1)
    incl = rr >= jnp.bitwise_and(cc, L - 1)
    ones_keys = _bd_ones(L)
    bd256 = jnp.concatenate([ones_ch, ones_ch], axis=-1) > 0.5
    gain2 = ng_ref[...]
    gain2 = jnp.concatenate([gain2, gain2], axis=-1)
    ones_l = jnp.ones((L, LANES), F32)
    outs = [[None] * PAIRS for _ in range(bb)]
    st_news = [[None] * PAIRS for _ in range(bb)]
    m_news = [None] * bb
    rows = []
    for i in range(bb):
        li = a_ref[i, :, 4 * ML_W:5 * ML_W] + bi_ref[...]
        lf = -_softplus(-(a_ref[i, :, 5 * ML_W:6 * ML_W] + bf_ref[...]))
        if tv < L:
            valid = lax.broadcasted_iota(jnp.int32, li.shape, 0) < tv
            li = jnp.where(valid, li, NEG)
            lf = jnp.where(valid, lf, 0.0)
        bcum = _scan_rows(lf, L, jnp.add, 0.0)
        a = li - bcum
        m0 = m_ref[i]
        m_t = bcum + jnp.maximum(m0, _scan_rows(a, L, jnp.maximum, NEG))
        e_inter = jnp.exp(bcum + m0 - m_t)
        m_new = m_t[L - 1:L, :]
        b_last = bcum[L - 1:L, :]
        e_c = jnp.exp(b_last + m0 - m_new)
        kw = a_ref[i, :, ML_W:2 * ML_W] * (ML_DH ** -0.5) * jnp.exp(b_last + a - m_new)
        m_news[i] = m_new
        rows.append((a, bcum, m_t, e_inter, e_c, kw))

    def pair(i, p):
        a, bcum, m_t, e_inter, e_c, kw = rows[i]
        sl = slice(LANES * p, LANES * (p + 1))
        q2 = a_ref[i, :, sl]
        k2 = a_ref[i, :, ML_W + LANES * p:ML_W + LANES * (p + 1)] * (ML_DH ** -0.5)
        v2 = a_ref[i, :, 2 * ML_W + LANES * p:2 * ML_W + LANES * (p + 1)]
        og2 = a_ref[i, :, 3 * ML_W + LANES * p:3 * ML_W + LANES * (p + 1)]
        st = st_ref[i, p]
        arow = _row_form(_score_cols(a, p, L), L)
        qk = _dot_nt(q2, _bd_stack(k2))
        qcn = _dot(q2, st)
        upd = _dot_tn(kw[:, sl], jnp.concatenate([v2, ones_l], axis=-1))
        ec2 = e_c[:, sl]
        st_news[i][p] = st * jnp.concatenate([ec2, ec2], axis=-1) + jnp.where(bd256, upd, 0.0)
        yield
        w_intra = jnp.where(incl, jnp.exp(_score_cols(bcum, p, L) + arow - _score_cols(m_t, p, L)), 0.0)
        s2 = qk * w_intra
        nd = _dot(s2, jnp.concatenate([_bd_stack(v2), ones_keys], axis=-1))
        yield
        e2 = e_inter[:, sl]
        num = e2 * qcn[:, 0:LANES] + nd[:, 0:LANES]
        den = e2 * qcn[:, LANES:] + nd[:, LANES:]
        hh = num / jnp.maximum(jnp.abs(den), jnp.exp(-m_t[:, sl]))
        ss = _half_sums(hh * hh, ones_ch)
        yield
        outs[i][p] = hh * lax.rsqrt(ss * (1.0 / ML_DH) + EPS) * gain2 * _sigmoid(og2)

    _interleave([pair(i, p) for i in range(bb) for p in range(PAIRS)])
    for i in range(bb):
        for p in range(PAIRS):
            st_ref[i, p] = st_news[i][p]
        m_ref[i] = m_news[i]
        o_ref[i] = jnp.concatenate(outs[i], axis=-1).astype(o_ref.dtype)

    @pl.when(cidx == nc - 1)
    def _():
        ones8 = jnp.ones((SUBLANES, LANES), BF16)
        for k in range(cn_ref.shape[0]):
            if k != lsel:
                cn_ref[k] = c0_ref[k]
        for i in range(bb):
            for p in range(PAIRS):
                st = st_news[i][p]
                cn_ref[lsel, i, 2 * p] = st[0:HALF, 0:HALF]
                cn_ref[lsel, i, 2 * p + 1] = st[HALF:, HALF:LANES]
                dg = jnp.where(diag128, st[:, LANES:], 0.0)
                acc = None
                for piece in _split3(dg):
                    t = jnp.dot(ones8, piece, preferred_element_type=F32)
                    acc = t if acc is None else acc + t
                nn_ref[i, p] = acc[0:1, :]
        mn_ref[...] = m_ref[...]


def _mlstm(slab, c_all, layer, n0, m0, bi, bf, ng, L, tv, bb):
    bsz, t, _ = slab.shape
    nc = t // L
    fix2 = lambda b, c: (0, 0)
    c_spec, lsel, aliases = _layer_state_spec(c_all, layer, bb, 1, 1)
    n_spec = pl.BlockSpec((bb, PAIRS, 1, LANES), lambda b, c: (b, 0, 0, 0))
    m_spec = pl.BlockSpec((bb, 1, ML_W), lambda b, c: (b, 0, 0))
    return pl.pallas_call(
        functools.partial(_mlstm_kernel, L=L, tv=tv, bb=bb, lsel=lsel),
        grid=(bsz // bb, nc),
        input_output_aliases=aliases,
        in_specs=[pl.BlockSpec((bb, L, MIX_SLAB_W), lambda b, c: (b, c, 0)),
                  c_spec, n_spec, m_spec,
                  pl.BlockSpec((1, ML_W), fix2), pl.BlockSpec((1, ML_W), fix2),
                  pl.BlockSpec((1, ML_DH), fix2)],
        out_specs=[pl.BlockSpec((bb, L, ML_W), lambda b, c: (b, c, 0)), c_spec, n_spec, m_spec],
        out_shape=[jax.ShapeDtypeStruct((bsz, t, ML_W), BF16),
                   jax.ShapeDtypeStruct(c_all.shape, F32),
                   jax.ShapeDtypeStruct((bsz, PAIRS, 1, LANES), F32),
                   jax.ShapeDtypeStruct((bsz, 1, ML_W), F32)],
        scratch_shapes=[pltpu.VMEM((bb, PAIRS, LANES, 2 * LANES), F32),
                        pltpu.VMEM((bb, 1, ML_W), F32)],
        compiler_params=_cparams("parallel", "arbitrary"),
        name="mlstm",
    )(slab, c_all, n0, m0, bi, bf, ng)


def _gdn_kernel(a_ref, conv0_ref, s0_ref, cw_ref, alog_ref, dtb_ref, ng_ref,
                o_ref, convn_ref, sn_ref, ext_ref, s_ref, *, L, tv, bb, lsel):
    c = pl.program_id(1)
    nc = pl.num_programs(1)

    @pl.when(c == 0)
    def _():
        ext_ref[:, 0:SUBLANES, :] = conv0_ref[...]
        s_ref[...] = jnp.zeros(s_ref.shape, F32)
        for i in range(bb):
            for p in range(PAIRS):
                s_ref[i, p, 0:HALF, 0:HALF] = s0_ref[lsel, i, 2 * p]
                s_ref[i, p, HALF:, HALF:] = s0_ref[lsel, i, 2 * p + 1]

    W = 2 * L
    rr = lax.broadcasted_iota(jnp.int32, (L, W), 0)
    cc = lax.broadcasted_iota(jnp.int32, (L, W), 1)
    key = jnp.bitwise_and(cc, L - 1)
    incl = rr >= key
    strict = rr > key
    first_half = cc < L
    lo = lax.broadcasted_iota(jnp.int32, (L, LANES), 1) < HALF
    ones_ch = _bd_ones(HALF)
    bd128 = ones_ch > 0.5
    cw = cw_ref[...]
    gain2 = ng_ref[...]
    gain2 = jnp.concatenate([gain2, gain2], axis=-1)
    outs = [[None] * PAIRS for _ in range(bb)]
    s_news = [[None] * PAIRS for _ in range(bb)]
    carries = [None] * bb
    rows = []
    for i in range(bb):
        raw = a_ref[i, :, 0:GDN_CONV_CH]
        ext_ref[i, SUBLANES:SUBLANES + L, :] = raw
        ext = ext_ref[i]
        acc = raw * cw[CONV_W - 1:CONV_W, :]
        for j in range(CONV_W - 1):
            sh = pltpu.roll(ext, CONV_W - 1 - j, axis=0)[SUBLANES:SUBLANES + L]
            acc = acc + sh * cw[j:j + 1, :]
        qkv = _silu(acc)
        carries[i] = pltpu.roll(ext, (L + SUBLANES - tv) % (L + SUBLANES), axis=0)[0:SUBLANES]

        beta = _sigmoid(a_ref[i, :, 4 * GDN_VW:5 * GDN_VW])
        g = -jnp.exp(alog_ref[...]) * _softplus(a_ref[i, :, 5 * GDN_VW:6 * GDN_VW] + dtb_ref[...])
        if tv < L:
            valid = lax.broadcasted_iota(jnp.int32, g.shape, 0) < tv
            beta = jnp.where(valid, beta, 0.0)
            g = jnp.where(valid, g, 0.0)
        gcum = _scan_rows(g, L, jnp.add, 0.0)
        g_last = gcum[L - 1:L, :]
        rows.append((qkv, beta, gcum, jnp.exp(gcum), jnp.exp(g_last - gcum), jnp.exp(g_last)))
    for i in range(bb):
        ext_ref[i, 0:SUBLANES, :] = carries[i]

    def pair(i, p):
        qkv, beta, gcum, eg, kdec, sdec = rows[i]
        sl = slice(LANES * p, LANES * (p + 1))
        q2 = qkv[:, LANES * p:LANES * (p + 1)]
        k2 = qkv[:, GDN_QK + LANES * p:GDN_QK + LANES * (p + 1)]
        v2 = qkv[:, 2 * GDN_QK + LANES * p:2 * GDN_QK + LANES * (p + 1)]
        z2 = a_ref[i, :, GDN_CONV_CH + LANES * p:GDN_CONV_CH + LANES * (p + 1)]
        s2 = s_ref[i, p]
        ssq = _half_sums(q2 * q2, ones_ch)
        ssk = _half_sums(k2 * k2, ones_ch)
        grow = _row_form(_score_cols(gcum, p, L), L)
        yield
        q2 = q2 * lax.rsqrt(ssq + EPS) * (GDN_DK ** -0.5)
        k2 = k2 * lax.rsqrt(ssk + EPS)
        beta2 = beta[:, sl]
        eg2 = eg[:, sl]
        kb2 = k2 * beta2
        kq = _dot_nt(jnp.concatenate([_bd_stack(kb2), q2], axis=0), _bd_stack(k2))
        qs = _dot(q2 * eg2, s2)
        yield
        gam = jnp.where(incl, jnp.exp(jnp.minimum(_score_cols(gcum, p, L) - grow, 0.0)), 0.0)
        gam_s = jnp.where(strict, gam, 0.0)
        gam_bd = jnp.concatenate([jnp.where(first_half, gam_s, 0.0),
                                  jnp.where(first_half, 0.0, gam_s)], axis=0)
        t_inv = yield from _unit_lower_inverse(kq[0:W] * gam_bd, W, L)
        rhs = jnp.concatenate([v2 * beta2, kb2 * eg2], axis=-1)
        sol = _dot(t_inv, jnp.concatenate([rhs, rhs], axis=0))
        yield
        u2 = jnp.where(lo, sol[0:L, 0:LANES], sol[L:W, 0:LANES])
        w2 = jnp.where(lo, sol[0:L, LANES:], sol[L:W, LANES:])
        v_new = u2 - _dot(w2, s2)
        yield
        o2 = qs + _dot(kq[W:] * gam, _bd_stack(v_new))
        upd = _dot_tn(k2 * kdec[:, sl], v_new)
        s_news[i][p] = s2 * sdec[:, sl] + jnp.where(bd128, upd, 0.0)
        yield
        ss = _half_sums(o2 * o2, ones_ch)
        yield
        outs[i][p] = o2 * lax.rsqrt(ss * (1.0 / GDN_DV) + EPS) * gain2 * _silu(z2)

    _interleave([pair(i, p) for i in range(bb) for p in range(PAIRS)])
    for i in range(bb):
        for p in range(PAIRS):
            s_ref[i, p] = s_news[i][p]
        o_ref[i] = jnp.concatenate(outs[i], axis=-1).astype(o_ref.dtype)

    @pl.when(c == nc - 1)
    def _():
        for k in range(sn_ref.shape[0]):
            if k != lsel:
                sn_ref[k] = s0_ref[k]
        for i in range(bb):
            convn_ref[i] = carries[i]
            for p in range(PAIRS):
                sn_ref[lsel, i, 2 * p] = s_news[i][p][0:HALF, 0:HALF]
                sn_ref[lsel, i, 2 * p + 1] = s_news[i][p][HALF:, HALF:]


def _gdn(slab, conv0, s_all, layer, cw, alog, dtb, ng, L, tv, bb):
    bsz, t, _ = slab.shape
    nc = t // L
    fix2 = lambda b, c: (0, 0)
    s_spec, lsel, aliases = _layer_state_spec(s_all, layer, bb, 2, 2)
    cv_spec = pl.BlockSpec((bb, SUBLANES, GDN_CONV_CH), lambda b, c: (b, 0, 0))
    return pl.pallas_call(
        functools.partial(_gdn_kernel, L=L, tv=tv, bb=bb, lsel=lsel),
        grid=(bsz // bb, nc),
        input_output_aliases=aliases,
        in_specs=[pl.BlockSpec((bb, L, MIX_SLAB_W), lambda b, c: (b, c, 0)),
                  cv_spec, s_spec,
                  pl.BlockSpec((CONV_W, GDN_CONV_CH), fix2),
                  pl.BlockSpec((1, GDN_VW), fix2), pl.BlockSpec((1, GDN_VW), fix2),
                  pl.BlockSpec((1, GDN_DV), fix2)],
        out_specs=[pl.BlockSpec((bb, L, GDN_VW), lambda b, c: (b, c, 0)), cv_spec, s_spec],
        out_shape=[jax.ShapeDtypeStruct((bsz, t, GDN_VW), BF16),
                   jax.ShapeDtypeStruct((bsz, SUBLANES, GDN_CONV_CH), F32),
                   jax.ShapeDtypeStruct(s_all.shape, F32)],
        scratch_shapes=[pltpu.VMEM((bb, L + SUBLANES, GDN_CONV_CH), F32),
                        pltpu.VMEM((bb, PAIRS, LANES, LANES), F32)],
        compiler_params=_cparams("parallel", "arbitrary"),
        name="gdn",
    )(slab, conv0, s_all, cw, alog, dtb, ng)


def _cmul(ar, ai, br, bi):
    return ar * br - ai * bi, ar * bi + ai * br


def _s5_kernel(u_ref, h0_ref, lb_ref, bw_ref, cw_ref, dsk_ref, wglu_ref,
               o_ref, hn_ref, car_ref, h_ref, *, L, bb):
    c = pl.program_id(1)
    nc = pl.num_programs(1)

    @pl.when(c == 0)
    def _():
        car_ref[...] = h0_ref[...]

    rows = bb * L
    u = u_ref[...].reshape(rows, S5_CH)
    bu = jnp.dot(u.astype(BF16), bw_ref[...], preferred_element_type=F32)
    tiles = rows // SUBLANES
    x_re = bu[:, 0:S5_N].reshape(tiles, SUBLANES, S5_N)
    x_im = bu[:, S5_N:].reshape(tiles, SUBLANES, S5_N)
    p1 = (lb_ref[0:1, :], lb_ref[1:2, :])
    p2 = _cmul(*p1, *p1)
    p4 = _cmul(*p2, *p2)
    p8 = _cmul(*p4, *p4)
    sub = lax.broadcasted_iota(jnp.int32, (SUBLANES, S5_N), 0)
    for d, (pr, pi) in ((1, p1), (2, p2), (4, p4)):
        pr = jnp.where(sub >= d, pr, 0.0)
        pi = jnp.where(sub >= d, pi, 0.0)
        s_re = pltpu.roll(x_re, d, axis=1)
        s_im = pltpu.roll(x_im, d, axis=1)
        x_re, x_im = x_re + (pr * s_re - pi * s_im), x_im + (pr * s_im + pi * s_re)
    x_re = x_re.reshape(rows, S5_N)
    x_im = x_im.reshape(rows, S5_N)
    k = sub + 1
    pw_re = jnp.ones((SUBLANES, S5_N), F32)
    pw_im = jnp.zeros((SUBLANES, S5_N), F32)
    for bit, (pr, pi) in ((1, p1), (2, p2), (4, p4), (8, p8)):
        m_re, m_im = _cmul(pw_re, pw_im, pr, pi)
        on = jnp.bitwise_and(k, bit) != 0
        pw_re = jnp.where(on, m_re, pw_re)
        pw_im = jnp.where(on, m_im, pw_im)
    for i in range(bb):
        car_re = car_ref[i, 0]
        car_im = car_ref[i, 1]
        for t in range(L // SUBLANES):
            r0 = i * L + t * SUBLANES
            a_re, a_im = _cmul(pw_re, pw_im, car_re, car_im)
            t_re = x_re[r0:r0 + SUBLANES, :] + a_re
            t_im = x_im[r0:r0 + SUBLANES, :] + a_im
            h_ref[r0:r0 + SUBLANES, 0:S5_N] = t_re
            h_ref[r0:r0 + SUBLANES, S5_N:] = t_im
            car_re = jnp.broadcast_to(t_re[SUBLANES - 1:SUBLANES, :], (SUBLANES, S5_N))
            car_im = jnp.broadcast_to(t_im[SUBLANES - 1:SUBLANES, :], (SUBLANES, S5_N))
        car_ref[i, 0] = car_re
        car_ref[i, 1] = car_im
        hn_ref[i, 0] = t_re
        hn_ref[i, 1] = t_im
    y = jnp.dot(h_ref[...].astype(BF16), cw_ref[...], preferred_element_type=F32) + dsk_ref[...] * u
    zg = jax.nn.gelu(y)
    out = zg * _sigmoid(jnp.dot(zg.astype(BF16), wglu_ref[...], preferred_element_type=F32))
    o_ref[...] = out.reshape(bb, L, S5_CH).astype(o_ref.dtype)


def _s5(u, h0, lb, bw, cw, dsk, wglu, L, bb):
    bsz, t, _ = u.shape
    nc = t // L
    fix2 = lambda b, c: (0, 0)
    h_spec = pl.BlockSpec((bb, 2, SUBLANES, S5_N), lambda b, c: (b, 0, 0, 0))
    return pl.pallas_call(
        functools.partial(_s5_kernel, L=L, bb=bb),
        grid=(bsz // bb, nc),
        in_specs=[pl.BlockSpec((bb, L, S5_CH), lambda b, c: (b, c, 0)),
                  h_spec,
                  pl.BlockSpec((SUBLANES, S5_N), fix2),
                  pl.BlockSpec((S5_CH, 2 * S5_N), fix2),
                  pl.BlockSpec((2 * S5_N, S5_CH), fix2),
                  pl.BlockSpec((1, S5_CH), fix2),
                  pl.BlockSpec((S5_CH, S5_CH), fix2)],
        out_specs=[pl.BlockSpec((bb, L, S5_CH), lambda b, c: (b, c, 0)), h_spec],
        out_shape=[jax.ShapeDtypeStruct((bsz, t, S5_CH), BF16),
                   jax.ShapeDtypeStruct((bsz, 2, SUBLANES, S5_N), F32)],
        scratch_shapes=[pltpu.VMEM((bb, 2, SUBLANES, S5_N), F32),
                        pltpu.VMEM((bb * L, 2 * S5_N), F32)],
        compiler_params=_cparams("parallel", "arbitrary"),
        name="s5",
    )(u, h0, lb, bw, cw, dsk, wglu)


def _prep_layer(l, P):
    w_in = P["w_in"][l]
    sizes = (GDN_CONV_CH, GDN_VW, GDN_HEADS, GDN_HEADS, S5_CH, 3 * ML_W, ML_HEADS, ML_HEADS, ML_W)
    offs = [0]
    for s in sizes:
        offs.append(offs[-1] + s)
    g_qkv, g_z, g_b, g_a, s_u, m_qkv, m_i, m_f, m_o = [w_in[:, offs[i]:offs[i + 1]] for i in range(9)]
    zpad = jnp.zeros((D_MODEL, LANES - 4 * GDN_HEADS), F32)
    w_cat = jnp.concatenate([g_qkv, g_z, s_u, m_qkv, m_o, g_b, g_a, m_i, m_f, zpad],
                            axis=1).astype(BF16)

    lr = P["s5_lam_re"][l].astype(F32)
    li = P["s5_lam_im"][l].astype(F32)
    dt = jnp.exp(P["s5_log_dt"][l].astype(F32))[:, None]
    mag = jnp.exp(lr * dt)
    lb_re = mag * jnp.cos(li * dt)
    lb_im = mag * jnp.sin(li * dt)
    den = lr * lr + li * li
    c_re = ((lb_re - 1.0) * lr + lb_im * li) / den
    c_im = (lb_im * lr - (lb_re - 1.0) * li) / den
    b_r = P["s5_B_re"][l].astype(F32)
    b_i = P["s5_B_im"][l].astype(F32)
    bb_re = c_re[..., None] * b_r - c_im[..., None] * b_i
    bb_im = c_re[..., None] * b_i + c_im[..., None] * b_r
    eye_g = jnp.eye(S5_GROUPS, dtype=F32)
    bd = lambda m: jnp.einsum("gph,gk->ghkp", m, eye_g).reshape(S5_CH, S5_N)
    bw = jnp.concatenate([bd(bb_re), bd(bb_im)], axis=1).astype(BF16)
    cd = lambda m: jnp.einsum("ghp,gk->gpkh", m, eye_g).reshape(S5_N, S5_CH)
    cw = jnp.concatenate([cd(P["s5_C_re"][l].astype(F32)),
                          -cd(P["s5_C_im"][l].astype(F32))], axis=0).astype(BF16)
    lb = jnp.zeros((SUBLANES, S5_N), F32).at[0].set(lb_re.reshape(-1)).at[1].set(lb_im.reshape(-1))

    wr = jnp.zeros((D_MODEL, LANES), F32)
    wr = wr.at[:, 0:N_GROUPS].set(P["w_router_group"][l])
    wr = wr.at[:, N_GROUPS:N_GROUPS + N_EXPERTS].set(P["w_router_expert"][l])
    br = jnp.zeros((1, LANES), F32)
    br = br.at[0, 0:N_GROUPS].set(P["b_router_group"][l])
    br = br.at[0, N_GROUPS:N_GROUPS + N_EXPERTS].set(P["b_router_expert"][l])

    rep_row = lambda v: jnp.repeat(v.astype(F32), ML_DH).reshape(1, ML_W)
    return dict(
        norm_mix=P["norm_mix"][l].reshape(1, D_MODEL).astype(F32),
        w_cat=w_cat,
        w_out=P["w_out"][l].astype(BF16),
        conv_w=P["gdn_conv_w"][l].astype(F32),
        alog=rep_row(P["gdn_A_log"][l]), dtb=rep_row(P["gdn_dt_bias"][l]),
        gdn_norm=P["gdn_norm"][l].reshape(1, GDN_DV).astype(F32),
        lb=lb, bw=bw, cw=cw,
        s5_d=P["s5_D"][l].reshape(1, S5_CH).astype(F32),
        w_glu=P["s5_w_glu"][l].astype(BF16),
        ml_bi=rep_row(P["ml_ig_bias"][l]), ml_bf=rep_row(P["ml_fg_bias"][l]),
        ml_norm=P["ml_norm"][l].reshape(1, ML_DH).astype(F32),
        norm_ffn=P["norm_ffn"][l].reshape(1, D_MODEL).astype(F32),
        wr=wr, br=br,
        wg=P["w_exp_gate"][l].astype(BF16), wu=P["w_exp_up"][l].astype(BF16),
        wd=P["w_exp_down"][l].astype(BF16),
        norm_ple=P["norm_ple"][l].reshape(1, D_MODEL).astype(F32),
        w_ple_gate=P["w_ple_gate"][l].astype(BF16),
        w_ple_proj=P["w_ple_proj"][l].astype(BF16),
    )


def _trunk(x, p, states, layers, final_norm, *, L, tv, Ls, tm, tm_ffn, bb, bbs):
    conv0, gdn0, s5re0, s5im0, mc0, mn0, mm0 = states
    bsz, t, _ = x.shape
    m = bsz * t
    h = x.reshape(m, D_MODEL)
    outs = [[] for _ in range(5)]
    last_row = (tv - 1) % SUBLANES
    p_all = p.reshape(p.shape[0], m, PLE_DIM)
    e_rep = (jnp.arange(2 * GATE_W)[None, :] // ML_DH == jnp.arange(LANES)[:, None]).astype(BF16)
    gdn_all, mc_all = gdn0, mc0
    for l, W in enumerate(layers):
        slab_g, s_u, slab_m = _norm_inproj(h, W["norm_mix"], W["w_cat"], e_rep, tm)
        conv_in = jnp.pad(conv0[l], ((0, 0), (SUBLANES - (CONV_W - 1), 0), (0, 0)))
        o_gdn, conv_n, gdn_all = _gdn(slab_g.reshape(bsz, t, MIX_SLAB_W), conv_in, gdn_all, l,
                                      W["conv_w"], W["alog"], W["dtb"], W["gdn_norm"], L, tv, bb)
        h0 = jnp.stack([s5re0[l].reshape(bsz, S5_N), s5im0[l].reshape(bsz, S5_N)], axis=1)
        h0 = jnp.broadcast_to(h0[:, :, None, :], (bsz, 2, SUBLANES, S5_N))
        o_s5, s5_n = _s5(s_u.reshape(bsz, t, S5_CH), h0, W["lb"], W["bw"], W["cw"], W["s5_d"],
                         W["w_glu"], Ls, bbs)
        o_ml, mc_all, n_n, m_n = _mlstm(slab_m.reshape(bsz, t, MIX_SLAB_W), mc_all, l,
                                        mn0[l].reshape(bsz, PAIRS, 1, LANES),
                                        jnp.repeat(mm0[l], ML_DH, axis=-1).reshape(bsz, 1, ML_W),
                                        W["ml_bi"], W["ml_bf"], W["ml_norm"], L, tv, bb)
        h = _ffn(o_gdn.reshape(m, GDN_VW), o_s5.reshape(m, S5_CH), o_ml.reshape(m, ML_W), h,
                 p_all, l, W, final_norm, tm_ffn, l == len(layers) - 1)
        outs[0].append(conv_n[:, SUBLANES - (CONV_W - 1):])
        outs[1].append(s5_n[:, 0, last_row].reshape(bsz, S5_GROUPS, S5_STATE))
        outs[2].append(s5_n[:, 1, last_row].reshape(bsz, S5_GROUPS, S5_STATE))
        outs[3].append(n_n.reshape(bsz, ML_HEADS, ML_DH))
        outs[4].append(m_n[:, 0, ::ML_DH])
    conv_o, s5re_o, s5im_o, mn_o, mm_o = (jnp.stack(o) for o in outs)
    return (h.reshape(bsz, t, D_MODEL), conv_o, gdn_all, s5re_o, s5im_o, mc_all, mn_o, mm_o)


def kernel(x_prompt, x_sample, p_prompt, p_sample, state_gdn_conv, state_gdn, state_s5_re, state_s5_im, state_mlstm_C, state_mlstm_n, state_mlstm_m, norm_mix, w_in, w_out, gdn_conv_w, gdn_A_log, gdn_dt_bias, gdn_norm, s5_lam_re, s5_lam_im, s5_log_dt, s5_B_re, s5_B_im, s5_C_re, s5_C_im, s5_D, s5_w_glu, ml_ig_bias, ml_fg_bias, ml_norm, norm_ffn, w_router_group, b_router_group, w_router_expert, b_router_expert, w_exp_gate, w_exp_up, w_exp_down, norm_ple, w_ple_gate, w_ple_proj, final_norm):
    P = dict(norm_mix=norm_mix, w_in=w_in, w_out=w_out, gdn_conv_w=gdn_conv_w, gdn_A_log=gdn_A_log,
             gdn_dt_bias=gdn_dt_bias, gdn_norm=gdn_norm, s5_lam_re=s5_lam_re, s5_lam_im=s5_lam_im,
             s5_log_dt=s5_log_dt, s5_B_re=s5_B_re, s5_B_im=s5_B_im, s5_C_re=s5_C_re, s5_C_im=s5_C_im,
             s5_D=s5_D, s5_w_glu=s5_w_glu, ml_ig_bias=ml_ig_bias, ml_fg_bias=ml_fg_bias,
             ml_norm=ml_norm, norm_ffn=norm_ffn, w_router_group=w_router_group,
             b_router_group=b_router_group, w_router_expert=w_router_expert,
             b_router_expert=b_router_expert, w_exp_gate=w_exp_gate, w_exp_up=w_exp_up,
             w_exp_down=w_exp_down, norm_ple=norm_ple, w_ple_gate=w_ple_gate, w_ple_proj=w_ple_proj)
    depth = norm_mix.shape[0]
    layers = [_prep_layer(l, P) for l in range(depth)]
    fnorm = final_norm.reshape(1, D_MODEL).astype(F32)

    bp, tp, _ = x_prompt.shape
    zeros = lambda *s: jnp.zeros((depth, bp) + s, F32)
    prompt_init = (zeros(CONV_W - 1, GDN_CONV_CH), zeros(GDN_HEADS, GDN_DK, GDN_DV),
                   zeros(S5_GROUPS, S5_STATE), zeros(S5_GROUPS, S5_STATE),
                   zeros(ML_HEADS, ML_DH, ML_DH), zeros(ML_HEADS, ML_DH), zeros(ML_HEADS))
    lp = math.gcd(tp, 64)
    lsp = math.gcd(tp, 256)
    res_p = _trunk(x_prompt, p_prompt, prompt_init, layers, fnorm,
                   L=lp, tv=lp, Ls=lsp, tm=512, tm_ffn=1024, bb=8, bbs=1)

    bs, ts, _ = x_sample.shape
    tpad = -(-ts // SUBLANES) * SUBLANES
    xs = jnp.pad(x_sample, ((0, 0), (0, tpad - ts), (0, 0)))
    ps = jnp.pad(p_sample, ((0, 0), (0, 0), (0, tpad - ts), (0, 0)))
    sample_init = (state_gdn_conv, state_gdn, state_s5_re, state_s5_im,
                   state_mlstm_C, state_mlstm_n, state_mlstm_m)
    res_s = _trunk(xs, ps, sample_init, layers, fnorm,
                   L=tpad, tv=ts, Ls=tpad, tm=512, tm_ffn=1024, bb=8, bbs=8)
    y_sample = res_s[0][:, :ts]
    return (res_p[0], y_sample) + res_p[1:] + res_s[1:]
```

```python
import functools
import math

import jax
import jax.numpy as jnp
from jax import lax
from jax.experimental import pallas as pl
from jax.experimental.pallas import tpu as pltpu

F32 = jnp.float32
BF16 = jnp.bfloat16

D_MODEL = 1024
DEPTH = 2
GDN_HEADS = 6
GDN_DK = 64
GDN_DV = 64
GDN_QK = GDN_HEADS * GDN_DK
GDN_VW = GDN_HEADS * GDN_DV
GDN_CONV_CH = 2 * GDN_QK + GDN_VW
CONV_W = 4
S5_GROUPS = 16
S5_GROUP_CH = 16
S5_CH = S5_GROUPS * S5_GROUP_CH
S5_STATE = 64
S5_N = S5_GROUPS * S5_STATE
ML_HEADS = 6
ML_DH = 64
ML_W = ML_HEADS * ML_DH
N_GROUPS = 4
EXPERTS_PER_GROUP = 4
N_EXPERTS = N_GROUPS * EXPERTS_PER_GROUP
D_EXPERT = 256
PLE_DIM = 256
EPS = 1e-6

LANES = 128
SUBLANES = 8
NEG = -1e30
VMEM_LIMIT = 56 * 1024 * 1024

def _cparams(*sem):
    return pltpu.CompilerParams(dimension_semantics=sem, vmem_limit_bytes=VMEM_LIMIT)


def _dot(a, b):
    return jnp.dot(a.astype(BF16), b.astype(BF16), preferred_element_type=F32)


def _dot_nt(a, b):
    return lax.dot_general(a.astype(BF16), b.astype(BF16), (((1,), (1,)), ((), ())),
                           preferred_element_type=F32)


def _dot_tn(a, b):
    return lax.dot_general(a.astype(BF16), b.astype(BF16), (((0,), (0,)), ((), ())),
                           preferred_element_type=F32)


def _split_bf16(a):
    hi = a.astype(BF16)
    lo = (a - hi.astype(F32)).astype(BF16)
    return hi, lo


def _rms(x, gain):
    return x * lax.rsqrt(jnp.mean(x * x, axis=-1, keepdims=True) + EPS) * gain


def _softplus(x):
    return jnp.maximum(x, 0.0) + jnp.log(1.0 + jnp.exp(-jnp.abs(x)))


def _sigmoid(x):
    return 1.0 / (1.0 + jnp.exp(-x))


def _silu(x):
    return x * _sigmoid(x)


def _interleave(gens):
    live = list(gens)
    while live:
        still = []
        for g in live:
            try:
                next(g)
                still.append(g)
            except StopIteration:
                pass
        live = still


def _unit_lower_inverse(n_mat, size, top=None):
    top = size if top is None else top
    r = lax.broadcasted_iota(jnp.int32, (size, size), 0)
    c = lax.broadcasted_iota(jnp.int32, (size, size), 1)
    base = min(16, top)
    same = jnp.bitwise_xor(r, c) < base
    nd = jnp.where(same, n_mat, 0.0)
    eye = jnp.where(r == c, 1.0, 0.0).astype(F32)
    t = eye - nd
    p = 1
    if 2 * p < base:
        x = _dot(nd, nd)
        yield
    while 2 * p < base:
        t_next = t + _dot(t, x)
        if 4 * p < base:
            x = _dot(x, x)
        t = t_next
        yield
        p *= 2
    blk = base
    while blk < top:
        pair = jnp.bitwise_xor(r, c)
        off = jnp.where((pair < 2 * blk) & (pair >= blk), n_mat, 0.0)
        ot = _dot(off, t)
        yield
        t = t - _dot(t, ot)
        yield
        blk *= 2
    return t


MIX_W = 4 * ML_W
GATE_W = 2 * ML_W


def _norm_inproj_kernel(x_ref, g_ref, w_ref, e_ref, og_ref, os_ref, om_ref):
    u = _rms(x_ref[...], g_ref[...]).astype(BF16)
    dot = functools.partial(jnp.dot, preferred_element_type=F32)
    og_ref[:, 0:MIX_W] = dot(u, w_ref[:, 0:MIX_W])
    os_ref[...] = dot(u, w_ref[:, MIX_W:MIX_W + S5_CH])
    om_ref[:, 0:MIX_W] = dot(u, w_ref[:, MIX_W + S5_CH:2 * MIX_W + S5_CH])
    small = dot(u, w_ref[:, 2 * MIX_W + S5_CH:])
    rep = None
    for piece in _split3(small):
        t = dot(piece, e_ref[...])
        rep = t if rep is None else rep + t
    og_ref[:, MIX_W:] = rep[:, 0:GATE_W]
    om_ref[:, MIX_W:] = rep[:, GATE_W:]


def _norm_inproj(x, gain, w, e_rep, tm):
    m = x.shape[0]
    fix = lambda i: (0, 0)
    widths = (MIX_SLAB_W, S5_CH, MIX_SLAB_W)
    return pl.pallas_call(
        _norm_inproj_kernel,
        grid=(m // tm,),
        in_specs=[pl.BlockSpec((tm, D_MODEL), lambda i: (i, 0)),
                  pl.BlockSpec((1, D_MODEL), fix),
                  pl.BlockSpec((D_MODEL, w.shape[1]), fix),
                  pl.BlockSpec((LANES, 2 * GATE_W), fix)],
        out_specs=[pl.BlockSpec((tm, wd), lambda i: (i, 0)) for wd in widths],
        out_shape=[jax.ShapeDtypeStruct((m, wd), F32) for wd in widths],
        compiler_params=_cparams("parallel"),
        name="norm_inproj",
    )(x, gain, w, e_rep)


def _route(f, wr, br):
    fh, fl = _split_bf16(f)
    wh, wl = _split_bf16(wr)
    d = functools.partial(jnp.dot, preferred_element_type=F32)
    logits = d(fh, wh) + (d(fh, wl) + d(fl, wh)) + br
    lane = lax.broadcasted_iota(jnp.int32, logits.shape, 1)
    is_g = lane < N_GROUPS
    gl = jnp.where(is_g, logits, NEG)
    gmax = jnp.max(gl, axis=-1, keepdims=True)
    ge = jnp.where(is_g, jnp.exp(gl - gmax), 0.0)
    p_grp = ge / jnp.sum(ge, axis=-1, keepdims=True)
    g_prob = jnp.max(p_grp, axis=-1, keepdims=True)
    g_idx = jnp.min(jnp.where(is_g & (gl == gmax), lane, LANES), axis=-1, keepdims=True)
    e_lane = lane - N_GROUPS
    is_e = (e_lane >= 0) & (e_lane < N_EXPERTS) & (jnp.right_shift(e_lane, 2) == g_idx)
    le = jnp.where(is_e, logits, NEG)
    m1 = jnp.max(le, axis=-1, keepdims=True)
    i1 = jnp.min(jnp.where(is_e & (le == m1), lane, LANES), axis=-1, keepdims=True)
    is_e2 = is_e & (lane != i1)
    le2 = jnp.where(is_e2, logits, NEG)
    m2 = jnp.max(le2, axis=-1, keepdims=True)
    i2 = jnp.min(jnp.where(is_e2 & (le2 == m2), lane, LANES), axis=-1, keepdims=True)
    e2 = jnp.exp(m2 - m1)
    w1 = g_prob / (1.0 + e2)
    w2 = g_prob * e2 / (1.0 + e2)
    return fh, jnp.where(lane == i1, w1, 0.0) + jnp.where(lane == i2, w2, 0.0)


def _ffn_kernel(og_ref, os_ref, om_ref, h_ref, p_ref, wo_ref, nf_ref, wr_ref, br_ref,
                wg_ref, wu_ref, wd_ref, np_ref, wpg_ref, wpp_ref, fn_ref,
                out_ref, f_ref, gates_ref, *, final):
    acc_ref = out_ref
    gi = pl.program_id(1)

    @pl.when(gi == 0)
    def _():
        mix = jnp.concatenate([og_ref[...], os_ref[...], om_ref[...]], axis=-1)
        h1 = h_ref[...] + jnp.dot(mix, wo_ref[...], preferred_element_type=F32)
        acc_ref[...] = h1
        fh, gates = _route(_rms(h1, nf_ref[...]), wr_ref[...], br_ref[...])
        f_ref[...] = fh
        gates_ref[...] = gates

    x = f_ref[...]
    gates = gates_ref[...]
    lane = lax.broadcasted_iota(jnp.int32, gates.shape, 1)
    base = N_GROUPS + EXPERTS_PER_GROUP * gi
    acc = None
    for j in range(EXPERTS_PER_GROUP):
        gcol = jnp.sum(jnp.where(lane == base + j, gates, 0.0), axis=-1, keepdims=True)
        hg = jnp.dot(x, wg_ref[j], preferred_element_type=F32)
        hu = jnp.dot(x, wu_ref[j], preferred_element_type=F32)
        hidden = (_silu(hg) * hu * gcol).astype(BF16)
        t = jnp.dot(hidden, wd_ref[j], preferred_element_type=F32)
        acc = t if acc is None else acc + t
    acc_ref[...] += acc

    @pl.when(gi == N_GROUPS - 1)
    def _():
        h = acc_ref[...]
        gate = _sigmoid(jnp.dot(_rms(h, np_ref[...]).astype(BF16), wpg_ref[...],
                                preferred_element_type=F32))
        proj = jnp.dot(p_ref[0].astype(BF16), wpp_ref[...], preferred_element_type=F32)
        h = h + proj * gate
        if final:
            h = _rms(h, fn_ref[...])
        out_ref[...] = h


def _ffn(og, os_, om, h, p_all, layer, W, fn, tm, final):
    m = h.shape[0]
    row = lambda i, g: (i, 0)
    wsel = lambda i, g: (g, 0, 0)
    e = EXPERTS_PER_GROUP

    def fix(shape):
        return pl.BlockSpec(shape, lambda i, g: (0, 0), pipeline_mode=pl.Buffered(1))

    return pl.pallas_call(
        functools.partial(_ffn_kernel, final=final),
        grid=(m // tm, N_GROUPS),
        in_specs=[pl.BlockSpec((tm, GDN_VW), row), pl.BlockSpec((tm, S5_CH), row),
                  pl.BlockSpec((tm, ML_W), row), pl.BlockSpec((tm, D_MODEL), row),
                  pl.BlockSpec((1, tm, PLE_DIM), lambda i, g: (layer, i, 0)),
                  fix((D_MODEL, D_MODEL)), fix((1, D_MODEL)),
                  fix((D_MODEL, LANES)), fix((1, LANES)),
                  pl.BlockSpec((e, D_MODEL, D_EXPERT), wsel),
                  pl.BlockSpec((e, D_MODEL, D_EXPERT), wsel),
                  pl.BlockSpec((e, D_EXPERT, D_MODEL), wsel),
                  fix((1, D_MODEL)), fix((D_MODEL, D_MODEL)),
                  fix((PLE_DIM, D_MODEL)), fix((1, D_MODEL))],
        out_specs=pl.BlockSpec((tm, D_MODEL), row),
        out_shape=jax.ShapeDtypeStruct((m, D_MODEL), F32),
        scratch_shapes=[pltpu.VMEM((tm, D_MODEL), BF16), pltpu.VMEM((tm, LANES), F32)],
        compiler_params=_cparams("parallel", "arbitrary"),
        name="ffn",
    )(og, os_, om, h, p_all, W["w_out"], W["norm_ffn"], W["wr"], W["br"],
      W["wg"], W["wu"], W["wd"], W["norm_ple"], W["w_ple_gate"], W["w_ple_proj"], fn)


MIX_SLAB_W = 6 * ML_W
PAIRS = ML_HEADS // 2
HALF = LANES // 2


def _scan_rows(x, size, op, fill):
    row = lax.broadcasted_iota(jnp.int32, x.shape, 0)
    d = 1
    while d < size:
        x = op(x, jnp.where(row >= d, pltpu.roll(x, d, axis=0), fill))
        d *= 2
    return x


def _split3(a):
    hi = a.astype(BF16)
    r1 = a - hi.astype(F32)
    mid = r1.astype(BF16)
    lo = (r1 - mid.astype(F32)).astype(BF16)
    return hi, mid, lo


def _row_form(x_s, L):
    rr = lax.broadcasted_iota(jnp.int32, x_s.shape, 0)
    cc = lax.broadcasted_iota(jnp.int32, x_s.shape, 1)
    dg = jnp.where(rr == jnp.bitwise_and(cc, L - 1), x_s, 0.0)
    ones = jnp.ones((SUBLANES, L), BF16)
    acc = None
    for piece in _split3(dg):
        t = jnp.dot(ones, piece, preferred_element_type=F32)
        acc = t if acc is None else acc + t
    return acc[0:1, :]


def _score_cols(x, p, L):
    if 2 * L == LANES:
        return x[:, LANES * p:LANES * (p + 1)]
    return jnp.concatenate([x[:, LANES * p:LANES * p + L],
                            x[:, LANES * p + HALF:LANES * p + HALF + L]], axis=-1)


def _bd_stack(x2):
    lo = lax.broadcasted_iota(jnp.int32, x2.shape, 1) < HALF
    return jnp.concatenate([jnp.where(lo, x2, 0.0), jnp.where(lo, 0.0, x2)], axis=0)


def _bd_ones(rows_per_half):
    shape = (2 * rows_per_half, LANES)
    r = lax.broadcasted_iota(jnp.int32, shape, 0) < rows_per_half
    c = lax.broadcasted_iota(jnp.int32, shape, 1) < HALF
    return jnp.where(r == c, 1.0, 0.0).astype(F32)


def _layer_state_spec(s_all, layer, bb, in_idx, out_idx):
    depth = s_all.shape[0]
    tail = s_all.shape[2:]
    if layer == 0 and depth > 1:
        return pl.BlockSpec((depth, bb) + tail, lambda b, c: (0, b, 0, 0, 0)), 0, {}
    return (pl.BlockSpec((1, bb) + tail, lambda b, c: (layer, b, 0, 0, 0)), 0,
            {in_idx: out_idx})


def _half_sums(x2, ones_bd, two_pieces=False):
    if not two_pieces:
        return _dot(x2, ones_bd)
    hi, lo = _split_bf16(x2)
    ob = ones_bd.astype(BF16)
    return (jnp.dot(hi, ob, preferred_element_type=F32)
            + jnp.dot(lo, ob, preferred_element_type=F32))


def _mlstm_kernel(a_ref, c0_ref, n0_ref, m0_ref, bi_ref, bf_ref, ng_ref,
                  o_ref, cn_ref, nn_ref, mn_ref, st_ref, m_ref, *, L, tv, bb, lsel):
    cidx = pl.program_id(1)
    nc = pl.num_programs(1)
    ones_ch = _bd_ones(HALF)
    r128 = lax.broadcasted_iota(jnp.int32, (LANES, LANES), 0)
    c128 = lax.broadcasted_iota(jnp.int32, (LANES, LANES), 1)
    diag128 = r128 == c128

    @pl.when(cidx == 0)
    def _():
        st_ref[...] = jnp.zeros(st_ref.shape, F32)
        m_ref[...] = m0_ref[...]
        ob = ones_ch.astype(BF16)
        for i in range(bb):
            for p in range(PAIRS):
                st_ref[i, p, 0:HALF, 0:HALF] = c0_ref[lsel, i, 2 * p]
                st_ref[i, p, HALF:, HALF:LANES] = c0_ref[lsel, i, 2 * p + 1]
                dg = jnp.where(diag128, n0_ref[i, p], 0.0)
                acc = None
                for piece in _split3(dg):
                    t = jnp.dot(piece, ob, preferred_element_type=F32)
                    acc = t if acc is None else acc + t
                st_ref[i, p, :, LANES:] = acc

    W = 2 * L
    rr = lax.broadcasted_iota(jnp.int32, (L, W), 0)
    cc = lax.broadcasted_iota(jnp.int32, (L, W), 1)
    incl = rr >= jnp.bitwise_and(cc, L - 1)
    ones_keys = _bd_ones(L)
    bd256 = jnp.concatenate([ones_ch, ones_ch], axis=-1) > 0.5
    gain2 = ng_ref[...]
    gain2 = jnp.concatenate([gain2, gain2], axis=-1)
    ones_l = jnp.ones((L, LANES), F32)
    outs = [[None] * PAIRS for _ in range(bb)]
    st_news = [[None] * PAIRS for _ in range(bb)]
    m_news = [None] * bb
    rows = []
    for i in range(bb):
        li = a_ref[i, :, 4 * ML_W:5 * ML_W] + bi_ref[...]
        lf = -_softplus(-(a_ref[i, :, 5 * ML_W:6 * ML_W] + bf_ref[...]))
        if tv < L:
            valid = lax.broadcasted_iota(jnp.int32, li.shape, 0) < tv
            li = jnp.where(valid, li, NEG)
            lf = jnp.where(valid, lf, 0.0)
        bcum = _scan_rows(lf, L, jnp.add, 0.0)
        a = li - bcum
        m0 = m_ref[i]
        m_t = bcum + jnp.maximum(m0, _scan_rows(a, L, jnp.maximum, NEG))
        e_inter = jnp.exp(bcum + m0 - m_t)
        m_new = m_t[L - 1:L, :]
        b_last = bcum[L - 1:L, :]
        e_c = jnp.exp(b_last + m0 - m_new)
        kw = a_ref[i, :, ML_W:2 * ML_W] * (ML_DH ** -0.5) * jnp.exp(b_last + a - m_new)
        m_news[i] = m_new
        rows.append((a, bcum, m_t, e_inter, e_c, kw))

    def pair(i, p):
        a, bcum, m_t, e_inter, e_c, kw = rows[i]
        sl = slice(LANES * p, LANES * (p + 1))
        q2 = a_ref[i, :, sl]
        k2 = a_ref[i, :, ML_W + LANES * p:ML_W + LANES * (p + 1)] * (ML_DH ** -0.5)
        v2 = a_ref[i, :, 2 * ML_W + LANES * p:2 * ML_W + LANES * (p + 1)]
        og2 = a_ref[i, :, 3 * ML_W + LANES * p:3 * ML_W + LANES * (p + 1)]
        st = st_ref[i, p]
        arow = _row_form(_score_cols(a, p, L), L)
        qk = _dot_nt(q2, _bd_stack(k2))
        qcn = _dot(q2, st)
        upd = _dot_tn(kw[:, sl], jnp.concatenate([v2, ones_l], axis=-1))
        ec2 = e_c[:, sl]
        st_news[i][p] = st * jnp.concatenate([ec2, ec2], axis=-1) + jnp.where(bd256, upd, 0.0)
        yield
        w_intra = jnp.where(incl, jnp.exp(_score_cols(bcum, p, L) + arow - _score_cols(m_t, p, L)), 0.0)
        s2 = qk * w_intra
        nd = _dot(s2, jnp.concatenate([_bd_stack(v2), ones_keys], axis=-1))
        yield
        e2 = e_inter[:, sl]
        num = e2 * qcn[:, 0:LANES] + nd[:, 0:LANES]
        den = e2 * qcn[:, LANES:] + nd[:, LANES:]
        hh = num / jnp.maximum(jnp.abs(den), jnp.exp(-m_t[:, sl]))
        ss = _half_sums(hh * hh, ones_ch)
        yield
        outs[i][p] = hh * lax.rsqrt(ss * (1.0 / ML_DH) + EPS) * gain2 * _sigmoid(og2)

    _interleave([pair(i, p) for i in range(bb) for p in range(PAIRS)])
    for i in range(bb):
        for p in range(PAIRS):
            st_ref[i, p] = st_news[i][p]
        m_ref[i] = m_news[i]
        o_ref[i] = jnp.concatenate(outs[i], axis=-1).astype(o_ref.dtype)

    @pl.when(cidx == nc - 1)
    def _():
        ones8 = jnp.ones((SUBLANES, LANES), BF16)
        for k in range(cn_ref.shape[0]):
            if k != lsel:
                cn_ref[k] = c0_ref[k]
        for i in range(bb):
            for p in range(PAIRS):
                st = st_news[i][p]
                cn_ref[lsel, i, 2 * p] = st[0:HALF, 0:HALF]
                cn_ref[lsel, i, 2 * p + 1] = st[HALF:, HALF:LANES]
                dg = jnp.where(diag128, st[:, LANES:], 0.0)
                acc = None
                for piece in _split3(dg):
                    t = jnp.dot(ones8, piece, preferred_element_type=F32)
                    acc = t if acc is None else acc + t
                nn_ref[i, p] = acc[0:1, :]
        mn_ref[...] = m_ref[...]


def _mlstm(slab, c_all, layer, n0, m0, bi, bf, ng, L, tv, bb):
    bsz, t, _ = slab.shape
    nc = t // L
    fix2 = lambda b, c: (0, 0)
    c_spec, lsel, aliases = _layer_state_spec(c_all, layer, bb, 1, 1)
    n_spec = pl.BlockSpec((bb, PAIRS, 1, LANES), lambda b, c: (b, 0, 0, 0))
    m_spec = pl.BlockSpec((bb, 1, ML_W), lambda b, c: (b, 0, 0))
    return pl.pallas_call(
        functools.partial(_mlstm_kernel, L=L, tv=tv, bb=bb, lsel=lsel),
        grid=(bsz // bb, nc),
        input_output_aliases=aliases,
        in_specs=[pl.BlockSpec((bb, L, MIX_SLAB_W), lambda b, c: (b, c, 0)),
                  c_spec, n_spec, m_spec,
                  pl.BlockSpec((1, ML_W), fix2), pl.BlockSpec((1, ML_W), fix2),
                  pl.BlockSpec((1, ML_DH), fix2)],
        out_specs=[pl.BlockSpec((bb, L, ML_W), lambda b, c: (b, c, 0)), c_spec, n_spec, m_spec],
        out_shape=[jax.ShapeDtypeStruct((bsz, t, ML_W), BF16),
                   jax.ShapeDtypeStruct(c_all.shape, F32),
                   jax.ShapeDtypeStruct((bsz, PAIRS, 1, LANES), F32),
                   jax.ShapeDtypeStruct((bsz, 1, ML_W), F32)],
        scratch_shapes=[pltpu.VMEM((bb, PAIRS, LANES, 2 * LANES), F32),
                        pltpu.VMEM((bb, 1, ML_W), F32)],
        compiler_params=_cparams("parallel", "arbitrary"),
        name="mlstm",
    )(slab, c_all, n0, m0, bi, bf, ng)


def _gdn_kernel(a_ref, conv0_ref, s0_ref, cw_ref, alog_ref, dtb_ref, ng_ref,
                o_ref, convn_ref, sn_ref, ext_ref, s_ref, *, L, tv, bb, lsel):
    c = pl.program_id(1)
    nc = pl.num_programs(1)

    @pl.when(c == 0)
    def _():
        ext_ref[:, 0:SUBLANES, :] = conv0_ref[...]
        s_ref[...] = jnp.zeros(s_ref.shape, F32)
        for i in range(bb):
            for p in range(PAIRS):
                s_ref[i, p, 0:HALF, 0:HALF] = s0_ref[lsel, i, 2 * p]
                s_ref[i, p, HALF:, HALF:] = s0_ref[lsel, i, 2 * p + 1]

    W = 2 * L
    rr = lax.broadcasted_iota(jnp.int32, (L, W), 0)
    cc = lax.broadcasted_iota(jnp.int32, (L, W), 1)
    key = jnp.bitwise_and(cc, L - 1)
    incl = rr >= key
    strict = rr > key
    first_half = cc < L
    lo = lax.broadcasted_iota(jnp.int32, (L, LANES), 1) < HALF
    ones_ch = _bd_ones(HALF)
    bd128 = ones_ch > 0.5
    cw = cw_ref[...]
    gain2 = ng_ref[...]
    gain2 = jnp.concatenate([gain2, gain2], axis=-1)
    outs = [[None] * PAIRS for _ in range(bb)]
    s_news = [[None] * PAIRS for _ in range(bb)]
    carries = [None] * bb
    rows = []
    for i in range(bb):
        raw = a_ref[i, :, 0:GDN_CONV_CH]
        ext_ref[i, SUBLANES:SUBLANES + L, :] = raw
        ext = ext_ref[i]
        acc = raw * cw[CONV_W - 1:CONV_W, :]
        for j in range(CONV_W - 1):
            sh = pltpu.roll(ext, CONV_W - 1 - j, axis=0)[SUBLANES:SUBLANES + L]
            acc = acc + sh * cw[j:j + 1, :]
        qkv = _silu(acc)
        carries[i] = pltpu.roll(ext, (L + SUBLANES - tv) % (L + SUBLANES), axis=0)[0:SUBLANES]

        beta = _sigmoid(a_ref[i, :, 4 * GDN_VW:5 * GDN_VW])
        g = -jnp.exp(alog_ref[...]) * _softplus(a_ref[i, :, 5 * GDN_VW:6 * GDN_VW] + dtb_ref[...])
        if tv < L:
            valid = lax.broadcasted_iota(jnp.int32, g.shape, 0) < tv
            beta = jnp.where(valid, beta, 0.0)
            g = jnp.where(valid, g, 0.0)
        gcum = _scan_rows(g, L, jnp.add, 0.0)
        g_last = gcum[L - 1:L, :]
        rows.append((qkv, beta, gcum, jnp.exp(gcum), jnp.exp(g_last - gcum), jnp.exp(g_last)))
    for i in range(bb):
        ext_ref[i, 0:SUBLANES, :] = carries[i]

    def pair(i, p):
        qkv, beta, gcum, eg, kdec, sdec = rows[i]
        sl = slice(LANES * p, LANES * (p + 1))
        q2 = qkv[:, LANES * p:LANES * (p + 1)]
        k2 = qkv[:, GDN_QK + LANES * p:GDN_QK + LANES * (p + 1)]
        v2 = qkv[:, 2 * GDN_QK + LANES * p:2 * GDN_QK + LANES * (p + 1)]
        z2 = a_ref[i, :, GDN_CONV_CH + LANES * p:GDN_CONV_CH + LANES * (p + 1)]
        s2 = s_ref[i, p]
        ssq = _half_sums(q2 * q2, ones_ch, L < HALF)
        ssk = _half_sums(k2 * k2, ones_ch, L < HALF)
        grow = _row_form(_score_cols(gcum, p, L), L)
        yield
        q2 = q2 * lax.rsqrt(ssq + EPS) * (GDN_DK ** -0.5)
        k2 = k2 * lax.rsqrt(ssk + EPS)
        beta2 = beta[:, sl]
        eg2 = eg[:, sl]
        kb2 = k2 * beta2
        kq = _dot_nt(jnp.concatenate([_bd_stack(kb2), q2], axis=0), _bd_stack(k2))
        qs = _dot(q2 * eg2, s2)
        yield
        gam = jnp.where(incl, jnp.exp(jnp.minimum(_score_cols(gcum, p, L) - grow, 0.0)), 0.0)
        gam_s = jnp.where(strict, gam, 0.0)
        gam_bd = jnp.concatenate([jnp.where(first_half, gam_s, 0.0),
                                  jnp.where(first_half, 0.0, gam_s)], axis=0)
        t_inv = yield from _unit_lower_inverse(kq[0:W] * gam_bd, W, L)
        rhs = jnp.concatenate([v2 * beta2, kb2 * eg2], axis=-1)
        sol = _dot(t_inv, jnp.concatenate([rhs, rhs], axis=0))
        yield
        u2 = jnp.where(lo, sol[0:L, 0:LANES], sol[L:W, 0:LANES])
        w2 = jnp.where(lo, sol[0:L, LANES:], sol[L:W, LANES:])
        v_new = u2 - _dot(w2, s2)
        yield
        o2 = qs + _dot(kq[W:] * gam, _bd_stack(v_new))
        upd = _dot_tn(k2 * kdec[:, sl], v_new)
        s_news[i][p] = s2 * sdec[:, sl] + jnp.where(bd128, upd, 0.0)
        yield
        ss = _half_sums(o2 * o2, ones_ch, L < HALF)
        yield
        outs[i][p] = o2 * lax.rsqrt(ss * (1.0 / GDN_DV) + EPS) * gain2 * _silu(z2)

    _interleave([pair(i, p) for i in range(bb) for p in range(PAIRS)])
    for i in range(bb):
        for p in range(PAIRS):
            s_ref[i, p] = s_news[i][p]
        o_ref[i] = jnp.concatenate(outs[i], axis=-1).astype(o_ref.dtype)

    @pl.when(c == nc - 1)
    def _():
        for k in range(sn_ref.shape[0]):
            if k != lsel:
                sn_ref[k] = s0_ref[k]
        for i in range(bb):
            convn_ref[i] = carries[i]
            for p in range(PAIRS):
                sn_ref[lsel, i, 2 * p] = s_news[i][p][0:HALF, 0:HALF]
                sn_ref[lsel, i, 2 * p + 1] = s_news[i][p][HALF:, HALF:]


def _gdn(slab, conv0, s_all, layer, cw, alog, dtb, ng, L, tv, bb):
    bsz, t, _ = slab.shape
    nc = t // L
    fix2 = lambda b, c: (0, 0)
    s_spec, lsel, aliases = _layer_state_spec(s_all, layer, bb, 2, 2)
    cv_spec = pl.BlockSpec((bb, SUBLANES, GDN_CONV_CH), lambda b, c: (b, 0, 0))
    return pl.pallas_call(
        functools.partial(_gdn_kernel, L=L, tv=tv, bb=bb, lsel=lsel),
        grid=(bsz // bb, nc),
        input_output_aliases=aliases,
        in_specs=[pl.BlockSpec((bb, L, MIX_SLAB_W), lambda b, c: (b, c, 0)),
                  cv_spec, s_spec,
                  pl.BlockSpec((CONV_W, GDN_CONV_CH), fix2),
                  pl.BlockSpec((1, GDN_VW), fix2), pl.BlockSpec((1, GDN_VW), fix2),
                  pl.BlockSpec((1, GDN_DV), fix2)],
        out_specs=[pl.BlockSpec((bb, L, GDN_VW), lambda b, c: (b, c, 0)), cv_spec, s_spec],
        out_shape=[jax.ShapeDtypeStruct((bsz, t, GDN_VW), BF16),
                   jax.ShapeDtypeStruct((bsz, SUBLANES, GDN_CONV_CH), F32),
                   jax.ShapeDtypeStruct(s_all.shape, F32)],
        scratch_shapes=[pltpu.VMEM((bb, L + SUBLANES, GDN_CONV_CH), F32),
                        pltpu.VMEM((bb, PAIRS, LANES, LANES), F32)],
        compiler_params=_cparams("parallel", "arbitrary"),
        name="gdn",
    )(slab, conv0, s_all, cw, alog, dtb, ng)


def _cmul(ar, ai, br, bi):
    return ar * br - ai * bi, ar * bi + ai * br


def _s5_kernel(u_ref, h0_ref, lb_ref, bw_ref, cw_ref, dsk_ref, wglu_ref,
               o_ref, hn_ref, car_ref, h_ref, *, L, bb):
    c = pl.program_id(1)
    nc = pl.num_programs(1)

    @pl.when(c == 0)
    def _():
        car_ref[...] = h0_ref[...]

    rows = bb * L
    u = u_ref[...].reshape(rows, S5_CH)
    bu = jnp.dot(u.astype(BF16), bw_ref[...], preferred_element_type=F32)
    tiles = rows // SUBLANES
    x_re = bu[:, 0:S5_N].reshape(tiles, SUBLANES, S5_N)
    x_im = bu[:, S5_N:].reshape(tiles, SUBLANES, S5_N)
    p1 = (lb_ref[0:1, :], lb_ref[1:2, :])
    p2 = _cmul(*p1, *p1)
    p4 = _cmul(*p2, *p2)
    p8 = _cmul(*p4, *p4)
    sub = lax.broadcasted_iota(jnp.int32, (SUBLANES, S5_N), 0)
    for d, (pr, pi) in ((1, p1), (2, p2), (4, p4)):
        pr = jnp.where(sub >= d, pr, 0.0)
        pi = jnp.where(sub >= d, pi, 0.0)
        s_re = pltpu.roll(x_re, d, axis=1)
        s_im = pltpu.roll(x_im, d, axis=1)
        x_re, x_im = x_re + (pr * s_re - pi * s_im), x_im + (pr * s_im + pi * s_re)
    x_re = x_re.reshape(rows, S5_N)
    x_im = x_im.reshape(rows, S5_N)
    k = sub + 1
    pw_re = jnp.ones((SUBLANES, S5_N), F32)
    pw_im = jnp.zeros((SUBLANES, S5_N), F32)
    for bit, (pr, pi) in ((1, p1), (2, p2), (4, p4), (8, p8)):
        m_re, m_im = _cmul(pw_re, pw_im, pr, pi)
        on = jnp.bitwise_and(k, bit) != 0
        pw_re = jnp.where(on, m_re, pw_re)
        pw_im = jnp.where(on, m_im, pw_im)
    for i in range(bb):
        car_re = car_ref[i, 0]
        car_im = car_ref[i, 1]
        for t in range(L // SUBLANES):
            r0 = i * L + t * SUBLANES
            a_re, a_im = _cmul(pw_re, pw_im, car_re, car_im)
            t_re = x_re[r0:r0 + SUBLANES, :] + a_re
            t_im = x_im[r0:r0 + SUBLANES, :] + a_im
            h_ref[r0:r0 + SUBLANES, 0:S5_N] = t_re
            h_ref[r0:r0 + SUBLANES, S5_N:] = t_im
            car_re = jnp.broadcast_to(t_re[SUBLANES - 1:SUBLANES, :], (SUBLANES, S5_N))
            car_im = jnp.broadcast_to(t_im[SUBLANES - 1:SUBLANES, :], (SUBLANES, S5_N))
        car_ref[i, 0] = car_re
        car_ref[i, 1] = car_im
        hn_ref[i, 0] = t_re
        hn_ref[i, 1] = t_im
    y = jnp.dot(h_ref[...].astype(BF16), cw_ref[...], preferred_element_type=F32) + dsk_ref[...] * u
    zg = jax.nn.gelu(y)
    out = zg * _sigmoid(jnp.dot(zg.astype(BF16), wglu_ref[...], preferred_element_type=F32))
    o_ref[...] = out.reshape(bb, L, S5_CH).astype(o_ref.dtype)


def _s5(u, h0, lb, bw, cw, dsk, wglu, L, bb):
    bsz, t, _ = u.shape
    nc = t // L
    fix2 = lambda b, c: (0, 0)
    h_spec = pl.BlockSpec((bb, 2, SUBLANES, S5_N), lambda b, c: (b, 0, 0, 0))
    return pl.pallas_call(
        functools.partial(_s5_kernel, L=L, bb=bb),
        grid=(bsz // bb, nc),
        in_specs=[pl.BlockSpec((bb, L, S5_CH), lambda b, c: (b, c, 0)),
                  h_spec,
                  pl.BlockSpec((SUBLANES, S5_N), fix2),
                  pl.BlockSpec((S5_CH, 2 * S5_N), fix2),
                  pl.BlockSpec((2 * S5_N, S5_CH), fix2),
                  pl.BlockSpec((1, S5_CH), fix2),
                  pl.BlockSpec((S5_CH, S5_CH), fix2)],
        out_specs=[pl.BlockSpec((bb, L, S5_CH), lambda b, c: (b, c, 0)), h_spec],
        out_shape=[jax.ShapeDtypeStruct((bsz, t, S5_CH), BF16),
                   jax.ShapeDtypeStruct((bsz, 2, SUBLANES, S5_N), F32)],
        scratch_shapes=[pltpu.VMEM((bb, 2, SUBLANES, S5_N), F32),
                        pltpu.VMEM((bb * L, 2 * S5_N), F32)],
        compiler_params=_cparams("parallel", "arbitrary"),
        name="s5",
    )(u, h0, lb, bw, cw, dsk, wglu)


def _prep_layer(l, P):
    w_in = P["w_in"][l]
    sizes = (GDN_CONV_CH, GDN_VW, GDN_HEADS, GDN_HEADS, S5_CH, 3 * ML_W, ML_HEADS, ML_HEADS, ML_W)
    offs = [0]
    for s in sizes:
        offs.append(offs[-1] + s)
    g_qkv, g_z, g_b, g_a, s_u, m_qkv, m_i, m_f, m_o = [w_in[:, offs[i]:offs[i + 1]] for i in range(9)]
    zpad = jnp.zeros((D_MODEL, LANES - 4 * GDN_HEADS), F32)
    w_cat = jnp.concatenate([g_qkv, g_z, s_u, m_qkv, m_o, g_b, g_a, m_i, m_f, zpad],
                            axis=1).astype(BF16)

    lr = P["s5_lam_re"][l].astype(F32)
    li = P["s5_lam_im"][l].astype(F32)
    dt = jnp.exp(P["s5_log_dt"][l].astype(F32))[:, None]
    mag = jnp.exp(lr * dt)
    lb_re = mag * jnp.cos(li * dt)
    lb_im = mag * jnp.sin(li * dt)
    den = lr * lr + li * li
    c_re = ((lb_re - 1.0) * lr + lb_im * li) / den
    c_im = (lb_im * lr - (lb_re - 1.0) * li) / den
    b_r = P["s5_B_re"][l].astype(F32)
    b_i = P["s5_B_im"][l].astype(F32)
    bb_re = c_re[..., None] * b_r - c_im[..., None] * b_i
    bb_im = c_re[..., None] * b_i + c_im[..., None] * b_r
    eye_g = jnp.eye(S5_GROUPS, dtype=F32)
    bd = lambda m: jnp.einsum("gph,gk->ghkp", m, eye_g).reshape(S5_CH, S5_N)
    bw = jnp.concatenate([bd(bb_re), bd(bb_im)], axis=1).astype(BF16)
    cd = lambda m: jnp.einsum("ghp,gk->gpkh", m, eye_g).reshape(S5_N, S5_CH)
    cw = jnp.concatenate([cd(P["s5_C_re"][l].astype(F32)),
                          -cd(P["s5_C_im"][l].astype(F32))], axis=0).astype(BF16)
    lb = jnp.zeros((SUBLANES, S5_N), F32).at[0].set(lb_re.reshape(-1)).at[1].set(lb_im.reshape(-1))

    wr = jnp.zeros((D_MODEL, LANES), F32)
    wr = wr.at[:, 0:N_GROUPS].set(P["w_router_group"][l])
    wr = wr.at[:, N_GROUPS:N_GROUPS + N_EXPERTS].set(P["w_router_expert"][l])
    br = jnp.zeros((1, LANES), F32)
    br = br.at[0, 0:N_GROUPS].set(P["b_router_group"][l])
    br = br.at[0, N_GROUPS:N_GROUPS + N_EXPERTS].set(P["b_router_expert"][l])

    rep_row = lambda v: jnp.repeat(v.astype(F32), ML_DH).reshape(1, ML_W)
    return dict(
        norm_mix=P["norm_mix"][l].reshape(1, D_MODEL).astype(F32),
        w_cat=w_cat,
        w_out=P["w_out"][l].astype(BF16),
        conv_w=P["gdn_conv_w"][l].astype(F32),
        alog=rep_row(P["gdn_A_log"][l]), dtb=rep_row(P["gdn_dt_bias"][l]),
        gdn_norm=P["gdn_norm"][l].reshape(1, GDN_DV).astype(F32),
        lb=lb, bw=bw, cw=cw,
        s5_d=P["s5_D"][l].reshape(1, S5_CH).astype(F32),
        w_glu=P["s5_w_glu"][l].astype(BF16),
        ml_bi=rep_row(P["ml_ig_bias"][l]), ml_bf=rep_row(P["ml_fg_bias"][l]),
        ml_norm=P["ml_norm"][l].reshape(1, ML_DH).astype(F32),
        norm_ffn=P["norm_ffn"][l].reshape(1, D_MODEL).astype(F32),
        wr=wr, br=br,
        wg=P["w_exp_gate"][l].astype(BF16), wu=P["w_exp_up"][l].astype(BF16),
        wd=P["w_exp_down"][l].astype(BF16),
        norm_ple=P["norm_ple"][l].reshape(1, D_MODEL).astype(F32),
        w_ple_gate=P["w_ple_gate"][l].astype(BF16),
        w_ple_proj=P["w_ple_proj"][l].astype(BF16),
    )


def _trunk(x, p, states, layers, final_norm, *, L, tv, Ls, tm, tm_ffn, bb, bbs):
    conv0, gdn0, s5re0, s5im0, mc0, mn0, mm0 = states
    bsz, t, _ = x.shape
    m = bsz * t
    h = x.reshape(m, D_MODEL)
    outs = [[] for _ in range(5)]
    last_row = (tv - 1) % SUBLANES
    p_all = p.reshape(p.shape[0], m, PLE_DIM)
    e_rep = (jnp.arange(2 * GATE_W)[None, :] // ML_DH == jnp.arange(LANES)[:, None]).astype(BF16)
    gdn_all, mc_all = gdn0, mc0
    for l, W in enumerate(layers):
        slab_g, s_u, slab_m = _norm_inproj(h, W["norm_mix"], W["w_cat"], e_rep, tm)
        conv_in = jnp.pad(conv0[l], ((0, 0), (SUBLANES - (CONV_W - 1), 0), (0, 0)))
        o_gdn, conv_n, gdn_all = _gdn(slab_g.reshape(bsz, t, MIX_SLAB_W), conv_in, gdn_all, l,
                                      W["conv_w"], W["alog"], W["dtb"], W["gdn_norm"], L, tv, bb)
        h0 = jnp.stack([s5re0[l].reshape(bsz, S5_N), s5im0[l].reshape(bsz, S5_N)], axis=1)
        h0 = jnp.broadcast_to(h0[:, :, None, :], (bsz, 2, SUBLANES, S5_N))
        o_s5, s5_n = _s5(s_u.reshape(bsz, t, S5_CH), h0, W["lb"], W["bw"], W["cw"], W["s5_d"],
                         W["w_glu"], Ls, bbs)
        o_ml, mc_all, n_n, m_n = _mlstm(slab_m.reshape(bsz, t, MIX_SLAB_W), mc_all, l,
                                        mn0[l].reshape(bsz, PAIRS, 1, LANES),
                                        jnp.repeat(mm0[l], ML_DH, axis=-1).reshape(bsz, 1, ML_W),
                                        W["ml_bi"], W["ml_bf"], W["ml_norm"], L, tv, bb)
        h = _ffn(o_gdn.reshape(m, GDN_VW), o_s5.reshape(m, S5_CH), o_ml.reshape(m, ML_W), h,
                 p_all, l, W, final_norm, tm_ffn, l == len(layers) - 1)
        outs[0].append(conv_n[:, SUBLANES - (CONV_W - 1):])
        outs[1].append(s5_n[:, 0, last_row].reshape(bsz, S5_GROUPS, S5_STATE))
        outs[2].append(s5_n[:, 1, last_row].reshape(bsz, S5_GROUPS, S5_STATE))
        outs[3].append(n_n.reshape(bsz, ML_HEADS, ML_DH))
        outs[4].append(m_n[:, 0, ::ML_DH])
    conv_o, s5re_o, s5im_o, mn_o, mm_o = (jnp.stack(o) for o in outs)
    return (h.reshape(bsz, t, D_MODEL), conv_o, gdn_all, s5re_o, s5im_o, mc_all, mn_o, mm_o)


def kernel(x_prompt, x_sample, p_prompt, p_sample, state_gdn_conv, state_gdn, state_s5_re, state_s5_im, state_mlstm_C, state_mlstm_n, state_mlstm_m, norm_mix, w_in, w_out, gdn_conv_w, gdn_A_log, gdn_dt_bias, gdn_norm, s5_lam_re, s5_lam_im, s5_log_dt, s5_B_re, s5_B_im, s5_C_re, s5_C_im, s5_D, s5_w_glu, ml_ig_bias, ml_fg_bias, ml_norm, norm_ffn, w_router_group, b_router_group, w_router_expert, b_router_expert, w_exp_gate, w_exp_up, w_exp_down, norm_ple, w_ple_gate, w_ple_proj, final_norm):
    P = dict(norm_mix=norm_mix, w_in=w_in, w_out=w_out, gdn_conv_w=gdn_conv_w, gdn_A_log=gdn_A_log,
             gdn_dt_bias=gdn_dt_bias, gdn_norm=gdn_norm, s5_lam_re=s5_lam_re, s5_lam_im=s5_lam_im,
             s5_log_dt=s5_log_dt, s5_B_re=s5_B_re, s5_B_im=s5_B_im, s5_C_re=s5_C_re, s5_C_im=s5_C_im,
             s5_D=s5_D, s5_w_glu=s5_w_glu, ml_ig_bias=ml_ig_bias, ml_fg_bias=ml_fg_bias,
             ml_norm=ml_norm, norm_ffn=norm_ffn, w_router_group=w_router_group,
             b_router_group=b_router_group, w_router_expert=w_router_expert,
             b_router_expert=b_router_expert, w_exp_gate=w_exp_gate, w_exp_up=w_exp_up,
             w_exp_down=w_exp_down, norm_ple=norm_ple, w_ple_gate=w_ple_gate, w_ple_proj=w_ple_proj)
    depth = norm_mix.shape[0]
    layers = [_prep_layer(l, P) for l in range(depth)]
    fnorm = final_norm.reshape(1, D_MODEL).astype(F32)

    bp, tp, _ = x_prompt.shape
    zeros = lambda *s: jnp.zeros((depth, bp) + s, F32)
    prompt_init = (zeros(CONV_W - 1, GDN_CONV_CH), zeros(GDN_HEADS, GDN_DK, GDN_DV),
                   zeros(S5_GROUPS, S5_STATE), zeros(S5_GROUPS, S5_STATE),
                   zeros(ML_HEADS, ML_DH, ML_DH), zeros(ML_HEADS, ML_DH), zeros(ML_HEADS))
    lp = math.gcd(tp, 64)
    lsp = math.gcd(tp, 256)
    res_p = _trunk(x_prompt, p_prompt, prompt_init, layers, fnorm,
                   L=lp, tv=lp, Ls=lsp, tm=512, tm_ffn=1024, bb=8, bbs=1)

    bs, ts, _ = x_sample.shape
    tpad = -(-ts // SUBLANES) * SUBLANES
    xs = jnp.pad(x_sample, ((0, 0), (0, tpad - ts), (0, 0)))
    ps = jnp.pad(p_sample, ((0, 0), (0, 0), (0, tpad - ts), (0, 0)))
    sample_init = (state_gdn_conv, state_gdn, state_s5_re, state_s5_im,
                   state_mlstm_C, state_mlstm_n, state_mlstm_m)
    res_s = _trunk(xs, ps, sample_init, layers, fnorm,
                   L=tpad, tv=ts, Ls=tpad, tm=512, tm_ffn=1024, bb=16, bbs=8)
    y_sample = res_s[0][:, :ts]
    return (res_p[0], y_sample) + res_p[1:] + res_s[1:]
```

```python
import functools
import math

import jax
import jax.numpy as jnp
from jax import lax
from jax.experimental import pallas as pl
from jax.experimental.pallas import tpu as pltpu

F32 = jnp.float32
BF16 = jnp.bfloat16

D_MODEL = 1024
DEPTH = 2
GDN_HEADS = 6
GDN_DK = 64
GDN_DV = 64
GDN_QK = GDN_HEADS * GDN_DK
GDN_VW = GDN_HEADS * GDN_DV
GDN_CONV_CH = 2 * GDN_QK + GDN_VW
CONV_W = 4
S5_GROUPS = 16
S5_GROUP_CH = 16
S5_CH = S5_GROUPS * S5_GROUP_CH
S5_STATE = 64
S5_N = S5_GROUPS * S5_STATE
ML_HEADS = 6
ML_DH = 64
ML_W = ML_HEADS * ML_DH
N_GROUPS = 4
EXPERTS_PER_GROUP = 4
N_EXPERTS = N_GROUPS * EXPERTS_PER_GROUP
D_EXPERT = 256
PLE_DIM = 256
EPS = 1e-6

LANES = 128
SUBLANES = 8
NEG = -1e30
VMEM_LIMIT = 56 * 1024 * 1024

def _cparams(*sem):
    return pltpu.CompilerParams(dimension_semantics=sem, vmem_limit_bytes=VMEM_LIMIT)


def _dot(a, b):
    return jnp.dot(a.astype(BF16), b.astype(BF16), preferred_element_type=F32)


def _dot_nt(a, b):
    return lax.dot_general(a.astype(BF16), b.astype(BF16), (((1,), (1,)), ((), ())),
                           preferred_element_type=F32)


def _dot_tn(a, b):
    return lax.dot_general(a.astype(BF16), b.astype(BF16), (((0,), (0,)), ((), ())),
                           preferred_element_type=F32)


def _split_bf16(a):
    hi = a.astype(BF16)
    lo = (a - hi.astype(F32)).astype(BF16)
    return hi, lo


def _rms(x, gain):
    return x * lax.rsqrt(jnp.mean(x * x, axis=-1, keepdims=True) + EPS) * gain


def _softplus(x):
    return jnp.maximum(x, 0.0) + jnp.log(1.0 + jnp.exp(-jnp.abs(x)))


def _sigmoid(x):
    return 1.0 / (1.0 + jnp.exp(-x))


def _silu(x):
    return x * _sigmoid(x)


def _interleave(gens):
    live = list(gens)
    while live:
        still = []
        for g in live:
            try:
                next(g)
                still.append(g)
            except StopIteration:
                pass
        live = still


def _unit_lower_inverse(n_mat, size, top=None):
    top = size if top is None else top
    r = lax.broadcasted_iota(jnp.int32, (size, size), 0)
    c = lax.broadcasted_iota(jnp.int32, (size, size), 1)
    base = min(16, top)
    same = jnp.bitwise_xor(r, c) < base
    nd = jnp.where(same, n_mat, 0.0)
    eye = jnp.where(r == c, 1.0, 0.0).astype(F32)
    t = eye - nd
    p = 1
    if 2 * p < base:
        x = _dot(nd, nd)
        yield
    while 2 * p < base:
        t_next = t + _dot(t, x)
        if 4 * p < base:
            x = _dot(x, x)
        t = t_next
        yield
        p *= 2
    blk = base
    while blk < top:
        pair = jnp.bitwise_xor(r, c)
        off = jnp.where((pair < 2 * blk) & (pair >= blk), n_mat, 0.0)
        ot = _dot(off, t)
        yield
        t = t - _dot(t, ot)
        yield
        blk *= 2
    return t


MIX_W = 4 * ML_W
GATE_W = 2 * ML_W


def _norm_inproj_kernel(x_ref, g_ref, w_ref, e_ref, og_ref, os_ref, om_ref):
    u = _rms(x_ref[...], g_ref[...]).astype(BF16)
    dot = functools.partial(jnp.dot, preferred_element_type=F32)
    og_ref[:, 0:MIX_W] = dot(u, w_ref[:, 0:MIX_W])
    os_ref[...] = dot(u, w_ref[:, MIX_W:MIX_W + S5_CH])
    om_ref[:, 0:MIX_W] = dot(u, w_ref[:, MIX_W + S5_CH:2 * MIX_W + S5_CH])
    small = dot(u, w_ref[:, 2 * MIX_W + S5_CH:])
    rep = None
    for piece in _split3(small):
        t = dot(piece, e_ref[...])
        rep = t if rep is None else rep + t
    og_ref[:, MIX_W:] = rep[:, 0:GATE_W]
    om_ref[:, MIX_W:] = rep[:, GATE_W:]


def _norm_inproj(x, gain, w, e_rep, tm):
    m = x.shape[0]
    fix = lambda i: (0, 0)
    widths = (MIX_SLAB_W, S5_CH, MIX_SLAB_W)
    return pl.pallas_call(
        _norm_inproj_kernel,
        grid=(m // tm,),
        in_specs=[pl.BlockSpec((tm, D_MODEL), lambda i: (i, 0)),
                  pl.BlockSpec((1, D_MODEL), fix),
                  pl.BlockSpec((D_MODEL, w.shape[1]), fix),
                  pl.BlockSpec((LANES, 2 * GATE_W), fix)],
        out_specs=[pl.BlockSpec((tm, wd), lambda i: (i, 0)) for wd in widths],
        out_shape=[jax.ShapeDtypeStruct((m, wd), F32) for wd in widths],
        compiler_params=_cparams("parallel"),
        name="norm_inproj",
    )(x, gain, w, e_rep)


def _route(f, wr, br):
    fh, fl = _split_bf16(f)
    wh, wl = _split_bf16(wr)
    d = functools.partial(jnp.dot, preferred_element_type=F32)
    logits = d(fh, wh) + (d(fh, wl) + d(fl, wh)) + br
    lane = lax.broadcasted_iota(jnp.int32, logits.shape, 1)
    is_g = lane < N_GROUPS
    gl = jnp.where(is_g, logits, NEG)
    gmax = jnp.max(gl, axis=-1, keepdims=True)
    ge = jnp.where(is_g, jnp.exp(gl - gmax), 0.0)
    p_grp = ge / jnp.sum(ge, axis=-1, keepdims=True)
    g_prob = jnp.max(p_grp, axis=-1, keepdims=True)
    g_idx = jnp.min(jnp.where(is_g & (gl == gmax), lane, LANES), axis=-1, keepdims=True)
    e_lane = lane - N_GROUPS
    is_e = (e_lane >= 0) & (e_lane < N_EXPERTS) & (jnp.right_shift(e_lane, 2) == g_idx)
    le = jnp.where(is_e, logits, NEG)
    m1 = jnp.max(le, axis=-1, keepdims=True)
    i1 = jnp.min(jnp.where(is_e & (le == m1), lane, LANES), axis=-1, keepdims=True)
    is_e2 = is_e & (lane != i1)
    le2 = jnp.where(is_e2, logits, NEG)
    m2 = jnp.max(le2, axis=-1, keepdims=True)
    i2 = jnp.min(jnp.where(is_e2 & (le2 == m2), lane, LANES), axis=-1, keepdims=True)
    e2 = jnp.exp(m2 - m1)
    w1 = g_prob / (1.0 + e2)
    w2 = g_prob * e2 / (1.0 + e2)
    return fh, jnp.where(lane == i1, w1, 0.0) + jnp.where(lane == i2, w2, 0.0)


def _ffn_kernel(og_ref, os_ref, om_ref, h_ref, p_ref, wo_ref, nf_ref, wr_ref, br_ref,
                wg_ref, wu_ref, wd_ref, np_ref, wpg_ref, wpp_ref, fn_ref,
                out_ref, f_ref, gates_ref, *, final):
    acc_ref = out_ref
    gi = pl.program_id(1)

    @pl.when(gi == 0)
    def _():
        mix = jnp.concatenate([og_ref[...], os_ref[...], om_ref[...]], axis=-1)
        h1 = h_ref[...] + jnp.dot(mix, wo_ref[...], preferred_element_type=F32)
        acc_ref[...] = h1
        fh, gates = _route(_rms(h1, nf_ref[...]), wr_ref[...], br_ref[...])
        f_ref[...] = fh
        gates_ref[...] = gates

    x = f_ref[...]
    gates = gates_ref[...]
    lane = lax.broadcasted_iota(jnp.int32, gates.shape, 1)
    base = N_GROUPS + EXPERTS_PER_GROUP * gi
    acc = None
    for j in range(EXPERTS_PER_GROUP):
        gcol = jnp.sum(jnp.where(lane == base + j, gates, 0.0), axis=-1, keepdims=True)
        hg = jnp.dot(x, wg_ref[j], preferred_element_type=F32)
        hu = jnp.dot(x, wu_ref[j], preferred_element_type=F32)
        hidden = (_silu(hg) * hu * gcol).astype(BF16)
        t = jnp.dot(hidden, wd_ref[j], preferred_element_type=F32)
        acc = t if acc is None else acc + t
    acc_ref[...] += acc

    @pl.when(gi == N_GROUPS - 1)
    def _():
        h = acc_ref[...]
        gate = _sigmoid(jnp.dot(_rms(h, np_ref[...]).astype(BF16), wpg_ref[...],
                                preferred_element_type=F32))
        proj = jnp.dot(p_ref[0].astype(BF16), wpp_ref[...], preferred_element_type=F32)
        h = h + proj * gate
        if final:
            h = _rms(h, fn_ref[...])
        out_ref[...] = h


def _ffn(og, os_, om, h, p_all, layer, W, fn, tm, final):
    m = h.shape[0]
    row = lambda i, g: (i, 0)
    wsel = lambda i, g: (g, 0, 0)
    e = EXPERTS_PER_GROUP

    def fix(shape):
        return pl.BlockSpec(shape, lambda i, g: (0, 0), pipeline_mode=pl.Buffered(1))

    return pl.pallas_call(
        functools.partial(_ffn_kernel, final=final),
        grid=(m // tm, N_GROUPS),
        in_specs=[pl.BlockSpec((tm, GDN_VW), row), pl.BlockSpec((tm, S5_CH), row),
                  pl.BlockSpec((tm, ML_W), row), pl.BlockSpec((tm, D_MODEL), row),
                  pl.BlockSpec((1, tm, PLE_DIM), lambda i, g: (layer, i, 0)),
                  fix((D_MODEL, D_MODEL)), fix((1, D_MODEL)),
                  fix((D_MODEL, LANES)), fix((1, LANES)),
                  pl.BlockSpec((e, D_MODEL, D_EXPERT), wsel),
                  pl.BlockSpec((e, D_MODEL, D_EXPERT), wsel),
                  pl.BlockSpec((e, D_EXPERT, D_MODEL), wsel),
                  fix((1, D_MODEL)), fix((D_MODEL, D_MODEL)),
                  fix((PLE_DIM, D_MODEL)), fix((1, D_MODEL))],
        out_specs=pl.BlockSpec((tm, D_MODEL), row),
        out_shape=jax.ShapeDtypeStruct((m, D_MODEL), F32),
        scratch_shapes=[pltpu.VMEM((tm, D_MODEL), BF16), pltpu.VMEM((tm, LANES), F32)],
        compiler_params=_cparams("parallel", "arbitrary"),
        name="ffn",
    )(og, os_, om, h, p_all, W["w_out"], W["norm_ffn"], W["wr"], W["br"],
      W["wg"], W["wu"], W["wd"], W["norm_ple"], W["w_ple_gate"], W["w_ple_proj"], fn)


MIX_SLAB_W = 6 * ML_W
PAIRS = ML_HEADS // 2
HALF = LANES // 2


def _scan_rows(x, size, op, fill):
    row = lax.broadcasted_iota(jnp.int32, x.shape, 0)
    d = 1
    while d < size:
        x = op(x, jnp.where(row >= d, pltpu.roll(x, d, axis=0), fill))
        d *= 2
    return x


def _split3(a):
    hi = a.astype(BF16)
    r1 = a - hi.astype(F32)
    mid = r1.astype(BF16)
    lo = (r1 - mid.astype(F32)).astype(BF16)
    return hi, mid, lo


def _row_form(x_s, L):
    rr = lax.broadcasted_iota(jnp.int32, x_s.shape, 0)
    cc = lax.broadcasted_iota(jnp.int32, x_s.shape, 1)
    dg = jnp.where(rr == jnp.bitwise_and(cc, L - 1), x_s, 0.0)
    return jnp.sum(dg, axis=0, keepdims=True)


def _score_cols(x, p, L):
    if 2 * L == LANES:
        return x[:, LANES * p:LANES * (p + 1)]
    return jnp.concatenate([x[:, LANES * p:LANES * p + L],
                            x[:, LANES * p + HALF:LANES * p + HALF + L]], axis=-1)


def _bd_stack(x2):
    lo = lax.broadcasted_iota(jnp.int32, x2.shape, 1) < HALF
    return jnp.concatenate([jnp.where(lo, x2, 0.0), jnp.where(lo, 0.0, x2)], axis=0)


def _bd_ones(rows_per_half):
    shape = (2 * rows_per_half, LANES)
    r = lax.broadcasted_iota(jnp.int32, shape, 0) < rows_per_half
    c = lax.broadcasted_iota(jnp.int32, shape, 1) < HALF
    return jnp.where(r == c, 1.0, 0.0).astype(F32)


def _layer_state_spec(s_all, layer, bb, in_idx, out_idx):
    depth = s_all.shape[0]
    tail = s_all.shape[2:]
    if layer == 0 and depth > 1:
        return pl.BlockSpec((depth, bb) + tail, lambda b, c: (0, b, 0, 0, 0)), 0, {}
    return (pl.BlockSpec((1, bb) + tail, lambda b, c: (layer, b, 0, 0, 0)), 0,
            {in_idx: out_idx})


def _half_sums(x2, ones_bd, two_pieces=False):
    if not two_pieces:
        return _dot(x2, ones_bd)
    hi, lo = _split_bf16(x2)
    ob = ones_bd.astype(BF16)
    return (jnp.dot(hi, ob, preferred_element_type=F32)
            + jnp.dot(lo, ob, preferred_element_type=F32))


def _mlstm_kernel(a_ref, c0_ref, n0_ref, m0_ref, bi_ref, bf_ref, ng_ref,
                  o_ref, cn_ref, nn_ref, mn_ref, st_ref, m_ref, *, L, tv, bb, lsel):
    cidx = pl.program_id(1)
    nc = pl.num_programs(1)
    ones_ch = _bd_ones(HALF)
    r128 = lax.broadcasted_iota(jnp.int32, (LANES, LANES), 0)
    c128 = lax.broadcasted_iota(jnp.int32, (LANES, LANES), 1)
    diag128 = r128 == c128

    @pl.when(cidx == 0)
    def _():
        st_ref[...] = jnp.zeros(st_ref.shape, F32)
        m_ref[...] = m0_ref[...]
        ob = ones_ch.astype(BF16)
        for i in range(bb):
            for p in range(PAIRS):
                st_ref[i, p, 0:HALF, 0:HALF] = c0_ref[lsel, i, 2 * p]
                st_ref[i, p, HALF:, HALF:LANES] = c0_ref[lsel, i, 2 * p + 1]
                acc = None
                for piece in _split3(n0_ref[i, p]):
                    dg = jnp.where(diag128, piece.astype(F32), 0.0).astype(BF16)
                    t = jnp.dot(dg, ob, preferred_element_type=F32)
                    acc = t if acc is None else acc + t
                st_ref[i, p, :, LANES:] = acc

    W = 2 * L
    rr = lax.broadcasted_iota(jnp.int32, (L, W), 0)
    cc = lax.broadcasted_iota(jnp.int32, (L, W), 1)
    incl = rr >= jnp.bitwise_and(cc, L - 1)
    ones_keys = _bd_ones(L)
    bd256 = jnp.concatenate([ones_ch, ones_ch], axis=-1) > 0.5
    gain2 = ng_ref[...]
    gain2 = jnp.concatenate([gain2, gain2], axis=-1)
    ones_l = jnp.ones((L, LANES), F32)
    outs = [[None] * PAIRS for _ in range(bb)]
    st_news = [[None] * PAIRS for _ in range(bb)]
    m_news = [None] * bb
    rows = []
    for i in range(bb):
        li = a_ref[i, :, 4 * ML_W:5 * ML_W] + bi_ref[...]
        lf = -_softplus(-(a_ref[i, :, 5 * ML_W:6 * ML_W] + bf_ref[...]))
        if tv < L:
            valid = lax.broadcasted_iota(jnp.int32, li.shape, 0) < tv
            li = jnp.where(valid, li, NEG)
            lf = jnp.where(valid, lf, 0.0)
        bcum = _scan_rows(lf, L, jnp.add, 0.0)
        a = li - bcum
        m0 = m_ref[i]
        m_t = bcum + jnp.maximum(m0, _scan_rows(a, L, jnp.maximum, NEG))
        e_inter = jnp.exp(bcum + m0 - m_t)
        m_new = m_t[L - 1:L, :]
        b_last = bcum[L - 1:L, :]
        e_c = jnp.exp(b_last + m0 - m_new)
        kw = a_ref[i, :, ML_W:2 * ML_W] * (ML_DH ** -0.5) * jnp.exp(b_last + a - m_new)
        m_news[i] = m_new
        rows.append((a, bcum, m_t, e_inter, e_c, kw))

    def pair(i, p):
        a, bcum, m_t, e_inter, e_c, kw = rows[i]
        sl = slice(LANES * p, LANES * (p + 1))
        q2 = a_ref[i, :, sl]
        k2 = a_ref[i, :, ML_W + LANES * p:ML_W + LANES * (p + 1)] * (ML_DH ** -0.5)
        v2 = a_ref[i, :, 2 * ML_W + LANES * p:2 * ML_W + LANES * (p + 1)]
        og2 = a_ref[i, :, 3 * ML_W + LANES * p:3 * ML_W + LANES * (p + 1)]
        st = st_ref[i, p]
        arow = _row_form(_score_cols(a, p, L), L)
        qk = _dot_nt(q2, _bd_stack(k2))
        qcn = _dot(q2, st)
        upd = _dot_tn(kw[:, sl], jnp.concatenate([v2, ones_l], axis=-1))
        ec2 = e_c[:, sl]
        st_news[i][p] = st * jnp.concatenate([ec2, ec2], axis=-1) + jnp.where(bd256, upd, 0.0)
        yield
        w_intra = jnp.where(incl, jnp.exp(_score_cols(bcum, p, L) + arow - _score_cols(m_t, p, L)), 0.0)
        s2 = qk * w_intra
        nd = _dot(s2, jnp.concatenate([_bd_stack(v2), ones_keys], axis=-1))
        yield
        e2 = e_inter[:, sl]
        num = e2 * qcn[:, 0:LANES] + nd[:, 0:LANES]
        den = e2 * qcn[:, LANES:] + nd[:, LANES:]
        hh = num / jnp.maximum(jnp.abs(den), jnp.exp(-m_t[:, sl]))
        ss = _half_sums(hh * hh, ones_ch)
        yield
        outs[i][p] = hh * lax.rsqrt(ss * (1.0 / ML_DH) + EPS) * gain2 * _sigmoid(og2)

    _interleave([pair(i, p) for i in range(bb) for p in range(PAIRS)])
    for i in range(bb):
        for p in range(PAIRS):
            st_ref[i, p] = st_news[i][p]
        m_ref[i] = m_news[i]
        o_ref[i] = jnp.concatenate(outs[i], axis=-1).astype(o_ref.dtype)

    @pl.when(cidx == nc - 1)
    def _():
        for k in range(cn_ref.shape[0]):
            if k != lsel:
                cn_ref[k] = c0_ref[k]
        for i in range(bb):
            for p in range(PAIRS):
                st = st_news[i][p]
                cn_ref[lsel, i, 2 * p] = st[0:HALF, 0:HALF]
                cn_ref[lsel, i, 2 * p + 1] = st[HALF:, HALF:LANES]
                dg = jnp.where(diag128, st[:, LANES:], 0.0)
                nn_ref[i, p] = jnp.sum(dg, axis=0, keepdims=True)
        mn_ref[...] = m_ref[...]


def _mlstm(slab, c_all, layer, n0, m0, bi, bf, ng, L, tv, bb):
    bsz, t, _ = slab.shape
    nc = t // L
    fix2 = lambda b, c: (0, 0)
    c_spec, lsel, aliases = _layer_state_spec(c_all, layer, bb, 1, 1)
    n_spec = pl.BlockSpec((bb, PAIRS, 1, LANES), lambda b, c: (b, 0, 0, 0))
    m_spec = pl.BlockSpec((bb, 1, ML_W), lambda b, c: (b, 0, 0))
    return pl.pallas_call(
        functools.partial(_mlstm_kernel, L=L, tv=tv, bb=bb, lsel=lsel),
        grid=(bsz // bb, nc),
        input_output_aliases=aliases,
        in_specs=[pl.BlockSpec((bb, L, MIX_SLAB_W), lambda b, c: (b, c, 0)),
                  c_spec, n_spec, m_spec,
                  pl.BlockSpec((1, ML_W), fix2), pl.BlockSpec((1, ML_W), fix2),
                  pl.BlockSpec((1, ML_DH), fix2)],
        out_specs=[pl.BlockSpec((bb, L, ML_W), lambda b, c: (b, c, 0)), c_spec, n_spec, m_spec],
        out_shape=[jax.ShapeDtypeStruct((bsz, t, ML_W), BF16),
                   jax.ShapeDtypeStruct(c_all.shape, F32),
                   jax.ShapeDtypeStruct((bsz, PAIRS, 1, LANES), F32),
                   jax.ShapeDtypeStruct((bsz, 1, ML_W), F32)],
        scratch_shapes=[pltpu.VMEM((bb, PAIRS, LANES, 2 * LANES), F32),
                        pltpu.VMEM((bb, 1, ML_W), F32)],
        compiler_params=_cparams("parallel", "arbitrary"),
        name="mlstm",
    )(slab, c_all, n0, m0, bi, bf, ng)


def _gdn_kernel(a_ref, conv0_ref, s0_ref, cw_ref, alog_ref, dtb_ref, ng_ref,
                o_ref, convn_ref, sn_ref, ext_ref, s_ref, *, L, tv, bb, lsel):
    c = pl.program_id(1)
    nc = pl.num_programs(1)

    @pl.when(c == 0)
    def _():
        ext_ref[:, 0:SUBLANES, :] = conv0_ref[...]
        s_ref[...] = jnp.zeros(s_ref.shape, F32)
        for i in range(bb):
            for p in range(PAIRS):
                s_ref[i, p, 0:HALF, 0:HALF] = s0_ref[lsel, i, 2 * p]
                s_ref[i, p, HALF:, HALF:] = s0_ref[lsel, i, 2 * p + 1]

    W = 2 * L
    rr = lax.broadcasted_iota(jnp.int32, (L, W), 0)
    cc = lax.broadcasted_iota(jnp.int32, (L, W), 1)
    key = jnp.bitwise_and(cc, L - 1)
    incl = rr >= key
    strict = rr > key
    first_half = cc < L
    lo = lax.broadcasted_iota(jnp.int32, (L, LANES), 1) < HALF
    ones_ch = _bd_ones(HALF)
    bd128 = ones_ch > 0.5
    cw = cw_ref[...]
    gain2 = ng_ref[...]
    gain2 = jnp.concatenate([gain2, gain2], axis=-1)
    outs = [[None] * PAIRS for _ in range(bb)]
    s_news = [[None] * PAIRS for _ in range(bb)]
    carries = [None] * bb
    rows = []
    for i in range(bb):
        raw = a_ref[i, :, 0:GDN_CONV_CH]
        ext_ref[i, SUBLANES:SUBLANES + L, :] = raw
        ext = ext_ref[i]
        acc = raw * cw[CONV_W - 1:CONV_W, :]
        for j in range(CONV_W - 1):
            sh = pltpu.roll(ext, CONV_W - 1 - j, axis=0)[SUBLANES:SUBLANES + L]
            acc = acc + sh * cw[j:j + 1, :]
        qkv = _silu(acc)
        carries[i] = pltpu.roll(ext, (L + SUBLANES - tv) % (L + SUBLANES), axis=0)[0:SUBLANES]

        beta = _sigmoid(a_ref[i, :, 4 * GDN_VW:5 * GDN_VW])
        g = -jnp.exp(alog_ref[...]) * _softplus(a_ref[i, :, 5 * GDN_VW:6 * GDN_VW] + dtb_ref[...])
        if tv < L:
            valid = lax.broadcasted_iota(jnp.int32, g.shape, 0) < tv
            beta = jnp.where(valid, beta, 0.0)
            g = jnp.where(valid, g, 0.0)
        gcum = _scan_rows(g, L, jnp.add, 0.0)
        g_last = gcum[L - 1:L, :]
        rows.append((qkv, beta, gcum, jnp.exp(gcum), jnp.exp(g_last - gcum), jnp.exp(g_last)))
    for i in range(bb):
        ext_ref[i, 0:SUBLANES, :] = carries[i]

    def pair(i, p):
        qkv, beta, gcum, eg, kdec, sdec = rows[i]
        sl = slice(LANES * p, LANES * (p + 1))
        q2 = qkv[:, LANES * p:LANES * (p + 1)]
        k2 = qkv[:, GDN_QK + LANES * p:GDN_QK + LANES * (p + 1)]
        v2 = qkv[:, 2 * GDN_QK + LANES * p:2 * GDN_QK + LANES * (p + 1)]
        z2 = a_ref[i, :, GDN_CONV_CH + LANES * p:GDN_CONV_CH + LANES * (p + 1)]
        s2 = s_ref[i, p]
        ssq = _half_sums(q2 * q2, ones_ch, L < HALF)
        ssk = _half_sums(k2 * k2, ones_ch, L < HALF)
        grow = _row_form(_score_cols(gcum, p, L), L)
        yield
        q2 = q2 * lax.rsqrt(ssq + EPS) * (GDN_DK ** -0.5)
        k2 = k2 * lax.rsqrt(ssk + EPS)
        beta2 = beta[:, sl]
        eg2 = eg[:, sl]
        kb2 = k2 * beta2
        kq = _dot_nt(jnp.concatenate([_bd_stack(kb2), q2], axis=0), _bd_stack(k2))
        qs = _dot(q2 * eg2, s2)
        yield
        gam = jnp.where(incl, jnp.exp(jnp.minimum(_score_cols(gcum, p, L) - grow, 0.0)), 0.0)
        gam_s = jnp.where(strict, gam, 0.0)
        gam_bd = jnp.concatenate([jnp.where(first_half, gam_s, 0.0),
                                  jnp.where(first_half, 0.0, gam_s)], axis=0)
        t_inv = yield from _unit_lower_inverse(kq[0:W] * gam_bd, W, L)
        rhs = jnp.concatenate([v2 * beta2, kb2 * eg2], axis=-1)
        sol = _dot(t_inv, jnp.concatenate([rhs, rhs], axis=0))
        yield
        u2 = jnp.where(lo, sol[0:L, 0:LANES], sol[L:W, 0:LANES])
        w2 = jnp.where(lo, sol[0:L, LANES:], sol[L:W, LANES:])
        v_new = u2 - _dot(w2, s2)
        yield
        o2 = qs + _dot(kq[W:] * gam, _bd_stack(v_new))
        upd = _dot_tn(k2 * kdec[:, sl], v_new)
        s_news[i][p] = s2 * sdec[:, sl] + jnp.where(bd128, upd, 0.0)
        yield
        ss = _half_sums(o2 * o2, ones_ch, L < HALF)
        yield
        outs[i][p] = o2 * lax.rsqrt(ss * (1.0 / GDN_DV) + EPS) * gain2 * _silu(z2)

    _interleave([pair(i, p) for i in range(bb) for p in range(PAIRS)])
    for i in range(bb):
        for p in range(PAIRS):
            s_ref[i, p] = s_news[i][p]
        o_ref[i] = jnp.concatenate(outs[i], axis=-1).astype(o_ref.dtype)

    @pl.when(c == nc - 1)
    def _():
        for k in range(sn_ref.shape[0]):
            if k != lsel:
                sn_ref[k] = s0_ref[k]
        for i in range(bb):
            convn_ref[i] = carries[i]
            for p in range(PAIRS):
                sn_ref[lsel, i, 2 * p] = s_news[i][p][0:HALF, 0:HALF]
                sn_ref[lsel, i, 2 * p + 1] = s_news[i][p][HALF:, HALF:]


def _gdn(slab, conv0, s_all, layer, cw, alog, dtb, ng, L, tv, bb):
    bsz, t, _ = slab.shape
    nc = t // L
    fix2 = lambda b, c: (0, 0)
    s_spec, lsel, aliases = _layer_state_spec(s_all, layer, bb, 2, 2)
    cv_spec = pl.BlockSpec((bb, SUBLANES, GDN_CONV_CH), lambda b, c: (b, 0, 0))
    return pl.pallas_call(
        functools.partial(_gdn_kernel, L=L, tv=tv, bb=bb, lsel=lsel),
        grid=(bsz // bb, nc),
        input_output_aliases=aliases,
        in_specs=[pl.BlockSpec((bb, L, MIX_SLAB_W), lambda b, c: (b, c, 0)),
                  cv_spec, s_spec,
                  pl.BlockSpec((CONV_W, GDN_CONV_CH), fix2),
                  pl.BlockSpec((1, GDN_VW), fix2), pl.BlockSpec((1, GDN_VW), fix2),
                  pl.BlockSpec((1, GDN_DV), fix2)],
        out_specs=[pl.BlockSpec((bb, L, GDN_VW), lambda b, c: (b, c, 0)), cv_spec, s_spec],
        out_shape=[jax.ShapeDtypeStruct((bsz, t, GDN_VW), BF16),
                   jax.ShapeDtypeStruct((bsz, SUBLANES, GDN_CONV_CH), F32),
                   jax.ShapeDtypeStruct(s_all.shape, F32)],
        scratch_shapes=[pltpu.VMEM((bb, L + SUBLANES, GDN_CONV_CH), F32),
                        pltpu.VMEM((bb, PAIRS, LANES, LANES), F32)],
        compiler_params=_cparams("parallel", "arbitrary"),
        name="gdn",
    )(slab, conv0, s_all, cw, alog, dtb, ng)


def _cmul(ar, ai, br, bi):
    return ar * br - ai * bi, ar * bi + ai * br


def _s5_kernel(u_ref, h0_ref, lb_ref, bw_ref, cw_ref, dsk_ref, wglu_ref,
               o_ref, hn_ref, car_ref, h_ref, *, L, bb):
    c = pl.program_id(1)
    nc = pl.num_programs(1)

    @pl.when(c == 0)
    def _():
        car_ref[...] = h0_ref[...]

    rows = bb * L
    u = u_ref[...].reshape(rows, S5_CH)
    bu = jnp.dot(u.astype(BF16), bw_ref[...], preferred_element_type=F32)
    tiles = rows // SUBLANES
    x_re = bu[:, 0:S5_N].reshape(tiles, SUBLANES, S5_N)
    x_im = bu[:, S5_N:].reshape(tiles, SUBLANES, S5_N)
    p1 = (lb_ref[0:1, :], lb_ref[1:2, :])
    p2 = _cmul(*p1, *p1)
    p4 = _cmul(*p2, *p2)
    p8 = _cmul(*p4, *p4)
    sub = lax.broadcasted_iota(jnp.int32, (SUBLANES, S5_N), 0)
    for d, (pr, pi) in ((1, p1), (2, p2), (4, p4)):
        pr = jnp.where(sub >= d, pr, 0.0)
        pi = jnp.where(sub >= d, pi, 0.0)
        s_re = pltpu.roll(x_re, d, axis=1)
        s_im = pltpu.roll(x_im, d, axis=1)
        x_re, x_im = x_re + (pr * s_re - pi * s_im), x_im + (pr * s_im + pi * s_re)
    x_re = x_re.reshape(rows, S5_N)
    x_im = x_im.reshape(rows, S5_N)
    k = sub + 1
    pw_re = jnp.ones((SUBLANES, S5_N), F32)
    pw_im = jnp.zeros((SUBLANES, S5_N), F32)
    for bit, (pr, pi) in ((1, p1), (2, p2), (4, p4), (8, p8)):
        m_re, m_im = _cmul(pw_re, pw_im, pr, pi)
        on = jnp.bitwise_and(k, bit) != 0
        pw_re = jnp.where(on, m_re, pw_re)
        pw_im = jnp.where(on, m_im, pw_im)
    for i in range(bb):
        car_re = car_ref[i, 0]
        car_im = car_ref[i, 1]
        for t in range(L // SUBLANES):
            r0 = i * L + t * SUBLANES
            a_re, a_im = _cmul(pw_re, pw_im, car_re, car_im)
            t_re = x_re[r0:r0 + SUBLANES, :] + a_re
            t_im = x_im[r0:r0 + SUBLANES, :] + a_im
            h_ref[r0:r0 + SUBLANES, 0:S5_N] = t_re
            h_ref[r0:r0 + SUBLANES, S5_N:] = t_im
            car_re = jnp.broadcast_to(t_re[SUBLANES - 1:SUBLANES, :], (SUBLANES, S5_N))
            car_im = jnp.broadcast_to(t_im[SUBLANES - 1:SUBLANES, :], (SUBLANES, S5_N))
        car_ref[i, 0] = car_re
        car_ref[i, 1] = car_im
        hn_ref[i, 0] = t_re
        hn_ref[i, 1] = t_im
    y = jnp.dot(h_ref[...].astype(BF16), cw_ref[...], preferred_element_type=F32) + dsk_ref[...] * u
    zg = jax.nn.gelu(y)
    out = zg * _sigmoid(jnp.dot(zg.astype(BF16), wglu_ref[...], preferred_element_type=F32))
    o_ref[...] = out.reshape(bb, L, S5_CH).astype(o_ref.dtype)


def _s5(u, h0, lb, bw, cw, dsk, wglu, L, bb):
    bsz, t, _ = u.shape
    nc = t // L
    fix2 = lambda b, c: (0, 0)
    h_spec = pl.BlockSpec((bb, 2, SUBLANES, S5_N), lambda b, c: (b, 0, 0, 0))
    return pl.pallas_call(
        functools.partial(_s5_kernel, L=L, bb=bb),
        grid=(bsz // bb, nc),
        in_specs=[pl.BlockSpec((bb, L, S5_CH), lambda b, c: (b, c, 0)),
                  h_spec,
                  pl.BlockSpec((SUBLANES, S5_N), fix2),
                  pl.BlockSpec((S5_CH, 2 * S5_N), fix2),
                  pl.BlockSpec((2 * S5_N, S5_CH), fix2),
                  pl.BlockSpec((1, S5_CH), fix2),
                  pl.BlockSpec((S5_CH, S5_CH), fix2)],
        out_specs=[pl.BlockSpec((bb, L, S5_CH), lambda b, c: (b, c, 0)), h_spec],
        out_shape=[jax.ShapeDtypeStruct((bsz, t, S5_CH), BF16),
                   jax.ShapeDtypeStruct((bsz, 2, SUBLANES, S5_N), F32)],
        scratch_shapes=[pltpu.VMEM((bb, 2, SUBLANES, S5_N), F32),
                        pltpu.VMEM((bb * L, 2 * S5_N), F32)],
        compiler_params=_cparams("parallel", "arbitrary"),
        name="s5",
    )(u, h0, lb, bw, cw, dsk, wglu)


def _prep_layer(l, P):
    w_in = P["w_in"][l]
    sizes = (GDN_CONV_CH, GDN_VW, GDN_HEADS, GDN_HEADS, S5_CH, 3 * ML_W, ML_HEADS, ML_HEADS, ML_W)
    offs = [0]
    for s in sizes:
        offs.append(offs[-1] + s)
    g_qkv, g_z, g_b, g_a, s_u, m_qkv, m_i, m_f, m_o = [w_in[:, offs[i]:offs[i + 1]] for i in range(9)]
    zpad = jnp.zeros((D_MODEL, LANES - 4 * GDN_HEADS), F32)
    w_cat = jnp.concatenate([g_qkv, g_z, s_u, m_qkv, m_o, g_b, g_a, m_i, m_f, zpad],
                            axis=1).astype(BF16)

    lr = P["s5_lam_re"][l].astype(F32)
    li = P["s5_lam_im"][l].astype(F32)
    dt = jnp.exp(P["s5_log_dt"][l].astype(F32))[:, None]
    mag = jnp.exp(lr * dt)
    lb_re = mag * jnp.cos(li * dt)
    lb_im = mag * jnp.sin(li * dt)
    den = lr * lr + li * li
    c_re = ((lb_re - 1.0) * lr + lb_im * li) / den
    c_im = (lb_im * lr - (lb_re - 1.0) * li) / den
    b_r = P["s5_B_re"][l].astype(F32)
    b_i = P["s5_B_im"][l].astype(F32)
    bb_re = c_re[..., None] * b_r - c_im[..., None] * b_i
    bb_im = c_re[..., None] * b_i + c_im[..., None] * b_r
    eye_g = jnp.eye(S5_GROUPS, dtype=F32)
    bd = lambda m: jnp.einsum("gph,gk->ghkp", m, eye_g).reshape(S5_CH, S5_N)
    bw = jnp.concatenate([bd(bb_re), bd(bb_im)], axis=1).astype(BF16)
    cd = lambda m: jnp.einsum("ghp,gk->gpkh", m, eye_g).reshape(S5_N, S5_CH)
    cw = jnp.concatenate([cd(P["s5_C_re"][l].astype(F32)),
                          -cd(P["s5_C_im"][l].astype(F32))], axis=0).astype(BF16)
    lb = jnp.zeros((SUBLANES, S5_N), F32).at[0].set(lb_re.reshape(-1)).at[1].set(lb_im.reshape(-1))

    wr = jnp.zeros((D_MODEL, LANES), F32)
    wr = wr.at[:, 0:N_GROUPS].set(P["w_router_group"][l])
    wr = wr.at[:, N_GROUPS:N_GROUPS + N_EXPERTS].set(P["w_router_expert"][l])
    br = jnp.zeros((1, LANES), F32)
    br = br.at[0, 0:N_GROUPS].set(P["b_router_group"][l])
    br = br.at[0, N_GROUPS:N_GROUPS + N_EXPERTS].set(P["b_router_expert"][l])

    rep_row = lambda v: jnp.repeat(v.astype(F32), ML_DH).reshape(1, ML_W)
    return dict(
        norm_mix=P["norm_mix"][l].reshape(1, D_MODEL).astype(F32),
        w_cat=w_cat,
        w_out=P["w_out"][l].astype(BF16),
        conv_w=P["gdn_conv_w"][l].astype(F32),
        alog=rep_row(P["gdn_A_log"][l]), dtb=rep_row(P["gdn_dt_bias"][l]),
        gdn_norm=P["gdn_norm"][l].reshape(1, GDN_DV).astype(F32),
        lb=lb, bw=bw, cw=cw,
        s5_d=P["s5_D"][l].reshape(1, S5_CH).astype(F32),
        w_glu=P["s5_w_glu"][l].astype(BF16),
        ml_bi=rep_row(P["ml_ig_bias"][l]), ml_bf=rep_row(P["ml_fg_bias"][l]),
        ml_norm=P["ml_norm"][l].reshape(1, ML_DH).astype(F32),
        norm_ffn=P["norm_ffn"][l].reshape(1, D_MODEL).astype(F32),
        wr=wr, br=br,
        wg=P["w_exp_gate"][l].astype(BF16), wu=P["w_exp_up"][l].astype(BF16),
        wd=P["w_exp_down"][l].astype(BF16),
        norm_ple=P["norm_ple"][l].reshape(1, D_MODEL).astype(F32),
        w_ple_gate=P["w_ple_gate"][l].astype(BF16),
        w_ple_proj=P["w_ple_proj"][l].astype(BF16),
    )


def _trunk(x, p, states, layers, final_norm, *, L, tv, Ls, tm, tm_ffn, bb, bbs):
    conv0, gdn0, s5re0, s5im0, mc0, mn0, mm0 = states
    bsz, t, _ = x.shape
    m = bsz * t
    h = x.reshape(m, D_MODEL)
    outs = [[] for _ in range(5)]
    last_row = (tv - 1) % SUBLANES
    p_all = p.reshape(p.shape[0], m, PLE_DIM)
    e_rep = (jnp.arange(2 * GATE_W)[None, :] // ML_DH == jnp.arange(LANES)[:, None]).astype(BF16)
    gdn_all, mc_all = gdn0, mc0
    for l, W in enumerate(layers):
        slab_g, s_u, slab_m = _norm_inproj(h, W["norm_mix"], W["w_cat"], e_rep, tm)
        conv_in = jnp.pad(conv0[l], ((0, 0), (SUBLANES - (CONV_W - 1), 0), (0, 0)))
        o_gdn, conv_n, gdn_all = _gdn(slab_g.reshape(bsz, t, MIX_SLAB_W), conv_in, gdn_all, l,
                                      W["conv_w"], W["alog"], W["dtb"], W["gdn_norm"], L, tv, bb)
        h0 = jnp.stack([s5re0[l].reshape(bsz, S5_N), s5im0[l].reshape(bsz, S5_N)], axis=1)
        h0 = jnp.broadcast_to(h0[:, :, None, :], (bsz, 2, SUBLANES, S5_N))
        o_s5, s5_n = _s5(s_u.reshape(bsz, t, S5_CH), h0, W["lb"], W["bw"], W["cw"], W["s5_d"],
                         W["w_glu"], Ls, bbs)
        o_ml, mc_all, n_n, m_n = _mlstm(slab_m.reshape(bsz, t, MIX_SLAB_W), mc_all, l,
                                        mn0[l].reshape(bsz, PAIRS, 1, LANES),
                                        jnp.repeat(mm0[l], ML_DH, axis=-1).reshape(bsz, 1, ML_W),
                                        W["ml_bi"], W["ml_bf"], W["ml_norm"], L, tv, bb)
        h = _ffn(o_gdn.reshape(m, GDN_VW), o_s5.reshape(m, S5_CH), o_ml.reshape(m, ML_W), h,
                 p_all, l, W, final_norm, tm_ffn, l == len(layers) - 1)
        outs[0].append(conv_n[:, SUBLANES - (CONV_W - 1):])
        outs[1].append(s5_n[:, 0, last_row].reshape(bsz, S5_GROUPS, S5_STATE))
        outs[2].append(s5_n[:, 1, last_row].reshape(bsz, S5_GROUPS, S5_STATE))
        outs[3].append(n_n.reshape(bsz, ML_HEADS, ML_DH))
        outs[4].append(m_n[:, 0, ::ML_DH])
    conv_o, s5re_o, s5im_o, mn_o, mm_o = (jnp.stack(o) for o in outs)
    return (h.reshape(bsz, t, D_MODEL), conv_o, gdn_all, s5re_o, s5im_o, mc_all, mn_o, mm_o)


def kernel(x_prompt, x_sample, p_prompt, p_sample, state_gdn_conv, state_gdn, state_s5_re, state_s5_im, state_mlstm_C, state_mlstm_n, state_mlstm_m, norm_mix, w_in, w_out, gdn_conv_w, gdn_A_log, gdn_dt_bias, gdn_norm, s5_lam_re, s5_lam_im, s5_log_dt, s5_B_re, s5_B_im, s5_C_re, s5_C_im, s5_D, s5_w_glu, ml_ig_bias, ml_fg_bias, ml_norm, norm_ffn, w_router_group, b_router_group, w_router_expert, b_router_expert, w_exp_gate, w_exp_up, w_exp_down, norm_ple, w_ple_gate, w_ple_proj, final_norm):
    P = dict(norm_mix=norm_mix, w_in=w_in, w_out=w_out, gdn_conv_w=gdn_conv_w, gdn_A_log=gdn_A_log,
             gdn_dt_bias=gdn_dt_bias, gdn_norm=gdn_norm, s5_lam_re=s5_lam_re, s5_lam_im=s5_lam_im,
             s5_log_dt=s5_log_dt, s5_B_re=s5_B_re, s5_B_im=s5_B_im, s5_C_re=s5_C_re, s5_C_im=s5_C_im,
             s5_D=s5_D, s5_w_glu=s5_w_glu, ml_ig_bias=ml_ig_bias, ml_fg_bias=ml_fg_bias,
             ml_norm=ml_norm, norm_ffn=norm_ffn, w_router_group=w_router_group,
             b_router_group=b_router_group, w_router_expert=w_router_expert,
             b_router_expert=b_router_expert, w_exp_gate=w_exp_gate, w_exp_up=w_exp_up,
             w_exp_down=w_exp_down, norm_ple=norm_ple, w_ple_gate=w_ple_gate, w_ple_proj=w_ple_proj)
    depth = norm_mix.shape[0]
    layers = [_prep_layer(l, P) for l in range(depth)]
    fnorm = final_norm.reshape(1, D_MODEL).astype(F32)

    bp, tp, _ = x_prompt.shape
    zeros = lambda *s: jnp.zeros((depth, bp) + s, F32)
    prompt_init = (zeros(CONV_W - 1, GDN_CONV_CH), zeros(GDN_HEADS, GDN_DK, GDN_DV),
                   zeros(S5_GROUPS, S5_STATE), zeros(S5_GROUPS, S5_STATE),
                   zeros(ML_HEADS, ML_DH, ML_DH), zeros(ML_HEADS, ML_DH), zeros(ML_HEADS))
    lp = math.gcd(tp, 64)
    lsp = math.gcd(tp, 256)
    res_p = _trunk(x_prompt, p_prompt, prompt_init, layers, fnorm,
                   L=lp, tv=lp, Ls=lsp, tm=512, tm_ffn=1024, bb=8, bbs=1)

    bs, ts, _ = x_sample.shape
    tpad = -(-ts // SUBLANES) * SUBLANES
    xs = jnp.pad(x_sample, ((0, 0), (0, tpad - ts), (0, 0)))
    ps = jnp.pad(p_sample, ((0, 0), (0, 0), (0, tpad - ts), (0, 0)))
    sample_init = (state_gdn_conv, state_gdn, state_s5_re, state_s5_im,
                   state_mlstm_C, state_mlstm_n, state_mlstm_m)
    res_s = _trunk(xs, ps, sample_init, layers, fnorm,
                   L=tpad, tv=ts, Ls=tpad, tm=512, tm_ffn=1024, bb=16, bbs=8)
    y_sample = res_s[0][:, :ts]
    return (res_p[0], y_sample) + res_p[1:] + res_s[1:]
```

```python
import functools
import math

import jax
import jax.numpy as jnp
from jax import lax
from jax.experimental import pallas as pl
from jax.experimental.pallas import tpu as pltpu

F32 = jnp.float32
BF16 = jnp.bfloat16

D_MODEL = 1024
DEPTH = 2
GDN_HEADS = 6
GDN_DK = 64
GDN_DV = 64
GDN_QK = GDN_HEADS * GDN_DK
GDN_VW = GDN_HEADS * GDN_DV
GDN_CONV_CH = 2 * GDN_QK + GDN_VW
CONV_W = 4
S5_GROUPS = 16
S5_GROUP_CH = 16
S5_CH = S5_GROUPS * S5_GROUP_CH
S5_STATE = 64
S5_N = S5_GROUPS * S5_STATE
ML_HEADS = 6
ML_DH = 64
ML_W = ML_HEADS * ML_DH
N_GROUPS = 4
EXPERTS_PER_GROUP = 4
N_EXPERTS = N_GROUPS * EXPERTS_PER_GROUP
D_EXPERT = 256
PLE_DIM = 256
EPS = 1e-6

LANES = 128
SUBLANES = 8
NEG = -1e30
VMEM_LIMIT = 56 * 1024 * 1024

def _cparams(*sem):
    return pltpu.CompilerParams(dimension_semantics=sem, vmem_limit_bytes=VMEM_LIMIT)


def _dot(a, b):
    return jnp.dot(a.astype(BF16), b.astype(BF16), preferred_element_type=F32)


def _dot_nt(a, b):
    return lax.dot_general(a.astype(BF16), b.astype(BF16), (((1,), (1,)), ((), ())),
                           preferred_element_type=F32)


def _dot_tn(a, b):
    return lax.dot_general(a.astype(BF16), b.astype(BF16), (((0,), (0,)), ((), ())),
                           preferred_element_type=F32)


def _split_bf16(a):
    hi = a.astype(BF16)
    lo = (a - hi.astype(F32)).astype(BF16)
    return hi, lo


def _rms(x, gain):
    return x * lax.rsqrt(jnp.mean(x * x, axis=-1, keepdims=True) + EPS) * gain


def _softplus(x):
    return jnp.maximum(x, 0.0) + jnp.log(1.0 + jnp.exp(-jnp.abs(x)))


def _sigmoid(x):
    return 0.5 * jnp.tanh(0.5 * x) + 0.5


def _silu(x):
    h = 0.5 * x
    return h * jnp.tanh(h) + h


def _interleave(gens):
    live = list(gens)
    while live:
        still = []
        for g in live:
            try:
                next(g)
                still.append(g)
            except StopIteration:
                pass
        live = still


def _unit_lower_inverse(n_mat, size, top=None):
    top = size if top is None else top
    r = lax.broadcasted_iota(jnp.int32, (size, size), 0)
    c = lax.broadcasted_iota(jnp.int32, (size, size), 1)
    base = min(16, top)
    same = jnp.bitwise_xor(r, c) < base
    nd = jnp.where(same, n_mat, 0.0)
    eye = jnp.where(r == c, 1.0, 0.0).astype(F32)
    t = eye - nd
    p = 1
    if 2 * p < base:
        x = _dot(nd, nd)
        yield
    while 2 * p < base:
        t_next = t + _dot(t, x)
        if 4 * p < base:
            x = _dot(x, x)
        t = t_next
        yield
        p *= 2
    blk = base
    while blk < top:
        pair = jnp.bitwise_xor(r, c)
        off = jnp.where((pair < 2 * blk) & (pair >= blk), n_mat, 0.0)
        ot = _dot(off, t)
        yield
        t = t - _dot(t, ot)
        yield
        blk *= 2
    return t


MIX_W = 4 * ML_W
GATE_W = 2 * ML_W


def _norm_inproj_kernel(x_ref, g_ref, w_ref, e_ref, og_ref, os_ref, om_ref):
    u = _rms(x_ref[...], g_ref[...]).astype(BF16)
    dot = functools.partial(jnp.dot, preferred_element_type=F32)
    og_ref[:, 0:MIX_W] = dot(u, w_ref[:, 0:MIX_W])
    os_ref[...] = dot(u, w_ref[:, MIX_W:MIX_W + S5_CH])
    om_ref[:, 0:MIX_W] = dot(u, w_ref[:, MIX_W + S5_CH:2 * MIX_W + S5_CH])
    small = dot(u, w_ref[:, 2 * MIX_W + S5_CH:])
    rep = None
    for piece in _split3(small):
        t = dot(piece, e_ref[...])
        rep = t if rep is None else rep + t
    og_ref[:, MIX_W:] = rep[:, 0:GATE_W]
    om_ref[:, MIX_W:] = rep[:, GATE_W:]


def _norm_inproj(x, gain, w, e_rep, tm):
    m = x.shape[0]
    fix = lambda i: (0, 0)
    widths = (MIX_SLAB_W, S5_CH, MIX_SLAB_W)
    return pl.pallas_call(
        _norm_inproj_kernel,
        grid=(m // tm,),
        in_specs=[pl.BlockSpec((tm, D_MODEL), lambda i: (i, 0)),
                  pl.BlockSpec((1, D_MODEL), fix),
                  pl.BlockSpec((D_MODEL, w.shape[1]), fix),
                  pl.BlockSpec((LANES, 2 * GATE_W), fix)],
        out_specs=[pl.BlockSpec((tm, wd), lambda i: (i, 0)) for wd in widths],
        out_shape=[jax.ShapeDtypeStruct((m, wd), F32) for wd in widths],
        compiler_params=_cparams("parallel"),
        name="norm_inproj",
    )(x, gain, w, e_rep)


def _route(f, wr, br):
    fh, fl = _split_bf16(f)
    wh, wl = _split_bf16(wr)
    d = functools.partial(jnp.dot, preferred_element_type=F32)
    logits = d(fh, wh) + (d(fh, wl) + d(fl, wh)) + br
    lane = lax.broadcasted_iota(jnp.int32, logits.shape, 1)
    is_g = lane < N_GROUPS
    gl = jnp.where(is_g, logits, NEG)
    gmax = jnp.max(gl, axis=-1, keepdims=True)
    ge = jnp.where(is_g, jnp.exp(gl - gmax), 0.0)
    p_grp = ge / jnp.sum(ge, axis=-1, keepdims=True)
    g_prob = jnp.max(p_grp, axis=-1, keepdims=True)
    g_idx = jnp.min(jnp.where(is_g & (gl == gmax), lane, LANES), axis=-1, keepdims=True)
    e_lane = lane - N_GROUPS
    is_e = (e_lane >= 0) & (e_lane < N_EXPERTS) & (jnp.right_shift(e_lane, 2) == g_idx)
    le = jnp.where(is_e, logits, NEG)
    m1 = jnp.max(le, axis=-1, keepdims=True)
    i1 = jnp.min(jnp.where(is_e & (le == m1), lane, LANES), axis=-1, keepdims=True)
    is_e2 = is_e & (lane != i1)
    le2 = jnp.where(is_e2, logits, NEG)
    m2 = jnp.max(le2, axis=-1, keepdims=True)
    i2 = jnp.min(jnp.where(is_e2 & (le2 == m2), lane, LANES), axis=-1, keepdims=True)
    e2 = jnp.exp(m2 - m1)
    w1 = g_prob / (1.0 + e2)
    w2 = g_prob * e2 / (1.0 + e2)
    return fh, jnp.where(lane == i1, w1, 0.0) + jnp.where(lane == i2, w2, 0.0)


def _ffn_kernel(og_ref, os_ref, om_ref, h_ref, p_ref, wo_ref, nf_ref, wr_ref, br_ref,
                wg_ref, wu_ref, wd_ref, np_ref, wpg_ref, wpp_ref, fn_ref,
                out_ref, f_ref, gates_ref, *, final):
    acc_ref = out_ref
    gi = pl.program_id(1)

    @pl.when(gi == 0)
    def _():
        mix = jnp.concatenate([og_ref[...], os_ref[...], om_ref[...]], axis=-1)
        h1 = h_ref[...] + jnp.dot(mix, wo_ref[...], preferred_element_type=F32)
        acc_ref[...] = h1
        fh, gates = _route(_rms(h1, nf_ref[...]), wr_ref[...], br_ref[...])
        f_ref[...] = fh
        gates_ref[...] = gates

    x = f_ref[...]
    gates = gates_ref[...]
    lane = lax.broadcasted_iota(jnp.int32, gates.shape, 1)
    base = N_GROUPS + EXPERTS_PER_GROUP * gi
    acc = None
    for j in range(EXPERTS_PER_GROUP):
        gcol = jnp.sum(jnp.where(lane == base + j, gates, 0.0), axis=-1, keepdims=True)
        hg = jnp.dot(x, wg_ref[j], preferred_element_type=F32)
        hu = jnp.dot(x, wu_ref[j], preferred_element_type=F32)
        hidden = (_silu(hg) * hu * gcol).astype(BF16)
        t = jnp.dot(hidden, wd_ref[j], preferred_element_type=F32)
        acc = t if acc is None else acc + t
    acc_ref[...] += acc

    @pl.when(gi == N_GROUPS - 1)
    def _():
        h = acc_ref[...]
        gate = _sigmoid(jnp.dot(_rms(h, np_ref[...]).astype(BF16), wpg_ref[...],
                                preferred_element_type=F32))
        proj = jnp.dot(p_ref[0].astype(BF16), wpp_ref[...], preferred_element_type=F32)
        h = h + proj * gate
        if final:
            h = _rms(h, fn_ref[...])
        out_ref[...] = h


def _ffn(og, os_, om, h, p_all, layer, W, fn, tm, final):
    m = h.shape[0]
    row = lambda i, g: (i, 0)
    wsel = lambda i, g: (g, 0, 0)
    e = EXPERTS_PER_GROUP

    def fix(shape):
        return pl.BlockSpec(shape, lambda i, g: (0, 0), pipeline_mode=pl.Buffered(1))

    return pl.pallas_call(
        functools.partial(_ffn_kernel, final=final),
        grid=(m // tm, N_GROUPS),
        in_specs=[pl.BlockSpec((tm, GDN_VW), row), pl.BlockSpec((tm, S5_CH), row),
                  pl.BlockSpec((tm, ML_W), row), pl.BlockSpec((tm, D_MODEL), row),
                  pl.BlockSpec((1, tm, PLE_DIM), lambda i, g: (layer, i, 0)),
                  fix((D_MODEL, D_MODEL)), fix((1, D_MODEL)),
                  fix((D_MODEL, LANES)), fix((1, LANES)),
                  pl.BlockSpec((e, D_MODEL, D_EXPERT), wsel),
                  pl.BlockSpec((e, D_MODEL, D_EXPERT), wsel),
                  pl.BlockSpec((e, D_EXPERT, D_MODEL), wsel),
                  fix((1, D_MODEL)), fix((D_MODEL, D_MODEL)),
                  fix((PLE_DIM, D_MODEL)), fix((1, D_MODEL))],
        out_specs=pl.BlockSpec((tm, D_MODEL), row),
        out_shape=jax.ShapeDtypeStruct((m, D_MODEL), F32),
        scratch_shapes=[pltpu.VMEM((tm, D_MODEL), BF16), pltpu.VMEM((tm, LANES), F32)],
        compiler_params=_cparams("parallel", "arbitrary"),
        name="ffn",
    )(og, os_, om, h, p_all, W["w_out"], W["norm_ffn"], W["wr"], W["br"],
      W["wg"], W["wu"], W["wd"], W["norm_ple"], W["w_ple_gate"], W["w_ple_proj"], fn)


MIX_SLAB_W = 6 * ML_W
PAIRS = ML_HEADS // 2
HALF = LANES // 2


def _scan_rows(x, size, op, fill):
    row = lax.broadcasted_iota(jnp.int32, x.shape, 0)
    d = 1
    while d < size:
        x = op(x, jnp.where(row >= d, pltpu.roll(x, d, axis=0), fill))
        d *= 2
    return x


def _split3(a):
    hi = a.astype(BF16)
    r1 = a - hi.astype(F32)
    mid = r1.astype(BF16)
    lo = (r1 - mid.astype(F32)).astype(BF16)
    return hi, mid, lo


def _row_form(x_s, L):
    rr = lax.broadcasted_iota(jnp.int32, x_s.shape, 0)
    cc = lax.broadcasted_iota(jnp.int32, x_s.shape, 1)
    dg = jnp.where(rr == jnp.bitwise_and(cc, L - 1), x_s, 0.0)
    return jnp.sum(dg, axis=0, keepdims=True)


def _score_cols(x, p, L):
    if 2 * L == LANES:
        return x[:, LANES * p:LANES * (p + 1)]
    return jnp.concatenate([x[:, LANES * p:LANES * p + L],
                            x[:, LANES * p + HALF:LANES * p + HALF + L]], axis=-1)


def _bd_stack(x2):
    lo = lax.broadcasted_iota(jnp.int32, x2.shape, 1) < HALF
    return jnp.concatenate([jnp.where(lo, x2, 0.0), jnp.where(lo, 0.0, x2)], axis=0)


def _bd_ones(rows_per_half):
    shape = (2 * rows_per_half, LANES)
    r = lax.broadcasted_iota(jnp.int32, shape, 0) < rows_per_half
    c = lax.broadcasted_iota(jnp.int32, shape, 1) < HALF
    return jnp.where(r == c, 1.0, 0.0).astype(F32)


def _layer_state_spec(s_all, layer, bb, in_idx, out_idx):
    depth = s_all.shape[0]
    tail = s_all.shape[2:]
    if layer == 0 and depth > 1:
        return pl.BlockSpec((depth, bb) + tail, lambda b, c: (0, b, 0, 0, 0)), 0, {}
    return (pl.BlockSpec((1, bb) + tail, lambda b, c: (layer, b, 0, 0, 0)), 0,
            {in_idx: out_idx})


def _half_sums(x2, ones_bd, two_pieces=False):
    if not two_pieces:
        return _dot(x2, ones_bd)
    hi, lo = _split_bf16(x2)
    ob = ones_bd.astype(BF16)
    return (jnp.dot(hi, ob, preferred_element_type=F32)
            + jnp.dot(lo, ob, preferred_element_type=F32))


def _mlstm_kernel(a_ref, c0_ref, n0_ref, m0_ref, bi_ref, bf_ref, ng_ref,
                  o_ref, cn_ref, nn_ref, mn_ref, st_ref, m_ref, *, L, tv, bb, lsel):
    cidx = pl.program_id(1)
    nc = pl.num_programs(1)
    ones_ch = _bd_ones(HALF)
    r128 = lax.broadcasted_iota(jnp.int32, (LANES, LANES), 0)
    c128 = lax.broadcasted_iota(jnp.int32, (LANES, LANES), 1)
    diag128 = r128 == c128

    @pl.when(cidx == 0)
    def _():
        st_ref[...] = jnp.zeros(st_ref.shape, F32)
        m_ref[...] = m0_ref[...]
        ob = ones_ch.astype(BF16)
        for i in range(bb):
            for p in range(PAIRS):
                st_ref[i, p, 0:HALF, 0:HALF] = c0_ref[lsel, i, 2 * p]
                st_ref[i, p, HALF:, HALF:LANES] = c0_ref[lsel, i, 2 * p + 1]
                acc = None
                for piece in _split3(n0_ref[i, p]):
                    dg = jnp.where(diag128, piece.astype(F32), 0.0).astype(BF16)
                    t = jnp.dot(dg, ob, preferred_element_type=F32)
                    acc = t if acc is None else acc + t
                st_ref[i, p, :, LANES:] = acc

    W = 2 * L
    rr = lax.broadcasted_iota(jnp.int32, (L, W), 0)
    cc = lax.broadcasted_iota(jnp.int32, (L, W), 1)
    incl = rr >= jnp.bitwise_and(cc, L - 1)
    ones_keys = _bd_ones(L)
    bd256 = jnp.concatenate([ones_ch, ones_ch], axis=-1) > 0.5
    gain2 = ng_ref[...]
    gain2 = jnp.concatenate([gain2, gain2], axis=-1)
    ones_l = jnp.ones((L, LANES), F32)
    outs = [[None] * PAIRS for _ in range(bb)]
    st_news = [[None] * PAIRS for _ in range(bb)]
    m_news = [None] * bb
    rows = []
    for i in range(bb):
        li = a_ref[i, :, 4 * ML_W:5 * ML_W] + bi_ref[...]
        lf = -_softplus(-(a_ref[i, :, 5 * ML_W:6 * ML_W] + bf_ref[...]))
        if tv < L:
            valid = lax.broadcasted_iota(jnp.int32, li.shape, 0) < tv
            li = jnp.where(valid, li, NEG)
            lf = jnp.where(valid, lf, 0.0)
        bcum = _scan_rows(lf, L, jnp.add, 0.0)
        a = li - bcum
        m0 = m_ref[i]
        m_t = bcum + jnp.maximum(m0, _scan_rows(a, L, jnp.maximum, NEG))
        e_inter = jnp.exp(bcum + m0 - m_t)
        m_new = m_t[L - 1:L, :]
        b_last = bcum[L - 1:L, :]
        e_c = jnp.exp(b_last + m0 - m_new)
        kw = a_ref[i, :, ML_W:2 * ML_W] * (ML_DH ** -0.5) * jnp.exp(b_last + a - m_new)
        m_news[i] = m_new
        rows.append((a, bcum, m_t, e_inter, e_c, kw))

    def pair(i, p):
        a, bcum, m_t, e_inter, e_c, kw = rows[i]
        sl = slice(LANES * p, LANES * (p + 1))
        q2 = a_ref[i, :, sl]
        k2 = a_ref[i, :, ML_W + LANES * p:ML_W + LANES * (p + 1)] * (ML_DH ** -0.5)
        v2 = a_ref[i, :, 2 * ML_W + LANES * p:2 * ML_W + LANES * (p + 1)]
        og2 = a_ref[i, :, 3 * ML_W + LANES * p:3 * ML_W + LANES * (p + 1)]
        st = st_ref[i, p]
        arow = _row_form(_score_cols(a, p, L), L)
        qk = _dot_nt(q2, _bd_stack(k2))
        qcn = _dot(q2, st)
        upd = _dot_tn(kw[:, sl], jnp.concatenate([v2, ones_l], axis=-1))
        ec2 = e_c[:, sl]
        st_news[i][p] = st * jnp.concatenate([ec2, ec2], axis=-1) + jnp.where(bd256, upd, 0.0)
        yield
        w_intra = jnp.where(incl, jnp.exp(_score_cols(bcum, p, L) + arow - _score_cols(m_t, p, L)), 0.0)
        s2 = qk * w_intra
        nd = _dot(s2, jnp.concatenate([_bd_stack(v2), ones_keys], axis=-1))
        yield
        e2 = e_inter[:, sl]
        num = e2 * qcn[:, 0:LANES] + nd[:, 0:LANES]
        den = e2 * qcn[:, LANES:] + nd[:, LANES:]
        hh = num / jnp.maximum(jnp.abs(den), jnp.exp(-m_t[:, sl]))
        ss = _half_sums(hh * hh, ones_ch)
        yield
        outs[i][p] = hh * lax.rsqrt(ss * (1.0 / ML_DH) + EPS) * gain2 * _sigmoid(og2)

    _interleave([pair(i, p) for i in range(bb) for p in range(PAIRS)])
    for i in range(bb):
        for p in range(PAIRS):
            st_ref[i, p] = st_news[i][p]
        m_ref[i] = m_news[i]
        o_ref[i] = jnp.concatenate(outs[i], axis=-1).astype(o_ref.dtype)

    @pl.when(cidx == nc - 1)
    def _():
        for k in range(cn_ref.shape[0]):
            if k != lsel:
                cn_ref[k] = c0_ref[k]
        for i in range(bb):
            for p in range(PAIRS):
                st = st_news[i][p]
                cn_ref[lsel, i, 2 * p] = st[0:HALF, 0:HALF]
                cn_ref[lsel, i, 2 * p + 1] = st[HALF:, HALF:LANES]
                dg = jnp.where(diag128, st[:, LANES:], 0.0)
                nn_ref[i, p] = jnp.sum(dg, axis=0, keepdims=True)
        mn_ref[...] = m_ref[...]


def _mlstm(slab, c_all, layer, n0, m0, bi, bf, ng, L, tv, bb):
    bsz, t, _ = slab.shape
    nc = t // L
    fix2 = lambda b, c: (0, 0)
    c_spec, lsel, aliases = _layer_state_spec(c_all, layer, bb, 1, 1)
    n_spec = pl.BlockSpec((bb, PAIRS, 1, LANES), lambda b, c: (b, 0, 0, 0))
    m_spec = pl.BlockSpec((bb, 1, ML_W), lambda b, c: (b, 0, 0))
    return pl.pallas_call(
        functools.partial(_mlstm_kernel, L=L, tv=tv, bb=bb, lsel=lsel),
        grid=(bsz // bb, nc),
        input_output_aliases=aliases,
        in_specs=[pl.BlockSpec((bb, L, MIX_SLAB_W), lambda b, c: (b, c, 0)),
                  c_spec, n_spec, m_spec,
                  pl.BlockSpec((1, ML_W), fix2), pl.BlockSpec((1, ML_W), fix2),
                  pl.BlockSpec((1, ML_DH), fix2)],
        out_specs=[pl.BlockSpec((bb, L, ML_W), lambda b, c: (b, c, 0)), c_spec, n_spec, m_spec],
        out_shape=[jax.ShapeDtypeStruct((bsz, t, ML_W), BF16),
                   jax.ShapeDtypeStruct(c_all.shape, F32),
                   jax.ShapeDtypeStruct((bsz, PAIRS, 1, LANES), F32),
                   jax.ShapeDtypeStruct((bsz, 1, ML_W), F32)],
        scratch_shapes=[pltpu.VMEM((bb, PAIRS, LANES, 2 * LANES), F32),
                        pltpu.VMEM((bb, 1, ML_W), F32)],
        compiler_params=_cparams("parallel", "arbitrary"),
        name="mlstm",
    )(slab, c_all, n0, m0, bi, bf, ng)


def _gdn_kernel(a_ref, conv0_ref, s0_ref, cw_ref, alog_ref, dtb_ref, ng_ref,
                o_ref, convn_ref, sn_ref, ext_ref, s_ref, *, L, tv, bb, lsel):
    c = pl.program_id(1)
    nc = pl.num_programs(1)

    @pl.when(c == 0)
    def _():
        ext_ref[:, 0:SUBLANES, :] = conv0_ref[...]
        s_ref[...] = jnp.zeros(s_ref.shape, F32)
        for i in range(bb):
            for p in range(PAIRS):
                s_ref[i, p, 0:HALF, 0:HALF] = s0_ref[lsel, i, 2 * p]
                s_ref[i, p, HALF:, HALF:] = s0_ref[lsel, i, 2 * p + 1]

    W = 2 * L
    rr = lax.broadcasted_iota(jnp.int32, (L, W), 0)
    cc = lax.broadcasted_iota(jnp.int32, (L, W), 1)
    key = jnp.bitwise_and(cc, L - 1)
    incl = rr >= key
    strict = rr > key
    first_half = cc < L
    lo = lax.broadcasted_iota(jnp.int32, (L, LANES), 1) < HALF
    ones_ch = _bd_ones(HALF)
    bd128 = ones_ch > 0.5
    cw = cw_ref[...]
    gain2 = ng_ref[...]
    gain2 = jnp.concatenate([gain2, gain2], axis=-1)
    outs = [[None] * PAIRS for _ in range(bb)]
    s_news = [[None] * PAIRS for _ in range(bb)]
    carries = [None] * bb
    rows = []
    for i in range(bb):
        raw = a_ref[i, :, 0:GDN_CONV_CH]
        ext_ref[i, SUBLANES:SUBLANES + L, :] = raw
        ext = ext_ref[i]
        acc = raw * cw[CONV_W - 1:CONV_W, :]
        for j in range(CONV_W - 1):
            sh = pltpu.roll(ext, CONV_W - 1 - j, axis=0)[SUBLANES:SUBLANES + L]
            acc = acc + sh * cw[j:j + 1, :]
        qkv = _silu(acc)
        carries[i] = pltpu.roll(ext, (L + SUBLANES - tv) % (L + SUBLANES), axis=0)[0:SUBLANES]

        beta = _sigmoid(a_ref[i, :, 4 * GDN_VW:5 * GDN_VW])
        g = -jnp.exp(alog_ref[...]) * _softplus(a_ref[i, :, 5 * GDN_VW:6 * GDN_VW] + dtb_ref[...])
        if tv < L:
            valid = lax.broadcasted_iota(jnp.int32, g.shape, 0) < tv
            beta = jnp.where(valid, beta, 0.0)
            g = jnp.where(valid, g, 0.0)
        gcum = _scan_rows(g, L, jnp.add, 0.0)
        g_last = gcum[L - 1:L, :]
        rows.append((qkv, beta, gcum, jnp.exp(gcum), jnp.exp(g_last - gcum), jnp.exp(g_last)))
    for i in range(bb):
        ext_ref[i, 0:SUBLANES, :] = carries[i]

    def pair(i, p):
        qkv, beta, gcum, eg, kdec, sdec = rows[i]
        sl = slice(LANES * p, LANES * (p + 1))
        q2 = qkv[:, LANES * p:LANES * (p + 1)]
        k2 = qkv[:, GDN_QK + LANES * p:GDN_QK + LANES * (p + 1)]
        v2 = qkv[:, 2 * GDN_QK + LANES * p:2 * GDN_QK + LANES * (p + 1)]
        z2 = a_ref[i, :, GDN_CONV_CH + LANES * p:GDN_CONV_CH + LANES * (p + 1)]
        s2 = s_ref[i, p]
        ssq = _half_sums(q2 * q2, ones_ch, L < HALF)
        ssk = _half_sums(k2 * k2, ones_ch, L < HALF)
        grow = _row_form(_score_cols(gcum, p, L), L)
        yield
        q2 = q2 * lax.rsqrt(ssq + EPS) * (GDN_DK ** -0.5)
        k2 = k2 * lax.rsqrt(ssk + EPS)
        beta2 = beta[:, sl]
        eg2 = eg[:, sl]
        kb2 = k2 * beta2
        kq = _dot_nt(jnp.concatenate([_bd_stack(kb2), q2], axis=0), _bd_stack(k2))
        qs = _dot(q2 * eg2, s2)
        yield
        gam = jnp.where(incl, jnp.exp(jnp.minimum(_score_cols(gcum, p, L) - grow, 0.0)), 0.0)
        gam_s = jnp.where(strict, gam, 0.0)
        gam_bd = jnp.concatenate([jnp.where(first_half, gam_s, 0.0),
                                  jnp.where(first_half, 0.0, gam_s)], axis=0)
        t_inv = yield from _unit_lower_inverse(kq[0:W] * gam_bd, W, L)
        rhs = jnp.concatenate([v2 * beta2, kb2 * eg2], axis=-1)
        sol = _dot(t_inv, jnp.concatenate([rhs, rhs], axis=0))
        yield
        u2 = jnp.where(lo, sol[0:L, 0:LANES], sol[L:W, 0:LANES])
        w2 = jnp.where(lo, sol[0:L, LANES:], sol[L:W, LANES:])
        v_new = u2 - _dot(w2, s2)
        yield
        o2 = qs + _dot(kq[W:] * gam, _bd_stack(v_new))
        upd = _dot_tn(k2 * kdec[:, sl], v_new)
        s_news[i][p] = s2 * sdec[:, sl] + jnp.where(bd128, upd, 0.0)
        yield
        ss = _half_sums(o2 * o2, ones_ch, L < HALF)
        yield
        outs[i][p] = o2 * lax.rsqrt(ss * (1.0 / GDN_DV) + EPS) * gain2 * _silu(z2)

    _interleave([pair(i, p) for i in range(bb) for p in range(PAIRS)])
    for i in range(bb):
        for p in range(PAIRS):
            s_ref[i, p] = s_news[i][p]
        o_ref[i] = jnp.concatenate(outs[i], axis=-1).astype(o_ref.dtype)

    @pl.when(c == nc - 1)
    def _():
        for k in range(sn_ref.shape[0]):
            if k != lsel:
                sn_ref[k] = s0_ref[k]
        for i in range(bb):
            convn_ref[i] = carries[i]
            for p in range(PAIRS):
                sn_ref[lsel, i, 2 * p] = s_news[i][p][0:HALF, 0:HALF]
                sn_ref[lsel, i, 2 * p + 1] = s_news[i][p][HALF:, HALF:]


def _gdn(slab, conv0, s_all, layer, cw, alog, dtb, ng, L, tv, bb):
    bsz, t, _ = slab.shape
    nc = t // L
    fix2 = lambda b, c: (0, 0)
    s_spec, lsel, aliases = _layer_state_spec(s_all, layer, bb, 2, 2)
    cv_spec = pl.BlockSpec((bb, SUBLANES, GDN_CONV_CH), lambda b, c: (b, 0, 0))
    return pl.pallas_call(
        functools.partial(_gdn_kernel, L=L, tv=tv, bb=bb, lsel=lsel),
        grid=(bsz // bb, nc),
        input_output_aliases=aliases,
        in_specs=[pl.BlockSpec((bb, L, MIX_SLAB_W), lambda b, c: (b, c, 0)),
                  cv_spec, s_spec,
                  pl.BlockSpec((CONV_W, GDN_CONV_CH), fix2),
                  pl.BlockSpec((1, GDN_VW), fix2), pl.BlockSpec((1, GDN_VW), fix2),
                  pl.BlockSpec((1, GDN_DV), fix2)],
        out_specs=[pl.BlockSpec((bb, L, GDN_VW), lambda b, c: (b, c, 0)), cv_spec, s_spec],
        out_shape=[jax.ShapeDtypeStruct((bsz, t, GDN_VW), BF16),
                   jax.ShapeDtypeStruct((bsz, SUBLANES, GDN_CONV_CH), F32),
                   jax.ShapeDtypeStruct(s_all.shape, F32)],
        scratch_shapes=[pltpu.VMEM((bb, L + SUBLANES, GDN_CONV_CH), F32),
                        pltpu.VMEM((bb, PAIRS, LANES, LANES), F32)],
        compiler_params=_cparams("parallel", "arbitrary"),
        name="gdn",
    )(slab, conv0, s_all, cw, alog, dtb, ng)


def _cmul(ar, ai, br, bi):
    return ar * br - ai * bi, ar * bi + ai * br


def _s5_kernel(u_ref, h0_ref, lb_ref, bw_ref, cw_ref, dsk_ref, wglu_ref,
               o_ref, hn_ref, car_ref, h_ref, *, L, bb):
    c = pl.program_id(1)
    nc = pl.num_programs(1)

    @pl.when(c == 0)
    def _():
        car_ref[...] = h0_ref[...]

    rows = bb * L
    u = u_ref[...].reshape(rows, S5_CH)
    bu = jnp.dot(u.astype(BF16), bw_ref[...], preferred_element_type=F32)
    tiles = rows // SUBLANES
    x_re = bu[:, 0:S5_N].reshape(tiles, SUBLANES, S5_N)
    x_im = bu[:, S5_N:].reshape(tiles, SUBLANES, S5_N)
    p1 = (lb_ref[0:1, :], lb_ref[1:2, :])
    p2 = _cmul(*p1, *p1)
    p4 = _cmul(*p2, *p2)
    p8 = _cmul(*p4, *p4)
    sub = lax.broadcasted_iota(jnp.int32, (SUBLANES, S5_N), 0)
    for d, (pr, pi) in ((1, p1), (2, p2), (4, p4)):
        pr = jnp.where(sub >= d, pr, 0.0)
        pi = jnp.where(sub >= d, pi, 0.0)
        s_re = pltpu.roll(x_re, d, axis=1)
        s_im = pltpu.roll(x_im, d, axis=1)
        x_re, x_im = x_re + (pr * s_re - pi * s_im), x_im + (pr * s_im + pi * s_re)
    x_re = x_re.reshape(rows, S5_N)
    x_im = x_im.reshape(rows, S5_N)
    k = sub + 1
    pw_re = jnp.ones((SUBLANES, S5_N), F32)
    pw_im = jnp.zeros((SUBLANES, S5_N), F32)
    for bit, (pr, pi) in ((1, p1), (2, p2), (4, p4), (8, p8)):
        m_re, m_im = _cmul(pw_re, pw_im, pr, pi)
        on = jnp.bitwise_and(k, bit) != 0
        pw_re = jnp.where(on, m_re, pw_re)
        pw_im = jnp.where(on, m_im, pw_im)
    for i in range(bb):
        car_re = car_ref[i, 0]
        car_im = car_ref[i, 1]
        for t in range(L // SUBLANES):
            r0 = i * L + t * SUBLANES
            a_re, a_im = _cmul(pw_re, pw_im, car_re, car_im)
            t_re = x_re[r0:r0 + SUBLANES, :] + a_re
            t_im = x_im[r0:r0 + SUBLANES, :] + a_im
            h_ref[r0:r0 + SUBLANES, 0:S5_N] = t_re
            h_ref[r0:r0 + SUBLANES, S5_N:] = t_im
            car_re = jnp.broadcast_to(t_re[SUBLANES - 1:SUBLANES, :], (SUBLANES, S5_N))
            car_im = jnp.broadcast_to(t_im[SUBLANES - 1:SUBLANES, :], (SUBLANES, S5_N))
        car_ref[i, 0] = car_re
        car_ref[i, 1] = car_im
        hn_ref[i, 0] = t_re
        hn_ref[i, 1] = t_im
    y = jnp.dot(h_ref[...].astype(BF16), cw_ref[...], preferred_element_type=F32) + dsk_ref[...] * u
    zg = jax.nn.gelu(y)
    out = zg * _sigmoid(jnp.dot(zg.astype(BF16), wglu_ref[...], preferred_element_type=F32))
    o_ref[...] = out.reshape(bb, L, S5_CH).astype(o_ref.dtype)


def _s5(u, h0, lb, bw, cw, dsk, wglu, L, bb):
    bsz, t, _ = u.shape
    nc = t // L
    fix2 = lambda b, c: (0, 0)
    h_spec = pl.BlockSpec((bb, 2, SUBLANES, S5_N), lambda b, c: (b, 0, 0, 0))
    return pl.pallas_call(
        functools.partial(_s5_kernel, L=L, bb=bb),
        grid=(bsz // bb, nc),
        in_specs=[pl.BlockSpec((bb, L, S5_CH), lambda b, c: (b, c, 0)),
                  h_spec,
                  pl.BlockSpec((SUBLANES, S5_N), fix2),
                  pl.BlockSpec((S5_CH, 2 * S5_N), fix2),
                  pl.BlockSpec((2 * S5_N, S5_CH), fix2),
                  pl.BlockSpec((1, S5_CH), fix2),
                  pl.BlockSpec((S5_CH, S5_CH), fix2)],
        out_specs=[pl.BlockSpec((bb, L, S5_CH), lambda b, c: (b, c, 0)), h_spec],
        out_shape=[jax.ShapeDtypeStruct((bsz, t, S5_CH), BF16),
                   jax.ShapeDtypeStruct((bsz, 2, SUBLANES, S5_N), F32)],
        scratch_shapes=[pltpu.VMEM((bb, 2, SUBLANES, S5_N), F32),
                        pltpu.VMEM((bb * L, 2 * S5_N), F32)],
        compiler_params=_cparams("parallel", "arbitrary"),
        name="s5",
    )(u, h0, lb, bw, cw, dsk, wglu)


def _prep_layer(l, P):
    w_in = P["w_in"][l]
    sizes = (GDN_CONV_CH, GDN_VW, GDN_HEADS, GDN_HEADS, S5_CH, 3 * ML_W, ML_HEADS, ML_HEADS, ML_W)
    offs = [0]
    for s in sizes:
        offs.append(offs[-1] + s)
    g_qkv, g_z, g_b, g_a, s_u, m_qkv, m_i, m_f, m_o = [w_in[:, offs[i]:offs[i + 1]] for i in range(9)]
    zpad = jnp.zeros((D_MODEL, LANES - 4 * GDN_HEADS), F32)
    w_cat = jnp.concatenate([g_qkv, g_z, s_u, m_qkv, m_o, g_b, g_a, m_i, m_f, zpad],
                            axis=1).astype(BF16)

    lr = P["s5_lam_re"][l].astype(F32)
    li = P["s5_lam_im"][l].astype(F32)
    dt = jnp.exp(P["s5_log_dt"][l].astype(F32))[:, None]
    mag = jnp.exp(lr * dt)
    lb_re = mag * jnp.cos(li * dt)
    lb_im = mag * jnp.sin(li * dt)
    den = lr * lr + li * li
    c_re = ((lb_re - 1.0) * lr + lb_im * li) / den
    c_im = (lb_im * lr - (lb_re - 1.0) * li) / den
    b_r = P["s5_B_re"][l].astype(F32)
    b_i = P["s5_B_im"][l].astype(F32)
    bb_re = c_re[..., None] * b_r - c_im[..., None] * b_i
    bb_im = c_re[..., None] * b_i + c_im[..., None] * b_r
    eye_g = jnp.eye(S5_GROUPS, dtype=F32)
    bd = lambda m: jnp.einsum("gph,gk->ghkp", m, eye_g).reshape(S5_CH, S5_N)
    bw = jnp.concatenate([bd(bb_re), bd(bb_im)], axis=1).astype(BF16)
    cd = lambda m: jnp.einsum("ghp,gk->gpkh", m, eye_g).reshape(S5_N, S5_CH)
    cw = jnp.concatenate([cd(P["s5_C_re"][l].astype(F32)),
                          -cd(P["s5_C_im"][l].astype(F32))], axis=0).astype(BF16)
    lb = jnp.zeros((SUBLANES, S5_N), F32).at[0].set(lb_re.reshape(-1)).at[1].set(lb_im.reshape(-1))

    wr = jnp.zeros((D_MODEL, LANES), F32)
    wr = wr.at[:, 0:N_GROUPS].set(P["w_router_group"][l])
    wr = wr.at[:, N_GROUPS:N_GROUPS + N_EXPERTS].set(P["w_router_expert"][l])
    br = jnp.zeros((1, LANES), F32)
    br = br.at[0, 0:N_GROUPS].set(P["b_router_group"][l])
    br = br.at[0, N_GROUPS:N_GROUPS + N_EXPERTS].set(P["b_router_expert"][l])

    rep_row = lambda v: jnp.repeat(v.astype(F32), ML_DH).reshape(1, ML_W)
    return dict(
        norm_mix=P["norm_mix"][l].reshape(1, D_MODEL).astype(F32),
        w_cat=w_cat,
        w_out=P["w_out"][l].astype(BF16),
        conv_w=P["gdn_conv_w"][l].astype(F32),
        alog=rep_row(P["gdn_A_log"][l]), dtb=rep_row(P["gdn_dt_bias"][l]),
        gdn_norm=P["gdn_norm"][l].reshape(1, GDN_DV).astype(F32),
        lb=lb, bw=bw, cw=cw,
        s5_d=P["s5_D"][l].reshape(1, S5_CH).astype(F32),
        w_glu=P["s5_w_glu"][l].astype(BF16),
        ml_bi=rep_row(P["ml_ig_bias"][l]), ml_bf=rep_row(P["ml_fg_bias"][l]),
        ml_norm=P["ml_norm"][l].reshape(1, ML_DH).astype(F32),
        norm_ffn=P["norm_ffn"][l].reshape(1, D_MODEL).astype(F32),
        wr=wr, br=br,
        wg=P["w_exp_gate"][l].astype(BF16), wu=P["w_exp_up"][l].astype(BF16),
        wd=P["w_exp_down"][l].astype(BF16),
        norm_ple=P["norm_ple"][l].reshape(1, D_MODEL).astype(F32),
        w_ple_gate=P["w_ple_gate"][l].astype(BF16),
        w_ple_proj=P["w_ple_proj"][l].astype(BF16),
    )


def _trunk(x, p, states, layers, final_norm, *, L, tv, Ls, tm, tm_ffn, bb, bbs):
    conv0, gdn0, s5re0, s5im0, mc0, mn0, mm0 = states
    bsz, t, _ = x.shape
    m = bsz * t
    h = x.reshape(m, D_MODEL)
    outs = [[] for _ in range(5)]
    last_row = (tv - 1) % SUBLANES
    p_all = p.reshape(p.shape[0], m, PLE_DIM)
    e_rep = (jnp.arange(2 * GATE_W)[None, :] // ML_DH == jnp.arange(LANES)[:, None]).astype(BF16)
    gdn_all, mc_all = gdn0, mc0
    for l, W in enumerate(layers):
        slab_g, s_u, slab_m = _norm_inproj(h, W["norm_mix"], W["w_cat"], e_rep, tm)
        conv_in = jnp.pad(conv0[l], ((0, 0), (SUBLANES - (CONV_W - 1), 0), (0, 0)))
        o_gdn, conv_n, gdn_all = _gdn(slab_g.reshape(bsz, t, MIX_SLAB_W), conv_in, gdn_all, l,
                                      W["conv_w"], W["alog"], W["dtb"], W["gdn_norm"], L, tv, bb)
        h0 = jnp.stack([s5re0[l].reshape(bsz, S5_N), s5im0[l].reshape(bsz, S5_N)], axis=1)
        h0 = jnp.broadcast_to(h0[:, :, None, :], (bsz, 2, SUBLANES, S5_N))
        o_s5, s5_n = _s5(s_u.reshape(bsz, t, S5_CH), h0, W["lb"], W["bw"], W["cw"], W["s5_d"],
                         W["w_glu"], Ls, bbs)
        o_ml, mc_all, n_n, m_n = _mlstm(slab_m.reshape(bsz, t, MIX_SLAB_W), mc_all, l,
                                        mn0[l].reshape(bsz, PAIRS, 1, LANES),
                                        jnp.repeat(mm0[l], ML_DH, axis=-1).reshape(bsz, 1, ML_W),
                                        W["ml_bi"], W["ml_bf"], W["ml_norm"], L, tv, bb)
        h = _ffn(o_gdn.reshape(m, GDN_VW), o_s5.reshape(m, S5_CH), o_ml.reshape(m, ML_W), h,
                 p_all, l, W, final_norm, tm_ffn, l == len(layers) - 1)
        outs[0].append(conv_n[:, SUBLANES - (CONV_W - 1):])
        outs[1].append(s5_n[:, 0, last_row].reshape(bsz, S5_GROUPS, S5_STATE))
        outs[2].append(s5_n[:, 1, last_row].reshape(bsz, S5_GROUPS, S5_STATE))
        outs[3].append(n_n.reshape(bsz, ML_HEADS, ML_DH))
        outs[4].append(m_n[:, 0, ::ML_DH])
    conv_o, s5re_o, s5im_o, mn_o, mm_o = (jnp.stack(o) for o in outs)
    return (h.reshape(bsz, t, D_MODEL), conv_o, gdn_all, s5re_o, s5im_o, mc_all, mn_o, mm_o)


def kernel(x_prompt, x_sample, p_prompt, p_sample, state_gdn_conv, state_gdn, state_s5_re, state_s5_im, state_mlstm_C, state_mlstm_n, state_mlstm_m, norm_mix, w_in, w_out, gdn_conv_w, gdn_A_log, gdn_dt_bias, gdn_norm, s5_lam_re, s5_lam_im, s5_log_dt, s5_B_re, s5_B_im, s5_C_re, s5_C_im, s5_D, s5_w_glu, ml_ig_bias, ml_fg_bias, ml_norm, norm_ffn, w_router_group, b_router_group, w_router_expert, b_router_expert, w_exp_gate, w_exp_up, w_exp_down, norm_ple, w_ple_gate, w_ple_proj, final_norm):
    P = dict(norm_mix=norm_mix, w_in=w_in, w_out=w_out, gdn_conv_w=gdn_conv_w, gdn_A_log=gdn_A_log,
             gdn_dt_bias=gdn_dt_bias, gdn_norm=gdn_norm, s5_lam_re=s5_lam_re, s5_lam_im=s5_lam_im,
             s5_log_dt=s5_log_dt, s5_B_re=s5_B_re, s5_B_im=s5_B_im, s5_C_re=s5_C_re, s5_C_im=s5_C_im,
             s5_D=s5_D, s5_w_glu=s5_w_glu, ml_ig_bias=ml_ig_bias, ml_fg_bias=ml_fg_bias,
             ml_norm=ml_norm, norm_ffn=norm_ffn, w_router_group=w_router_group,
             b_router_group=b_router_group, w_router_expert=w_router_expert,
             b_router_expert=b_router_expert, w_exp_gate=w_exp_gate, w_exp_up=w_exp_up,
             w_exp_down=w_exp_down, norm_ple=norm_ple, w_ple_gate=w_ple_gate, w_ple_proj=w_ple_proj)
    depth = norm_mix.shape[0]
    layers = [_prep_layer(l, P) for l in range(depth)]
    fnorm = final_norm.reshape(1, D_MODEL).astype(F32)

    bp, tp, _ = x_prompt.shape
    zeros = lambda *s: jnp.zeros((depth, bp) + s, F32)
    prompt_init = (zeros(CONV_W - 1, GDN_CONV_CH), zeros(GDN_HEADS, GDN_DK, GDN_DV),
                   zeros(S5_GROUPS, S5_STATE), zeros(S5_GROUPS, S5_STATE),
                   zeros(ML_HEADS, ML_DH, ML_DH), zeros(ML_HEADS, ML_DH), zeros(ML_HEADS))
    lp = math.gcd(tp, 64)
    lsp = math.gcd(tp, 256)
    res_p = _trunk(x_prompt, p_prompt, prompt_init, layers, fnorm,
                   L=lp, tv=lp, Ls=lsp, tm=512, tm_ffn=1024, bb=8, bbs=1)

    bs, ts, _ = x_sample.shape
    tpad = -(-ts // SUBLANES) * SUBLANES
    xs = jnp.pad(x_sample, ((0, 0), (0, tpad - ts), (0, 0)))
    ps = jnp.pad(p_sample, ((0, 0), (0, 0), (0, tpad - ts), (0, 0)))
    sample_init = (state_gdn_conv, state_gdn, state_s5_re, state_s5_im,
                   state_mlstm_C, state_mlstm_n, state_mlstm_m)
    res_s = _trunk(xs, ps, sample_init, layers, fnorm,
                   L=tpad, tv=ts, Ls=tpad, tm=512, tm_ffn=1024, bb=16, bbs=8)
    y_sample = res_s[0][:, :ts]
    return (res_p[0], y_sample) + res_p[1:] + res_s[1:]
```

```python
import functools
import math

import jax
import jax.numpy as jnp
from jax import lax
from jax.experimental import pallas as pl
from jax.experimental.pallas import tpu as pltpu

F32 = jnp.float32
BF16 = jnp.bfloat16

D_MODEL = 1024
DEPTH = 2
GDN_HEADS = 6
GDN_DK = 64
GDN_DV = 64
GDN_QK = GDN_HEADS * GDN_DK
GDN_VW = GDN_HEADS * GDN_DV
GDN_CONV_CH = 2 * GDN_QK + GDN_VW
CONV_W = 4
S5_GROUPS = 16
S5_GROUP_CH = 16
S5_CH = S5_GROUPS * S5_GROUP_CH
S5_STATE = 64
S5_N = S5_GROUPS * S5_STATE
ML_HEADS = 6
ML_DH = 64
ML_W = ML_HEADS * ML_DH
N_GROUPS = 4
EXPERTS_PER_GROUP = 4
N_EXPERTS = N_GROUPS * EXPERTS_PER_GROUP
D_EXPERT = 256
PLE_DIM = 256
EPS = 1e-6

LANES = 128
SUBLANES = 8
NEG = -1e30
VMEM_LIMIT = 56 * 1024 * 1024

def _cparams(*sem):
    return pltpu.CompilerParams(dimension_semantics=sem, vmem_limit_bytes=VMEM_LIMIT)


def _dot(a, b):
    return jnp.dot(a.astype(BF16), b.astype(BF16), preferred_element_type=F32)


def _dot_nt(a, b):
    return lax.dot_general(a.astype(BF16), b.astype(BF16), (((1,), (1,)), ((), ())),
                           preferred_element_type=F32)


def _dot_tn(a, b):
    return lax.dot_general(a.astype(BF16), b.astype(BF16), (((0,), (0,)), ((), ())),
                           preferred_element_type=F32)


def _split_bf16(a):
    hi = a.astype(BF16)
    lo = (a - hi.astype(F32)).astype(BF16)
    return hi, lo


def _rms(x, gain):
    return x * lax.rsqrt(jnp.mean(x * x, axis=-1, keepdims=True) + EPS) * gain


def _softplus(x):
    return jnp.maximum(x, 0.0) + jnp.log(1.0 + jnp.exp(-jnp.abs(x)))


def _sigmoid(x):
    return 0.5 * jnp.tanh(0.5 * x) + 0.5


def _silu(x):
    h = 0.5 * x
    return h * jnp.tanh(h) + h


def _interleave(gens):
    live = list(gens)
    while live:
        still = []
        for g in live:
            try:
                next(g)
                still.append(g)
            except StopIteration:
                pass
        live = still


def _unit_lower_inverse(n_mat, size, top=None):
    top = size if top is None else top
    r = lax.broadcasted_iota(jnp.int32, (size, size), 0)
    c = lax.broadcasted_iota(jnp.int32, (size, size), 1)
    base = min(16, top)
    same = jnp.bitwise_xor(r, c) < base
    nd = jnp.where(same, n_mat, 0.0)
    eye = jnp.where(r == c, 1.0, 0.0).astype(F32)
    t = eye - nd
    p = 1
    if 2 * p < base:
        x = _dot(nd, nd)
        yield
    while 2 * p < base:
        t_next = t + _dot(t, x)
        if 4 * p < base:
            x = _dot(x, x)
        t = t_next
        yield
        p *= 2
    blk = base
    while blk < top:
        pair = jnp.bitwise_xor(r, c)
        off = jnp.where((pair < 2 * blk) & (pair >= blk), n_mat, 0.0)
        ot = _dot(off, t)
        yield
        t = t - _dot(t, ot)
        yield
        blk *= 2
    return t


MIX_W = 4 * ML_W
GATE_W = 2 * ML_W


def _norm_inproj_kernel(x_ref, g_ref, w_ref, e_ref, og_ref, os_ref, om_ref):
    u = _rms(x_ref[...], g_ref[...]).astype(BF16)
    dot = functools.partial(jnp.dot, preferred_element_type=F32)
    og_ref[:, 0:MIX_W] = dot(u, w_ref[:, 0:MIX_W])
    os_ref[...] = dot(u, w_ref[:, MIX_W:MIX_W + S5_CH])
    om_ref[:, 0:MIX_W] = dot(u, w_ref[:, MIX_W + S5_CH:2 * MIX_W + S5_CH])
    small = dot(u, w_ref[:, 2 * MIX_W + S5_CH:])
    rep = None
    for piece in _split3(small):
        t = dot(piece, e_ref[...])
        rep = t if rep is None else rep + t
    og_ref[:, MIX_W:] = rep[:, 0:GATE_W]
    om_ref[:, MIX_W:] = rep[:, GATE_W:]


def _norm_inproj(x, gain, w, e_rep, tm):
    m = x.shape[0]
    fix = lambda i: (0, 0)
    widths = (MIX_SLAB_W, S5_CH, MIX_SLAB_W)
    return pl.pallas_call(
        _norm_inproj_kernel,
        grid=(m // tm,),
        in_specs=[pl.BlockSpec((tm, D_MODEL), lambda i: (i, 0)),
                  pl.BlockSpec((1, D_MODEL), fix),
                  pl.BlockSpec((D_MODEL, w.shape[1]), fix),
                  pl.BlockSpec((LANES, 2 * GATE_W), fix)],
        out_specs=[pl.BlockSpec((tm, wd), lambda i: (i, 0)) for wd in widths],
        out_shape=[jax.ShapeDtypeStruct((m, wd), F32) for wd in widths],
        compiler_params=_cparams("parallel"),
        name="norm_inproj",
    )(x, gain, w, e_rep)


def _route(f, wr, br):
    fh, fl = _split_bf16(f)
    wh, wl = _split_bf16(wr)
    d = functools.partial(jnp.dot, preferred_element_type=F32)
    logits = d(fh, wh) + (d(fh, wl) + d(fl, wh)) + br
    lane = lax.broadcasted_iota(jnp.int32, logits.shape, 1)
    is_g = lane < N_GROUPS
    gl = jnp.where(is_g, logits, NEG)
    gmax = jnp.max(gl, axis=-1, keepdims=True)
    ge = jnp.where(is_g, jnp.exp(gl - gmax), 0.0)
    p_grp = ge / jnp.sum(ge, axis=-1, keepdims=True)
    g_prob = jnp.max(p_grp, axis=-1, keepdims=True)
    g_idx = jnp.min(jnp.where(is_g & (gl == gmax), lane, LANES), axis=-1, keepdims=True)
    e_lane = lane - N_GROUPS
    is_e = (e_lane >= 0) & (e_lane < N_EXPERTS) & (jnp.right_shift(e_lane, 2) == g_idx)
    le = jnp.where(is_e, logits, NEG)
    m1 = jnp.max(le, axis=-1, keepdims=True)
    i1 = jnp.min(jnp.where(is_e & (le == m1), lane, LANES), axis=-1, keepdims=True)
    is_e2 = is_e & (lane != i1)
    le2 = jnp.where(is_e2, logits, NEG)
    m2 = jnp.max(le2, axis=-1, keepdims=True)
    i2 = jnp.min(jnp.where(is_e2 & (le2 == m2), lane, LANES), axis=-1, keepdims=True)
    e2 = jnp.exp(m2 - m1)
    w1 = g_prob / (1.0 + e2)
    w2 = g_prob * e2 / (1.0 + e2)
    return fh, jnp.where(lane == i1, w1, 0.0) + jnp.where(lane == i2, w2, 0.0)


def _ffn_kernel(og_ref, os_ref, om_ref, h_ref, p_ref, wo_ref, nf_ref, wr_ref, br_ref,
                wg_ref, wu_ref, wd_ref, np_ref, wpg_ref, wpp_ref, fn_ref,
                out_ref, f_ref, gates_ref, *, final):
    acc_ref = out_ref
    gi = pl.program_id(1)

    @pl.when(gi == 0)
    def _():
        mix = jnp.concatenate([og_ref[...], os_ref[...], om_ref[...]], axis=-1)
        h1 = h_ref[...] + jnp.dot(mix, wo_ref[...], preferred_element_type=F32)
        acc_ref[...] = h1
        fh, gates = _route(_rms(h1, nf_ref[...]), wr_ref[...], br_ref[...])
        f_ref[...] = fh
        gates_ref[...] = gates

    x = f_ref[...]
    gates = gates_ref[...]
    lane = lax.broadcasted_iota(jnp.int32, gates.shape, 1)
    base = N_GROUPS + EXPERTS_PER_GROUP * gi
    acc = None
    for j in range(EXPERTS_PER_GROUP):
        gcol = jnp.sum(jnp.where(lane == base + j, gates, 0.0), axis=-1, keepdims=True)
        hg = jnp.dot(x, wg_ref[j], preferred_element_type=F32)
        hu = jnp.dot(x, wu_ref[j], preferred_element_type=F32)
        hidden = (_silu(hg) * hu * gcol).astype(BF16)
        t = jnp.dot(hidden, wd_ref[j], preferred_element_type=F32)
        acc = t if acc is None else acc + t
    acc_ref[...] += acc

    @pl.when(gi == N_GROUPS - 1)
    def _():
        h = acc_ref[...]
        gate = _sigmoid(jnp.dot(_rms(h, np_ref[...]).astype(BF16), wpg_ref[...],
                                preferred_element_type=F32))
        proj = jnp.dot(p_ref[0].astype(BF16), wpp_ref[...], preferred_element_type=F32)
        h = h + proj * gate
        if final:
            h = _rms(h, fn_ref[...])
        out_ref[...] = h


def _ffn(og, os_, om, h, p_all, layer, W, fn, tm, final):
    m = h.shape[0]
    row = lambda i, g: (i, 0)
    wsel = lambda i, g: (g, 0, 0)
    e = EXPERTS_PER_GROUP

    def fix(shape):
        return pl.BlockSpec(shape, lambda i, g: (0, 0), pipeline_mode=pl.Buffered(1))

    return pl.pallas_call(
        functools.partial(_ffn_kernel, final=final),
        grid=(m // tm, N_GROUPS),
        in_specs=[pl.BlockSpec((tm, GDN_VW), row), pl.BlockSpec((tm, S5_CH), row),
                  pl.BlockSpec((tm, ML_W), row), pl.BlockSpec((tm, D_MODEL), row),
                  pl.BlockSpec((1, tm, PLE_DIM), lambda i, g: (layer, i, 0)),
                  fix((D_MODEL, D_MODEL)), fix((1, D_MODEL)),
                  fix((D_MODEL, LANES)), fix((1, LANES)),
                  pl.BlockSpec((e, D_MODEL, D_EXPERT), wsel),
                  pl.BlockSpec((e, D_MODEL, D_EXPERT), wsel),
                  pl.BlockSpec((e, D_EXPERT, D_MODEL), wsel),
                  fix((1, D_MODEL)), fix((D_MODEL, D_MODEL)),
                  fix((PLE_DIM, D_MODEL)), fix((1, D_MODEL))],
        out_specs=pl.BlockSpec((tm, D_MODEL), row),
        out_shape=jax.ShapeDtypeStruct((m, D_MODEL), F32),
        scratch_shapes=[pltpu.VMEM((tm, D_MODEL), BF16), pltpu.VMEM((tm, LANES), F32)],
        compiler_params=_cparams("parallel", "arbitrary"),
        name="ffn",
    )(og, os_, om, h, p_all, W["w_out"], W["norm_ffn"], W["wr"], W["br"],
      W["wg"], W["wu"], W["wd"], W["norm_ple"], W["w_ple_gate"], W["w_ple_proj"], fn)


MIX_SLAB_W = 6 * ML_W
PAIRS = ML_HEADS // 2
HALF = LANES // 2


def _scan_rows(x, size, op, fill):
    row = lax.broadcasted_iota(jnp.int32, x.shape, 0)
    d = 1
    while d < size:
        x = op(x, jnp.where(row >= d, pltpu.roll(x, d, axis=0), fill))
        d *= 2
    return x


def _split3(a):
    hi = a.astype(BF16)
    r1 = a - hi.astype(F32)
    mid = r1.astype(BF16)
    lo = (r1 - mid.astype(F32)).astype(BF16)
    return hi, mid, lo


def _row_form(x_s, L):
    rr = lax.broadcasted_iota(jnp.int32, x_s.shape, 0)
    cc = lax.broadcasted_iota(jnp.int32, x_s.shape, 1)
    dg = jnp.where(rr == jnp.bitwise_and(cc, L - 1), x_s, 0.0)
    return jnp.sum(dg, axis=0, keepdims=True)


def _score_cols(x, p, L):
    if 2 * L == LANES:
        return x[:, LANES * p:LANES * (p + 1)]
    return jnp.concatenate([x[:, LANES * p:LANES * p + L],
                            x[:, LANES * p + HALF:LANES * p + HALF + L]], axis=-1)


def _bd_stack(x2):
    lo = lax.broadcasted_iota(jnp.int32, x2.shape, 1) < HALF
    return jnp.concatenate([jnp.where(lo, x2, 0.0), jnp.where(lo, 0.0, x2)], axis=0)


def _bd_ones(rows_per_half):
    shape = (2 * rows_per_half, LANES)
    r = lax.broadcasted_iota(jnp.int32, shape, 0) < rows_per_half
    c = lax.broadcasted_iota(jnp.int32, shape, 1) < HALF
    return jnp.where(r == c, 1.0, 0.0).astype(F32)


def _layer_state_spec(s_all, layer, bb, in_idx, out_idx):
    depth = s_all.shape[0]
    tail = s_all.shape[2:]
    if layer == 0 and depth > 1:
        return pl.BlockSpec((depth, bb) + tail, lambda b, c: (0, b, 0, 0, 0)), 0, {}
    return (pl.BlockSpec((1, bb) + tail, lambda b, c: (layer, b, 0, 0, 0)), 0,
            {in_idx: out_idx})


def _half_sums(x2, ones_bd, two_pieces=False):
    if not two_pieces:
        return _dot(x2, ones_bd)
    hi, lo = _split_bf16(x2)
    ob = ones_bd.astype(BF16)
    return (jnp.dot(hi, ob, preferred_element_type=F32)
            + jnp.dot(lo, ob, preferred_element_type=F32))


def _mlstm_kernel(a_ref, c0_ref, n0_ref, m0_ref, bi_ref, bf_ref, ng_ref,
                  o_ref, cn_ref, nn_ref, mn_ref, st_ref, m_ref, *, L, tv, bb, lsel):
    cidx = pl.program_id(1)
    nc = pl.num_programs(1)
    ones_ch = _bd_ones(HALF)
    r128 = lax.broadcasted_iota(jnp.int32, (LANES, LANES), 0)
    c128 = lax.broadcasted_iota(jnp.int32, (LANES, LANES), 1)
    diag128 = r128 == c128

    @pl.when(cidx == 0)
    def _():
        st_ref[...] = jnp.zeros(st_ref.shape, F32)
        m_ref[...] = m0_ref[...]
        ob = ones_ch.astype(BF16)
        for i in range(bb):
            for p in range(PAIRS):
                st_ref[i, p, 0:HALF, 0:HALF] = c0_ref[lsel, i, 2 * p]
                st_ref[i, p, HALF:, HALF:LANES] = c0_ref[lsel, i, 2 * p + 1]
                acc = None
                for piece in _split3(n0_ref[i, p]):
                    dg = jnp.where(diag128, piece.astype(F32), 0.0).astype(BF16)
                    t = jnp.dot(dg, ob, preferred_element_type=F32)
                    acc = t if acc is None else acc + t
                st_ref[i, p, :, LANES:] = acc

    W = 2 * L
    rr = lax.broadcasted_iota(jnp.int32, (L, W), 0)
    cc = lax.broadcasted_iota(jnp.int32, (L, W), 1)
    incl = rr >= jnp.bitwise_and(cc, L - 1)
    ones_keys = _bd_ones(L)
    bd256 = jnp.concatenate([ones_ch, ones_ch], axis=-1) > 0.5
    gain2 = ng_ref[...]
    gain2 = jnp.concatenate([gain2, gain2], axis=-1)
    ones_l = jnp.ones((L, LANES), F32)
    outs = [[None] * PAIRS for _ in range(bb)]
    st_news = [[None] * PAIRS for _ in range(bb)]
    m_news = [None] * bb
    rows = []
    for i in range(bb):
        li = a_ref[i, :, 4 * ML_W:5 * ML_W] + bi_ref[...]
        lf = -_softplus(-(a_ref[i, :, 5 * ML_W:6 * ML_W] + bf_ref[...]))
        if tv < L:
            valid = lax.broadcasted_iota(jnp.int32, li.shape, 0) < tv
            li = jnp.where(valid, li, NEG)
            lf = jnp.where(valid, lf, 0.0)
        bcum = _scan_rows(lf, L, jnp.add, 0.0)
        a = li - bcum
        m0 = m_ref[i]
        m_t = bcum + jnp.maximum(m0, _scan_rows(a, L, jnp.maximum, NEG))
        e_inter = jnp.exp(bcum + m0 - m_t)
        m_new = m_t[L - 1:L, :]
        b_last = bcum[L - 1:L, :]
        e_c = jnp.exp(b_last + m0 - m_new)
        kw = a_ref[i, :, ML_W:2 * ML_W] * (ML_DH ** -0.5) * jnp.exp(b_last + a - m_new)
        m_news[i] = m_new
        rows.append((a, bcum, m_t, e_inter, e_c, kw))

    def pair(i, p):
        a, bcum, m_t, e_inter, e_c, kw = rows[i]
        sl = slice(LANES * p, LANES * (p + 1))
        q2 = a_ref[i, :, sl]
        k2 = a_ref[i, :, ML_W + LANES * p:ML_W + LANES * (p + 1)] * (ML_DH ** -0.5)
        v2 = a_ref[i, :, 2 * ML_W + LANES * p:2 * ML_W + LANES * (p + 1)]
        og2 = a_ref[i, :, 3 * ML_W + LANES * p:3 * ML_W + LANES * (p + 1)]
        st = st_ref[i, p]
        arow = _row_form(_score_cols(a, p, L), L)
        qk = _dot_nt(q2, _bd_stack(k2))
        qcn = _dot(q2, st)
        upd = _dot_tn(kw[:, sl], jnp.concatenate([v2, ones_l], axis=-1))
        ec2 = e_c[:, sl]
        st_news[i][p] = st * jnp.concatenate([ec2, ec2], axis=-1) + jnp.where(bd256, upd, 0.0)
        yield
        w_intra = jnp.where(incl, jnp.exp(_score_cols(bcum, p, L) + arow - _score_cols(m_t, p, L)), 0.0)
        s2 = qk * w_intra
        nd = _dot(s2, jnp.concatenate([_bd_stack(v2), ones_keys], axis=-1))
        yield
        e2 = e_inter[:, sl]
        num = e2 * qcn[:, 0:LANES] + nd[:, 0:LANES]
        den = e2 * qcn[:, LANES:] + nd[:, LANES:]
        hh = num / jnp.maximum(jnp.abs(den), jnp.exp(-m_t[:, sl]))
        ss = _half_sums(hh * hh, ones_ch)
        yield
        outs[i][p] = hh * lax.rsqrt(ss * (1.0 / ML_DH) + EPS) * gain2 * _sigmoid(og2)

    _interleave([pair(i, p) for i in range(bb) for p in range(PAIRS)])
    for i in range(bb):
        for p in range(PAIRS):
            st_ref[i, p] = st_news[i][p]
        m_ref[i] = m_news[i]
        o_ref[i] = jnp.concatenate(outs[i], axis=-1).astype(o_ref.dtype)

    @pl.when(cidx == nc - 1)
    def _():
        for k in range(cn_ref.shape[0]):
            if k != lsel:
                cn_ref[k] = c0_ref[k]
        for i in range(bb):
            for p in range(PAIRS):
                st = st_news[i][p]
                cn_ref[lsel, i, 2 * p] = st[0:HALF, 0:HALF]
                cn_ref[lsel, i, 2 * p + 1] = st[HALF:, HALF:LANES]
                dg = jnp.where(diag128, st[:, LANES:], 0.0)
                nn_ref[i, p] = jnp.sum(dg, axis=0, keepdims=True)
        mn_ref[...] = m_ref[...]


def _mlstm(slab, c_all, layer, n0, m0, bi, bf, ng, L, tv, bb):
    bsz, t, _ = slab.shape
    nc = t // L
    fix2 = lambda b, c: (0, 0)
    c_spec, lsel, aliases = _layer_state_spec(c_all, layer, bb, 1, 1)
    n_spec = pl.BlockSpec((bb, PAIRS, 1, LANES), lambda b, c: (b, 0, 0, 0))
    m_spec = pl.BlockSpec((bb, 1, ML_W), lambda b, c: (b, 0, 0))
    return pl.pallas_call(
        functools.partial(_mlstm_kernel, L=L, tv=tv, bb=bb, lsel=lsel),
        grid=(bsz // bb, nc),
        input_output_aliases=aliases,
        in_specs=[pl.BlockSpec((bb, L, MIX_SLAB_W), lambda b, c: (b, c, 0)),
                  c_spec, n_spec, m_spec,
                  pl.BlockSpec((1, ML_W), fix2), pl.BlockSpec((1, ML_W), fix2),
                  pl.BlockSpec((1, ML_DH), fix2)],
        out_specs=[pl.BlockSpec((bb, L, ML_W), lambda b, c: (b, c, 0)), c_spec, n_spec, m_spec],
        out_shape=[jax.ShapeDtypeStruct((bsz, t, ML_W), BF16),
                   jax.ShapeDtypeStruct(c_all.shape, F32),
                   jax.ShapeDtypeStruct((bsz, PAIRS, 1, LANES), F32),
                   jax.ShapeDtypeStruct((bsz, 1, ML_W), F32)],
        scratch_shapes=[pltpu.VMEM((bb, PAIRS, LANES, 2 * LANES), F32),
                        pltpu.VMEM((bb, 1, ML_W), F32)],
        compiler_params=_cparams("parallel", "arbitrary"),
        name="mlstm",
    )(slab, c_all, n0, m0, bi, bf, ng)


def _gdn_kernel(a_ref, conv0_ref, s0_ref, cw_ref, alog_ref, dtb_ref, ng_ref,
                o_ref, convn_ref, sn_ref, ext_ref, s_ref, *, L, tv, bb, lsel):
    c = pl.program_id(1)
    nc = pl.num_programs(1)

    @pl.when(c == 0)
    def _():
        ext_ref[:, 0:SUBLANES, :] = conv0_ref[...]
        s_ref[...] = jnp.zeros(s_ref.shape, F32)
        for i in range(bb):
            for p in range(PAIRS):
                s_ref[i, p, 0:HALF, 0:HALF] = s0_ref[lsel, i, 2 * p]
                s_ref[i, p, HALF:, HALF:] = s0_ref[lsel, i, 2 * p + 1]

    W = 2 * L
    rr = lax.broadcasted_iota(jnp.int32, (L, W), 0)
    cc = lax.broadcasted_iota(jnp.int32, (L, W), 1)
    key = jnp.bitwise_and(cc, L - 1)
    incl = rr >= key
    strict = rr > key
    first_half = cc < L
    lo = lax.broadcasted_iota(jnp.int32, (L, LANES), 1) < HALF
    ones_ch = _bd_ones(HALF)
    bd128 = ones_ch > 0.5
    cw = cw_ref[...]
    gain2 = ng_ref[...]
    gain2 = jnp.concatenate([gain2, gain2], axis=-1)
    outs = [[None] * PAIRS for _ in range(bb)]
    s_news = [[None] * PAIRS for _ in range(bb)]
    carries = [None] * bb
    rows = []
    for i in range(bb):
        raw = a_ref[i, :, 0:GDN_CONV_CH]
        ext_ref[i, SUBLANES:SUBLANES + L, :] = raw
        ext = ext_ref[i]
        acc = raw * cw[CONV_W - 1:CONV_W, :]
        for j in range(CONV_W - 1):
            sh = pltpu.roll(ext, CONV_W - 1 - j, axis=0)[SUBLANES:SUBLANES + L]
            acc = acc + sh * cw[j:j + 1, :]
        qkv = _silu(acc)
        carries[i] = pltpu.roll(ext, (L + SUBLANES - tv) % (L + SUBLANES), axis=0)[0:SUBLANES]

        beta = _sigmoid(a_ref[i, :, 4 * GDN_VW:5 * GDN_VW])
        g = -jnp.exp(alog_ref[...]) * _softplus(a_ref[i, :, 5 * GDN_VW:6 * GDN_VW] + dtb_ref[...])
        if tv < L:
            valid = lax.broadcasted_iota(jnp.int32, g.shape, 0) < tv
            beta = jnp.where(valid, beta, 0.0)
            g = jnp.where(valid, g, 0.0)
        gcum = _scan_rows(g, L, jnp.add, 0.0)
        g_last = gcum[L - 1:L, :]
        rows.append((qkv, beta, gcum, jnp.exp(gcum), jnp.exp(g_last - gcum), jnp.exp(g_last)))
    for i in range(bb):
        ext_ref[i, 0:SUBLANES, :] = carries[i]

    def pair(i, p):
        qkv, beta, gcum, eg, kdec, sdec = rows[i]
        sl = slice(LANES * p, LANES * (p + 1))
        q2 = qkv[:, LANES * p:LANES * (p + 1)]
        k2 = qkv[:, GDN_QK + LANES * p:GDN_QK + LANES * (p + 1)]
        v2 = qkv[:, 2 * GDN_QK + LANES * p:2 * GDN_QK + LANES * (p + 1)]
        z2 = a_ref[i, :, GDN_CONV_CH + LANES * p:GDN_CONV_CH + LANES * (p + 1)]
        s2 = s_ref[i, p]
        ssq = _half_sums(q2 * q2, ones_ch, L < HALF)
        ssk = _half_sums(k2 * k2, ones_ch, L < HALF)
        grow = _row_form(_score_cols(gcum, p, L), L)
        yield
        q2 = q2 * lax.rsqrt(ssq + EPS) * (GDN_DK ** -0.5)
        k2 = k2 * lax.rsqrt(ssk + EPS)
        beta2 = beta[:, sl]
        eg2 = eg[:, sl]
        kb2 = k2 * beta2
        kq = _dot_nt(jnp.concatenate([_bd_stack(kb2), q2], axis=0), _bd_stack(k2))
        qs = _dot(q2 * eg2, s2)
        yield
        gam = jnp.where(incl, jnp.exp(jnp.minimum(_score_cols(gcum, p, L) - grow, 0.0)), 0.0)
        gam_s = jnp.where(strict, gam, 0.0)
        gam_bd = jnp.concatenate([jnp.where(first_half, gam_s, 0.0),
                                  jnp.where(first_half, 0.0, gam_s)], axis=0)
        t_inv = yield from _unit_lower_inverse(kq[0:W] * gam_bd, W, L)
        rhs = jnp.concatenate([v2 * beta2, kb2 * eg2], axis=-1)
        sol = _dot(t_inv, jnp.concatenate([rhs, rhs], axis=0))
        yield
        u2 = jnp.where(lo, sol[0:L, 0:LANES], sol[L:W, 0:LANES])
        w2 = jnp.where(lo, sol[0:L, LANES:], sol[L:W, LANES:])
        v_new = u2 - _dot(w2, s2)
        yield
        o2 = qs + _dot(kq[W:] * gam, _bd_stack(v_new))
        upd = _dot_tn(k2 * kdec[:, sl], v_new)
        s_news[i][p] = s2 * sdec[:, sl] + jnp.where(bd128, upd, 0.0)
        yield
        ss = _half_sums(o2 * o2, ones_ch, L < HALF)
        yield
        outs[i][p] = o2 * lax.rsqrt(ss * (1.0 / GDN_DV) + EPS) * gain2 * _silu(z2)

    _interleave([pair(i, p) for i in range(bb) for p in range(PAIRS)])
    for i in range(bb):
        for p in range(PAIRS):
            s_ref[i, p] = s_news[i][p]
        o_ref[i] = jnp.concatenate(outs[i], axis=-1).astype(o_ref.dtype)

    @pl.when(c == nc - 1)
    def _():
        for k in range(sn_ref.shape[0]):
            if k != lsel:
                sn_ref[k] = s0_ref[k]
        for i in range(bb):
            convn_ref[i] = carries[i]
            for p in range(PAIRS):
                sn_ref[lsel, i, 2 * p] = s_news[i][p][0:HALF, 0:HALF]
                sn_ref[lsel, i, 2 * p + 1] = s_news[i][p][HALF:, HALF:]


def _gdn(slab, conv0, s_all, layer, cw, alog, dtb, ng, L, tv, bb):
    bsz, t, _ = slab.shape
    nc = t // L
    fix2 = lambda b, c: (0, 0)
    s_spec, lsel, aliases = _layer_state_spec(s_all, layer, bb, 2, 2)
    cv_spec = pl.BlockSpec((bb, SUBLANES, GDN_CONV_CH), lambda b, c: (b, 0, 0))
    return pl.pallas_call(
        functools.partial(_gdn_kernel, L=L, tv=tv, bb=bb, lsel=lsel),
        grid=(bsz // bb, nc),
        input_output_aliases=aliases,
        in_specs=[pl.BlockSpec((bb, L, MIX_SLAB_W), lambda b, c: (b, c, 0)),
                  cv_spec, s_spec,
                  pl.BlockSpec((CONV_W, GDN_CONV_CH), fix2),
                  pl.BlockSpec((1, GDN_VW), fix2), pl.BlockSpec((1, GDN_VW), fix2),
                  pl.BlockSpec((1, GDN_DV), fix2)],
        out_specs=[pl.BlockSpec((bb, L, GDN_VW), lambda b, c: (b, c, 0)), cv_spec, s_spec],
        out_shape=[jax.ShapeDtypeStruct((bsz, t, GDN_VW), BF16),
                   jax.ShapeDtypeStruct((bsz, SUBLANES, GDN_CONV_CH), F32),
                   jax.ShapeDtypeStruct(s_all.shape, F32)],
        scratch_shapes=[pltpu.VMEM((bb, L + SUBLANES, GDN_CONV_CH), F32),
                        pltpu.VMEM((bb, PAIRS, LANES, LANES), F32)],
        compiler_params=_cparams("parallel", "arbitrary"),
        name="gdn",
    )(slab, conv0, s_all, cw, alog, dtb, ng)


def _cmul(ar, ai, br, bi):
    return ar * br - ai * bi, ar * bi + ai * br


def _s5_kernel(u_ref, h0_ref, lb_ref, bw_ref, cw_ref, dsk_ref, wglu_ref,
               o_ref, hn_ref, car_ref, h_ref, *, L, bb):
    c = pl.program_id(1)
    nc = pl.num_programs(1)

    @pl.when(c == 0)
    def _():
        car_ref[...] = h0_ref[...]

    rows = bb * L
    u = u_ref[...].reshape(rows, S5_CH)
    bu = jnp.dot(u.astype(BF16), bw_ref[...], preferred_element_type=F32)
    tiles = rows // SUBLANES
    x_re = bu[:, 0:S5_N].reshape(tiles, SUBLANES, S5_N)
    x_im = bu[:, S5_N:].reshape(tiles, SUBLANES, S5_N)
    p1 = (lb_ref[0:1, :], lb_ref[1:2, :])
    p2 = _cmul(*p1, *p1)
    p4 = _cmul(*p2, *p2)
    p8 = _cmul(*p4, *p4)
    sub = lax.broadcasted_iota(jnp.int32, (SUBLANES, S5_N), 0)
    for d, (pr, pi) in ((1, p1), (2, p2), (4, p4)):
        pr = jnp.where(sub >= d, pr, 0.0)
        pi = jnp.where(sub >= d, pi, 0.0)
        s_re = pltpu.roll(x_re, d, axis=1)
        s_im = pltpu.roll(x_im, d, axis=1)
        x_re, x_im = x_re + (pr * s_re - pi * s_im), x_im + (pr * s_im + pi * s_re)
    x_re = x_re.reshape(rows, S5_N)
    x_im = x_im.reshape(rows, S5_N)
    k = sub + 1
    pw_re = jnp.ones((SUBLANES, S5_N), F32)
    pw_im = jnp.zeros((SUBLANES, S5_N), F32)
    for bit, (pr, pi) in ((1, p1), (2, p2), (4, p4), (8, p8)):
        m_re, m_im = _cmul(pw_re, pw_im, pr, pi)
        on = jnp.bitwise_and(k, bit) != 0
        pw_re = jnp.where(on, m_re, pw_re)
        pw_im = jnp.where(on, m_im, pw_im)
    for i in range(bb):
        car_re = car_ref[i, 0]
        car_im = car_ref[i, 1]
        for t in range(L // SUBLANES):
            r0 = i * L + t * SUBLANES
            a_re, a_im = _cmul(pw_re, pw_im, car_re, car_im)
            t_re = x_re[r0:r0 + SUBLANES, :] + a_re
            t_im = x_im[r0:r0 + SUBLANES, :] + a_im
            h_ref[r0:r0 + SUBLANES, 0:S5_N] = t_re
            h_ref[r0:r0 + SUBLANES, S5_N:] = t_im
            car_re = jnp.broadcast_to(t_re[SUBLANES - 1:SUBLANES, :], (SUBLANES, S5_N))
            car_im = jnp.broadcast_to(t_im[SUBLANES - 1:SUBLANES, :], (SUBLANES, S5_N))
        car_ref[i, 0] = car_re
        car_ref[i, 1] = car_im
        hn_ref[i, 0] = t_re
        hn_ref[i, 1] = t_im
    y = jnp.dot(h_ref[...].astype(BF16), cw_ref[...], preferred_element_type=F32) + dsk_ref[...] * u
    zg = jax.nn.gelu(y)
    out = zg * _sigmoid(jnp.dot(zg.astype(BF16), wglu_ref[...], preferred_element_type=F32))
    o_ref[...] = out.reshape(bb, L, S5_CH).astype(o_ref.dtype)


def _s5(u, h0, lb, bw, cw, dsk, wglu, L, bb):
    bsz, t, _ = u.shape
    nc = t // L
    fix2 = lambda b, c: (0, 0)
    h_spec = pl.BlockSpec((bb, 2, SUBLANES, S5_N), lambda b, c: (b, 0, 0, 0))
    return pl.pallas_call(
        functools.partial(_s5_kernel, L=L, bb=bb),
        grid=(bsz // bb, nc),
        in_specs=[pl.BlockSpec((bb, L, S5_CH), lambda b, c: (b, c, 0)),
                  h_spec,
                  pl.BlockSpec((SUBLANES, S5_N), fix2),
                  pl.BlockSpec((S5_CH, 2 * S5_N), fix2),
                  pl.BlockSpec((2 * S5_N, S5_CH), fix2),
                  pl.BlockSpec((1, S5_CH), fix2),
                  pl.BlockSpec((S5_CH, S5_CH), fix2)],
        out_specs=[pl.BlockSpec((bb, L, S5_CH), lambda b, c: (b, c, 0)), h_spec],
        out_shape=[jax.ShapeDtypeStruct((bsz, t, S5_CH), BF16),
                   jax.ShapeDtypeStruct((bsz, 2, SUBLANES, S5_N), F32)],
        scratch_shapes=[pltpu.VMEM((bb, 2, SUBLANES, S5_N), F32),
                        pltpu.VMEM((bb * L, 2 * S5_N), F32)],
        compiler_params=_cparams("parallel", "arbitrary"),
        name="s5",
    )(u, h0, lb, bw, cw, dsk, wglu)


def _prep_layer(l, P):
    w_in = P["w_in"][l]
    sizes = (GDN_CONV_CH, GDN_VW, GDN_HEADS, GDN_HEADS, S5_CH, 3 * ML_W, ML_HEADS, ML_HEADS, ML_W)
    offs = [0]
    for s in sizes:
        offs.append(offs[-1] + s)
    g_qkv, g_z, g_b, g_a, s_u, m_qkv, m_i, m_f, m_o = [w_in[:, offs[i]:offs[i + 1]] for i in range(9)]
    zpad = jnp.zeros((D_MODEL, LANES - 4 * GDN_HEADS), F32)
    w_cat = jnp.concatenate([g_qkv, g_z, s_u, m_qkv, m_o, g_b, g_a, m_i, m_f, zpad],
                            axis=1).astype(BF16)

    lr = P["s5_lam_re"][l].astype(F32)
    li = P["s5_lam_im"][l].astype(F32)
    dt = jnp.exp(P["s5_log_dt"][l].astype(F32))[:, None]
    mag = jnp.exp(lr * dt)
    lb_re = mag * jnp.cos(li * dt)
    lb_im = mag * jnp.sin(li * dt)
    den = lr * lr + li * li
    c_re = ((lb_re - 1.0) * lr + lb_im * li) / den
    c_im = (lb_im * lr - (lb_re - 1.0) * li) / den
    b_r = P["s5_B_re"][l].astype(F32)
    b_i = P["s5_B_im"][l].astype(F32)
    bb_re = c_re[..., None] * b_r - c_im[..., None] * b_i
    bb_im = c_re[..., None] * b_i + c_im[..., None] * b_r
    eye_g = jnp.eye(S5_GROUPS, dtype=F32)
    bd = lambda m: jnp.einsum("gph,gk->ghkp", m, eye_g).reshape(S5_CH, S5_N)
    bw = jnp.concatenate([bd(bb_re), bd(bb_im)], axis=1).astype(BF16)
    cd = lambda m: jnp.einsum("ghp,gk->gpkh", m, eye_g).reshape(S5_N, S5_CH)
    cw = jnp.concatenate([cd(P["s5_C_re"][l].astype(F32)),
                          -cd(P["s5_C_im"][l].astype(F32))], axis=0).astype(BF16)
    lb = jnp.zeros((SUBLANES, S5_N), F32).at[0].set(lb_re.reshape(-1)).at[1].set(lb_im.reshape(-1))

    wr = jnp.zeros((D_MODEL, LANES), F32)
    wr = wr.at[:, 0:N_GROUPS].set(P["w_router_group"][l])
    wr = wr.at[:, N_GROUPS:N_GROUPS + N_EXPERTS].set(P["w_router_expert"][l])
    br = jnp.zeros((1, LANES), F32)
    br = br.at[0, 0:N_GROUPS].set(P["b_router_group"][l])
    br = br.at[0, N_GROUPS:N_GROUPS + N_EXPERTS].set(P["b_router_expert"][l])

    rep_row = lambda v: jnp.repeat(v.astype(F32), ML_DH).reshape(1, ML_W)
    return dict(
        norm_mix=P["norm_mix"][l].reshape(1, D_MODEL).astype(F32),
        w_cat=w_cat,
        w_out=P["w_out"][l].astype(BF16),
        conv_w=P["gdn_conv_w"][l].astype(F32),
        alog=rep_row(P["gdn_A_log"][l]), dtb=rep_row(P["gdn_dt_bias"][l]),
        gdn_norm=P["gdn_norm"][l].reshape(1, GDN_DV).astype(F32),
        lb=lb, bw=bw, cw=cw,
        s5_d=P["s5_D"][l].reshape(1, S5_CH).astype(F32),
        w_glu=P["s5_w_glu"][l].astype(BF16),
        ml_bi=rep_row(P["ml_ig_bias"][l]), ml_bf=rep_row(P["ml_fg_bias"][l]),
        ml_norm=P["ml_norm"][l].reshape(1, ML_DH).astype(F32),
        norm_ffn=P["norm_ffn"][l].reshape(1, D_MODEL).astype(F32),
        wr=wr, br=br,
        wg=P["w_exp_gate"][l].astype(BF16), wu=P["w_exp_up"][l].astype(BF16),
        wd=P["w_exp_down"][l].astype(BF16),
        norm_ple=P["norm_ple"][l].reshape(1, D_MODEL).astype(F32),
        w_ple_gate=P["w_ple_gate"][l].astype(BF16),
        w_ple_proj=P["w_ple_proj"][l].astype(BF16),
    )


def _trunk(x, p, states, layers, final_norm, *, L, tv, Ls, tm, tm_ffn, bb, bbs):
    conv0, gdn0, s5re0, s5im0, mc0, mn0, mm0 = states
    bsz, t, _ = x.shape
    m = bsz * t
    h = x.reshape(m, D_MODEL)
    outs = [[] for _ in range(5)]
    last_row = (tv - 1) % SUBLANES
    p_all = p.reshape(p.shape[0], m, PLE_DIM)
    e_rep = (jnp.arange(2 * GATE_W)[None, :] // ML_DH == jnp.arange(LANES)[:, None]).astype(BF16)
    gdn_all, mc_all = gdn0, mc0
    for l, W in enumerate(layers):
        slab_g, s_u, slab_m = _norm_inproj(h, W["norm_mix"], W["w_cat"], e_rep, tm)
        conv_in = jnp.pad(conv0[l], ((0, 0), (SUBLANES - (CONV_W - 1), 0), (0, 0)))
        o_gdn, conv_n, gdn_all = _gdn(slab_g.reshape(bsz, t, MIX_SLAB_W), conv_in, gdn_all, l,
                                      W["conv_w"], W["alog"], W["dtb"], W["gdn_norm"], L, tv, bb)
        h0 = jnp.stack([s5re0[l].reshape(bsz, S5_N), s5im0[l].reshape(bsz, S5_N)], axis=1)
        h0 = jnp.broadcast_to(h0[:, :, None, :], (bsz, 2, SUBLANES, S5_N))
        o_s5, s5_n = _s5(s_u.reshape(bsz, t, S5_CH), h0, W["lb"], W["bw"], W["cw"], W["s5_d"],
                         W["w_glu"], Ls, bbs)
        o_ml, mc_all, n_n, m_n = _mlstm(slab_m.reshape(bsz, t, MIX_SLAB_W), mc_all, l,
                                        mn0[l].reshape(bsz, PAIRS, 1, LANES),
                                        jnp.repeat(mm0[l], ML_DH, axis=-1).reshape(bsz, 1, ML_W),
                                        W["ml_bi"], W["ml_bf"], W["ml_norm"], L, tv, bb)
        h = _ffn(o_gdn.reshape(m, GDN_VW), o_s5.reshape(m, S5_CH), o_ml.reshape(m, ML_W), h,
                 p_all, l, W, final_norm, tm_ffn, l == len(layers) - 1)
        outs[0].append(conv_n[:, SUBLANES - (CONV_W - 1):])
        outs[1].append(s5_n[:, 0, last_row].reshape(bsz, S5_GROUPS, S5_STATE))
        outs[2].append(s5_n[:, 1, last_row].reshape(bsz, S5_GROUPS, S5_STATE))
        outs[3].append(n_n.reshape(bsz, ML_HEADS, ML_DH))
        outs[4].append(m_n[:, 0, ::ML_DH])
    conv_o, s5re_o, s5im_o, mn_o, mm_o = (jnp.stack(o) for o in outs)
    return (h.reshape(bsz, t, D_MODEL), conv_o, gdn_all, s5re_o, s5im_o, mc_all, mn_o, mm_o)


def kernel(x_prompt, x_sample, p_prompt, p_sample, state_gdn_conv, state_gdn, state_s5_re, state_s5_im, state_mlstm_C, state_mlstm_n, state_mlstm_m, norm_mix, w_in, w_out, gdn_conv_w, gdn_A_log, gdn_dt_bias, gdn_norm, s5_lam_re, s5_lam_im, s5_log_dt, s5_B_re, s5_B_im, s5_C_re, s5_C_im, s5_D, s5_w_glu, ml_ig_bias, ml_fg_bias, ml_norm, norm_ffn, w_router_group, b_router_group, w_router_expert, b_router_expert, w_exp_gate, w_exp_up, w_exp_down, norm_ple, w_ple_gate, w_ple_proj, final_norm):
    P = dict(norm_mix=norm_mix, w_in=w_in, w_out=w_out, gdn_conv_w=gdn_conv_w, gdn_A_log=gdn_A_log,
             gdn_dt_bias=gdn_dt_bias, gdn_norm=gdn_norm, s5_lam_re=s5_lam_re, s5_lam_im=s5_lam_im,
             s5_log_dt=s5_log_dt, s5_B_re=s5_B_re, s5_B_im=s5_B_im, s5_C_re=s5_C_re, s5_C_im=s5_C_im,
             s5_D=s5_D, s5_w_glu=s5_w_glu, ml_ig_bias=ml_ig_bias, ml_fg_bias=ml_fg_bias,
             ml_norm=ml_norm, norm_ffn=norm_ffn, w_router_group=w_router_group,
             b_router_group=b_router_group, w_router_expert=w_router_expert,
             b_router_expert=b_router_expert, w_exp_gate=w_exp_gate, w_exp_up=w_exp_up,
             w_exp_down=w_exp_down, norm_ple=norm_ple, w_ple_gate=w_ple_gate, w_ple_proj=w_ple_proj)
    depth = norm_mix.shape[0]
    layers = [_prep_layer(l, P) for l in range(depth)]
    fnorm = final_norm.reshape(1, D_MODEL).astype(F32)

    bp, tp, _ = x_prompt.shape
    zeros = lambda *s: jnp.zeros((depth, bp) + s, F32)
    prompt_init = (zeros(CONV_W - 1, GDN_CONV_CH), zeros(GDN_HEADS, GDN_DK, GDN_DV),
                   zeros(S5_GROUPS, S5_STATE), zeros(S5_GROUPS, S5_STATE),
                   zeros(ML_HEADS, ML_DH, ML_DH), zeros(ML_HEADS, ML_DH), zeros(ML_HEADS))
    lp = math.gcd(tp, 64)
    lsp = math.gcd(tp, 512)
    res_p = _trunk(x_prompt, p_prompt, prompt_init, layers, fnorm,
                   L=lp, tv=lp, Ls=lsp, tm=512, tm_ffn=1024, bb=8, bbs=1)

    bs, ts, _ = x_sample.shape
    tpad = -(-ts // SUBLANES) * SUBLANES
    xs = jnp.pad(x_sample, ((0, 0), (0, tpad - ts), (0, 0)))
    ps = jnp.pad(p_sample, ((0, 0), (0, 0), (0, tpad - ts), (0, 0)))
    sample_init = (state_gdn_conv, state_gdn, state_s5_re, state_s5_im,
                   state_mlstm_C, state_mlstm_n, state_mlstm_m)
    res_s = _trunk(xs, ps, sample_init, layers, fnorm,
                   L=tpad, tv=ts, Ls=tpad, tm=512, tm_ffn=1024, bb=16, bbs=8)
    y_sample = res_s[0][:, :ts]
    return (res_p[0], y_sample) + res_p[1:] + res_s[1:]
```

```python
import functools
import math

import jax
import jax.numpy as jnp
from jax import lax
from jax.experimental import pallas as pl
from jax.experimental.pallas import tpu as pltpu

F32 = jnp.float32
BF16 = jnp.bfloat16

D_MODEL = 1024
DEPTH = 2
GDN_HEADS = 6
GDN_DK = 64
GDN_DV = 64
GDN_QK = GDN_HEADS * GDN_DK
GDN_VW = GDN_HEADS * GDN_DV
GDN_CONV_CH = 2 * GDN_QK + GDN_VW
CONV_W = 4
S5_GROUPS = 16
S5_GROUP_CH = 16
S5_CH = S5_GROUPS * S5_GROUP_CH
S5_STATE = 64
S5_N = S5_GROUPS * S5_STATE
ML_HEADS = 6
ML_DH = 64
ML_W = ML_HEADS * ML_DH
N_GROUPS = 4
EXPERTS_PER_GROUP = 4
N_EXPERTS = N_GROUPS * EXPERTS_PER_GROUP
D_EXPERT = 256
PLE_DIM = 256
EPS = 1e-6

LANES = 128
SUBLANES = 8
NEG = -1e30
VMEM_LIMIT = 56 * 1024 * 1024

def _cparams(*sem):
    return pltpu.CompilerParams(dimension_semantics=sem, vmem_limit_bytes=VMEM_LIMIT)


def _dot(a, b):
    return jnp.dot(a.astype(BF16), b.astype(BF16), preferred_element_type=F32)


def _dot_nt(a, b):
    return lax.dot_general(a.astype(BF16), b.astype(BF16), (((1,), (1,)), ((), ())),
                           preferred_element_type=F32)


def _dot_tn(a, b):
    return lax.dot_general(a.astype(BF16), b.astype(BF16), (((0,), (0,)), ((), ())),
                           preferred_element_type=F32)


def _split_bf16(a):
    hi = a.astype(BF16)
    lo = (a - hi.astype(F32)).astype(BF16)
    return hi, lo


def _rms(x, gain):
    return x * lax.rsqrt(jnp.mean(x * x, axis=-1, keepdims=True) + EPS) * gain


def _softplus(x):
    return jnp.maximum(x, 0.0) + jnp.log(1.0 + jnp.exp(-jnp.abs(x)))


def _sigmoid(x):
    return 0.5 * jnp.tanh(0.5 * x) + 0.5


def _silu(x):
    h = 0.5 * x
    return h * jnp.tanh(h) + h


def _interleave(gens):
    live = list(gens)
    while live:
        still = []
        for g in live:
            try:
                next(g)
                still.append(g)
            except StopIteration:
                pass
        live = still


def _unit_lower_inverse(n_mat, size, top=None):
    top = size if top is None else top
    r = lax.broadcasted_iota(jnp.int32, (size, size), 0)
    c = lax.broadcasted_iota(jnp.int32, (size, size), 1)
    base = min(16, top)
    same = jnp.bitwise_xor(r, c) < base
    nd = jnp.where(same, n_mat, 0.0)
    eye = jnp.where(r == c, 1.0, 0.0).astype(F32)
    t = eye - nd
    p = 1
    if 2 * p < base:
        x = _dot(nd, nd)
        yield
    while 2 * p < base:
        t_next = t + _dot(t, x)
        if 4 * p < base:
            x = _dot(x, x)
        t = t_next
        yield
        p *= 2
    blk = base
    while blk < top:
        pair = jnp.bitwise_xor(r, c)
        off = jnp.where((pair < 2 * blk) & (pair >= blk), n_mat, 0.0)
        ot = _dot(off, t)
        yield
        t = t - _dot(t, ot)
        yield
        blk *= 2
    return t


MIX_W = 4 * ML_W
GATE_W = 2 * ML_W


def _norm_inproj_kernel(x_ref, g_ref, w_ref, e_ref, og_ref, os_ref, om_ref):
    u = _rms(x_ref[...], g_ref[...]).astype(BF16)
    dot = functools.partial(jnp.dot, preferred_element_type=F32)
    og_ref[:, 0:MIX_W] = dot(u, w_ref[:, 0:MIX_W])
    os_ref[...] = dot(u, w_ref[:, MIX_W:MIX_W + S5_CH])
    om_ref[:, 0:MIX_W] = dot(u, w_ref[:, MIX_W + S5_CH:2 * MIX_W + S5_CH])
    small = dot(u, w_ref[:, 2 * MIX_W + S5_CH:])
    rep = None
    for piece in _split3(small):
        t = dot(piece, e_ref[...])
        rep = t if rep is None else rep + t
    og_ref[:, MIX_W:] = rep[:, 0:GATE_W]
    om_ref[:, MIX_W:] = rep[:, GATE_W:]


def _norm_inproj(x, gain, w, e_rep, tm):
    m = x.shape[0]
    fix = lambda i: (0, 0)
    widths = (MIX_SLAB_W, S5_CH, MIX_SLAB_W)
    return pl.pallas_call(
        _norm_inproj_kernel,
        grid=(m // tm,),
        in_specs=[pl.BlockSpec((tm, D_MODEL), lambda i: (i, 0)),
                  pl.BlockSpec((1, D_MODEL), fix),
                  pl.BlockSpec((D_MODEL, w.shape[1]), fix),
                  pl.BlockSpec((LANES, 2 * GATE_W), fix)],
        out_specs=[pl.BlockSpec((tm, wd), lambda i: (i, 0)) for wd in widths],
        out_shape=[jax.ShapeDtypeStruct((m, wd), F32) for wd in widths],
        compiler_params=_cparams("parallel"),
        name="norm_inproj",
    )(x, gain, w, e_rep)


def _route(f, wr, br):
    fh, fl = _split_bf16(f)
    wh, wl = _split_bf16(wr)
    d = functools.partial(jnp.dot, preferred_element_type=F32)
    logits = d(fh, wh) + (d(fh, wl) + d(fl, wh)) + br
    lane = lax.broadcasted_iota(jnp.int32, logits.shape, 1)
    is_g = lane < N_GROUPS
    gl = jnp.where(is_g, logits, NEG)
    gmax = jnp.max(gl, axis=-1, keepdims=True)
    ge = jnp.where(is_g, jnp.exp(gl - gmax), 0.0)
    p_grp = ge / jnp.sum(ge, axis=-1, keepdims=True)
    g_prob = jnp.max(p_grp, axis=-1, keepdims=True)
    g_idx = jnp.min(jnp.where(is_g & (gl == gmax), lane, LANES), axis=-1, keepdims=True)
    e_lane = lane - N_GROUPS
    is_e = (e_lane >= 0) & (e_lane < N_EXPERTS) & (jnp.right_shift(e_lane, 2) == g_idx)
    le = jnp.where(is_e, logits, NEG)
    m1 = jnp.max(le, axis=-1, keepdims=True)
    i1 = jnp.min(jnp.where(is_e & (le == m1), lane, LANES), axis=-1, keepdims=True)
    is_e2 = is_e & (lane != i1)
    le2 = jnp.where(is_e2, logits, NEG)
    m2 = jnp.max(le2, axis=-1, keepdims=True)
    i2 = jnp.min(jnp.where(is_e2 & (le2 == m2), lane, LANES), axis=-1, keepdims=True)
    e2 = jnp.exp(m2 - m1)
    w1 = g_prob / (1.0 + e2)
    w2 = g_prob * e2 / (1.0 + e2)
    return fh, jnp.where(lane == i1, w1, 0.0) + jnp.where(lane == i2, w2, 0.0)


def _ffn_kernel(og_ref, os_ref, om_ref, h_ref, p_ref, wo_ref, nf_ref, wr_ref, br_ref,
                wg_ref, wu_ref, wd_ref, np_ref, wpg_ref, wpp_ref, fn_ref,
                out_ref, f_ref, gates_ref, *, final):
    acc_ref = out_ref
    gi = pl.program_id(1)

    @pl.when(gi == 0)
    def _():
        mix = jnp.concatenate([og_ref[...], os_ref[...], om_ref[...]], axis=-1)
        h1 = h_ref[...] + jnp.dot(mix, wo_ref[...], preferred_element_type=F32)
        acc_ref[...] = h1
        fh, gates = _route(_rms(h1, nf_ref[...]), wr_ref[...], br_ref[...])
        f_ref[...] = fh
        gates_ref[...] = gates

    x = f_ref[...]
    gates = gates_ref[...]
    lane = lax.broadcasted_iota(jnp.int32, gates.shape, 1)
    base = N_GROUPS + EXPERTS_PER_GROUP * gi
    acc = None
    for j in range(EXPERTS_PER_GROUP):
        gcol = jnp.sum(jnp.where(lane == base + j, gates, 0.0), axis=-1, keepdims=True)
        hg = jnp.dot(x, wg_ref[j], preferred_element_type=F32)
        hu = jnp.dot(x, wu_ref[j], preferred_element_type=F32)
        hidden = (_silu(hg) * hu * gcol).astype(BF16)
        t = jnp.dot(hidden, wd_ref[j], preferred_element_type=F32)
        acc = t if acc is None else acc + t
    acc_ref[...] += acc

    @pl.when(gi == N_GROUPS - 1)
    def _():
        h = acc_ref[...]
        gate = _sigmoid(jnp.dot(_rms(h, np_ref[...]).astype(BF16), wpg_ref[...],
                                preferred_element_type=F32))
        proj = jnp.dot(p_ref[0].astype(BF16), wpp_ref[...], preferred_element_type=F32)
        h = h + proj * gate
        if final:
            h = _rms(h, fn_ref[...])
        out_ref[...] = h


def _ffn(og, os_, om, h, p_all, layer, W, fn, tm, final):
    m = h.shape[0]
    row = lambda i, g: (i, 0)
    wsel = lambda i, g: (g, 0, 0)
    e = EXPERTS_PER_GROUP

    def fix(shape):
        return pl.BlockSpec(shape, lambda i, g: (0, 0), pipeline_mode=pl.Buffered(1))

    return pl.pallas_call(
        functools.partial(_ffn_kernel, final=final),
        grid=(m // tm, N_GROUPS),
        in_specs=[pl.BlockSpec((tm, GDN_VW), row), pl.BlockSpec((tm, S5_CH), row),
                  pl.BlockSpec((tm, ML_W), row), pl.BlockSpec((tm, D_MODEL), row),
                  pl.BlockSpec((1, tm, PLE_DIM), lambda i, g: (layer, i, 0)),
                  fix((D_MODEL, D_MODEL)), fix((1, D_MODEL)),
                  fix((D_MODEL, LANES)), fix((1, LANES)),
                  pl.BlockSpec((e, D_MODEL, D_EXPERT), wsel),
                  pl.BlockSpec((e, D_MODEL, D_EXPERT), wsel),
                  pl.BlockSpec((e, D_EXPERT, D_MODEL), wsel),
                  fix((1, D_MODEL)), fix((D_MODEL, D_MODEL)),
                  fix((PLE_DIM, D_MODEL)), fix((1, D_MODEL))],
        out_specs=pl.BlockSpec((tm, D_MODEL), row),
        out_shape=jax.ShapeDtypeStruct((m, D_MODEL), F32),
        scratch_shapes=[pltpu.VMEM((tm, D_MODEL), BF16), pltpu.VMEM((tm, LANES), F32)],
        compiler_params=_cparams("parallel", "arbitrary"),
        name="ffn",
    )(og, os_, om, h, p_all, W["w_out"], W["norm_ffn"], W["wr"], W["br"],
      W["wg"], W["wu"], W["wd"], W["norm_ple"], W["w_ple_gate"], W["w_ple_proj"], fn)


MIX_SLAB_W = 6 * ML_W
PAIRS = ML_HEADS // 2
HALF = LANES // 2


def _scan_rows(x, size, op, fill):
    row = lax.broadcasted_iota(jnp.int32, x.shape, 0)
    d = 1
    while d < size:
        x = op(x, jnp.where(row >= d, pltpu.roll(x, d, axis=0), fill))
        d *= 2
    return x


def _split3(a):
    hi = a.astype(BF16)
    r1 = a - hi.astype(F32)
    mid = r1.astype(BF16)
    lo = (r1 - mid.astype(F32)).astype(BF16)
    return hi, mid, lo


def _row_form(x_s, L):
    rr = lax.broadcasted_iota(jnp.int32, x_s.shape, 0)
    cc = lax.broadcasted_iota(jnp.int32, x_s.shape, 1)
    dg = jnp.where(rr == jnp.bitwise_and(cc, L - 1), x_s, 0.0)
    return jnp.sum(dg, axis=0, keepdims=True)


def _score_cols(x, p, L):
    if 2 * L == LANES:
        return x[:, LANES * p:LANES * (p + 1)]
    return jnp.concatenate([x[:, LANES * p:LANES * p + L],
                            x[:, LANES * p + HALF:LANES * p + HALF + L]], axis=-1)


def _bd_stack(x2):
    lo = lax.broadcasted_iota(jnp.int32, x2.shape, 1) < HALF
    return jnp.concatenate([jnp.where(lo, x2, 0.0), jnp.where(lo, 0.0, x2)], axis=0)


def _bd_ones(rows_per_half):
    shape = (2 * rows_per_half, LANES)
    r = lax.broadcasted_iota(jnp.int32, shape, 0) < rows_per_half
    c = lax.broadcasted_iota(jnp.int32, shape, 1) < HALF
    return jnp.where(r == c, 1.0, 0.0).astype(F32)


def _layer_state_spec(s_all, layer, bb, in_idx, out_idx):
    depth = s_all.shape[0]
    tail = s_all.shape[2:]
    if layer == 0 and depth > 1:
        return pl.BlockSpec((depth, bb) + tail, lambda b, c: (0, b, 0, 0, 0)), 0, {}
    return (pl.BlockSpec((1, bb) + tail, lambda b, c: (layer, b, 0, 0, 0)), 0,
            {in_idx: out_idx})


def _half_sums(x2, ones_bd, two_pieces=False):
    if not two_pieces:
        return _dot(x2, ones_bd)
    hi, lo = _split_bf16(x2)
    ob = ones_bd.astype(BF16)
    return (jnp.dot(hi, ob, preferred_element_type=F32)
            + jnp.dot(lo, ob, preferred_element_type=F32))


def _mlstm_kernel(a_ref, c0_ref, n0_ref, m0_ref, bi_ref, bf_ref, ng_ref,
                  o_ref, cn_ref, nn_ref, mn_ref, st_ref, m_ref, *, L, tv, bb, lsel):
    cidx = pl.program_id(1)
    nc = pl.num_programs(1)
    ones_ch = _bd_ones(HALF)
    r128 = lax.broadcasted_iota(jnp.int32, (LANES, LANES), 0)
    c128 = lax.broadcasted_iota(jnp.int32, (LANES, LANES), 1)
    diag128 = r128 == c128

    @pl.when(cidx == 0)
    def _():
        st_ref[...] = jnp.zeros(st_ref.shape, F32)
        m_ref[...] = m0_ref[...]
        ob = ones_ch.astype(BF16)
        for i in range(bb):
            for p in range(PAIRS):
                st_ref[i, p, 0:HALF, 0:HALF] = c0_ref[lsel, i, 2 * p]
                st_ref[i, p, HALF:, HALF:LANES] = c0_ref[lsel, i, 2 * p + 1]
                acc = None
                for piece in _split3(n0_ref[i, p]):
                    dg = jnp.where(diag128, piece.astype(F32), 0.0).astype(BF16)
                    t = jnp.dot(dg, ob, preferred_element_type=F32)
                    acc = t if acc is None else acc + t
                st_ref[i, p, :, LANES:] = acc

    W = 2 * L
    rr = lax.broadcasted_iota(jnp.int32, (L, W), 0)
    cc = lax.broadcasted_iota(jnp.int32, (L, W), 1)
    incl = rr >= jnp.bitwise_and(cc, L - 1)
    ones_keys = _bd_ones(L)
    bd256 = jnp.concatenate([ones_ch, ones_ch], axis=-1) > 0.5
    gain2 = ng_ref[...]
    gain2 = jnp.concatenate([gain2, gain2], axis=-1)
    ones_l = jnp.ones((L, LANES), F32)
    outs = [[None] * PAIRS for _ in range(bb)]
    st_news = [[None] * PAIRS for _ in range(bb)]
    m_news = [None] * bb
    rows = []
    for i in range(bb):
        li = a_ref[i, :, 4 * ML_W:5 * ML_W] + bi_ref[...]
        lf = -_softplus(-(a_ref[i, :, 5 * ML_W:6 * ML_W] + bf_ref[...]))
        if tv < L:
            valid = lax.broadcasted_iota(jnp.int32, li.shape, 0) < tv
            li = jnp.where(valid, li, NEG)
            lf = jnp.where(valid, lf, 0.0)
        bcum = _scan_rows(lf, L, jnp.add, 0.0)
        a = li - bcum
        m0 = m_ref[i]
        m_t = bcum + jnp.maximum(m0, _scan_rows(a, L, jnp.maximum, NEG))
        e_inter = jnp.exp(bcum + m0 - m_t)
        m_new = m_t[L - 1:L, :]
        b_last = bcum[L - 1:L, :]
        e_c = jnp.exp(b_last + m0 - m_new)
        kw = a_ref[i, :, ML_W:2 * ML_W] * (ML_DH ** -0.5) * jnp.exp(b_last + a - m_new)
        m_news[i] = m_new
        rows.append((a, bcum, m_t, e_inter, e_c, kw))

    def pair(i, p):
        a, bcum, m_t, e_inter, e_c, kw = rows[i]
        sl = slice(LANES * p, LANES * (p + 1))
        q2 = a_ref[i, :, sl]
        k2 = a_ref[i, :, ML_W + LANES * p:ML_W + LANES * (p + 1)] * (ML_DH ** -0.5)
        v2 = a_ref[i, :, 2 * ML_W + LANES * p:2 * ML_W + LANES * (p + 1)]
        og2 = a_ref[i, :, 3 * ML_W + LANES * p:3 * ML_W + LANES * (p + 1)]
        st = st_ref[i, p]
        arow = _row_form(_score_cols(a, p, L), L)
        qk = _dot_nt(q2, _bd_stack(k2))
        qcn = _dot(q2, st)
        upd = _dot_tn(kw[:, sl], jnp.concatenate([v2, ones_l], axis=-1))
        ec2 = e_c[:, sl]
        st_news[i][p] = st * jnp.concatenate([ec2, ec2], axis=-1) + jnp.where(bd256, upd, 0.0)
        yield
        w_intra = jnp.where(incl, jnp.exp(_score_cols(bcum, p, L) + arow - _score_cols(m_t, p, L)), 0.0)
        s2 = qk * w_intra
        nd = _dot(s2, jnp.concatenate([_bd_stack(v2), ones_keys], axis=-1))
        yield
        e2 = e_inter[:, sl]
        num = e2 * qcn[:, 0:LANES] + nd[:, 0:LANES]
        den = e2 * qcn[:, LANES:] + nd[:, LANES:]
        hh = num / jnp.maximum(jnp.abs(den), jnp.exp(-m_t[:, sl]))
        ss = _half_sums(hh * hh, ones_ch)
        yield
        outs[i][p] = hh * lax.rsqrt(ss * (1.0 / ML_DH) + EPS) * gain2 * _sigmoid(og2)

    _interleave([pair(i, p) for i in range(bb) for p in range(PAIRS)])
    for i in range(bb):
        for p in range(PAIRS):
            st_ref[i, p] = st_news[i][p]
        m_ref[i] = m_news[i]
        o_ref[i] = jnp.concatenate(outs[i], axis=-1).astype(o_ref.dtype)

    @pl.when(cidx == nc - 1)
    def _():
        for k in range(cn_ref.shape[0]):
            if k != lsel:
                cn_ref[k] = c0_ref[k]
        for i in range(bb):
            for p in range(PAIRS):
                st = st_news[i][p]
                cn_ref[lsel, i, 2 * p] = st[0:HALF, 0:HALF]
                cn_ref[lsel, i, 2 * p + 1] = st[HALF:, HALF:LANES]
                dg = jnp.where(diag128, st[:, LANES:], 0.0)
                nn_ref[i, p] = jnp.sum(dg, axis=0, keepdims=True)
        mn_ref[...] = m_ref[...]


def _mlstm(slab, c_all, layer, n0, m0, bi, bf, ng, L, tv, bb):
    bsz, t, _ = slab.shape
    nc = t // L
    fix2 = lambda b, c: (0, 0)
    c_spec, lsel, aliases = _layer_state_spec(c_all, layer, bb, 1, 1)
    n_spec = pl.BlockSpec((bb, PAIRS, 1, LANES), lambda b, c: (b, 0, 0, 0))
    m_spec = pl.BlockSpec((bb, 1, ML_W), lambda b, c: (b, 0, 0))
    return pl.pallas_call(
        functools.partial(_mlstm_kernel, L=L, tv=tv, bb=bb, lsel=lsel),
        grid=(bsz // bb, nc),
        input_output_aliases=aliases,
        in_specs=[pl.BlockSpec((bb, L, MIX_SLAB_W), lambda b, c: (b, c, 0)),
                  c_spec, n_spec, m_spec,
                  pl.BlockSpec((1, ML_W), fix2), pl.BlockSpec((1, ML_W), fix2),
                  pl.BlockSpec((1, ML_DH), fix2)],
        out_specs=[pl.BlockSpec((bb, L, ML_W), lambda b, c: (b, c, 0)), c_spec, n_spec, m_spec],
        out_shape=[jax.ShapeDtypeStruct((bsz, t, ML_W), BF16),
                   jax.ShapeDtypeStruct(c_all.shape, F32),
                   jax.ShapeDtypeStruct((bsz, PAIRS, 1, LANES), F32),
                   jax.ShapeDtypeStruct((bsz, 1, ML_W), F32)],
        scratch_shapes=[pltpu.VMEM((bb, PAIRS, LANES, 2 * LANES), F32),
                        pltpu.VMEM((bb, 1, ML_W), F32)],
        compiler_params=_cparams("parallel", "arbitrary"),
        name="mlstm",
    )(slab, c_all, n0, m0, bi, bf, ng)


def _gdn_kernel(a_ref, conv0_ref, s0_ref, cw_ref, alog_ref, dtb_ref, ng_ref,
                o_ref, convn_ref, sn_ref, ext_ref, s_ref, *, L, tv, bb, lsel):
    c = pl.program_id(1)
    nc = pl.num_programs(1)

    @pl.when(c == 0)
    def _():
        ext_ref[:, 0:SUBLANES, :] = conv0_ref[...]
        s_ref[...] = jnp.zeros(s_ref.shape, F32)
        for i in range(bb):
            for p in range(PAIRS):
                s_ref[i, p, 0:HALF, 0:HALF] = s0_ref[lsel, i, 2 * p]
                s_ref[i, p, HALF:, HALF:] = s0_ref[lsel, i, 2 * p + 1]

    W = 2 * L
    rr = lax.broadcasted_iota(jnp.int32, (L, W), 0)
    cc = lax.broadcasted_iota(jnp.int32, (L, W), 1)
    key = jnp.bitwise_and(cc, L - 1)
    incl = rr >= key
    strict = rr > key
    first_half = cc < L
    lo = lax.broadcasted_iota(jnp.int32, (L, LANES), 1) < HALF
    ones_ch = _bd_ones(HALF)
    bd128 = ones_ch > 0.5
    cw = cw_ref[...]
    gain2 = ng_ref[...]
    gain2 = jnp.concatenate([gain2, gain2], axis=-1)
    outs = [[None] * PAIRS for _ in range(bb)]
    s_news = [[None] * PAIRS for _ in range(bb)]
    carries = [None] * bb
    rows = []
    for i in range(bb):
        raw = a_ref[i, :, 0:GDN_CONV_CH]
        ext_ref[i, SUBLANES:SUBLANES + L, :] = raw
        ext = ext_ref[i]
        acc = raw * cw[CONV_W - 1:CONV_W, :]
        for j in range(CONV_W - 1):
            sh = pltpu.roll(ext, CONV_W - 1 - j, axis=0)[SUBLANES:SUBLANES + L]
            acc = acc + sh * cw[j:j + 1, :]
        qkv = _silu(acc)
        carries[i] = pltpu.roll(ext, (L + SUBLANES - tv) % (L + SUBLANES), axis=0)[0:SUBLANES]

        beta = _sigmoid(a_ref[i, :, 4 * GDN_VW:5 * GDN_VW])
        g = -jnp.exp(alog_ref[...]) * _softplus(a_ref[i, :, 5 * GDN_VW:6 * GDN_VW] + dtb_ref[...])
        if tv < L:
            valid = lax.broadcasted_iota(jnp.int32, g.shape, 0) < tv
            beta = jnp.where(valid, beta, 0.0)
            g = jnp.where(valid, g, 0.0)
        gcum = _scan_rows(g, L, jnp.add, 0.0)
        g_last = gcum[L - 1:L, :]
        rows.append((qkv, beta, gcum, jnp.exp(gcum), jnp.exp(g_last - gcum), jnp.exp(g_last)))
    for i in range(bb):
        ext_ref[i, 0:SUBLANES, :] = carries[i]

    def pair(i, p):
        qkv, beta, gcum, eg, kdec, sdec = rows[i]
        sl = slice(LANES * p, LANES * (p + 1))
        q2 = qkv[:, LANES * p:LANES * (p + 1)]
        k2 = qkv[:, GDN_QK + LANES * p:GDN_QK + LANES * (p + 1)]
        v2 = qkv[:, 2 * GDN_QK + LANES * p:2 * GDN_QK + LANES * (p + 1)]
        z2 = a_ref[i, :, GDN_CONV_CH + LANES * p:GDN_CONV_CH + LANES * (p + 1)]
        s2 = s_ref[i, p]
        ssq = _half_sums(q2 * q2, ones_ch, L < HALF)
        ssk = _half_sums(k2 * k2, ones_ch, L < HALF)
        grow = _row_form(_score_cols(gcum, p, L), L)
        yield
        q2 = q2 * lax.rsqrt(ssq + EPS) * (GDN_DK ** -0.5)
        k2 = k2 * lax.rsqrt(ssk + EPS)
        beta2 = beta[:, sl]
        eg2 = eg[:, sl]
        kb2 = k2 * beta2
        kq = _dot_nt(jnp.concatenate([_bd_stack(kb2), q2], axis=0), _bd_stack(k2))
        qs = _dot(q2 * eg2, s2)
        yield
        gam = jnp.where(incl, jnp.exp(jnp.minimum(_score_cols(gcum, p, L) - grow, 0.0)), 0.0)
        gam_s = jnp.where(strict, gam, 0.0)
        gam_bd = jnp.concatenate([jnp.where(first_half, gam_s, 0.0),
                                  jnp.where(first_half, 0.0, gam_s)], axis=0)
        t_inv = yield from _unit_lower_inverse(kq[0:W] * gam_bd, W, L)
        rhs = jnp.concatenate([v2 * beta2, kb2 * eg2], axis=-1)
        sol = _dot(t_inv, jnp.concatenate([rhs, rhs], axis=0))
        yield
        u2 = jnp.where(lo, sol[0:L, 0:LANES], sol[L:W, 0:LANES])
        w2 = jnp.where(lo, sol[0:L, LANES:], sol[L:W, LANES:])
        v_new = u2 - _dot(w2, s2)
        yield
        o2 = qs + _dot(kq[W:] * gam, _bd_stack(v_new))
        upd = _dot_tn(k2 * kdec[:, sl], v_new)
        s_news[i][p] = s2 * sdec[:, sl] + jnp.where(bd128, upd, 0.0)
        yield
        ss = _half_sums(o2 * o2, ones_ch, L < HALF)
        yield
        outs[i][p] = o2 * lax.rsqrt(ss * (1.0 / GDN_DV) + EPS) * gain2 * _silu(z2)

    _interleave([pair(i, p) for i in range(bb) for p in range(PAIRS)])
    for i in range(bb):
        for p in range(PAIRS):
            s_ref[i, p] = s_news[i][p]
        o_ref[i] = jnp.concatenate(outs[i], axis=-1).astype(o_ref.dtype)

    @pl.when(c == nc - 1)
    def _():
        for k in range(sn_ref.shape[0]):
            if k != lsel:
                sn_ref[k] = s0_ref[k]
        for i in range(bb):
            convn_ref[i] = carries[i]
            for p in range(PAIRS):
                sn_ref[lsel, i, 2 * p] = s_news[i][p][0:HALF, 0:HALF]
                sn_ref[lsel, i, 2 * p + 1] = s_news[i][p][HALF:, HALF:]


def _gdn(slab, conv0, s_all, layer, cw, alog, dtb, ng, L, tv, bb):
    bsz, t, _ = slab.shape
    nc = t // L
    fix2 = lambda b, c: (0, 0)
    s_spec, lsel, aliases = _layer_state_spec(s_all, layer, bb, 2, 2)
    cv_spec = pl.BlockSpec((bb, SUBLANES, GDN_CONV_CH), lambda b, c: (b, 0, 0))
    return pl.pallas_call(
        functools.partial(_gdn_kernel, L=L, tv=tv, bb=bb, lsel=lsel),
        grid=(bsz // bb, nc),
        input_output_aliases=aliases,
        in_specs=[pl.BlockSpec((bb, L, MIX_SLAB_W), lambda b, c: (b, c, 0)),
                  cv_spec, s_spec,
                  pl.BlockSpec((CONV_W, GDN_CONV_CH), fix2),
                  pl.BlockSpec((1, GDN_VW), fix2), pl.BlockSpec((1, GDN_VW), fix2),
                  pl.BlockSpec((1, GDN_DV), fix2)],
        out_specs=[pl.BlockSpec((bb, L, GDN_VW), lambda b, c: (b, c, 0)), cv_spec, s_spec],
        out_shape=[jax.ShapeDtypeStruct((bsz, t, GDN_VW), BF16),
                   jax.ShapeDtypeStruct((bsz, SUBLANES, GDN_CONV_CH), F32),
                   jax.ShapeDtypeStruct(s_all.shape, F32)],
        scratch_shapes=[pltpu.VMEM((bb, L + SUBLANES, GDN_CONV_CH), F32),
                        pltpu.VMEM((bb, PAIRS, LANES, LANES), F32)],
        compiler_params=_cparams("parallel", "arbitrary"),
        name="gdn",
    )(slab, conv0, s_all, cw, alog, dtb, ng)


def _cmul(ar, ai, br, bi):
    return ar * br - ai * bi, ar * bi + ai * br


def _s5_kernel(u_ref, h0_ref, lb_ref, bw_ref, cw_ref, dsk_ref, wglu_ref,
               o_ref, hn_ref, car_ref, h_ref, *, L, bb):
    c = pl.program_id(1)
    nc = pl.num_programs(1)

    @pl.when(c == 0)
    def _():
        car_ref[...] = h0_ref[...]

    rows = bb * L
    u = u_ref[...].reshape(rows, S5_CH)
    bu = jnp.dot(u.astype(BF16), bw_ref[...], preferred_element_type=F32)
    tiles = rows // SUBLANES
    x_re = bu[:, 0:S5_N].reshape(tiles, SUBLANES, S5_N)
    x_im = bu[:, S5_N:].reshape(tiles, SUBLANES, S5_N)
    p1 = (lb_ref[0:1, :], lb_ref[1:2, :])
    p2 = _cmul(*p1, *p1)
    p4 = _cmul(*p2, *p2)
    p8 = _cmul(*p4, *p4)
    sub = lax.broadcasted_iota(jnp.int32, (SUBLANES, S5_N), 0)
    for d, (pr, pi) in ((1, p1), (2, p2), (4, p4)):
        pr = jnp.where(sub >= d, pr, 0.0)
        pi = jnp.where(sub >= d, pi, 0.0)
        s_re = pltpu.roll(x_re, d, axis=1)
        s_im = pltpu.roll(x_im, d, axis=1)
        x_re, x_im = x_re + (pr * s_re - pi * s_im), x_im + (pr * s_im + pi * s_re)
    x_re = x_re.reshape(rows, S5_N)
    x_im = x_im.reshape(rows, S5_N)
    k = sub + 1
    pw_re = jnp.ones((SUBLANES, S5_N), F32)
    pw_im = jnp.zeros((SUBLANES, S5_N), F32)
    for bit, (pr, pi) in ((1, p1), (2, p2), (4, p4), (8, p8)):
        m_re, m_im = _cmul(pw_re, pw_im, pr, pi)
        on = jnp.bitwise_and(k, bit) != 0
        pw_re = jnp.where(on, m_re, pw_re)
        pw_im = jnp.where(on, m_im, pw_im)
    for i in range(bb):
        car_re = car_ref[i, 0]
        car_im = car_ref[i, 1]
        for t in range(L // SUBLANES):
            r0 = i * L + t * SUBLANES
            a_re, a_im = _cmul(pw_re, pw_im, car_re, car_im)
            t_re = x_re[r0:r0 + SUBLANES, :] + a_re
            t_im = x_im[r0:r0 + SUBLANES, :] + a_im
            h_ref[r0:r0 + SUBLANES, 0:S5_N] = t_re
            h_ref[r0:r0 + SUBLANES, S5_N:] = t_im
            car_re = jnp.broadcast_to(t_re[SUBLANES - 1:SUBLANES, :], (SUBLANES, S5_N))
            car_im = jnp.broadcast_to(t_im[SUBLANES - 1:SUBLANES, :], (SUBLANES, S5_N))
        car_ref[i, 0] = car_re
        car_ref[i, 1] = car_im
        hn_ref[i, 0] = t_re
        hn_ref[i, 1] = t_im
    y = jnp.dot(h_ref[...].astype(BF16), cw_ref[...], preferred_element_type=F32) + dsk_ref[...] * u
    zg = jax.nn.gelu(y)
    out = zg * _sigmoid(jnp.dot(zg.astype(BF16), wglu_ref[...], preferred_element_type=F32))
    o_ref[...] = out.reshape(bb, L, S5_CH).astype(o_ref.dtype)


def _s5(u, h0, lb, bw, cw, dsk, wglu, L, bb):
    bsz, t, _ = u.shape
    nc = t // L
    fix2 = lambda b, c: (0, 0)
    h_spec = pl.BlockSpec((bb, 2, SUBLANES, S5_N), lambda b, c: (b, 0, 0, 0))
    return pl.pallas_call(
        functools.partial(_s5_kernel, L=L, bb=bb),
        grid=(bsz // bb, nc),
        in_specs=[pl.BlockSpec((bb, L, S5_CH), lambda b, c: (b, c, 0)),
                  h_spec,
                  pl.BlockSpec((SUBLANES, S5_N), fix2),
                  pl.BlockSpec((S5_CH, 2 * S5_N), fix2),
                  pl.BlockSpec((2 * S5_N, S5_CH), fix2),
                  pl.BlockSpec((1, S5_CH), fix2),
                  pl.BlockSpec((S5_CH, S5_CH), fix2)],
        out_specs=[pl.BlockSpec((bb, L, S5_CH), lambda b, c: (b, c, 0)), h_spec],
        out_shape=[jax.ShapeDtypeStruct((bsz, t, S5_CH), BF16),
                   jax.ShapeDtypeStruct((bsz, 2, SUBLANES, S5_N), F32)],
        scratch_shapes=[pltpu.VMEM((bb, 2, SUBLANES, S5_N), F32),
                        pltpu.VMEM((bb * L, 2 * S5_N), F32)],
        compiler_params=_cparams("parallel", "arbitrary"),
        name="s5",
    )(u, h0, lb, bw, cw, dsk, wglu)


def _prep_layer(l, P):
    w_in = P["w_in"][l]
    sizes = (GDN_CONV_CH, GDN_VW, GDN_HEADS, GDN_HEADS, S5_CH, 3 * ML_W, ML_HEADS, ML_HEADS, ML_W)
    offs = [0]
    for s in sizes:
        offs.append(offs[-1] + s)
    g_qkv, g_z, g_b, g_a, s_u, m_qkv, m_i, m_f, m_o = [w_in[:, offs[i]:offs[i + 1]] for i in range(9)]
    zpad = jnp.zeros((D_MODEL, LANES - 4 * GDN_HEADS), F32)
    w_cat = jnp.concatenate([g_qkv, g_z, s_u, m_qkv, m_o, g_b, g_a, m_i, m_f, zpad],
                            axis=1).astype(BF16)

    lr = P["s5_lam_re"][l].astype(F32)
    li = P["s5_lam_im"][l].astype(F32)
    dt = jnp.exp(P["s5_log_dt"][l].astype(F32))[:, None]
    mag = jnp.exp(lr * dt)
    lb_re = mag * jnp.cos(li * dt)
    lb_im = mag * jnp.sin(li * dt)
    den = lr * lr + li * li
    c_re = ((lb_re - 1.0) * lr + lb_im * li) / den
    c_im = (lb_im * lr - (lb_re - 1.0) * li) / den
    b_r = P["s5_B_re"][l].astype(F32)
    b_i = P["s5_B_im"][l].astype(F32)
    bb_re = c_re[..., None] * b_r - c_im[..., None] * b_i
    bb_im = c_re[..., None] * b_i + c_im[..., None] * b_r
    eye_g = jnp.eye(S5_GROUPS, dtype=F32)
    bd = lambda m: jnp.einsum("gph,gk->ghkp", m, eye_g).reshape(S5_CH, S5_N)
    bw = jnp.concatenate([bd(bb_re), bd(bb_im)], axis=1).astype(BF16)
    cd = lambda m: jnp.einsum("ghp,gk->gpkh", m, eye_g).reshape(S5_N, S5_CH)
    cw = jnp.concatenate([cd(P["s5_C_re"][l].astype(F32)),
                          -cd(P["s5_C_im"][l].astype(F32))], axis=0).astype(BF16)
    lb = jnp.zeros((SUBLANES, S5_N), F32).at[0].set(lb_re.reshape(-1)).at[1].set(lb_im.reshape(-1))

    wr = jnp.zeros((D_MODEL, LANES), F32)
    wr = wr.at[:, 0:N_GROUPS].set(P["w_router_group"][l])
    wr = wr.at[:, N_GROUPS:N_GROUPS + N_EXPERTS].set(P["w_router_expert"][l])
    br = jnp.zeros((1, LANES), F32)
    br = br.at[0, 0:N_GROUPS].set(P["b_router_group"][l])
    br = br.at[0, N_GROUPS:N_GROUPS + N_EXPERTS].set(P["b_router_expert"][l])

    rep_row = lambda v: jnp.repeat(v.astype(F32), ML_DH).reshape(1, ML_W)
    return dict(
        norm_mix=P["norm_mix"][l].reshape(1, D_MODEL).astype(F32),
        w_cat=w_cat,
        w_out=P["w_out"][l].astype(BF16),
        conv_w=P["gdn_conv_w"][l].astype(F32),
        alog=rep_row(P["gdn_A_log"][l]), dtb=rep_row(P["gdn_dt_bias"][l]),
        gdn_norm=P["gdn_norm"][l].reshape(1, GDN_DV).astype(F32),
        lb=lb, bw=bw, cw=cw,
        s5_d=P["s5_D"][l].reshape(1, S5_CH).astype(F32),
        w_glu=P["s5_w_glu"][l].astype(BF16),
        ml_bi=rep_row(P["ml_ig_bias"][l]), ml_bf=rep_row(P["ml_fg_bias"][l]),
        ml_norm=P["ml_norm"][l].reshape(1, ML_DH).astype(F32),
        norm_ffn=P["norm_ffn"][l].reshape(1, D_MODEL).astype(F32),
        wr=wr, br=br,
        wg=P["w_exp_gate"][l].astype(BF16), wu=P["w_exp_up"][l].astype(BF16),
        wd=P["w_exp_down"][l].astype(BF16),
        norm_ple=P["norm_ple"][l].reshape(1, D_MODEL).astype(F32),
        w_ple_gate=P["w_ple_gate"][l].astype(BF16),
        w_ple_proj=P["w_ple_proj"][l].astype(BF16),
    )


def _trunk(x, p, states, layers, final_norm, *, L, tv, Ls, tm, tm_ffn, bb, bbs):
    conv0, gdn0, s5re0, s5im0, mc0, mn0, mm0 = states
    bsz, t, _ = x.shape
    m = bsz * t
    h = x.reshape(m, D_MODEL)
    outs = [[] for _ in range(5)]
    last_row = (tv - 1) % SUBLANES
    p_all = p.reshape(p.shape[0], m, PLE_DIM)
    e_rep = (jnp.arange(2 * GATE_W)[None, :] // ML_DH == jnp.arange(LANES)[:, None]).astype(BF16)
    gdn_all, mc_all = gdn0, mc0
    for l, W in enumerate(layers):
        slab_g, s_u, slab_m = _norm_inproj(h, W["norm_mix"], W["w_cat"], e_rep, tm)
        conv_in = jnp.pad(conv0[l], ((0, 0), (SUBLANES - (CONV_W - 1), 0), (0, 0)))
        o_gdn, conv_n, gdn_all = _gdn(slab_g.reshape(bsz, t, MIX_SLAB_W), conv_in, gdn_all, l,
                                      W["conv_w"], W["alog"], W["dtb"], W["gdn_norm"], L, tv, bb)
        h0 = jnp.stack([s5re0[l].reshape(bsz, S5_N), s5im0[l].reshape(bsz, S5_N)], axis=1)
        h0 = jnp.broadcast_to(h0[:, :, None, :], (bsz, 2, SUBLANES, S5_N))
        o_s5, s5_n = _s5(s_u.reshape(bsz, t, S5_CH), h0, W["lb"], W["bw"], W["cw"], W["s5_d"],
                         W["w_glu"], Ls, bbs)
        o_ml, mc_all, n_n, m_n = _mlstm(slab_m.reshape(bsz, t, MIX_SLAB_W), mc_all, l,
                                        mn0[l].reshape(bsz, PAIRS, 1, LANES),
                                        jnp.repeat(mm0[l], ML_DH, axis=-1).reshape(bsz, 1, ML_W),
                                        W["ml_bi"], W["ml_bf"], W["ml_norm"], L, tv, bb)
        h = _ffn(o_gdn.reshape(m, GDN_VW), o_s5.reshape(m, S5_CH), o_ml.reshape(m, ML_W), h,
                 p_all, l, W, final_norm, tm_ffn, l == len(layers) - 1)
        outs[0].append(conv_n[:, SUBLANES - (CONV_W - 1):])
        outs[1].append(s5_n[:, 0, last_row].reshape(bsz, S5_GROUPS, S5_STATE))
        outs[2].append(s5_n[:, 1, last_row].reshape(bsz, S5_GROUPS, S5_STATE))
        outs[3].append(n_n.reshape(bsz, ML_HEADS, ML_DH))
        outs[4].append(m_n[:, 0, ::ML_DH])
    conv_o, s5re_o, s5im_o, mn_o, mm_o = (jnp.stack(o) for o in outs)
    return (h.reshape(bsz, t, D_MODEL), conv_o, gdn_all, s5re_o, s5im_o, mc_all, mn_o, mm_o)


def kernel(x_prompt, x_sample, p_prompt, p_sample, state_gdn_conv, state_gdn, state_s5_re, state_s5_im, state_mlstm_C, state_mlstm_n, state_mlstm_m, norm_mix, w_in, w_out, gdn_conv_w, gdn_A_log, gdn_dt_bias, gdn_norm, s5_lam_re, s5_lam_im, s5_log_dt, s5_B_re, s5_B_im, s5_C_re, s5_C_im, s5_D, s5_w_glu, ml_ig_bias, ml_fg_bias, ml_norm, norm_ffn, w_router_group, b_router_group, w_router_expert, b_router_expert, w_exp_gate, w_exp_up, w_exp_down, norm_ple, w_ple_gate, w_ple_proj, final_norm):
    P = dict(norm_mix=norm_mix, w_in=w_in, w_out=w_out, gdn_conv_w=gdn_conv_w, gdn_A_log=gdn_A_log,
             gdn_dt_bias=gdn_dt_bias, gdn_norm=gdn_norm, s5_lam_re=s5_lam_re, s5_lam_im=s5_lam_im,
             s5_log_dt=s5_log_dt, s5_B_re=s5_B_re, s5_B_im=s5_B_im, s5_C_re=s5_C_re, s5_C_im=s5_C_im,
             s5_D=s5_D, s5_w_glu=s5_w_glu, ml_ig_bias=ml_ig_bias, ml_fg_bias=ml_fg_bias,
             ml_norm=ml_norm, norm_ffn=norm_ffn, w_router_group=w_router_group,
             b_router_group=b_router_group, w_router_expert=w_router_expert,
             b_router_expert=b_router_expert, w_exp_gate=w_exp_gate, w_exp_up=w_exp_up,
             w_exp_down=w_exp_down, norm_ple=norm_ple, w_ple_gate=w_ple_gate, w_ple_proj=w_ple_proj)
    depth = norm_mix.shape[0]
    layers = [_prep_layer(l, P) for l in range(depth)]
    fnorm = final_norm.reshape(1, D_MODEL).astype(F32)

    bp, tp, _ = x_prompt.shape
    zeros = lambda *s: jnp.zeros((depth, bp) + s, F32)
    prompt_init = (zeros(CONV_W - 1, GDN_CONV_CH), zeros(GDN_HEADS, GDN_DK, GDN_DV),
                   zeros(S5_GROUPS, S5_STATE), zeros(S5_GROUPS, S5_STATE),
                   zeros(ML_HEADS, ML_DH, ML_DH), zeros(ML_HEADS, ML_DH), zeros(ML_HEADS))
    lp = math.gcd(tp, 64)
    lsp = math.gcd(tp, 1024)
    res_p = _trunk(x_prompt, p_prompt, prompt_init, layers, fnorm,
                   L=lp, tv=lp, Ls=lsp, tm=512, tm_ffn=1024, bb=8, bbs=1)

    bs, ts, _ = x_sample.shape
    tpad = -(-ts // SUBLANES) * SUBLANES
    xs = jnp.pad(x_sample, ((0, 0), (0, tpad - ts), (0, 0)))
    ps = jnp.pad(p_sample, ((0, 0), (0, 0), (0, tpad - ts), (0, 0)))
    sample_init = (state_gdn_conv, state_gdn, state_s5_re, state_s5_im,
                   state_mlstm_C, state_mlstm_n, state_mlstm_m)
    res_s = _trunk(xs, ps, sample_init, layers, fnorm,
                   L=tpad, tv=ts, Ls=tpad, tm=512, tm_ffn=1024, bb=16, bbs=8)
    y_sample = res_s[0][:, :ts]
    return (res_p[0], y_sample) + res_p[1:] + res_s[1:]
```
